```python
import math
import jax
import jax.numpy as jnp
from jax import lax
import numpy as np

D_MODEL = 2048
BATCH = 2
SEQ = 4096
DEPTH = 1

HEAD_DIM = 128
MOBA_HEADS = D_MODEL // (2 * HEAD_DIM)
MOBA_WIDTH = MOBA_HEADS * HEAD_DIM
MOBA_BLOCK = 256
MOBA_TOPK = 3
MOBA_Q_CHUNK = 32
DIFF_HEAD_DIM = 128
DIFF_HEADS = D_MODEL // (4 * DIFF_HEAD_DIM)
DIFF_QK_WIDTH = DIFF_HEADS * 2 * DIFF_HEAD_DIM
DIFF_V_WIDTH = DIFF_HEADS * 2 * DIFF_HEAD_DIM
DIFF_Q_BLOCK = 128
N_ATTN_HEADS = MOBA_HEADS + DIFF_HEADS
REL_BUCKETS = 32
REL_MAX_DISTANCE = 128
MEM_LEN = 256
XATTN_HEADS = 4
XATTN_WIDTH = XATTN_HEADS * HEAD_DIM
N_EXPERTS = 32
TOP_K = 4
D_FF = D_MODEL
SWIGLU_LIMIT = 7.0
SWIGLU_ALPHA = 1.702
NORM_EPS = 1e-5
IN_SECTIONS = (MOBA_WIDTH, MOBA_WIDTH, MOBA_WIDTH, DIFF_QK_WIDTH, DIFF_QK_WIDTH, DIFF_V_WIDTH, D_MODEL, D_MODEL)
IN_COLS = 3 * MOBA_WIDTH + 2 * DIFF_QK_WIDTH + DIFF_V_WIDTH + 2 * D_MODEL

kernel_name = 'hybrid_moba_diffattn_xattn_moe_block'


def rms_norm(x, gain):
    xf = x.astype(jnp.float32)
    y = xf * lax.rsqrt(jnp.mean(xf * xf, axis=-1, keepdims=True) + NORM_EPS)
    return (y * gain.astype(jnp.float32)).astype(x.dtype)


def rel_bucket(dist):
    n = jnp.maximum(dist, 0)
    max_exact = REL_BUCKETS // 2
    nf = jnp.maximum(n, max_exact).astype(jnp.float32)
    large = max_exact + (jnp.log(nf / max_exact) / math.log(REL_MAX_DISTANCE / max_exact)
                         * (REL_BUCKETS - max_exact)).astype(jnp.int32)
    return jnp.where(n < max_exact, n, jnp.minimum(large, REL_BUCKETS - 1))


def moba_attention(q, k, v, table):
    b, nh, s, dh = q.shape
    nb = -(-s // MOBA_BLOCK)
    s_pad = nb * MOBA_BLOCK
    topk = min(MOBA_TOPK, nb)
    pad = ((0, 0), (0, 0), (0, s_pad - s), (0, 0))
    q, k, v = jnp.pad(q, pad), jnp.pad(k, pad), jnp.pad(v, pad)
    k_blocks = k.reshape(b * nh, nb, MOBA_BLOCK, dh)
    v_blocks = v.reshape(b * nh, nb, MOBA_BLOCK, dh)
    k_mean = jnp.mean(k_blocks.astype(jnp.float32), axis=2).reshape(b, nh, nb, dh)
    gather = jax.vmap(lambda blocks, idx: blocks[idx])
    head_ids = jnp.arange(nh)[None, :, None, None, None]
    offs = jnp.arange(MOBA_BLOCK)
    scale = dh ** -0.5

    def chunk(c):
        start = c * MOBA_Q_CHUNK
        blk = start // MOBA_BLOCK
        q_pos = start + jnp.arange(MOBA_Q_CHUNK)
        qc = lax.dynamic_slice_in_dim(q, start, MOBA_Q_CHUNK, axis=2)
        gate = jnp.einsum('bhqd,bhnd->bhqn', qc.astype(jnp.float32), k_mean)
        gate = jnp.where(jnp.arange(nb) < blk, gate, -jnp.inf)
        _, sel = lax.top_k(gate, topk)
        sel_flat = sel.reshape(b * nh, MOBA_Q_CHUNK, topk)
        k_sel = gather(k_blocks, sel_flat).reshape(b, nh, MOBA_Q_CHUNK, topk, MOBA_BLOCK, dh)
        v_sel = gather(v_blocks, sel_flat).reshape(b, nh, MOBA_Q_CHUNK, topk, MOBA_BLOCK, dh)
        k_own = lax.dynamic_slice_in_dim(k, blk * MOBA_BLOCK, MOBA_BLOCK, axis=2)
        v_own = lax.dynamic_slice_in_dim(v, blk * MOBA_BLOCK, MOBA_BLOCK, axis=2)
        kpos_sel = sel[..., None] * MOBA_BLOCK + offs
        bias_sel = table[rel_bucket(q_pos[:, None, None] - kpos_sel), head_ids]
        s_sel = jnp.einsum('bhqd,bhqjld->bhqjl', qc, k_sel).astype(jnp.float32) * scale + bias_sel
        s_sel = jnp.where((jnp.arange(topk) < blk)[:, None], s_sel, -jnp.inf)
        kpos_own = blk * MOBA_BLOCK + offs
        dist_own = q_pos[:, None] - kpos_own[None, :]
        bias_own = jnp.moveaxis(table[rel_bucket(dist_own)], -1, 0)
        s_own = jnp.einsum('bhqd,bhld->bhql', qc, k_own).astype(jnp.float32) * scale + bias_own
        s_own = jnp.where(dist_own >= 0, s_own, -jnp.inf)
        logits = jnp.concatenate([s_sel.reshape(b, nh, MOBA_Q_CHUNK, topk * MOBA_BLOCK), s_own], axis=-1)
        p = jax.nn.softmax(logits, axis=-1).astype(v.dtype)
        p_sel = p[..., :topk * MOBA_BLOCK].reshape(b, nh, MOBA_Q_CHUNK, topk, MOBA_BLOCK)
        p_own = p[..., topk * MOBA_BLOCK:]
        return (jnp.einsum('bhqjl,bhqjld->bhqd', p_sel, v_sel)
                + jnp.einsum('bhql,bhld->bhqd', p_own, v_own))

    out = lax.map(chunk, jnp.arange(s_pad // MOBA_Q_CHUNK))
    out = jnp.moveaxis(out, 0, 2).reshape(b, nh, s_pad, dh)
    return out[:, :, :s]


def diff_attention(q, k, v, table, lam):
    b, nh, _, s, dh = q.shape
    k_pos = jnp.arange(s)
    scale = dh ** -0.5

    def block(c):
        start = c * DIFF_Q_BLOCK
        q_pos = start + jnp.arange(DIFF_Q_BLOCK)
        qc = lax.dynamic_slice_in_dim(q, start, DIFF_Q_BLOCK, axis=3)
        dist = q_pos[:, None] - k_pos[None, :]
        bias = jnp.moveaxis(table[rel_bucket(dist)], -1, 0)[:, None]
        logits = jnp.einsum('bhiqd,bhikd->bhiqk', qc, k).astype(jnp.float32) * scale + bias
        logits = jnp.where(dist >= 0, logits, -jnp.inf)
        p = jax.nn.softmax(logits, axis=-1)
        attn = p[:, :, 0] - lam * p[:, :, 1]
        return jnp.einsum('bhqk,bhkd->bhqd', attn.astype(v.dtype), v)

    out = lax.map(block, jnp.arange(s // DIFF_Q_BLOCK))
    return jnp.moveaxis(out, 0, 2).reshape(b, nh, s, 2 * dh)


def cross_attention(h, mem_n, w_q, w_k, w_v, w_o):
    b, s, _ = h.shape
    m = mem_n.shape[1]
    q = (h @ w_q).reshape(b, s, XATTN_HEADS, HEAD_DIM)
    k = (mem_n @ w_k).reshape(b, m, XATTN_HEADS, HEAD_DIM)
    v = (mem_n @ w_v).reshape(b, m, XATTN_HEADS, HEAD_DIM)
    logits = jnp.einsum('bqhd,bmhd->bhqm', q, k).astype(jnp.float32) * (HEAD_DIM ** -0.5)
    p = jax.nn.softmax(logits, axis=-1).astype(v.dtype)
    o = jnp.einsum('bhqm,bmhd->bqhd', p, v).reshape(b, s, XATTN_WIDTH)
    return o @ w_o


def routed_ffn(h, w_router, b_router, w_gate_up, b_gate_up, w_down, b_down):
    b, s, d = h.shape
    t = h.reshape(b * s, d)
    logits = (t @ w_router).astype(jnp.float32) + b_router.astype(jnp.float32)
    top_val, top_idx = lax.top_k(logits, TOP_K)
    top_w = jax.nn.softmax(top_val, axis=-1)
    combine = jnp.sum(jax.nn.one_hot(top_idx, N_EXPERTS, dtype=jnp.float32) * top_w[..., None], axis=1)
    out = jnp.zeros((b * s, d), jnp.float32)
    for e in range(N_EXPERTS):
        gu = t @ w_gate_up[e] + b_gate_up[e]
        gate = jnp.minimum(gu[:, 0::2], SWIGLU_LIMIT)
        up = jnp.clip(gu[:, 1::2], -SWIGLU_LIMIT, SWIGLU_LIMIT)
        act = (up + 1.0) * gate * jax.nn.sigmoid(SWIGLU_ALPHA * gate)
        y = act @ w_down[e] + b_down[e]
        out = out + combine[:, e:e + 1] * y.astype(jnp.float32)
    return out.astype(h.dtype).reshape(b, s, d)


def split_heads(t, n_heads, head_dim):
    b, s, _ = t.shape
    return t.reshape(b, s, n_heads, head_dim).transpose(0, 2, 1, 3)


def split_pair_heads(t):
    b, s, _ = t.shape
    return t.reshape(b, s, DIFF_HEADS, 2, DIFF_HEAD_DIM).transpose(0, 2, 3, 1, 4)


def merge_heads(t):
    b, nh, s, dh = t.shape
    return t.transpose(0, 2, 1, 3).reshape(b, s, nh * dh)


def setup_inputs(seed: int = 0) -> dict:
    key = jax.random.key(seed)
    ks = jax.random.split(key, 32)
    f32 = jnp.float32

    def nrm(k, shape, scale):
        return jax.random.normal(k, shape, f32) * scale

    def gain(k, shape):
        return 1.0 + 0.01 * jax.random.normal(k, shape, f32)

    L = DEPTH
    return {
        'x': nrm(ks[0], (BATCH, SEQ, D_MODEL), 1.0),
        'mem': nrm(ks[1], (BATCH, MEM_LEN, D_MODEL), 1.0),
        'rel_bias_table': nrm(ks[2], (REL_BUCKETS, N_ATTN_HEADS), 0.5),
        'mix_norm': gain(ks[3], (L, D_MODEL)),
        'w_in': nrm(ks[4], (L, D_MODEL, IN_COLS), D_MODEL ** -0.5),
        'diff_lambda_q1': nrm(ks[5], (L, DIFF_HEAD_DIM), 0.1),
        'diff_lambda_k1': nrm(ks[6], (L, DIFF_HEAD_DIM), 0.1),
        'diff_lambda_q2': nrm(ks[7], (L, DIFF_HEAD_DIM), 0.1),
        'diff_lambda_k2': nrm(ks[8], (L, DIFF_HEAD_DIM), 0.1),
        'diff_subln': gain(ks[9], (L, 2 * DIFF_HEAD_DIM)),
        'w_branch_moba': nrm(ks[10], (L, MOBA_WIDTH, D_MODEL), MOBA_WIDTH ** -0.5),
        'w_branch_diff': nrm(ks[11], (L, DIFF_V_WIDTH, D_MODEL), DIFF_V_WIDTH ** -0.5),
        'w_mix_out': nrm(ks[12], (L, D_MODEL, D_MODEL), D_MODEL ** -0.5),
        'xattn_norm': gain(ks[13], (L, D_MODEL)),
        'mem_norm': gain(ks[14], (L, D_MODEL)),
        'w_xq': nrm(ks[15], (L, D_MODEL, XATTN_WIDTH), D_MODEL ** -0.5),
        'w_xk': nrm(ks[16], (L, D_MODEL, XATTN_WIDTH), D_MODEL ** -0.5),
        'w_xv': nrm(ks[17], (L, D_MODEL, XATTN_WIDTH), D_MODEL ** -0.5),
        'w_xo': nrm(ks[18], (L, XATTN_WIDTH, D_MODEL), XATTN_WIDTH ** -0.5),
        'ffn_norm': gain(ks[19], (L, D_MODEL)),
        'w_router': nrm(ks[20], (L, D_MODEL, N_EXPERTS), D_MODEL ** -0.5),
        'b_router': nrm(ks[21], (L, N_EXPERTS), 0.01),
        'w_gate_up': nrm(ks[22], (L, N_EXPERTS, D_MODEL, 2 * D_FF), D_MODEL ** -0.5),
        'b_gate_up': nrm(ks[23], (L, N_EXPERTS, 2 * D_FF), 0.01),
        'w_down': nrm(ks[24], (L, N_EXPERTS, D_FF, D_MODEL), D_FF ** -0.5),
        'b_down': nrm(ks[25], (L, N_EXPERTS, D_MODEL), 0.01),
        'final_norm': gain(ks[26], (D_MODEL,)),
    }


def reference(x, mem, rel_bias_table, mix_norm, w_in, diff_lambda_q1, diff_lambda_k1,
              diff_lambda_q2, diff_lambda_k2, diff_subln, w_branch_moba, w_branch_diff,
              w_mix_out, xattn_norm, mem_norm, w_xq, w_xk, w_xv, w_xo, ffn_norm,
              w_router, b_router, w_gate_up, b_gate_up, w_down, b_down, final_norm):
    splits = np.cumsum(IN_SECTIONS)[:-1].tolist()
    table_moba = rel_bias_table[:, :MOBA_HEADS]
    table_diff = rel_bias_table[:, MOBA_HEADS:]
    for l in range(DEPTH):
        h = rms_norm(x, mix_norm[l])
        qa, ka, va, qb, kb, vb, ga, gb = jnp.split(h @ w_in[l], splits, axis=-1)
        o_moba = moba_attention(split_heads(qa, MOBA_HEADS, HEAD_DIM), split_heads(ka, MOBA_HEADS, HEAD_DIM),
                                split_heads(va, MOBA_HEADS, HEAD_DIM), table_moba)
        o_moba = merge_heads(o_moba)
        lambda_init = 0.8 - 0.6 * math.exp(-0.3 * l)
        lam = (jnp.exp(jnp.sum(diff_lambda_q1[l].astype(jnp.float32) * diff_lambda_k1[l].astype(jnp.float32)))
               - jnp.exp(jnp.sum(diff_lambda_q2[l].astype(jnp.float32) * diff_lambda_k2[l].astype(jnp.float32)))
               + lambda_init)
        o_diff = diff_attention(split_pair_heads(qb), split_pair_heads(kb),
                                split_heads(vb, DIFF_HEADS, 2 * DIFF_HEAD_DIM), table_diff, lam)
        o_diff = merge_heads(rms_norm(o_diff, diff_subln[l]) * (1.0 - lambda_init))
        merged = (jax.nn.sigmoid(ga) * (o_moba @ w_branch_moba[l])
                  + jax.nn.sigmoid(gb) * (o_diff @ w_branch_diff[l]))
        x = x + merged @ w_mix_out[l]
        x = x + cross_attention(rms_norm(x, xattn_norm[l]), rms_norm(mem, mem_norm[l]),
                                w_xq[l], w_xk[l], w_xv[l], w_xo[l])
        x = x + routed_ffn(rms_norm(x, ffn_norm[l]), w_router[l], b_router[l], w_gate_up[l],
                           b_gate_up[l], w_down[l], b_down[l])
    return rms_norm(x, final_norm)
```

```python
import functools
import math

import jax
import jax.numpy as jnp
from jax import lax
from jax.experimental import pallas as pl
from jax.experimental.pallas import tpu as pltpu

F32 = jnp.float32
BF16 = jnp.bfloat16
NEG_INF = float("-inf")

D_MODEL = 2048
HEAD_DIM = 128
MOBA_HEADS = 8
MOBA_WIDTH = MOBA_HEADS * HEAD_DIM
MOBA_BLOCK = 256
MOBA_TOPK = 3
DIFF_HEADS = 4
REL_BUCKETS = 32
REL_MAX_DISTANCE = 128
XATTN_HEADS = 4
N_EXPERTS = 32
TOP_K = 4
SWIGLU_LIMIT = 7.0
SWIGLU_ALPHA = 1.702
NORM_EPS = 1e-5
LAMBDA_INIT = 0.8 - 0.6 * math.exp(-0.3 * 0)
ATTN_SCALE = HEAD_DIM ** -0.5

ATTN_TILE = MOBA_BLOCK
LANES = 128
EXPERT_ROW_TILE = 256
EXPERT_UNIT_ROWS = 2048
EXPERT_COL_CHUNK = 512
VMEM_LIMIT = 56 * 1024 * 1024


def _params(n_axes):
    return pltpu.CompilerParams(dimension_semantics=("arbitrary",) * n_axes,
                                vmem_limit_bytes=VMEM_LIMIT)


def _rms(x, gain):
    return x * lax.rsqrt(jnp.mean(x * x, axis=-1, keepdims=True) + NORM_EPS) * gain


def _sigmoid(x):
    return 1.0 / (1.0 + jnp.exp(-x))


def _dot_nt(a, b):
    return lax.dot_general(a, b, (((1,), (1,)), ((), ())), preferred_element_type=F32)


def _rmsnorm_body(x_ref, g_ref, o_ref):
    o_ref[...] = _rms(x_ref[...], g_ref[...]).astype(o_ref.dtype)


def rmsnorm_rows(x, gain, tm=512):
    t, d = x.shape
    return pl.pallas_call(
        _rmsnorm_body,
        grid=(t // tm,),
        in_specs=[pl.BlockSpec((tm, d), lambda i: (i, 0)),
                  pl.BlockSpec((1, d), lambda i: (0, 0))],
        out_specs=pl.BlockSpec((tm, d), lambda i: (i, 0)),
        out_shape=jax.ShapeDtypeStruct((t, d), BF16),
        compiler_params=_params(1),
        name="rmsnorm_rows",
    )(x, gain.reshape(1, d))


def _matmul_body(a_ref, w_ref, o_ref):
    o_ref[...] = jnp.dot(a_ref[...], w_ref[...], preferred_element_type=F32).astype(o_ref.dtype)


def matmul(a, w, tm=1024, tn=1024, out_dtype=BF16):
    m, k = a.shape
    n = w.shape[1]
    tm, tn = min(tm, m), min(tn, n)
    return pl.pallas_call(
        _matmul_body,
        grid=(n // tn, m // tm),
        in_specs=[pl.BlockSpec((tm, k), lambda j, i: (i, 0)),
                  pl.BlockSpec((k, tn), lambda j, i: (0, j))],
        out_specs=pl.BlockSpec((tm, tn), lambda j, i: (i, j)),
        out_shape=jax.ShapeDtypeStruct((m, n), out_dtype),
        compiler_params=_params(2),
        name="matmul",
    )(a, w)


def _softmax_step(q, k_blk, v_blk, bias, mask, m, l, acc):
    s = _dot_nt(q, k_blk) * ATTN_SCALE + bias
    s = jnp.where(mask, s, NEG_INF)
    m_new = jnp.maximum(m, jnp.max(s, axis=1, keepdims=True))
    alpha = jnp.exp(m - m_new)
    p = jnp.exp(s - m_new)
    l = alpha * l + jnp.sum(p, axis=1, keepdims=True)
    acc = alpha * acc + jnp.dot(p.astype(BF16), v_blk, preferred_element_type=F32)
    return m_new, l, acc


def _rel_bucket(dist):
    n = jnp.maximum(dist, 0)
    max_exact = REL_BUCKETS // 2
    nf = jnp.maximum(n, max_exact).astype(F32)
    large = max_exact + (jnp.log(nf / max_exact) / math.log(REL_MAX_DISTANCE / max_exact)
                         * (REL_BUCKETS - max_exact)).astype(jnp.int32)
    return jnp.where(n < max_exact, n, jnp.minimum(large, REL_BUCKETS - 1))


def _bias_tiles(table):
    t = ATTN_TILE
    r = jnp.arange(t)[:, None]
    c = jnp.arange(t)[None, :]
    dist = jnp.stack([r - c, t + r - c])
    near = jnp.moveaxis(table[_rel_bucket(dist)], -1, 0)
    far = jnp.broadcast_to(table[REL_BUCKETS - 1][:, None, None], (table.shape[1], 1, t))
    return near.astype(F32), far.astype(F32)


def _moba_body(q_ref, k_ref, v_ref, near_ref, far_ref, o_ref, kmean_ref, *, n_blocks):
    qi = pl.program_id(2)
    t = ATTN_TILE

    @pl.when(qi == 0)
    def _():
        for n in range(n_blocks):
            kmean_ref[n:n + 1, :] = jnp.mean(k_ref[n * t:(n + 1) * t, :].astype(F32), axis=0, keepdims=True)

    q = q_ref[...]
    gate = lax.dot_general(q.astype(F32), kmean_ref[...], (((1,), (1,)), ((), ())),
                           precision=lax.Precision.HIGHEST, preferred_element_type=F32)
    blk = lax.broadcasted_iota(jnp.int32, gate.shape, 1)
    valid = blk < qi
    g = jnp.where(valid, gate, NEG_INF)
    kth = g
    for _ in range(MOBA_TOPK - 1):
        top = jnp.max(kth, axis=1, keepdims=True)
        kth = jnp.where(kth == top, NEG_INF, kth)
    third = jnp.max(kth, axis=1, keepdims=True)
    sel = jnp.where(valid & (g >= third), 1.0, 0.0)

    row = lax.broadcasted_iota(jnp.int32, (t, t), 0)
    col = lax.broadcasted_iota(jnp.int32, (t, t), 1)
    m0 = jnp.full((t, 1), NEG_INF, F32)
    l0 = jnp.zeros((t, 1), F32)
    acc0 = jnp.zeros((t, HEAD_DIM), F32)

    own = pl.ds(pl.multiple_of(qi * t, t), t)
    carry = _softmax_step(q, k_ref[own, :], v_ref[own, :], near_ref[0, 0], col <= row, m0, l0, acc0)

    def past(n, carry):
        rows = pl.ds(pl.multiple_of(n * t, t), t)
        bias = jnp.where(n == qi - 1, near_ref[0, 1], far_ref[0])
        chosen = jnp.max(jnp.where(blk == n, sel, 0.0), axis=1, keepdims=True) > 0.0
        return _softmax_step(q, k_ref[rows, :], v_ref[rows, :], bias, chosen, *carry)

    _, l, acc = lax.fori_loop(0, qi, past, carry)
    o_ref[...] = (acc / l).astype(o_ref.dtype)


def moba_attention(y, near, far, batch, seq):
    t = ATTN_TILE
    nq = seq // t
    h = MOBA_HEADS
    return pl.pallas_call(
        functools.partial(_moba_body, n_blocks=nq),
        grid=(batch, h, nq),
        in_specs=[pl.BlockSpec((t, HEAD_DIM), lambda b, hh, i: (b * nq + i, hh)),
                  pl.BlockSpec((seq, HEAD_DIM), lambda b, hh, i: (b, h + hh)),
                  pl.BlockSpec((seq, HEAD_DIM), lambda b, hh, i: (b, 2 * h + hh)),
                  pl.BlockSpec((1, 2, t, t), lambda b, hh, i: (hh, 0, 0, 0)),
                  pl.BlockSpec((1, 1, t), lambda b, hh, i: (hh, 0, 0))],
        out_specs=pl.BlockSpec((t, HEAD_DIM), lambda b, hh, i: (b * nq + i, hh)),
        out_shape=jax.ShapeDtypeStruct((batch * seq, MOBA_WIDTH), BF16),
        scratch_shapes=[pltpu.VMEM((nq, HEAD_DIM), F32)],
        compiler_params=_params(3),
        name="moba_attention",
    )(y, y, y, near, far)


def _diff_body(q_ref, k_ref, v_ref, near_ref, far_ref, lq1_ref, lk1_ref, lq2_ref, lk2_ref, subln_ref, o_ref):
    qi = pl.program_id(2)
    t = ATTN_TILE
    dh = HEAD_DIM
    q1 = q_ref[:, :dh]
    q2 = q_ref[:, dh:]
    row = lax.broadcasted_iota(jnp.int32, (t, t), 0)
    col = lax.broadcasted_iota(jnp.int32, (t, t), 1)
    m0 = jnp.full((t, 1), NEG_INF, F32)
    l0 = jnp.zeros((t, 1), F32)
    acc0 = jnp.zeros((t, 2 * dh), F32)

    own = pl.ds(pl.multiple_of(qi * t, t), t)
    causal = col <= row
    v_own = v_ref[own, :]
    c1 = _softmax_step(q1, k_ref[own, :dh], v_own, near_ref[0, 0], causal, m0, l0, acc0)
    c2 = _softmax_step(q2, k_ref[own, dh:], v_own, near_ref[0, 0], causal, m0, l0, acc0)

    def past(n, carry):
        c1, c2 = carry
        rows = pl.ds(pl.multiple_of(n * t, t), t)
        bias = jnp.where(n == qi - 1, near_ref[0, 1], far_ref[0])
        v_blk = v_ref[rows, :]
        c1 = _softmax_step(q1, k_ref[rows, :dh], v_blk, bias, True, *c1)
        c2 = _softmax_step(q2, k_ref[rows, dh:], v_blk, bias, True, *c2)
        return c1, c2

    (_, l1, acc1), (_, l2, acc2) = lax.fori_loop(0, qi, past, (c1, c2))
    lam = (jnp.exp(jnp.sum(lq1_ref[...] * lk1_ref[...], axis=1, keepdims=True))
           - jnp.exp(jnp.sum(lq2_ref[...] * lk2_ref[...], axis=1, keepdims=True)) + LAMBDA_INIT)
    o = acc1 / l1 - lam * (acc2 / l2)
    o_ref[...] = (_rms(o, subln_ref[...]) * (1.0 - LAMBDA_INIT)).astype(o_ref.dtype)


def diff_attention(y, near, far, lq1, lk1, lq2, lk2, subln, batch, seq, col0):
    t = ATTN_TILE
    nq = seq // t
    h = DIFF_HEADS
    w = 2 * HEAD_DIM
    base = col0 // w
    vec = lambda a: a.reshape(1, -1).astype(F32)
    small = lambda n: pl.BlockSpec((1, n), lambda b, hh, i: (0, 0))
    return pl.pallas_call(
        _diff_body,
        grid=(batch, h, nq),
        in_specs=[pl.BlockSpec((t, w), lambda b, hh, i: (b * nq + i, base + hh)),
                  pl.BlockSpec((seq, w), lambda b, hh, i: (b, base + h + hh)),
                  pl.BlockSpec((seq, w), lambda b, hh, i: (b, base + 2 * h + hh)),
                  pl.BlockSpec((1, 2, t, t), lambda b, hh, i: (hh, 0, 0, 0)),
                  pl.BlockSpec((1, 1, t), lambda b, hh, i: (hh, 0, 0)),
                  small(HEAD_DIM), small(HEAD_DIM), small(HEAD_DIM), small(HEAD_DIM), small(w)],
        out_specs=pl.BlockSpec((t, w), lambda b, hh, i: (b * nq + i, hh)),
        out_shape=jax.ShapeDtypeStruct((batch * seq, h * w), BF16),
        compiler_params=_params(3),
        name="diff_attention",
    )(y, y, y, near, far, vec(lq1), vec(lk1), vec(lq2), vec(lk2), vec(subln))


def _branch_body(om_ref, od_ref, wm_ref, wd_ref, ga_ref, gb_ref, o_ref):
    a = jnp.dot(om_ref[...], wm_ref[...], preferred_element_type=F32)
    b = jnp.dot(od_ref[...], wd_ref[...], preferred_element_type=F32)
    o_ref[...] = (_sigmoid(ga_ref[...].astype(F32)) * a + _sigmoid(gb_ref[...].astype(F32)) * b).astype(o_ref.dtype)


def branch_merge(o_moba, o_diff, w_m, w_d, y, gate_col0, tm=1024, tn=1024):
    m, k = o_moba.shape
    n = w_m.shape[1]
    g0 = gate_col0 // tn
    nj = n // tn
    return pl.pallas_call(
        _branch_body,
        grid=(nj, m // tm),
        in_specs=[pl.BlockSpec((tm, k), lambda j, i: (i, 0)),
                  pl.BlockSpec((tm, k), lambda j, i: (i, 0)),
                  pl.BlockSpec((k, tn), lambda j, i: (0, j)),
                  pl.BlockSpec((k, tn), lambda j, i: (0, j)),
                  pl.BlockSpec((tm, tn), lambda j, i: (i, g0 + j)),
                  pl.BlockSpec((tm, tn), lambda j, i: (i, g0 + nj + j))],
        out_specs=pl.BlockSpec((tm, tn), lambda j, i: (i, j)),
        out_shape=jax.ShapeDtypeStruct((m, n), BF16),
        compiler_params=_params(2),
        name="branch_merge",
    )(o_moba, o_diff, w_m, w_d, y, y)


def _mixout_body(a_ref, w_ref, x_ref, g_ref, x1_ref, h_ref):
    x1 = x_ref[...] + jnp.dot(a_ref[...], w_ref[...], preferred_element_type=F32)
    x1_ref[...] = x1
    h_ref[...] = _rms(x1, g_ref[...]).astype(h_ref.dtype)


def mixout(merged, w, x, gain, tm=512):
    m, k = merged.shape
    n = w.shape[1]
    return pl.pallas_call(
        _mixout_body,
        grid=(m // tm,),
        in_specs=[pl.BlockSpec((tm, k), lambda i: (i, 0)),
                  pl.BlockSpec((k, n), lambda i: (0, 0)),
                  pl.BlockSpec((tm, n), lambda i: (i, 0)),
                  pl.BlockSpec((1, n), lambda i: (0, 0))],
        out_specs=[pl.BlockSpec((tm, n), lambda i: (i, 0)),
                   pl.BlockSpec((tm, n), lambda i: (i, 0))],
        out_shape=[jax.ShapeDtypeStruct((m, n), F32), jax.ShapeDtypeStruct((m, n), BF16)],
        compiler_params=_params(1),
        name="mixout",
    )(merged, w, x, gain.reshape(1, n))


def _memkv_body(mem_ref, g_ref, wk_ref, wv_ref, k_ref, v_ref):
    mn = _rms(mem_ref[...], g_ref[...]).astype(BF16)
    k_ref[...] = jnp.dot(mn, wk_ref[...], preferred_element_type=F32).astype(k_ref.dtype)
    v_ref[...] = jnp.dot(mn, wv_ref[...], preferred_element_type=F32).astype(v_ref.dtype)


def memory_kv(mem2d, gain, w_k, w_v, rows):
    m, d = mem2d.shape
    n = w_k.shape[1]
    return pl.pallas_call(
        _memkv_body,
        grid=(m // rows,),
        in_specs=[pl.BlockSpec((rows, d), lambda i: (i, 0)),
                  pl.BlockSpec((1, d), lambda i: (0, 0)),
                  pl.BlockSpec((d, n), lambda i: (0, 0)),
                  pl.BlockSpec((d, n), lambda i: (0, 0))],
        out_specs=[pl.BlockSpec((rows, n), lambda i: (i, 0)),
                   pl.BlockSpec((rows, n), lambda i: (i, 0))],
        out_shape=[jax.ShapeDtypeStruct((m, n), BF16), jax.ShapeDtypeStruct((m, n), BF16)],
        compiler_params=_params(1),
        name="memory_kv",
    )(mem2d, gain.reshape(1, d), w_k, w_v)


def _xattn_body(h_ref, x1_ref, k_ref, v_ref, wq_ref, wo_ref, g_ref, wr_ref, br_ref,
                x2_ref, h2_ref, idx_ref, wgt_ref):
    q = jnp.dot(h_ref[...], wq_ref[...], preferred_element_type=F32).astype(BF16)
    outs = []
    for hh in range(XATTN_HEADS):
        sl = slice(hh * HEAD_DIM, (hh + 1) * HEAD_DIM)
        s = _dot_nt(q[:, sl], k_ref[:, sl]) * ATTN_SCALE
        p = jnp.exp(s - jnp.max(s, axis=1, keepdims=True))
        o = jnp.dot(p.astype(BF16), v_ref[:, sl], preferred_element_type=F32)
        outs.append((o / jnp.sum(p, axis=1, keepdims=True)).astype(BF16))
    o = jnp.concatenate(outs, axis=1)
    x2 = x1_ref[...] + jnp.dot(o, wo_ref[...], preferred_element_type=F32)
    x2_ref[...] = x2
    h2 = _rms(x2, g_ref[...])
    h2_ref[...] = h2.astype(h2_ref.dtype)

    logits = jnp.dot(h2, wr_ref[...], precision=lax.Precision.HIGHEST,
                     preferred_element_type=F32) + br_ref[...]
    lane = lax.broadcasted_iota(jnp.int32, logits.shape, 1)
    out_lane = lax.broadcasted_iota(jnp.int32, idx_ref.shape, 1)
    idx_out = jnp.zeros(idx_ref.shape, jnp.int32)
    exp_out = jnp.zeros(wgt_ref.shape, F32)
    denom = jnp.zeros((logits.shape[0], 1), F32)
    top0 = None
    for kk in range(TOP_K):
        top = jnp.max(logits, axis=1, keepdims=True)
        arg = jnp.min(jnp.where(logits == top, lane, N_EXPERTS), axis=1, keepdims=True)
        logits = jnp.where(lane == arg, NEG_INF, logits)
        top0 = top if top0 is None else top0
        e = jnp.exp(top - top0)
        denom = denom + e
        idx_out = jnp.where(out_lane == kk, arg, idx_out)
        exp_out = jnp.where(out_lane == kk, e, exp_out)
    idx_ref[...] = idx_out
    wgt_ref[...] = exp_out / denom


def cross_attention_router(hx, x1, k_mem, v_mem, w_q, w_o, gain, w_router, b_router, seq, tm=512):
    m, d = hx.shape
    mem_len = k_mem.shape[0] // (m // seq)
    n = w_q.shape[1]
    per_b = seq // tm
    const = lambda shape: pl.BlockSpec(shape, lambda i: (0,) * len(shape))
    rows = lambda cols: pl.BlockSpec((tm, cols), lambda i: (i, 0))
    return pl.pallas_call(
        _xattn_body,
        grid=(m // tm,),
        in_specs=[rows(d), rows(d),
                  pl.BlockSpec((mem_len, n), lambda i: (i // per_b, 0)),
                  pl.BlockSpec((mem_len, n), lambda i: (i // per_b, 0)),
                  const((d, n)), const((n, d)), const((1, d)), const((d, N_EXPERTS)), const((1, N_EXPERTS))],
        out_specs=[rows(d), rows(d), rows(LANES), rows(LANES)],
        out_shape=[jax.ShapeDtypeStruct((m, d), F32), jax.ShapeDtypeStruct((m, d), BF16),
                   jax.ShapeDtypeStruct((m, LANES), jnp.int32), jax.ShapeDtypeStruct((m, LANES), F32)],
        compiler_params=_params(1),
        name="cross_attention_router",
    )(hx, x1, k_mem, v_mem, w_q, w_o, gain.reshape(1, d), w_router.astype(F32), b_router.reshape(1, -1).astype(F32))


def _expert_body(ue_ref, us_ref, un_ref, nu_ref,
                 x_hbm, wgu_ref, bgu_ref, wd_ref, bd_ref, y_hbm,
                 xbuf, ybuf, wgu_bf, wd_f32, wd_bf, sem, *, n_chunks):
    u = pl.program_id(0)
    c = pl.program_id(1)
    rt = EXPERT_ROW_TILE
    half = EXPERT_COL_CHUNK // 2
    quarter = half // 2
    live = u < nu_ref[0]
    start = us_ref[u]
    n_tiles = un_ref[u]

    def rows_copy(j, to_vmem):
        hbm_rows = pl.ds(pl.multiple_of(start + j * rt, rt), rt)
        buf_rows = pl.ds(pl.multiple_of(j * rt, rt), rt)
        if to_vmem:
            return pltpu.make_async_copy(x_hbm.at[hbm_rows, :], xbuf.at[buf_rows, :], sem.at[0])
        return pltpu.make_async_copy(ybuf.at[buf_rows, :], y_hbm.at[hbm_rows, :], sem.at[1])

    def for_tiles(fn):
        def body(j, carry):
            fn(j)
            return carry
        lax.fori_loop(0, n_tiles, body, 0)

    @pl.when(live & (c == 0))
    def _():
        for_tiles(lambda j: rows_copy(j, True).start())
        for_tiles(lambda j: rows_copy(j, True).wait())

    @pl.when(live)
    def _():
        wgu_bf[...] = wgu_ref[0].astype(BF16)
        for g in range(wd_f32.shape[0]):
            cols = slice(g * LANES, (g + 1) * LANES)
            wd_f32[g, pl.ds(0, quarter, stride=2), :] = wd_ref[0, :quarter, cols]
            wd_f32[g, pl.ds(1, quarter, stride=2), :] = wd_ref[0, quarter:, cols]
            wd_bf[:, cols] = wd_f32[g].astype(BF16)
        bias = bgu_ref[0]
        even = (lax.broadcasted_iota(jnp.int32, (rt, half), 1) % 2) == 0

        def tile(j):
            rows = pl.ds(pl.multiple_of(j * rt, rt), rt)
            gu = jnp.dot(xbuf[rows, :], wgu_bf[...], preferred_element_type=F32) + bias
            lo = gu[:, :half]
            hi = gu[:, half:]
            gate = jnp.where(even, lo, pltpu.roll(hi, 1, axis=1))
            up = jnp.where(even, pltpu.roll(lo, half - 1, axis=1), hi)
            gate = jnp.minimum(gate, SWIGLU_LIMIT)
            up = jnp.clip(up, -SWIGLU_LIMIT, SWIGLU_LIMIT)
            act = (up + 1.0) * gate * _sigmoid(SWIGLU_ALPHA * gate)
            y = jnp.dot(act.astype(BF16), wd_bf[...], preferred_element_type=F32)

            @pl.when(c == 0)
            def _():
                ybuf[rows, :] = y + bd_ref[0]

            @pl.when(c > 0)
            def _():
                ybuf[rows, :] += y

        for_tiles(tile)

    @pl.when(live & (c == n_chunks - 1))
    def _():
        for_tiles(lambda j: rows_copy(j, False).start())
        for_tiles(lambda j: rows_copy(j, False).wait())

    @pl.when((u == pl.num_programs(0) - 1) & (c == n_chunks - 1))
    def _():
        ybuf[:rt, :] = jnp.zeros((rt, ybuf.shape[1]), F32)
        used_tiles = nu_ref[1]

        def pad_copy(j):
            return pltpu.make_async_copy(ybuf.at[:rt, :], y_hbm.at[pl.ds(pl.multiple_of(j * rt, rt), rt), :], sem.at[1])

        def start(j, carry):
            pad_copy(j).start()
            return carry

        def wait(j, carry):
            pad_copy(j).wait()
            return carry

        lax.fori_loop(used_tiles, y_hbm.shape[0] // rt, start, 0)
        lax.fori_loop(used_tiles, y_hbm.shape[0] // rt, wait, 0)


def expert_ffn(x_sorted, w_gate_up, b_gate_up, w_down, b_down, unit_expert, unit_start, unit_tiles, n_units):
    p_rows, d = x_sorted.shape
    n_exp, _, two_ff = w_gate_up.shape
    d_ff = two_ff // 2
    chunk = EXPERT_COL_CHUNK
    n_chunks = two_ff // chunk
    max_units = unit_expert.shape[0]

    def cidx(u, c, nu):
        return jnp.where(u < nu[0], c, n_chunks - 1)

    grid_spec = pltpu.PrefetchScalarGridSpec(
        num_scalar_prefetch=4,
        grid=(max_units, n_chunks),
        in_specs=[pl.BlockSpec(memory_space=pl.ANY),
                  pl.BlockSpec((1, d, chunk), lambda u, c, ue, us, un, nu: (ue[u], 0, cidx(u, c, nu))),
                  pl.BlockSpec((1, 1, chunk), lambda u, c, ue, us, un, nu: (ue[u], 0, cidx(u, c, nu))),
                  pl.BlockSpec((1, chunk // 2, d), lambda u, c, ue, us, un, nu: (ue[u], cidx(u, c, nu), 0)),
                  pl.BlockSpec((1, 1, d), lambda u, c, ue, us, un, nu: (ue[u], 0, 0))],
        out_specs=pl.BlockSpec(memory_space=pl.ANY),
        scratch_shapes=[pltpu.VMEM((EXPERT_UNIT_ROWS, d), BF16),
                        pltpu.VMEM((EXPERT_UNIT_ROWS, d), F32),
                        pltpu.VMEM((d, chunk), BF16),
                        pltpu.VMEM((d // LANES, chunk // 2, LANES), F32),
                        pltpu.VMEM((chunk // 2, d), BF16),
                        pltpu.SemaphoreType.DMA((2,))],
    )
    return pl.pallas_call(
        functools.partial(_expert_body, n_chunks=n_chunks),
        grid_spec=grid_spec,
        out_shape=jax.ShapeDtypeStruct((p_rows, d), F32),
        compiler_params=_params(2),
        name="expert_ffn",
    )(unit_expert, unit_start, unit_tiles, n_units,
      x_sorted, w_gate_up, b_gate_up.reshape(n_exp, 1, two_ff), w_down, b_down.reshape(n_exp, 1, d))


def _routing_plan(top_idx, n_tokens):
    rt = EXPERT_ROW_TILE
    tiles_per_unit = EXPERT_UNIT_ROWS // rt
    onehot = (top_idx[:, :, None] == jnp.arange(N_EXPERTS)[None, None, :]).astype(jnp.int32).sum(axis=1)
    before = jnp.cumsum(onehot, axis=0) - onehot
    count = onehot.sum(axis=0)
    tiles = (count + rt - 1) // rt
    tile_start = jnp.cumsum(tiles) - tiles
    pos = (tile_start * rt)[top_idx] + jnp.take_along_axis(before, top_idx, axis=1)

    units = (tiles + tiles_per_unit - 1) // tiles_per_unit
    unit_first = jnp.cumsum(units) - units
    n_units = units.sum()
    max_units = N_EXPERTS + (n_tokens * TOP_K) // EXPERT_UNIT_ROWS
    uid = jnp.arange(max_units)
    e_of = jnp.clip(jnp.searchsorted(jnp.cumsum(units), uid, side="right"), 0, N_EXPERTS - 1)
    k_in = uid - unit_first[e_of]
    live = uid < n_units
    last_e = e_of[jnp.maximum(n_units - 1, 0)]
    unit_expert = jnp.where(live, e_of, last_e).astype(jnp.int32)
    unit_start = jnp.where(live, (tile_start[e_of] + k_in * tiles_per_unit) * rt, 0).astype(jnp.int32)
    unit_tiles = jnp.where(live, jnp.minimum(tiles[e_of] - k_in * tiles_per_unit, tiles_per_unit), 0).astype(jnp.int32)
    totals = jnp.stack([n_units, tiles.sum()]).astype(jnp.int32)
    return pos.astype(jnp.int32), unit_expert, unit_start, unit_tiles, totals


def _combine_body(x2_ref, yg_ref, w_ref, g_ref, o_ref, *, final_norm):
    moe = jnp.zeros(x2_ref.shape, F32)
    for kk in range(TOP_K):
        moe = moe + w_ref[:, kk:kk + 1] * yg_ref[kk]
    x3 = x2_ref[...] + moe
    o_ref[...] = _rms(x3, g_ref[...]) if final_norm else x3


def combine(x2, y_gathered, weights, gain, tm=256):
    m, d = x2.shape
    final_norm = gain is not None
    gain = gain if final_norm else jnp.ones((d,), F32)
    return pl.pallas_call(
        functools.partial(_combine_body, final_norm=final_norm),
        grid=(m // tm,),
        in_specs=[pl.BlockSpec((tm, d), lambda i: (i, 0)),
                  pl.BlockSpec((TOP_K, tm, d), lambda i: (0, i, 0)),
                  pl.BlockSpec((tm, LANES), lambda i: (i, 0)),
                  pl.BlockSpec((1, d), lambda i: (0, 0))],
        out_specs=pl.BlockSpec((tm, d), lambda i: (i, 0)),
        out_shape=jax.ShapeDtypeStruct((m, d), F32),
        compiler_params=_params(1),
        name="combine_final",
    )(x2, y_gathered, weights, gain.reshape(1, d))


def kernel(x, mem, rel_bias_table, mix_norm, w_in, diff_lambda_q1, diff_lambda_k1, diff_lambda_q2, diff_lambda_k2, diff_subln, w_branch_moba, w_branch_diff, w_mix_out, xattn_norm, mem_norm, w_xq, w_xk, w_xv, w_xo, ffn_norm, w_router, b_router, w_gate_up, b_gate_up, w_down, b_down, final_norm):
    batch, seq, d = x.shape
    n_tok = batch * seq
    x2d = x.reshape(n_tok, d)
    near, far = _bias_tiles(rel_bias_table)
    for l in range(w_in.shape[0]):
        h = rmsnorm_rows(x2d, mix_norm[l])
        y = matmul(h, w_in[l].astype(BF16))
        o_moba = moba_attention(y, near[:MOBA_HEADS], far[:MOBA_HEADS], batch, seq)
        o_diff = diff_attention(y, near[MOBA_HEADS:], far[MOBA_HEADS:], diff_lambda_q1[l], diff_lambda_k1[l],
                                diff_lambda_q2[l], diff_lambda_k2[l], diff_subln[l], batch, seq, 3 * MOBA_WIDTH)
        merged = branch_merge(o_moba, o_diff, w_branch_moba[l].astype(BF16), w_branch_diff[l].astype(BF16),
                              y, 3 * MOBA_WIDTH + 3 * DIFF_HEADS * 2 * HEAD_DIM)
        x1, hx = mixout(merged, w_mix_out[l].astype(BF16), x2d, xattn_norm[l])
        k_mem, v_mem = memory_kv(mem.reshape(-1, d), mem_norm[l], w_xk[l].astype(BF16), w_xv[l].astype(BF16),
                                 mem.shape[1])
        x2, h2, idx_pad, wgt_pad = cross_attention_router(hx, x1, k_mem, v_mem, w_xq[l].astype(BF16),
                                                          w_xo[l].astype(BF16), ffn_norm[l], w_router[l],
                                                          b_router[l], seq)
        top_idx = idx_pad[:, :TOP_K]
        pos, unit_expert, unit_start, unit_tiles, n_units = _routing_plan(top_idx, n_tok)
        p_rows = n_tok * TOP_K + N_EXPERTS * EXPERT_ROW_TILE
        row_token = jnp.zeros((p_rows,), jnp.int32).at[pos.reshape(-1)].set(
            jnp.repeat(jnp.arange(n_tok, dtype=jnp.int32), TOP_K))
        x_sorted = h2[row_token]
        y_sorted = expert_ffn(x_sorted, w_gate_up[l], b_gate_up[l], w_down[l], b_down[l],
                              unit_expert, unit_start, unit_tiles, n_units)
        y_gathered = jnp.moveaxis(y_sorted[pos], 1, 0)
        last = l == w_in.shape[0] - 1
        x2d = combine(x2, y_gathered, wgt_pad, final_norm if last else None)
    return x2d.reshape(batch, seq, d)
```

```python
import functools
import math

import jax
import jax.numpy as jnp
from jax import lax
from jax.experimental import pallas as pl
from jax.experimental.pallas import tpu as pltpu

F32 = jnp.float32
BF16 = jnp.bfloat16
NEG_INF = float("-inf")

D_MODEL = 2048
HEAD_DIM = 128
MOBA_HEADS = 8
MOBA_WIDTH = MOBA_HEADS * HEAD_DIM
MOBA_BLOCK = 256
MOBA_TOPK = 3
DIFF_HEADS = 4
DIFF_WIDTH = DIFF_HEADS * 2 * HEAD_DIM
REL_BUCKETS = 32
REL_MAX_DISTANCE = 128
XATTN_HEADS = 4
N_EXPERTS = 32
TOP_K = 4
SWIGLU_LIMIT = 7.0
SWIGLU_ALPHA = 1.702
NORM_EPS = 1e-5
LAMBDA_INIT = 0.8 - 0.6 * math.exp(-0.3 * 0)
ATTN_SCALE = HEAD_DIM ** -0.5
LOG2E = math.log2(math.e)

ATTN_TILE = MOBA_BLOCK
LANES = 128
EXPERT_ROW_TILE = 256
EXPERT_UNIT_ROWS = 2048
EXPERT_COL_CHUNK = 512
GATHER_TOKENS = 256
VMEM_LIMIT = 56 * 1024 * 1024


def _params(n_axes):
    return pltpu.CompilerParams(dimension_semantics=("arbitrary",) * n_axes,
                                vmem_limit_bytes=VMEM_LIMIT)


def _rms(x, gain):
    return x * lax.rsqrt(jnp.mean(x * x, axis=-1, keepdims=True) + NORM_EPS) * gain


def _sigmoid(x):
    return 1.0 / (1.0 + jnp.exp(-x))


def _dot_nt(a, b):
    return lax.dot_general(a, b, (((1,), (1,)), ((), ())), preferred_element_type=F32)


def _rmsnorm_body(x_ref, g_ref, o_ref):
    o_ref[...] = _rms(x_ref[...], g_ref[...]).astype(o_ref.dtype)


def rmsnorm_rows(x, gain, tm=512):
    t, d = x.shape
    return pl.pallas_call(
        _rmsnorm_body,
        grid=(t // tm,),
        in_specs=[pl.BlockSpec((tm, d), lambda i: (i, 0)),
                  pl.BlockSpec((1, d), lambda i: (0, 0))],
        out_specs=pl.BlockSpec((tm, d), lambda i: (i, 0)),
        out_shape=jax.ShapeDtypeStruct((t, d), BF16),
        compiler_params=_params(1),
        name="rmsnorm_rows",
    )(x, gain.reshape(1, d))


def _matmul_body(a_ref, w_ref, cs_ref, o_ref):
    acc = jnp.dot(a_ref[...], w_ref[...], preferred_element_type=F32)
    o_ref[...] = (acc * cs_ref[...]).astype(o_ref.dtype)


def matmul_colscale(a, w, col_scale, tm=1024, tn=1024, out_dtype=BF16):
    m, k = a.shape
    n = w.shape[1]
    tm, tn = min(tm, m), min(tn, n)
    return pl.pallas_call(
        _matmul_body,
        grid=(n // tn, m // tm),
        in_specs=[pl.BlockSpec((tm, k), lambda j, i: (i, 0)),
                  pl.BlockSpec((k, tn), lambda j, i: (0, j)),
                  pl.BlockSpec((1, tn), lambda j, i: (0, j))],
        out_specs=pl.BlockSpec((tm, tn), lambda j, i: (i, j)),
        out_shape=jax.ShapeDtypeStruct((m, n), out_dtype),
        compiler_params=_params(2),
        name="matmul",
    )(a, w, col_scale.reshape(1, n).astype(F32))


def _rel_bucket(dist):
    n = jnp.maximum(dist, 0)
    max_exact = REL_BUCKETS // 2
    nf = jnp.maximum(n, max_exact).astype(F32)
    large = max_exact + (jnp.log(nf / max_exact) / math.log(REL_MAX_DISTANCE / max_exact)
                         * (REL_BUCKETS - max_exact)).astype(jnp.int32)
    return jnp.where(n < max_exact, n, jnp.minimum(large, REL_BUCKETS - 1))


def _bias_tiles(table):
    t = ATTN_TILE
    r = jnp.arange(t)[:, None]
    c = jnp.arange(t)[None, :]
    dist = jnp.stack([r - c, t + r - c])
    near = jnp.moveaxis(table[_rel_bucket(dist)], -1, 0)
    far = jnp.broadcast_to(table[REL_BUCKETS - 1][:, None, None], (table.shape[1], 1, t))
    return near.astype(F32) * LOG2E, far.astype(F32) * LOG2E


def _lane_halves(x, op):
    return op(x[:, :LANES], x[:, LANES:])


def _causal_attention(q, k_rows, v_rows, near_ref, cfar, pen_at, qi, s_scr):
    t = ATTN_TILE
    own_slot = s_scr.shape[0] - 1
    spare_slot = own_slot - 1
    prev = jnp.maximum(qi - 1, 0)
    n_far = prev
    odd = n_far % 2 == 1
    tail = jnp.where(odd, n_far - 1, 0)
    tail_slot = jnp.where(odd, n_far - 1, spare_slot)
    row = lax.broadcasted_iota(jnp.int32, (t, t), 0)
    col = lax.broadcasted_iota(jnp.int32, (t, t), 1)

    def masked(s, n):
        return s if pen_at is None else s + pen_at(n)

    def tile_max(m, s):
        return jnp.maximum(m, _lane_halves(s, jnp.maximum))

    def tile_sum(l, p):
        return l + _lane_halves(p, jnp.add)

    s3 = _dot_nt(q, jnp.concatenate([k_rows(qi, 1), k_rows(prev, 1), k_rows(tail, 1)], axis=0))
    s_own = jnp.where(col <= row, s3[:, :t] + near_ref[0, 0], NEG_INF)
    s_prev = masked(s3[:, t:2 * t] + near_ref[0, 1] + jnp.where(qi >= 1, 0.0, NEG_INF), prev)
    s_tail = masked(s3[:, 2 * t:] + jnp.where(odd, 0.0, NEG_INF), tail)
    s_scr[own_slot] = s_own
    s_scr[prev] = s_prev - cfar
    s_scr[tail_slot] = s_tail
    m_near = tile_max(_lane_halves(s_own, jnp.maximum), s_prev)

    def pair_scores(i, m_far):
        s = _dot_nt(q, k_rows(2 * i, 2))
        for half in range(2):
            sh = masked(s[:, half * t:(half + 1) * t], 2 * i + half)
            s_scr[2 * i + half] = sh
            m_far = tile_max(m_far, sh)
        return m_far

    m_far = lax.fori_loop(0, n_far // 2, pair_scores, _lane_halves(s_tail, jnp.maximum))
    m_row = jnp.maximum(jnp.max(m_near, axis=1, keepdims=True),
                        jnp.max(m_far, axis=1, keepdims=True) + cfar)
    m_past = m_row - cfar

    p_own = jnp.exp2(s_scr[own_slot] - m_row)
    p_prev = jnp.exp2(s_scr[prev] - m_past)
    p_tail = jnp.exp2(s_scr[tail_slot] - m_past)
    l_part = tile_sum(tile_sum(_lane_halves(p_own, jnp.add), p_prev), p_tail)
    acc = jnp.dot(jnp.concatenate([p_own, p_prev, p_tail], axis=1).astype(BF16),
                  jnp.concatenate([v_rows(qi, 1), v_rows(prev, 1), v_rows(tail, 1)], axis=0),
                  preferred_element_type=F32)

    def pair_weights(i, carry):
        l_part, acc = carry
        p = jnp.exp2(jnp.concatenate([s_scr[2 * i], s_scr[2 * i + 1]], axis=1) - m_past)
        l_part = tile_sum(tile_sum(l_part, p[:, :t]), p[:, t:])
        return l_part, acc + jnp.dot(p.astype(BF16), v_rows(2 * i, 2), preferred_element_type=F32)

    l_part, acc = lax.fori_loop(0, n_far // 2, pair_weights, (l_part, acc))
    return acc, jnp.sum(l_part, axis=1, keepdims=True)


def _block_rows(n, w=1):
    return pl.ds(pl.multiple_of(n * ATTN_TILE, ATTN_TILE), w * ATTN_TILE)


def _moba_body(q_ref, k_ref, v_ref, near_ref, far_ref, o_ref, kmean_ref, s_scr, *, n_blocks):
    qi = pl.program_id(2)
    t = ATTN_TILE

    @pl.when(qi == 0)
    def _():
        for n in range(n_blocks):
            kmean_ref[n:n + 1, :] = jnp.mean(k_ref[n * t:(n + 1) * t, :].astype(F32), axis=0, keepdims=True)

    q = q_ref[...]
    gate = lax.dot_general(q.astype(F32), kmean_ref[...], (((1,), (1,)), ((), ())),
                           precision=lax.Precision.HIGHEST, preferred_element_type=F32)
    blk = lax.broadcasted_iota(jnp.int32, gate.shape, 1)
    valid = blk < qi
    g = jnp.where(valid, gate, NEG_INF)
    kth = g
    for _ in range(MOBA_TOPK - 1):
        top = jnp.max(kth, axis=1, keepdims=True)
        kth = jnp.where(kth == top, NEG_INF, kth)
    third = jnp.max(kth, axis=1, keepdims=True)
    pen = jnp.where(valid & (g >= third), 0.0, NEG_INF)

    def pen_at(n):
        return jnp.max(jnp.where(blk == n, pen, NEG_INF), axis=1, keepdims=True)

    acc, l = _causal_attention(q, lambda n, w: k_ref[_block_rows(n, w), :], lambda n, w: v_ref[_block_rows(n, w), :],
                               near_ref, far_ref[0][:, :1], pen_at, qi, s_scr)
    o_ref[...] = (acc / l).astype(o_ref.dtype)


def moba_attention(y, near, far, batch, seq):
    t = ATTN_TILE
    nq = seq // t
    h = MOBA_HEADS
    return pl.pallas_call(
        functools.partial(_moba_body, n_blocks=nq),
        grid=(batch, h, nq),
        in_specs=[pl.BlockSpec((t, HEAD_DIM), lambda b, hh, i: (b * nq + i, hh)),
                  pl.BlockSpec((seq, HEAD_DIM), lambda b, hh, i: (b, h + hh)),
                  pl.BlockSpec((seq, HEAD_DIM), lambda b, hh, i: (b, 2 * h + hh)),
                  pl.BlockSpec((1, 2, t, t), lambda b, hh, i: (hh, 0, 0, 0)),
                  pl.BlockSpec((1, 1, t), lambda b, hh, i: (hh, 0, 0))],
        out_specs=pl.BlockSpec((t, HEAD_DIM), lambda b, hh, i: (b * nq + i, hh)),
        out_shape=jax.ShapeDtypeStruct((batch * seq, MOBA_WIDTH), BF16),
        scratch_shapes=[pltpu.VMEM((nq, HEAD_DIM), F32),
                        pltpu.VMEM((nq + 1, t, t), F32)],
        compiler_params=_params(3),
        name="moba_attention",
    )(y, y, y, near, far)


def _diff_body(q_ref, k_ref, v_ref, near_ref, far_ref, lq1_ref, lk1_ref, lq2_ref, lk2_ref, subln_ref, o_ref, s_scr):
    qi = pl.program_id(2)
    dh = HEAD_DIM
    cfar = far_ref[0][:, :1]
    v_rows = lambda n, w: v_ref[_block_rows(n, w), :]
    acc1, l1 = _causal_attention(q_ref[:, :dh], lambda n, w: k_ref[_block_rows(n, w), :dh], v_rows,
                                 near_ref, cfar, None, qi, s_scr)
    acc2, l2 = _causal_attention(q_ref[:, dh:], lambda n, w: k_ref[_block_rows(n, w), dh:], v_rows,
                                 near_ref, cfar, None, qi, s_scr)
    lam = (jnp.exp(jnp.sum(lq1_ref[...] * lk1_ref[...], axis=1, keepdims=True))
           - jnp.exp(jnp.sum(lq2_ref[...] * lk2_ref[...], axis=1, keepdims=True)) + LAMBDA_INIT)
    o = acc1 / l1 - lam * (acc2 / l2)
    o_ref[...] = (_rms(o, subln_ref[...]) * (1.0 - LAMBDA_INIT)).astype(o_ref.dtype)


def diff_attention(y, near, far, lq1, lk1, lq2, lk2, subln, batch, seq, col0):
    t = ATTN_TILE
    nq = seq // t
    h = DIFF_HEADS
    w = 2 * HEAD_DIM
    base = col0 // w
    vec = lambda a: a.reshape(1, -1).astype(F32)
    small = lambda n: pl.BlockSpec((1, n), lambda b, hh, i: (0, 0))
    return pl.pallas_call(
        _diff_body,
        grid=(batch, h, nq),
        in_specs=[pl.BlockSpec((t, w), lambda b, hh, i: (b * nq + i, base + hh)),
                  pl.BlockSpec((seq, w), lambda b, hh, i: (b, base + h + hh)),
                  pl.BlockSpec((seq, w), lambda b, hh, i: (b, base + 2 * h + hh)),
                  pl.BlockSpec((1, 2, t, t), lambda b, hh, i: (hh, 0, 0, 0)),
                  pl.BlockSpec((1, 1, t), lambda b, hh, i: (hh, 0, 0)),
                  small(HEAD_DIM), small(HEAD_DIM), small(HEAD_DIM), small(HEAD_DIM), small(w)],
        out_specs=pl.BlockSpec((t, w), lambda b, hh, i: (b * nq + i, hh)),
        out_shape=jax.ShapeDtypeStruct((batch * seq, h * w), BF16),
        scratch_shapes=[pltpu.VMEM((nq + 1, t, t), F32)],
        compiler_params=_params(3),
        name="diff_attention",
    )(y, y, y, near, far, vec(lq1), vec(lk1), vec(lq2), vec(lk2), vec(subln))


def _branch_body(om_ref, od_ref, wm_ref, wd_ref, ga_ref, gb_ref, o_ref):
    a = jnp.dot(om_ref[...], wm_ref[...], preferred_element_type=F32)
    b = jnp.dot(od_ref[...], wd_ref[...], preferred_element_type=F32)
    o_ref[...] = (_sigmoid(ga_ref[...].astype(F32)) * a + _sigmoid(gb_ref[...].astype(F32)) * b).astype(o_ref.dtype)


def branch_merge(o_moba, o_diff, w_m, w_d, y, gate_col0, tm=1024, tn=1024):
    m, k = o_moba.shape
    n = w_m.shape[1]
    g0 = gate_col0 // tn
    nj = n // tn
    return pl.pallas_call(
        _branch_body,
        grid=(nj, m // tm),
        in_specs=[pl.BlockSpec((tm, k), lambda j, i: (i, 0)),
                  pl.BlockSpec((tm, k), lambda j, i: (i, 0)),
                  pl.BlockSpec((k, tn), lambda j, i: (0, j)),
                  pl.BlockSpec((k, tn), lambda j, i: (0, j)),
                  pl.BlockSpec((tm, tn), lambda j, i: (i, g0 + j)),
                  pl.BlockSpec((tm, tn), lambda j, i: (i, g0 + nj + j))],
        out_specs=pl.BlockSpec((tm, tn), lambda j, i: (i, j)),
        out_shape=jax.ShapeDtypeStruct((m, n), BF16),
        compiler_params=_params(2),
        name="branch_merge",
    )(o_moba, o_diff, w_m, w_d, y, y)


def _mixout_body(a_ref, w_ref, x_ref, g_ref, x1_ref, h_ref):
    x1 = x_ref[...] + jnp.dot(a_ref[...], w_ref[...], preferred_element_type=F32)
    x1_ref[...] = x1
    h_ref[...] = _rms(x1, g_ref[...]).astype(h_ref.dtype)


def mixout(merged, w, x, gain, tm=512):
    m, k = merged.shape
    n = w.shape[1]
    return pl.pallas_call(
        _mixout_body,
        grid=(m // tm,),
        in_specs=[pl.BlockSpec((tm, k), lambda i: (i, 0)),
                  pl.BlockSpec((k, n), lambda i: (0, 0)),
                  pl.BlockSpec((tm, n), lambda i: (i, 0)),
                  pl.BlockSpec((1, n), lambda i: (0, 0))],
        out_specs=[pl.BlockSpec((tm, n), lambda i: (i, 0)),
                   pl.BlockSpec((tm, n), lambda i: (i, 0))],
        out_shape=[jax.ShapeDtypeStruct((m, n), F32), jax.ShapeDtypeStruct((m, n), BF16)],
        compiler_params=_params(1),
        name="mixout",
    )(merged, w, x, gain.reshape(1, n))


def _memkv_body(mem_ref, g_ref, wk_ref, wv_ref, k_ref, v_ref):
    mn = _rms(mem_ref[...], g_ref[...]).astype(BF16)
    k_ref[...] = jnp.dot(mn, wk_ref[...], preferred_element_type=F32).astype(k_ref.dtype)
    v_ref[...] = jnp.dot(mn, wv_ref[...], preferred_element_type=F32).astype(v_ref.dtype)


def memory_kv(mem2d, gain, w_k, w_v, rows):
    m, d = mem2d.shape
    n = w_k.shape[1]
    return pl.pallas_call(
        _memkv_body,
        grid=(m // rows,),
        in_specs=[pl.BlockSpec((rows, d), lambda i: (i, 0)),
                  pl.BlockSpec((1, d), lambda i: (0, 0)),
                  pl.BlockSpec((d, n), lambda i: (0, 0)),
                  pl.BlockSpec((d, n), lambda i: (0, 0))],
        out_specs=[pl.BlockSpec((rows, n), lambda i: (i, 0)),
                   pl.BlockSpec((rows, n), lambda i: (i, 0))],
        out_shape=[jax.ShapeDtypeStruct((m, n), BF16), jax.ShapeDtypeStruct((m, n), BF16)],
        compiler_params=_params(1),
        name="memory_kv",
    )(mem2d, gain.reshape(1, d), w_k, w_v)


def _xattn_body(h_ref, x1_ref, k_ref, v_ref, wq_ref, wo_ref, g_ref, wr_ref, br_ref,
                x2_ref, h2_ref, idx_ref, wgt_ref):
    q = jnp.dot(h_ref[...], wq_ref[...], preferred_element_type=F32).astype(BF16)
    outs = []
    for hh in range(XATTN_HEADS):
        sl = slice(hh * HEAD_DIM, (hh + 1) * HEAD_DIM)
        s = _dot_nt(q[:, sl], k_ref[:, sl]) * ATTN_SCALE
        p = jnp.exp(s - jnp.max(s, axis=1, keepdims=True))
        o = jnp.dot(p.astype(BF16), v_ref[:, sl], preferred_element_type=F32)
        outs.append((o / jnp.sum(p, axis=1, keepdims=True)).astype(BF16))
    o = jnp.concatenate(outs, axis=1)
    x2 = x1_ref[...] + jnp.dot(o, wo_ref[...], preferred_element_type=F32)
    x2_ref[...] = x2
    h2 = _rms(x2, g_ref[...])
    h2_ref[...] = h2

    logits = jnp.dot(h2, wr_ref[...], precision=lax.Precision.HIGHEST,
                     preferred_element_type=F32) + br_ref[...]
    lane = lax.broadcasted_iota(jnp.int32, logits.shape, 1)
    out_lane = lax.broadcasted_iota(jnp.int32, idx_ref.shape, 1)
    idx_out = jnp.zeros(idx_ref.shape, jnp.int32)
    exp_out = jnp.zeros(wgt_ref.shape, F32)
    denom = jnp.zeros((logits.shape[0], 1), F32)
    top0 = None
    for kk in range(TOP_K):
        top = jnp.max(logits, axis=1, keepdims=True)
        arg = jnp.min(jnp.where(logits == top, lane, N_EXPERTS), axis=1, keepdims=True)
        logits = jnp.where(lane == arg, NEG_INF, logits)
        top0 = top if top0 is None else top0
        e = jnp.exp(top - top0)
        denom = denom + e
        idx_out = jnp.where(out_lane == kk, arg, idx_out)
        exp_out = jnp.where(out_lane == kk, e, exp_out)
    idx_ref[...] = idx_out
    wgt_ref[...] = exp_out / denom


def cross_attention_router(hx, x1, k_mem, v_mem, w_q, w_o, gain, w_router, b_router, seq, tm=512):
    m, d = hx.shape
    mem_len = k_mem.shape[0] // (m // seq)
    n = w_q.shape[1]
    per_b = seq // tm
    const = lambda shape: pl.BlockSpec(shape, lambda i: (0,) * len(shape))
    rows = lambda cols: pl.BlockSpec((tm, cols), lambda i: (i, 0))
    return pl.pallas_call(
        _xattn_body,
        grid=(m // tm,),
        in_specs=[rows(d), rows(d),
                  pl.BlockSpec((mem_len, n), lambda i: (i // per_b, 0)),
                  pl.BlockSpec((mem_len, n), lambda i: (i // per_b, 0)),
                  const((d, n)), const((n, d)), const((1, d)), const((d, N_EXPERTS)), const((1, N_EXPERTS))],
        out_specs=[rows(d), rows(d), rows(LANES), rows(LANES)],
        out_shape=[jax.ShapeDtypeStruct((m, d), F32), jax.ShapeDtypeStruct((m, d), F32),
                   jax.ShapeDtypeStruct((m, LANES), jnp.int32), jax.ShapeDtypeStruct((m, LANES), F32)],
        compiler_params=_params(1),
        name="cross_attention_router",
    )(hx, x1, k_mem, v_mem, w_q, w_o, gain.reshape(1, d), w_router.astype(F32), b_router.reshape(1, -1).astype(F32))


def _routing_plan(top_idx, n_tokens):
    rt = EXPERT_ROW_TILE
    tiles_per_unit = EXPERT_UNIT_ROWS // rt
    onehot = (top_idx[:, :, None] == jnp.arange(N_EXPERTS)[None, None, :]).astype(jnp.int32).sum(axis=1)
    before = jnp.cumsum(onehot, axis=0) - onehot
    count = onehot.sum(axis=0)
    tiles = (count + rt - 1) // rt
    tile_start = jnp.cumsum(tiles) - tiles
    pos = (tile_start * rt)[top_idx] + jnp.take_along_axis(before, top_idx, axis=1)

    units = (tiles + tiles_per_unit - 1) // tiles_per_unit
    unit_first = jnp.cumsum(units) - units
    n_units = units.sum()
    max_units = N_EXPERTS + (n_tokens * TOP_K) // EXPERT_UNIT_ROWS
    uid = jnp.arange(max_units)
    e_of = jnp.clip(jnp.searchsorted(jnp.cumsum(units), uid, side="right"), 0, N_EXPERTS - 1)
    k_in = uid - unit_first[e_of]
    live = uid < n_units
    last_e = e_of[jnp.maximum(n_units - 1, 0)]
    unit_expert = jnp.where(live, e_of, last_e).astype(jnp.int32)
    unit_start = jnp.where(live, (tile_start[e_of] + k_in * tiles_per_unit) * rt, 0).astype(jnp.int32)
    unit_tiles = jnp.where(live, jnp.minimum(tiles[e_of] - k_in * tiles_per_unit, tiles_per_unit), 0).astype(jnp.int32)
    totals = jnp.stack([n_units, tiles.sum()]).astype(jnp.int32)
    return pos.astype(jnp.int32), unit_expert, unit_start, unit_tiles, totals


def _dispatch_body(pos_hbm, h_ref, xs_in, xs_hbm, pos_smem, sem):
    del xs_in
    i = pl.program_id(0)
    tm = h_ref.shape[0]
    idx_copy = pltpu.make_async_copy(pos_hbm.at[i], pos_smem, sem.at[0])
    idx_copy.start()
    idx_copy.wait()

    def send(t, carry):
        for kk in range(TOP_K):
            r = pos_smem[t * TOP_K + kk]
            pltpu.make_async_copy(h_ref.at[pl.ds(t, 1), :], xs_hbm.at[pl.ds(r, 1), :], sem.at[1]).start()
        return carry

    lax.fori_loop(0, tm, send, 0, unroll=8)
    for kk in range(TOP_K):
        pltpu.make_async_copy(h_ref, xs_hbm.at[pl.ds(0, tm), :], sem.at[1]).wait()


def dispatch_rows(h2, pos, p_rows):
    t, d = h2.shape
    tm = GATHER_TOKENS
    return pl.pallas_call(
        _dispatch_body,
        grid=(t // tm,),
        in_specs=[pl.BlockSpec(memory_space=pl.ANY),
                  pl.BlockSpec((tm, d), lambda i: (i, 0)),
                  pl.BlockSpec(memory_space=pl.ANY)],
        out_specs=pl.BlockSpec(memory_space=pl.ANY),
        out_shape=jax.ShapeDtypeStruct((p_rows, d), h2.dtype),
        scratch_shapes=[pltpu.SMEM((tm * TOP_K,), jnp.int32), pltpu.SemaphoreType.DMA((2,))],
        input_output_aliases={2: 0},
        compiler_params=_params(1),
        name="dispatch_rows",
    )(pos.reshape(t // tm, tm * TOP_K), h2, jnp.zeros((p_rows, d), h2.dtype))


def _expert_body(ue_ref, us_ref, un_ref, nu_ref,
                 x_hbm, wgu_ref, bgu_ref, wd_ref, bd_ref, y_hbm,
                 xbuf, actbuf, xstage, ystage, wgu_bf, wd_f32, wd_bf, sem_x, sem_y, *, n_up, n_down):
    u = pl.program_id(0)
    c = pl.program_id(1)
    rt = EXPERT_ROW_TILE
    chunk = EXPERT_COL_CHUNK
    half = chunk // 2
    quarter = half // 2
    live = u < nu_ref[0]
    start = us_ref[u]
    n_tiles = un_ref[u]

    def tile_rows(j):
        return pl.ds(pl.multiple_of(j * rt, rt), rt)

    def for_tiles(fn):
        def body(j, carry):
            fn(j)
            return carry
        lax.fori_loop(0, n_tiles, body, 0)

    @pl.when(live & (c == 0))
    def _():
        def x_copy(j, slot):
            rows = pl.ds(pl.multiple_of(start + j * rt, rt), rt)
            return pltpu.make_async_copy(x_hbm.at[rows, :], xstage.at[slot], sem_x.at[slot])

        x_copy(0, 0).start()

        def load(j):
            slot = j % 2

            @pl.when(j + 1 < n_tiles)
            def _():
                x_copy(j + 1, 1 - slot).start()

            x_copy(j, slot).wait()
            xbuf[tile_rows(j), :] = xstage[slot].astype(BF16)

        for_tiles(load)

    @pl.when(live & (c < n_up))
    def _():
        wgu_bf[...] = wgu_ref[0].astype(BF16)
        bias = bgu_ref[0]
        even = (lax.broadcasted_iota(jnp.int32, (rt, half), 1) % 2) == 0

        def tile(j):
            gu = jnp.dot(xbuf[tile_rows(j), :], wgu_bf[...], preferred_element_type=F32) + bias
            lo = gu[:, :half]
            hi = gu[:, half:]
            gate = jnp.where(even, lo, pltpu.roll(hi, 1, axis=1))
            up = jnp.where(even, pltpu.roll(lo, half - 1, axis=1), hi)
            gate = jnp.minimum(gate, SWIGLU_LIMIT)
            up = jnp.clip(up, -SWIGLU_LIMIT, SWIGLU_LIMIT)
            act = (up + 1.0) * gate * _sigmoid(SWIGLU_ALPHA * gate)
            actbuf[c, tile_rows(j), :] = act.astype(BF16)

        for_tiles(tile)

    @pl.when(live & (c >= n_up))
    def _():
        cd = c - n_up
        for g in range(chunk // LANES):
            lanes = slice(g * LANES, (g + 1) * LANES)
            for f in range(n_up):
                base = f * half
                wd_f32[g, pl.ds(base, quarter, stride=2), :] = wd_ref[0, base:base + quarter, lanes]
                wd_f32[g, pl.ds(base + 1, quarter, stride=2), :] = wd_ref[0, base + quarter:base + half, lanes]
            wd_bf[:, lanes] = wd_f32[g].astype(BF16)
        bias = bd_ref[0]

        def y_copy(j, slot):
            rows = pl.ds(pl.multiple_of(start + j * rt, rt), rt)
            cols = pl.ds(pl.multiple_of(cd * chunk, chunk), chunk)
            return pltpu.make_async_copy(ystage.at[slot], y_hbm.at[rows, cols], sem_y.at[slot])

        def tile(j):
            slot = j % 2

            @pl.when(j >= 2)
            def _():
                y_copy(j - 2, slot).wait()

            act = jnp.concatenate([actbuf[f, tile_rows(j), :] for f in range(n_up)], axis=1)
            ystage[slot] = jnp.dot(act, wd_bf[...], preferred_element_type=F32) + bias
            y_copy(j, slot).start()

        for_tiles(tile)

        @pl.when(n_tiles >= 2)
        def _():
            y_copy(n_tiles - 2, n_tiles % 2).wait()

        y_copy(n_tiles - 1, (n_tiles - 1) % 2).wait()

    @pl.when((u == pl.num_programs(0) - 1) & (c == n_up + n_down - 1))
    def _():
        zero_rows = xstage.at[0]
        zero_rows[...] = jnp.zeros(zero_rows.shape, F32)
        used_tiles = nu_ref[1]

        def pad_copy(j):
            return pltpu.make_async_copy(zero_rows, y_hbm.at[pl.ds(pl.multiple_of(j * rt, rt), rt), :], sem_x.at[0])

        def pad_start(j, carry):
            pad_copy(j).start()
            return carry

        def pad_wait(j, carry):
            pad_copy(j).wait()
            return carry

        lax.fori_loop(used_tiles, y_hbm.shape[0] // rt, pad_start, 0)
        lax.fori_loop(used_tiles, y_hbm.shape[0] // rt, pad_wait, 0)


def expert_ffn(x_sorted, w_gate_up, b_gate_up, w_down, b_down, unit_expert, unit_start, unit_tiles, totals):
    p_rows, d = x_sorted.shape
    n_exp, _, two_ff = w_gate_up.shape
    d_ff = two_ff // 2
    chunk = EXPERT_COL_CHUNK
    n_up = two_ff // chunk
    n_down = d // chunk
    n_steps = n_up + n_down
    max_units = unit_expert.shape[0]
    rt = EXPERT_ROW_TILE

    def up_idx(u, c, nu):
        return jnp.where(u < nu[0], jnp.minimum(c, n_up - 1), n_up - 1)

    def down_idx(u, c, nu):
        return jnp.where(u < nu[0], jnp.maximum(c - n_up, 0), n_down - 1)

    grid_spec = pltpu.PrefetchScalarGridSpec(
        num_scalar_prefetch=4,
        grid=(max_units, n_steps),
        in_specs=[pl.BlockSpec(memory_space=pl.ANY),
                  pl.BlockSpec((1, d, chunk), lambda u, c, ue, us, un, nu: (ue[u], 0, up_idx(u, c, nu))),
                  pl.BlockSpec((1, 1, chunk), lambda u, c, ue, us, un, nu: (ue[u], 0, up_idx(u, c, nu))),
                  pl.BlockSpec((1, d_ff, chunk), lambda u, c, ue, us, un, nu: (ue[u], 0, down_idx(u, c, nu))),
                  pl.BlockSpec((1, 1, chunk), lambda u, c, ue, us, un, nu: (ue[u], 0, down_idx(u, c, nu)))],
        out_specs=pl.BlockSpec(memory_space=pl.ANY),
        scratch_shapes=[pltpu.VMEM((EXPERT_UNIT_ROWS, d), BF16),
                        pltpu.VMEM((n_up, EXPERT_UNIT_ROWS, chunk // 2), BF16),
                        pltpu.VMEM((2, rt, d), F32),
                        pltpu.VMEM((2, rt, chunk), F32),
                        pltpu.VMEM((d, chunk), BF16),
                        pltpu.VMEM((chunk // LANES, d_ff, LANES), F32),
                        pltpu.VMEM((d_ff, chunk), BF16),
                        pltpu.SemaphoreType.DMA((2,)),
                        pltpu.SemaphoreType.DMA((2,))],
    )
    return pl.pallas_call(
        functools.partial(_expert_body, n_up=n_up, n_down=n_down),
        grid_spec=grid_spec,
        out_shape=jax.ShapeDtypeStruct((p_rows, d), F32),
        compiler_params=_params(2),
        name="expert_ffn",
    )(unit_expert, unit_start, unit_tiles, totals,
      x_sorted, w_gate_up, b_gate_up.reshape(n_exp, 1, two_ff), w_down, b_down.reshape(n_exp, 1, d))


def _combine_body(pos_hbm, x2_ref, w_ref, g_ref, y_hbm, o_ref, pos_smem, ybuf, sem, *, final_norm):
    i = pl.program_id(0)
    n = pl.num_programs(0)
    tm = x2_ref.shape[0]
    slot = i % 2

    def fetch(step, into):
        idx_copy = pltpu.make_async_copy(pos_hbm.at[step], pos_smem.at[into], sem.at[2])
        idx_copy.start()
        idx_copy.wait()

        def recv(t, carry):
            for kk in range(TOP_K):
                r = pos_smem[into, t * TOP_K + kk]
                pltpu.make_async_copy(y_hbm.at[pl.ds(r, 1), :], ybuf.at[into, kk, pl.ds(t, 1), :], sem.at[into]).start()
            return carry

        lax.fori_loop(0, tm, recv, 0, unroll=8)

    @pl.when(i == 0)
    def _():
        fetch(0, 0)

    @pl.when(i + 1 < n)
    def _():
        fetch(i + 1, 1 - slot)

    for kk in range(TOP_K):
        pltpu.make_async_copy(y_hbm.at[pl.ds(0, tm), :], ybuf.at[slot, kk], sem.at[slot]).wait()

    x3 = x2_ref[...]
    for kk in range(TOP_K):
        x3 = x3 + w_ref[:, kk:kk + 1] * ybuf[slot, kk]
    o_ref[...] = _rms(x3, g_ref[...]) if final_norm else x3


def combine(x2, y_sorted, pos, weights, gain):
    m, d = x2.shape
    tm = GATHER_TOKENS
    final_norm = gain is not None
    gain = gain if final_norm else jnp.ones((d,), F32)
    return pl.pallas_call(
        functools.partial(_combine_body, final_norm=final_norm),
        grid=(m // tm,),
        in_specs=[pl.BlockSpec(memory_space=pl.ANY),
                  pl.BlockSpec((tm, d), lambda i: (i, 0)),
                  pl.BlockSpec((tm, LANES), lambda i: (i, 0)),
                  pl.BlockSpec((1, d), lambda i: (0, 0)),
                  pl.BlockSpec(memory_space=pl.ANY)],
        out_specs=pl.BlockSpec((tm, d), lambda i: (i, 0)),
        out_shape=jax.ShapeDtypeStruct((m, d), F32),
        scratch_shapes=[pltpu.SMEM((2, tm * TOP_K), jnp.int32),
                        pltpu.VMEM((2, TOP_K, tm, d), F32),
                        pltpu.SemaphoreType.DMA((3,))],
        compiler_params=_params(1),
        name="combine",
    )(pos.reshape(m // tm, tm * TOP_K), x2, weights, gain.reshape(1, d), y_sorted)


def kernel(x, mem, rel_bias_table, mix_norm, w_in, diff_lambda_q1, diff_lambda_k1, diff_lambda_q2, diff_lambda_k2, diff_subln, w_branch_moba, w_branch_diff, w_mix_out, xattn_norm, mem_norm, w_xq, w_xk, w_xv, w_xo, ffn_norm, w_router, b_router, w_gate_up, b_gate_up, w_down, b_down, final_norm):
    batch, seq, d = x.shape
    n_tok = batch * seq
    x2d = x.reshape(n_tok, d)
    near, far = _bias_tiles(rel_bias_table)
    diff_col0 = 3 * MOBA_WIDTH
    gate_col0 = diff_col0 + 3 * DIFF_WIDTH
    cols = jnp.arange(w_in.shape[2])
    is_q = (cols < MOBA_WIDTH) | ((cols >= diff_col0) & (cols < diff_col0 + DIFF_WIDTH))
    col_scale = jnp.where(is_q, ATTN_SCALE * LOG2E, 1.0)
    p_rows = n_tok * TOP_K + N_EXPERTS * EXPERT_ROW_TILE
    for l in range(w_in.shape[0]):
        h = rmsnorm_rows(x2d, mix_norm[l])
        y = matmul_colscale(h, w_in[l].astype(BF16), col_scale)
        o_moba = moba_attention(y, near[:MOBA_HEADS], far[:MOBA_HEADS], batch, seq)
        o_diff = diff_attention(y, near[MOBA_HEADS:], far[MOBA_HEADS:], diff_lambda_q1[l], diff_lambda_k1[l],
                                diff_lambda_q2[l], diff_lambda_k2[l], diff_subln[l], batch, seq, diff_col0)
        merged = branch_merge(o_moba, o_diff, w_branch_moba[l].astype(BF16), w_branch_diff[l].astype(BF16),
                              y, gate_col0)
        x1, hx = mixout(merged, w_mix_out[l].astype(BF16), x2d, xattn_norm[l])
        k_mem, v_mem = memory_kv(mem.reshape(-1, d), mem_norm[l], w_xk[l].astype(BF16), w_xv[l].astype(BF16),
                                 mem.shape[1])
        x2, h2, idx_pad, wgt_pad = cross_attention_router(hx, x1, k_mem, v_mem, w_xq[l].astype(BF16),
                                                          w_xo[l].astype(BF16), ffn_norm[l], w_router[l],
                                                          b_router[l], seq)
        pos, unit_expert, unit_start, unit_tiles, totals = _routing_plan(idx_pad[:, :TOP_K], n_tok)
        x_sorted = dispatch_rows(h2, pos, p_rows)
        y_sorted = expert_ffn(x_sorted, w_gate_up[l], b_gate_up[l], w_down[l], b_down[l],
                              unit_expert, unit_start, unit_tiles, totals)
        last = l == w_in.shape[0] - 1
        x2d = combine(x2, y_sorted, pos, wgt_pad, final_norm if last else None)
    return x2d.reshape(batch, seq, d)
```

```python
import collections
import functools
import math

import jax
import jax.numpy as jnp
from jax import lax
from jax.experimental import pallas as pl
from jax.experimental.pallas import tpu as pltpu

F32 = jnp.float32
BF16 = jnp.bfloat16
NEG_INF = float("-inf")

D_MODEL = 2048
HEAD_DIM = 128
MOBA_HEADS = 8
MOBA_WIDTH = MOBA_HEADS * HEAD_DIM
MOBA_BLOCK = 256
MOBA_TOPK = 3
DIFF_HEADS = 4
DIFF_WIDTH = DIFF_HEADS * 2 * HEAD_DIM
REL_BUCKETS = 32
REL_MAX_DISTANCE = 128
XATTN_HEADS = 4
N_EXPERTS = 32
TOP_K = 4
SWIGLU_LIMIT = 7.0
SWIGLU_ALPHA = 1.702
NORM_EPS = 1e-5
LAMBDA_INIT = 0.8 - 0.6 * math.exp(-0.3 * 0)
ATTN_SCALE = HEAD_DIM ** -0.5
LOG2E = math.log2(math.e)

ATTN_TILE = MOBA_BLOCK
MOBA_HEADS_PER_STEP = 2
LANES = 128
EXPERT_ROW_TILE = 256
EXPERT_UNIT_ROWS = 1536
EXPERT_COL_CHUNK = 512
GATHER_TOKENS = 256
VMEM_LIMIT = 56 * 1024 * 1024


def _params(n_axes):
    return pltpu.CompilerParams(dimension_semantics=("arbitrary",) * n_axes,
                                vmem_limit_bytes=VMEM_LIMIT)


def _rms(x, gain):
    return x * lax.rsqrt(jnp.mean(x * x, axis=-1, keepdims=True) + NORM_EPS) * gain


def _sigmoid(x):
    return 1.0 / (1.0 + jnp.exp(-x))


def _dot_nt(a, b):
    return lax.dot_general(a, b, (((1,), (1,)), ((), ())), preferred_element_type=F32)


def _rmsnorm_body(x_ref, g_ref, o_ref):
    o_ref[...] = _rms(x_ref[...], g_ref[...]).astype(o_ref.dtype)


def rmsnorm_rows(x, gain, tm=512):
    t, d = x.shape
    return pl.pallas_call(
        _rmsnorm_body,
        grid=(t // tm,),
        in_specs=[pl.BlockSpec((tm, d), lambda i: (i, 0)),
                  pl.BlockSpec((1, d), lambda i: (0, 0))],
        out_specs=pl.BlockSpec((tm, d), lambda i: (i, 0)),
        out_shape=jax.ShapeDtypeStruct((t, d), BF16),
        compiler_params=_params(1),
        name="rmsnorm_rows",
    )(x, gain.reshape(1, d))


def _matmul_body(a_ref, w_ref, cs_ref, o_ref):
    acc = jnp.dot(a_ref[...], w_ref[...], preferred_element_type=F32)
    o_ref[...] = (acc * cs_ref[...]).astype(o_ref.dtype)


def matmul_colscale(a, w, col_scale, tm=1024, tn=1024, out_dtype=BF16):
    m, k = a.shape
    n = w.shape[1]
    tm, tn = min(tm, m), min(tn, n)
    return pl.pallas_call(
        _matmul_body,
        grid=(n // tn, m // tm),
        in_specs=[pl.BlockSpec((tm, k), lambda j, i: (i, 0)),
                  pl.BlockSpec((k, tn), lambda j, i: (0, j)),
                  pl.BlockSpec((1, tn), lambda j, i: (0, j))],
        out_specs=pl.BlockSpec((tm, tn), lambda j, i: (i, j)),
        out_shape=jax.ShapeDtypeStruct((m, n), out_dtype),
        compiler_params=_params(2),
        name="matmul",
    )(a, w, col_scale.reshape(1, n).astype(F32))


def _rel_bucket(dist):
    n = jnp.maximum(dist, 0)
    max_exact = REL_BUCKETS // 2
    nf = jnp.maximum(n, max_exact).astype(F32)
    large = max_exact + (jnp.log(nf / max_exact) / math.log(REL_MAX_DISTANCE / max_exact)
                         * (REL_BUCKETS - max_exact)).astype(jnp.int32)
    return jnp.where(n < max_exact, n, jnp.minimum(large, REL_BUCKETS - 1))


def _bias_tiles(table):
    t = ATTN_TILE
    r = jnp.arange(t)[:, None]
    c = jnp.arange(t)[None, :]
    dist = jnp.stack([r - c, t + r - c])
    near = jnp.moveaxis(table[_rel_bucket(dist)], -1, 0)
    far = jnp.broadcast_to(table[REL_BUCKETS - 1][:, None, None], (table.shape[1], 1, t))
    return near.astype(F32) * LOG2E, far.astype(F32) * LOG2E


def _lane_halves(x, op):
    return op(x[:, :LANES], x[:, LANES:])


AttnStream = collections.namedtuple("AttnStream", "q k_rows v_rows near cfar pen_at s_scr")


def _causal_attention(streams, qi):
    t = ATTN_TILE
    own_slot = streams[0].s_scr.shape[0] - 1
    spare_slot = own_slot - 1
    prev = jnp.maximum(qi - 1, 0)
    n_far = prev
    odd = n_far % 2 == 1
    tail = jnp.where(odd, n_far - 1, 0)
    tail_slot = jnp.where(odd, n_far - 1, spare_slot)
    row = lax.broadcasted_iota(jnp.int32, (t, t), 0)
    col = lax.broadcasted_iota(jnp.int32, (t, t), 1)
    has_prev = jnp.where(qi >= 1, 0.0, NEG_INF)
    has_tail = jnp.where(odd, 0.0, NEG_INF)

    def masked(st, s, n):
        return s if st.pen_at is None else s + st.pen_at(n)

    def tile_max(m, s):
        return jnp.maximum(m, _lane_halves(s, jnp.maximum))

    def tile_sum(l, p):
        return l + _lane_halves(p, jnp.add)

    m_near, m_far = [], []
    for st in streams:
        s3 = _dot_nt(st.q, jnp.concatenate([st.k_rows(qi, 1), st.k_rows(prev, 1), st.k_rows(tail, 1)], axis=0))
        s_own = jnp.where(col <= row, s3[:, :t] + st.near(0), NEG_INF)
        s_prev = masked(st, s3[:, t:2 * t] + st.near(1) + has_prev, prev)
        s_tail = masked(st, s3[:, 2 * t:] + has_tail, tail)
        st.s_scr[own_slot] = s_own
        st.s_scr[prev] = s_prev - st.cfar
        st.s_scr[tail_slot] = s_tail
        m_near.append(tile_max(_lane_halves(s_own, jnp.maximum), s_prev))
        m_far.append(_lane_halves(s_tail, jnp.maximum))

    def pair_scores(i, m_far):
        out = []
        for st, m in zip(streams, m_far):
            s = _dot_nt(st.q, st.k_rows(2 * i, 2))
            for half in range(2):
                sh = masked(st, s[:, half * t:(half + 1) * t], 2 * i + half)
                st.s_scr[2 * i + half] = sh
                m = tile_max(m, sh)
            out.append(m)
        return tuple(out)

    m_far = lax.fori_loop(0, n_far // 2, pair_scores, tuple(m_far))

    m_past, carry = [], []
    for st, mn, mf in zip(streams, m_near, m_far):
        m_row = jnp.maximum(jnp.max(mn, axis=1, keepdims=True),
                            jnp.max(mf, axis=1, keepdims=True) + st.cfar)
        mp = m_row - st.cfar
        p_own = jnp.exp2(st.s_scr[own_slot] - m_row)
        p_prev = jnp.exp2(st.s_scr[prev] - mp)
        p_tail = jnp.exp2(st.s_scr[tail_slot] - mp)
        l_part = tile_sum(tile_sum(_lane_halves(p_own, jnp.add), p_prev), p_tail)
        acc = jnp.dot(jnp.concatenate([p_own, p_prev, p_tail], axis=1).astype(BF16),
                      jnp.concatenate([st.v_rows(qi, 1), st.v_rows(prev, 1), st.v_rows(tail, 1)], axis=0),
                      preferred_element_type=F32)
        m_past.append(mp)
        carry.append((l_part, acc))

    def pair_weights(i, carry):
        out = []
        for st, mp, (l_part, acc) in zip(streams, m_past, carry):
            p = jnp.exp2(jnp.concatenate([st.s_scr[2 * i], st.s_scr[2 * i + 1]], axis=1) - mp)
            l_part = tile_sum(tile_sum(l_part, p[:, :t]), p[:, t:])
            out.append((l_part, acc + jnp.dot(p.astype(BF16), st.v_rows(2 * i, 2), preferred_element_type=F32)))
        return tuple(out)

    carry = lax.fori_loop(0, n_far // 2, pair_weights, tuple(carry))
    return [(acc, jnp.sum(l_part, axis=1, keepdims=True)) for l_part, acc in carry]


def _block_rows(n, w=1):
    return pl.ds(pl.multiple_of(n * ATTN_TILE, ATTN_TILE), w * ATTN_TILE)


def _moba_body(q_ref, k_ref, v_ref, near_ref, far_ref, o_ref, kmean_ref, s_scr, *, n_blocks):
    qi = pl.program_id(2)
    t = ATTN_TILE
    dh = HEAD_DIM
    heads = MOBA_HEADS_PER_STEP

    @pl.when(qi == 0)
    def _():
        for j in range(heads):
            for n in range(n_blocks):
                kmean_ref[j, n:n + 1, :] = jnp.mean(k_ref[n * t:(n + 1) * t, j * dh:(j + 1) * dh].astype(F32),
                                                    axis=0, keepdims=True)

    def stream(j):
        cols = slice(j * dh, (j + 1) * dh)
        q = q_ref[:, cols]
        gate = lax.dot_general(q.astype(F32), kmean_ref[j], (((1,), (1,)), ((), ())),
                               precision=lax.Precision.HIGHEST, preferred_element_type=F32)
        blk = lax.broadcasted_iota(jnp.int32, gate.shape, 1)
        valid = blk < qi
        g = jnp.where(valid, gate, NEG_INF)
        kth = g
        for _ in range(MOBA_TOPK - 1):
            top = jnp.max(kth, axis=1, keepdims=True)
            kth = jnp.where(kth == top, NEG_INF, kth)
        third = jnp.max(kth, axis=1, keepdims=True)
        pen = jnp.where(valid & (g >= third), 0.0, NEG_INF)

        def pen_at(n):
            return jnp.max(jnp.where(blk == n, pen, NEG_INF), axis=1, keepdims=True)

        return AttnStream(q, lambda n, w: k_ref[_block_rows(n, w), cols], lambda n, w: v_ref[_block_rows(n, w), cols],
                          lambda i: near_ref[j, i], far_ref[j][:, :1], pen_at, s_scr.at[j])

    results = _causal_attention([stream(j) for j in range(heads)], qi)
    for j, (acc, l) in enumerate(results):
        o_ref[:, j * dh:(j + 1) * dh] = (acc / l).astype(o_ref.dtype)


def moba_attention(y, near, far, batch, seq):
    t = ATTN_TILE
    nq = seq // t
    hs = MOBA_HEADS_PER_STEP
    groups = MOBA_HEADS // hs
    w = hs * HEAD_DIM
    return pl.pallas_call(
        functools.partial(_moba_body, n_blocks=nq),
        grid=(batch, groups, nq),
        in_specs=[pl.BlockSpec((t, w), lambda b, g, i: (b * nq + i, g)),
                  pl.BlockSpec((seq, w), lambda b, g, i: (b, groups + g)),
                  pl.BlockSpec((seq, w), lambda b, g, i: (b, 2 * groups + g)),
                  pl.BlockSpec((hs, 2, t, t), lambda b, g, i: (g, 0, 0, 0)),
                  pl.BlockSpec((hs, 1, t), lambda b, g, i: (g, 0, 0))],
        out_specs=pl.BlockSpec((t, w), lambda b, g, i: (b * nq + i, g)),
        out_shape=jax.ShapeDtypeStruct((batch * seq, MOBA_WIDTH), BF16),
        scratch_shapes=[pltpu.VMEM((hs, nq, HEAD_DIM), F32),
                        pltpu.VMEM((hs, nq + 1, t, t), F32)],
        compiler_params=_params(3),
        name="moba_attention",
    )(y, y, y, near, far)


def _diff_body(q_ref, k_ref, v_ref, near_ref, far_ref, lq1_ref, lk1_ref, lq2_ref, lk2_ref, subln_ref, o_ref, s_scr):
    qi = pl.program_id(2)
    dh = HEAD_DIM
    cfar = far_ref[0][:, :1]
    v_rows = lambda n, w: v_ref[_block_rows(n, w), :]
    near = lambda i: near_ref[0, i]

    def stream(j):
        cols = slice(j * dh, (j + 1) * dh)
        return AttnStream(q_ref[:, cols], lambda n, w: k_ref[_block_rows(n, w), cols], v_rows, near, cfar,
                          None, s_scr.at[j])

    (acc1, l1), (acc2, l2) = _causal_attention([stream(0), stream(1)], qi)
    lam = (jnp.exp(jnp.sum(lq1_ref[...] * lk1_ref[...], axis=1, keepdims=True))
           - jnp.exp(jnp.sum(lq2_ref[...] * lk2_ref[...], axis=1, keepdims=True)) + LAMBDA_INIT)
    o = acc1 / l1 - lam * (acc2 / l2)
    o_ref[...] = (_rms(o, subln_ref[...]) * (1.0 - LAMBDA_INIT)).astype(o_ref.dtype)


def diff_attention(y, near, far, lq1, lk1, lq2, lk2, subln, batch, seq, col0):
    t = ATTN_TILE
    nq = seq // t
    h = DIFF_HEADS
    w = 2 * HEAD_DIM
    base = col0 // w
    vec = lambda a: a.reshape(1, -1).astype(F32)
    small = lambda n: pl.BlockSpec((1, n), lambda b, hh, i: (0, 0))
    return pl.pallas_call(
        _diff_body,
        grid=(batch, h, nq),
        in_specs=[pl.BlockSpec((t, w), lambda b, hh, i: (b * nq + i, base + hh)),
                  pl.BlockSpec((seq, w), lambda b, hh, i: (b, base + h + hh)),
                  pl.BlockSpec((seq, w), lambda b, hh, i: (b, base + 2 * h + hh)),
                  pl.BlockSpec((1, 2, t, t), lambda b, hh, i: (hh, 0, 0, 0)),
                  pl.BlockSpec((1, 1, t), lambda b, hh, i: (hh, 0, 0)),
                  small(HEAD_DIM), small(HEAD_DIM), small(HEAD_DIM), small(HEAD_DIM), small(w)],
        out_specs=pl.BlockSpec((t, w), lambda b, hh, i: (b * nq + i, hh)),
        out_shape=jax.ShapeDtypeStruct((batch * seq, h * w), BF16),
        scratch_shapes=[pltpu.VMEM((2, nq + 1, t, t), F32)],
        compiler_params=_params(3),
        name="diff_attention",
    )(y, y, y, near, far, vec(lq1), vec(lk1), vec(lq2), vec(lk2), vec(subln))


def _branch_body(om_ref, od_ref, wm_ref, wd_ref, ga_ref, gb_ref, o_ref):
    a = jnp.dot(om_ref[...], wm_ref[...], preferred_element_type=F32)
    b = jnp.dot(od_ref[...], wd_ref[...], preferred_element_type=F32)
    o_ref[...] = (_sigmoid(ga_ref[...].astype(F32)) * a + _sigmoid(gb_ref[...].astype(F32)) * b).astype(o_ref.dtype)


def branch_merge(o_moba, o_diff, w_m, w_d, y, gate_col0, tm=1024, tn=1024):
    m, k = o_moba.shape
    n = w_m.shape[1]
    g0 = gate_col0 // tn
    nj = n // tn
    return pl.pallas_call(
        _branch_body,
        grid=(nj, m // tm),
        in_specs=[pl.BlockSpec((tm, k), lambda j, i: (i, 0)),
                  pl.BlockSpec((tm, k), lambda j, i: (i, 0)),
                  pl.BlockSpec((k, tn), lambda j, i: (0, j)),
                  pl.BlockSpec((k, tn), lambda j, i: (0, j)),
                  pl.BlockSpec((tm, tn), lambda j, i: (i, g0 + j)),
                  pl.BlockSpec((tm, tn), lambda j, i: (i, g0 + nj + j))],
        out_specs=pl.BlockSpec((tm, tn), lambda j, i: (i, j)),
        out_shape=jax.ShapeDtypeStruct((m, n), BF16),
        compiler_params=_params(2),
        name="branch_merge",
    )(o_moba, o_diff, w_m, w_d, y, y)


def _mixout_body(a_ref, w_ref, x_ref, g_ref, x1_ref, h_ref):
    x1 = x_ref[...] + jnp.dot(a_ref[...], w_ref[...], preferred_element_type=F32)
    x1_ref[...] = x1
    h_ref[...] = _rms(x1, g_ref[...]).astype(h_ref.dtype)


def mixout(merged, w, x, gain, tm=512):
    m, k = merged.shape
    n = w.shape[1]
    return pl.pallas_call(
        _mixout_body,
        grid=(m // tm,),
        in_specs=[pl.BlockSpec((tm, k), lambda i: (i, 0)),
                  pl.BlockSpec((k, n), lambda i: (0, 0)),
                  pl.BlockSpec((tm, n), lambda i: (i, 0)),
                  pl.BlockSpec((1, n), lambda i: (0, 0))],
        out_specs=[pl.BlockSpec((tm, n), lambda i: (i, 0)),
                   pl.BlockSpec((tm, n), lambda i: (i, 0))],
        out_shape=[jax.ShapeDtypeStruct((m, n), F32), jax.ShapeDtypeStruct((m, n), BF16)],
        compiler_params=_params(1),
        name="mixout",
    )(merged, w, x, gain.reshape(1, n))


def _memkv_body(mem_ref, g_ref, wk_ref, wv_ref, k_ref, v_ref):
    mn = _rms(mem_ref[...], g_ref[...]).astype(BF16)
    k_ref[...] = jnp.dot(mn, wk_ref[...], preferred_element_type=F32).astype(k_ref.dtype)
    v_ref[...] = jnp.dot(mn, wv_ref[...], preferred_element_type=F32).astype(v_ref.dtype)


def memory_kv(mem2d, gain, w_k, w_v, rows):
    m, d = mem2d.shape
    n = w_k.shape[1]
    return pl.pallas_call(
        _memkv_body,
        grid=(m // rows,),
        in_specs=[pl.BlockSpec((rows, d), lambda i: (i, 0)),
                  pl.BlockSpec((1, d), lambda i: (0, 0)),
                  pl.BlockSpec((d, n), lambda i: (0, 0)),
                  pl.BlockSpec((d, n), lambda i: (0, 0))],
        out_specs=[pl.BlockSpec((rows, n), lambda i: (i, 0)),
                   pl.BlockSpec((rows, n), lambda i: (i, 0))],
        out_shape=[jax.ShapeDtypeStruct((m, n), BF16), jax.ShapeDtypeStruct((m, n), BF16)],
        compiler_params=_params(1),
        name="memory_kv",
    )(mem2d, gain.reshape(1, d), w_k, w_v)


def _xattn_body(h_ref, x1_ref, k_ref, v_ref, wq_ref, wo_ref, g_ref, wr_ref, br_ref,
                x2_ref, h2_ref, idx_ref, wgt_ref):
    q = jnp.dot(h_ref[...], wq_ref[...], preferred_element_type=F32).astype(BF16)
    outs = []
    for hh in range(XATTN_HEADS):
        sl = slice(hh * HEAD_DIM, (hh + 1) * HEAD_DIM)
        s = _dot_nt(q[:, sl], k_ref[:, sl]) * ATTN_SCALE
        p = jnp.exp(s - jnp.max(s, axis=1, keepdims=True))
        o = jnp.dot(p.astype(BF16), v_ref[:, sl], preferred_element_type=F32)
        outs.append((o / jnp.sum(p, axis=1, keepdims=True)).astype(BF16))
    o = jnp.concatenate(outs, axis=1)
    x2 = x1_ref[...] + jnp.dot(o, wo_ref[...], preferred_element_type=F32)
    x2_ref[...] = x2
    h2 = _rms(x2, g_ref[...])
    h2_ref[...] = h2

    logits = jnp.dot(h2, wr_ref[...], precision=lax.Precision.HIGHEST,
                     preferred_element_type=F32) + br_ref[...]
    lane = lax.broadcasted_iota(jnp.int32, logits.shape, 1)
    out_lane = lax.broadcasted_iota(jnp.int32, idx_ref.shape, 1)
    idx_out = jnp.zeros(idx_ref.shape, jnp.int32)
    exp_out = jnp.zeros(wgt_ref.shape, F32)
    denom = jnp.zeros((logits.shape[0], 1), F32)
    top0 = None
    for kk in range(TOP_K):
        top = jnp.max(logits, axis=1, keepdims=True)
        arg = jnp.min(jnp.where(logits == top, lane, N_EXPERTS), axis=1, keepdims=True)
        logits = jnp.where(lane == arg, NEG_INF, logits)
        top0 = top if top0 is None else top0
        e = jnp.exp(top - top0)
        denom = denom + e
        idx_out = jnp.where(out_lane == kk, arg, idx_out)
        exp_out = jnp.where(out_lane == kk, e, exp_out)
    idx_ref[...] = idx_out
    wgt_ref[...] = exp_out / denom


def cross_attention_router(hx, x1, k_mem, v_mem, w_q, w_o, gain, w_router, b_router, seq, tm=512):
    m, d = hx.shape
    mem_len = k_mem.shape[0] // (m // seq)
    n = w_q.shape[1]
    per_b = seq // tm
    const = lambda shape: pl.BlockSpec(shape, lambda i: (0,) * len(shape))
    rows = lambda cols: pl.BlockSpec((tm, cols), lambda i: (i, 0))
    return pl.pallas_call(
        _xattn_body,
        grid=(m // tm,),
        in_specs=[rows(d), rows(d),
                  pl.BlockSpec((mem_len, n), lambda i: (i // per_b, 0)),
                  pl.BlockSpec((mem_len, n), lambda i: (i // per_b, 0)),
                  const((d, n)), const((n, d)), const((1, d)), const((d, N_EXPERTS)), const((1, N_EXPERTS))],
        out_specs=[rows(d), rows(d), rows(LANES), rows(LANES)],
        out_shape=[jax.ShapeDtypeStruct((m, d), F32), jax.ShapeDtypeStruct((m, d), F32),
                   jax.ShapeDtypeStruct((m, LANES), jnp.int32), jax.ShapeDtypeStruct((m, LANES), F32)],
        compiler_params=_params(1),
        name="cross_attention_router",
    )(hx, x1, k_mem, v_mem, w_q, w_o, gain.reshape(1, d), w_router.astype(F32), b_router.reshape(1, -1).astype(F32))


def _routing_plan(top_idx, n_tokens):
    rt = EXPERT_ROW_TILE
    tiles_per_unit = EXPERT_UNIT_ROWS // rt
    slot_onehot = (top_idx[:, :, None] == jnp.arange(N_EXPERTS)[None, None, :]).astype(jnp.int32)
    onehot = slot_onehot.sum(axis=1)
    before = jnp.cumsum(onehot, axis=0) - onehot
    count = onehot.sum(axis=0)
    tiles = (count + rt - 1) // rt
    tile_start = jnp.cumsum(tiles) - tiles
    pos = (slot_onehot * (tile_start * rt + before)[:, None, :]).sum(axis=-1)

    units = (tiles + tiles_per_unit - 1) // tiles_per_unit
    unit_first = jnp.cumsum(units) - units
    n_units = units.sum()
    max_units = N_EXPERTS + (n_tokens * TOP_K) // EXPERT_UNIT_ROWS
    uid = jnp.arange(max_units)
    e_of = jnp.clip(jnp.searchsorted(jnp.cumsum(units), uid, side="right"), 0, N_EXPERTS - 1)
    k_in = uid - unit_first[e_of]
    live = uid < n_units
    last_e = e_of[jnp.maximum(n_units - 1, 0)]
    unit_expert = jnp.where(live, e_of, last_e).astype(jnp.int32)
    unit_start = jnp.where(live, (tile_start[e_of] + k_in * tiles_per_unit) * rt, 0).astype(jnp.int32)
    unit_tiles = jnp.where(live, jnp.minimum(tiles[e_of] - k_in * tiles_per_unit, tiles_per_unit), 0).astype(jnp.int32)
    totals = jnp.stack([n_units, tiles.sum()]).astype(jnp.int32)
    return pos.astype(jnp.int32), unit_expert, unit_start, unit_tiles, totals


def _dispatch_body(pos_hbm, h_ref, xs_in, xs_hbm, pos_smem, sem):
    del xs_in
    i = pl.program_id(0)
    tm = h_ref.shape[0]
    idx_copy = pltpu.make_async_copy(pos_hbm.at[i], pos_smem, sem.at[0])
    idx_copy.start()
    idx_copy.wait()

    def send(t, carry):
        for kk in range(TOP_K):
            r = pos_smem[t * TOP_K + kk]
            pltpu.make_async_copy(h_ref.at[pl.ds(t, 1), :], xs_hbm.at[pl.ds(r, 1), :], sem.at[1]).start()
        return carry

    lax.fori_loop(0, tm, send, 0, unroll=8)
    for kk in range(TOP_K):
        pltpu.make_async_copy(h_ref, xs_hbm.at[pl.ds(0, tm), :], sem.at[1]).wait()


def dispatch_rows(h2, pos, p_rows):
    t, d = h2.shape
    tm = GATHER_TOKENS
    return pl.pallas_call(
        _dispatch_body,
        grid=(t // tm,),
        in_specs=[pl.BlockSpec(memory_space=pl.ANY),
                  pl.BlockSpec((tm, d), lambda i: (i, 0)),
                  pl.BlockSpec(memory_space=pl.ANY)],
        out_specs=pl.BlockSpec(memory_space=pl.ANY),
        out_shape=jax.ShapeDtypeStruct((p_rows, d), h2.dtype),
        scratch_shapes=[pltpu.SMEM((tm * TOP_K,), jnp.int32), pltpu.SemaphoreType.DMA((2,))],
        input_output_aliases={2: 0},
        compiler_params=_params(1),
        name="dispatch_rows",
    )(pos.reshape(t // tm, tm * TOP_K), h2, jnp.zeros((p_rows, d), h2.dtype))


def _expert_body(ue_ref, us_ref, un_ref, nu_ref,
                 x_hbm, wgu_ref, bgu_ref, wd_ref, bd_ref, y_hbm,
                 xbuf, actbuf, gubuf, ystage, wgu_bf, wd_f32, wd_bf, sem_x, sem_y, *, n_up, n_down):
    u = pl.program_id(0)
    c = pl.program_id(1)
    rt = EXPERT_ROW_TILE
    chunk = EXPERT_COL_CHUNK
    half = chunk // 2
    quarter = half // 2
    n_live = nu_ref[0]
    live = u < n_live
    start = us_ref[u]
    n_tiles = un_ref[u]

    def tile_rows(j):
        return pl.ds(pl.multiple_of(j * rt, rt), rt)

    def x_copy(unit, j):
        rows = pl.ds(pl.multiple_of(us_ref[unit] + j * rt, rt), rt)
        return pltpu.make_async_copy(x_hbm.at[rows, :], xbuf.at[tile_rows(j), :], sem_x.at[0])

    def fetch_rows(unit):
        def body(j, carry):
            x_copy(unit, j).start()
            return carry
        lax.fori_loop(0, un_ref[unit], body, 0)

    @pl.when((u == 0) & (c == 0) & live)
    def _():
        fetch_rows(0)

    @pl.when(live & (c == 0))
    def _():
        def body(j, carry):
            x_copy(u, j).wait()
            return carry
        lax.fori_loop(0, n_tiles, body, 0)

    @pl.when((c == n_up) & (u + 1 < n_live))
    def _():
        fetch_rows(u + 1)

    @pl.when(live & (c < n_up))
    def _():
        wgu_bf[...] = wgu_ref[0].astype(BF16)
        bias = bgu_ref[0]
        even = (lax.broadcasted_iota(jnp.int32, (rt, half), 1) % 2) == 0

        def project(j):
            x = xbuf[tile_rows(j), :].astype(BF16)
            gubuf[j % 2] = jnp.dot(x, wgu_bf[...], preferred_element_type=F32) + bias

        def activate(j):
            gu = gubuf[j % 2]
            lo = gu[:, :half]
            hi = gu[:, half:]
            gate = jnp.where(even, lo, pltpu.roll(hi, 1, axis=1))
            up = jnp.where(even, pltpu.roll(lo, half - 1, axis=1), hi)
            gate = jnp.minimum(gate, SWIGLU_LIMIT)
            up = jnp.clip(up, -SWIGLU_LIMIT, SWIGLU_LIMIT)
            act = (up + 1.0) * gate * _sigmoid(SWIGLU_ALPHA * gate)
            actbuf[c, tile_rows(j), :] = act.astype(BF16)

        project(0)

        def body(j, carry):
            activate(j - 1)
            project(j)
            return carry

        lax.fori_loop(1, n_tiles, body, 0)
        activate(n_tiles - 1)

    @pl.when(live & (c >= n_up))
    def _():
        cd = c - n_up
        for g in range(chunk // LANES):
            lanes = slice(g * LANES, (g + 1) * LANES)
            for f in range(n_up):
                base = f * half
                wd_f32[g, pl.ds(base, quarter, stride=2), :] = wd_ref[0, base:base + quarter, lanes]
                wd_f32[g, pl.ds(base + 1, quarter, stride=2), :] = wd_ref[0, base + quarter:base + half, lanes]
            wd_bf[:, lanes] = wd_f32[g].astype(BF16)
        bias = bd_ref[0]

        def y_copy(j, slot):
            rows = pl.ds(pl.multiple_of(start + j * rt, rt), rt)
            cols = pl.ds(pl.multiple_of(cd * chunk, chunk), chunk)
            return pltpu.make_async_copy(ystage.at[slot], y_hbm.at[rows, cols], sem_y.at[slot])

        def tile(j, carry):
            slot = j % 2

            @pl.when(j >= 2)
            def _():
                y_copy(j - 2, slot).wait()

            act = jnp.concatenate([actbuf[f, tile_rows(j), :] for f in range(n_up)], axis=1)
            ystage[slot] = jnp.dot(act, wd_bf[...], preferred_element_type=F32) + bias
            y_copy(j, slot).start()
            return carry

        lax.fori_loop(0, n_tiles, tile, 0)

        @pl.when(n_tiles >= 2)
        def _():
            y_copy(n_tiles - 2, n_tiles % 2).wait()

        y_copy(n_tiles - 1, (n_tiles - 1) % 2).wait()

    @pl.when((u == pl.num_programs(0) - 1) & (c == n_up + n_down - 1))
    def _():
        zero_rows = xbuf.at[:rt, :]
        zero_rows[...] = jnp.zeros(zero_rows.shape, F32)
        used_tiles = nu_ref[1]

        def pad_copy(j):
            return pltpu.make_async_copy(zero_rows, y_hbm.at[pl.ds(pl.multiple_of(j * rt, rt), rt), :], sem_x.at[0])

        def pad_start(j, carry):
            pad_copy(j).start()
            return carry

        def pad_wait(j, carry):
            pad_copy(j).wait()
            return carry

        lax.fori_loop(used_tiles, y_hbm.shape[0] // rt, pad_start, 0)
        lax.fori_loop(used_tiles, y_hbm.shape[0] // rt, pad_wait, 0)


def expert_ffn(x_sorted, w_gate_up, b_gate_up, w_down, b_down, unit_expert, unit_start, unit_tiles, totals):
    p_rows, d = x_sorted.shape
    n_exp, _, two_ff = w_gate_up.shape
    d_ff = two_ff // 2
    chunk = EXPERT_COL_CHUNK
    n_up = two_ff // chunk
    n_down = d // chunk
    n_steps = n_up + n_down
    max_units = unit_expert.shape[0]
    rt = EXPERT_ROW_TILE

    def up_idx(u, c, nu):
        return jnp.where(u < nu[0], jnp.minimum(c, n_up - 1), n_up - 1)

    def down_idx(u, c, nu):
        return jnp.where(u < nu[0], jnp.maximum(c - n_up, 0), n_down - 1)

    grid_spec = pltpu.PrefetchScalarGridSpec(
        num_scalar_prefetch=4,
        grid=(max_units, n_steps),
        in_specs=[pl.BlockSpec(memory_space=pl.ANY),
                  pl.BlockSpec((1, d, chunk), lambda u, c, ue, us, un, nu: (ue[u], 0, up_idx(u, c, nu))),
                  pl.BlockSpec((1, 1, chunk), lambda u, c, ue, us, un, nu: (ue[u], 0, up_idx(u, c, nu))),
                  pl.BlockSpec((1, d_ff, chunk), lambda u, c, ue, us, un, nu: (ue[u], 0, down_idx(u, c, nu))),
                  pl.BlockSpec((1, 1, chunk), lambda u, c, ue, us, un, nu: (ue[u], 0, down_idx(u, c, nu)))],
        out_specs=pl.BlockSpec(memory_space=pl.ANY),
        scratch_shapes=[pltpu.VMEM((EXPERT_UNIT_ROWS, d), F32),
                        pltpu.VMEM((n_up, EXPERT_UNIT_ROWS, chunk // 2), BF16),
                        pltpu.VMEM((2, rt, chunk), F32),
                        pltpu.VMEM((2, rt, chunk), F32),
                        pltpu.VMEM((d, chunk), BF16),
                        pltpu.VMEM((chunk // LANES, d_ff, LANES), F32),
                        pltpu.VMEM((d_ff, chunk), BF16),
                        pltpu.SemaphoreType.DMA((1,)),
                        pltpu.SemaphoreType.DMA((2,))],
    )
    return pl.pallas_call(
        functools.partial(_expert_body, n_up=n_up, n_down=n_down),
        grid_spec=grid_spec,
        out_shape=jax.ShapeDtypeStruct((p_rows, d), F32),
        compiler_params=_params(2),
        name="expert_ffn",
    )(unit_expert, unit_start, unit_tiles, totals,
      x_sorted, w_gate_up, b_gate_up.reshape(n_exp, 1, two_ff), w_down, b_down.reshape(n_exp, 1, d))


def _combine_body(pos_hbm, x2_ref, w_ref, g_ref, y_hbm, o_ref, pos_smem, ybuf, sem, *, final_norm):
    i = pl.program_id(0)
    n = pl.num_programs(0)
    tm = x2_ref.shape[0]
    slot = i % 2

    def fetch(step, into):
        idx_copy = pltpu.make_async_copy(pos_hbm.at[step], pos_smem.at[into], sem.at[2])
        idx_copy.start()
        idx_copy.wait()

        def recv(t, carry):
            for kk in range(TOP_K):
                r = pos_smem[into, t * TOP_K + kk]
                pltpu.make_async_copy(y_hbm.at[pl.ds(r, 1), :], ybuf.at[into, kk, pl.ds(t, 1), :], sem.at[into]).start()
            return carry

        lax.fori_loop(0, tm, recv, 0, unroll=8)

    @pl.when(i == 0)
    def _():
        fetch(0, 0)

    @pl.when(i + 1 < n)
    def _():
        fetch(i + 1, 1 - slot)

    for kk in range(TOP_K):
        pltpu.make_async_copy(y_hbm.at[pl.ds(0, tm), :], ybuf.at[slot, kk], sem.at[slot]).wait()

    x3 = x2_ref[...]
    for kk in range(TOP_K):
        x3 = x3 + w_ref[:, kk:kk + 1] * ybuf[slot, kk]
    o_ref[...] = _rms(x3, g_ref[...]) if final_norm else x3


def combine(x2, y_sorted, pos, weights, gain):
    m, d = x2.shape
    tm = GATHER_TOKENS
    final_norm = gain is not None
    gain = gain if final_norm else jnp.ones((d,), F32)
    return pl.pallas_call(
        functools.partial(_combine_body, final_norm=final_norm),
        grid=(m // tm,),
        in_specs=[pl.BlockSpec(memory_space=pl.ANY),
                  pl.BlockSpec((tm, d), lambda i: (i, 0)),
                  pl.BlockSpec((tm, LANES), lambda i: (i, 0)),
                  pl.BlockSpec((1, d), lambda i: (0, 0)),
                  pl.BlockSpec(memory_space=pl.ANY)],
        out_specs=pl.BlockSpec((tm, d), lambda i: (i, 0)),
        out_shape=jax.ShapeDtypeStruct((m, d), F32),
        scratch_shapes=[pltpu.SMEM((2, tm * TOP_K), jnp.int32),
                        pltpu.VMEM((2, TOP_K, tm, d), F32),
                        pltpu.SemaphoreType.DMA((3,))],
        compiler_params=_params(1),
        name="combine",
    )(pos.reshape(m // tm, tm * TOP_K), x2, weights, gain.reshape(1, d), y_sorted)


def kernel(x, mem, rel_bias_table, mix_norm, w_in, diff_lambda_q1, diff_lambda_k1, diff_lambda_q2, diff_lambda_k2, diff_subln, w_branch_moba, w_branch_diff, w_mix_out, xattn_norm, mem_norm, w_xq, w_xk, w_xv, w_xo, ffn_norm, w_router, b_router, w_gate_up, b_gate_up, w_down, b_down, final_norm):
    batch, seq, d = x.shape
    n_tok = batch * seq
    x2d = x.reshape(n_tok, d)
    near, far = _bias_tiles(rel_bias_table)
    diff_col0 = 3 * MOBA_WIDTH
    gate_col0 = diff_col0 + 3 * DIFF_WIDTH
    cols = jnp.arange(w_in.shape[2])
    is_q = (cols < MOBA_WIDTH) | ((cols >= diff_col0) & (cols < diff_col0 + DIFF_WIDTH))
    col_scale = jnp.where(is_q, ATTN_SCALE * LOG2E, 1.0)
    p_rows = n_tok * TOP_K + N_EXPERTS * EXPERT_ROW_TILE
    for l in range(w_in.shape[0]):
        h = rmsnorm_rows(x2d, mix_norm[l])
        y = matmul_colscale(h, w_in[l].astype(BF16), col_scale)
        o_moba = moba_attention(y, near[:MOBA_HEADS], far[:MOBA_HEADS], batch, seq)
        o_diff = diff_attention(y, near[MOBA_HEADS:], far[MOBA_HEADS:], diff_lambda_q1[l], diff_lambda_k1[l],
                                diff_lambda_q2[l], diff_lambda_k2[l], diff_subln[l], batch, seq, diff_col0)
        merged = branch_merge(o_moba, o_diff, w_branch_moba[l].astype(BF16), w_branch_diff[l].astype(BF16),
                              y, gate_col0)
        x1, hx = mixout(merged, w_mix_out[l].astype(BF16), x2d, xattn_norm[l])
        k_mem, v_mem = memory_kv(mem.reshape(-1, d), mem_norm[l], w_xk[l].astype(BF16), w_xv[l].astype(BF16),
                                 mem.shape[1])
        x2, h2, idx_pad, wgt_pad = cross_attention_router(hx, x1, k_mem, v_mem, w_xq[l].astype(BF16),
                                                          w_xo[l].astype(BF16), ffn_norm[l], w_router[l],
                                                          b_router[l], seq)
        pos, unit_expert, unit_start, unit_tiles, totals = _routing_plan(idx_pad[:, :TOP_K], n_tok)
        x_sorted = dispatch_rows(h2, pos, p_rows)
        y_sorted = expert_ffn(x_sorted, w_gate_up[l], b_gate_up[l], w_down[l], b_down[l],
                              unit_expert, unit_start, unit_tiles, totals)
        last = l == w_in.shape[0] - 1
        x2d = combine(x2, y_sorted, pos, wgt_pad, final_norm if last else None)
    return x2d.reshape(batch, seq, d)
```

```python
import collections
import functools
import math

import jax
import jax.numpy as jnp
from jax import lax
from jax.experimental import pallas as pl
from jax.experimental.pallas import tpu as pltpu

F32 = jnp.float32
BF16 = jnp.bfloat16
NEG_INF = float("-inf")

D_MODEL = 2048
HEAD_DIM = 128
MOBA_HEADS = 8
MOBA_WIDTH = MOBA_HEADS * HEAD_DIM
MOBA_BLOCK = 256
MOBA_TOPK = 3
DIFF_HEADS = 4
DIFF_WIDTH = DIFF_HEADS * 2 * HEAD_DIM
REL_BUCKETS = 32
REL_MAX_DISTANCE = 128
XATTN_HEADS = 4
N_EXPERTS = 32
TOP_K = 4
SWIGLU_LIMIT = 7.0
SWIGLU_ALPHA = 1.702
NORM_EPS = 1e-5
LAMBDA_INIT = 0.8 - 0.6 * math.exp(-0.3 * 0)
ATTN_SCALE = HEAD_DIM ** -0.5
LOG2E = math.log2(math.e)

ATTN_TILE = MOBA_BLOCK
MOBA_HEADS_PER_STEP = 2
LANES = 128
EXPERT_ROW_TILE = 256
EXPERT_UNIT_ROWS = 1536
EXPERT_COL_CHUNK = 512
GATHER_TOKENS = 256
VMEM_LIMIT = 56 * 1024 * 1024


def _params(n_axes):
    return pltpu.CompilerParams(dimension_semantics=("arbitrary",) * n_axes,
                                vmem_limit_bytes=VMEM_LIMIT)


def _rms(x, gain):
    return x * lax.rsqrt(jnp.mean(x * x, axis=-1, keepdims=True) + NORM_EPS) * gain


def _sigmoid(x):
    return 1.0 / (1.0 + jnp.exp(-x))


def _dot_nt(a, b):
    return lax.dot_general(a, b, (((1,), (1,)), ((), ())), preferred_element_type=F32)


def _rmsnorm_body(x_ref, g_ref, o_ref):
    o_ref[...] = _rms(x_ref[...], g_ref[...]).astype(o_ref.dtype)


def rmsnorm_rows(x, gain, tm=512):
    t, d = x.shape
    return pl.pallas_call(
        _rmsnorm_body,
        grid=(t // tm,),
        in_specs=[pl.BlockSpec((tm, d), lambda i: (i, 0)),
                  pl.BlockSpec((1, d), lambda i: (0, 0))],
        out_specs=pl.BlockSpec((tm, d), lambda i: (i, 0)),
        out_shape=jax.ShapeDtypeStruct((t, d), BF16),
        compiler_params=_params(1),
        name="rmsnorm_rows",
    )(x, gain.reshape(1, d))


def _matmul_body(a_ref, w_ref, cs_ref, o_ref):
    acc = jnp.dot(a_ref[...], w_ref[...], preferred_element_type=F32)
    o_ref[...] = (acc * cs_ref[...]).astype(o_ref.dtype)


def matmul_colscale(a, w, col_scale, tm=1024, tn=1024, out_dtype=BF16):
    m, k = a.shape
    n = w.shape[1]
    tm, tn = min(tm, m), min(tn, n)
    return pl.pallas_call(
        _matmul_body,
        grid=(n // tn, m // tm),
        in_specs=[pl.BlockSpec((tm, k), lambda j, i: (i, 0)),
                  pl.BlockSpec((k, tn), lambda j, i: (0, j)),
                  pl.BlockSpec((1, tn), lambda j, i: (0, j))],
        out_specs=pl.BlockSpec((tm, tn), lambda j, i: (i, j)),
        out_shape=jax.ShapeDtypeStruct((m, n), out_dtype),
        compiler_params=_params(2),
        name="matmul",
    )(a, w, col_scale.reshape(1, n).astype(F32))


def _rel_bucket(dist):
    n = jnp.maximum(dist, 0)
    max_exact = REL_BUCKETS // 2
    nf = jnp.maximum(n, max_exact).astype(F32)
    large = max_exact + (jnp.log(nf / max_exact) / math.log(REL_MAX_DISTANCE / max_exact)
                         * (REL_BUCKETS - max_exact)).astype(jnp.int32)
    return jnp.where(n < max_exact, n, jnp.minimum(large, REL_BUCKETS - 1))


def _bias_tiles(table):
    t = ATTN_TILE
    r = jnp.arange(t)[:, None]
    c = jnp.arange(t)[None, :]
    dist = jnp.stack([r - c, t + r - c])
    onehot = (_rel_bucket(dist)[..., None] == jnp.arange(REL_BUCKETS)).astype(F32)
    near = jnp.einsum("irck,kh->hirc", onehot, table.astype(F32), precision=lax.Precision.HIGHEST)
    far = jnp.broadcast_to(table[REL_BUCKETS - 1][:, None, None], (table.shape[1], 1, t))
    return near * LOG2E, far.astype(F32) * LOG2E


def _lane_halves(x, op):
    return op(x[:, :LANES], x[:, LANES:])


AttnStream = collections.namedtuple("AttnStream", "q k_rows v_rows near cfar pen_at s_scr")


def _causal_attention(streams, qi):
    t = ATTN_TILE
    own_slot = streams[0].s_scr.shape[0] - 1
    spare_slot = own_slot - 1
    prev = jnp.maximum(qi - 1, 0)
    n_far = prev
    odd = n_far % 2 == 1
    tail = jnp.where(odd, n_far - 1, 0)
    tail_slot = jnp.where(odd, n_far - 1, spare_slot)
    row = lax.broadcasted_iota(jnp.int32, (t, t), 0)
    col = lax.broadcasted_iota(jnp.int32, (t, t), 1)
    has_prev = jnp.where(qi >= 1, 0.0, NEG_INF)
    has_tail = jnp.where(odd, 0.0, NEG_INF)

    def masked(st, s, n):
        return s if st.pen_at is None else s + st.pen_at(n)

    def tile_max(m, s):
        return jnp.maximum(m, _lane_halves(s, jnp.maximum))

    def tile_sum(l, p):
        return l + _lane_halves(p, jnp.add)

    m_near, m_far = [], []
    for st in streams:
        s3 = _dot_nt(st.q, jnp.concatenate([st.k_rows(qi, 1), st.k_rows(prev, 1), st.k_rows(tail, 1)], axis=0))
        s_own = jnp.where(col <= row, s3[:, :t] + st.near(0), NEG_INF)
        s_prev = masked(st, s3[:, t:2 * t] + st.near(1) + has_prev, prev)
        s_tail = masked(st, s3[:, 2 * t:] + has_tail, tail)
        st.s_scr[own_slot] = s_own
        st.s_scr[prev] = s_prev - st.cfar
        st.s_scr[tail_slot] = s_tail
        m_near.append(tile_max(_lane_halves(s_own, jnp.maximum), s_prev))
        m_far.append(_lane_halves(s_tail, jnp.maximum))

    def pair_scores(i, m_far):
        out = []
        for st, m in zip(streams, m_far):
            s = _dot_nt(st.q, st.k_rows(2 * i, 2))
            for half in range(2):
                sh = masked(st, s[:, half * t:(half + 1) * t], 2 * i + half)
                st.s_scr[2 * i + half] = sh
                m = tile_max(m, sh)
            out.append(m)
        return tuple(out)

    m_far = lax.fori_loop(0, n_far // 2, pair_scores, tuple(m_far))

    m_past, carry = [], []
    for st, mn, mf in zip(streams, m_near, m_far):
        m_row = jnp.maximum(jnp.max(mn, axis=1, keepdims=True),
                            jnp.max(mf, axis=1, keepdims=True) + st.cfar)
        mp = m_row - st.cfar
        p_own = jnp.exp2(st.s_scr[own_slot] - m_row)
        p_prev = jnp.exp2(st.s_scr[prev] - mp)
        p_tail = jnp.exp2(st.s_scr[tail_slot] - mp)
        l_part = tile_sum(tile_sum(_lane_halves(p_own, jnp.add), p_prev), p_tail)
        acc = jnp.dot(jnp.concatenate([p_own, p_prev, p_tail], axis=1).astype(BF16),
                      jnp.concatenate([st.v_rows(qi, 1), st.v_rows(prev, 1), st.v_rows(tail, 1)], axis=0),
                      preferred_element_type=F32)
        m_past.append(mp)
        carry.append((l_part, acc))

    def pair_weights(i, carry):
        out = []
        for st, mp, (l_part, acc) in zip(streams, m_past, carry):
            p = jnp.exp2(jnp.concatenate([st.s_scr[2 * i], st.s_scr[2 * i + 1]], axis=1) - mp)
            l_part = tile_sum(tile_sum(l_part, p[:, :t]), p[:, t:])
            out.append((l_part, acc + jnp.dot(p.astype(BF16), st.v_rows(2 * i, 2), preferred_element_type=F32)))
        return tuple(out)

    carry = lax.fori_loop(0, n_far // 2, pair_weights, tuple(carry))
    return [(acc, jnp.sum(l_part, axis=1, keepdims=True)) for l_part, acc in carry]


def _block_rows(n, w=1):
    return pl.ds(pl.multiple_of(n * ATTN_TILE, ATTN_TILE), w * ATTN_TILE)


def _moba_body(q_ref, k_ref, v_ref, near_ref, far_ref, o_ref, kmean_ref, s_scr, *, n_blocks):
    qi = pl.program_id(2)
    t = ATTN_TILE
    dh = HEAD_DIM
    heads = MOBA_HEADS_PER_STEP

    @pl.when(qi == 0)
    def _():
        for j in range(heads):
            for n in range(n_blocks):
                kmean_ref[j, n:n + 1, :] = jnp.mean(k_ref[n * t:(n + 1) * t, j * dh:(j + 1) * dh].astype(F32),
                                                    axis=0, keepdims=True)

    def stream(j):
        cols = slice(j * dh, (j + 1) * dh)
        q = q_ref[:, cols]
        gate = lax.dot_general(q.astype(F32), kmean_ref[j], (((1,), (1,)), ((), ())),
                               precision=lax.Precision.HIGHEST, preferred_element_type=F32)
        blk = lax.broadcasted_iota(jnp.int32, gate.shape, 1)
        valid = blk < qi
        g = jnp.where(valid, gate, NEG_INF)
        kth = g
        for _ in range(MOBA_TOPK - 1):
            top = jnp.max(kth, axis=1, keepdims=True)
            kth = jnp.where(kth == top, NEG_INF, kth)
        third = jnp.max(kth, axis=1, keepdims=True)
        pen = jnp.where(valid & (g >= third), 0.0, NEG_INF)

        def pen_at(n):
            return jnp.max(jnp.where(blk == n, pen, NEG_INF), axis=1, keepdims=True)

        return AttnStream(q, lambda n, w: k_ref[_block_rows(n, w), cols], lambda n, w: v_ref[_block_rows(n, w), cols],
                          lambda i: near_ref[j, i], far_ref[j][:, :1], pen_at, s_scr.at[j])

    results = _causal_attention([stream(j) for j in range(heads)], qi)
    for j, (acc, l) in enumerate(results):
        o_ref[:, j * dh:(j + 1) * dh] = (acc / l).astype(o_ref.dtype)


def moba_attention(y, near, far, batch, seq):
    t = ATTN_TILE
    nq = seq // t
    hs = MOBA_HEADS_PER_STEP
    groups = MOBA_HEADS // hs
    w = hs * HEAD_DIM
    return pl.pallas_call(
        functools.partial(_moba_body, n_blocks=nq),
        grid=(batch, groups, nq),
        in_specs=[pl.BlockSpec((t, w), lambda b, g, i: (b * nq + i, g)),
                  pl.BlockSpec((seq, w), lambda b, g, i: (b, groups + g)),
                  pl.BlockSpec((seq, w), lambda b, g, i: (b, 2 * groups + g)),
                  pl.BlockSpec((hs, 2, t, t), lambda b, g, i: (g, 0, 0, 0)),
                  pl.BlockSpec((hs, 1, t), lambda b, g, i: (g, 0, 0))],
        out_specs=pl.BlockSpec((t, w), lambda b, g, i: (b * nq + i, g)),
        out_shape=jax.ShapeDtypeStruct((batch * seq, MOBA_WIDTH), BF16),
        scratch_shapes=[pltpu.VMEM((hs, nq, HEAD_DIM), F32),
                        pltpu.VMEM((hs, nq + 1, t, t), F32)],
        compiler_params=_params(3),
        name="moba_attention",
    )(y, y, y, near, far)


def _diff_body(q_ref, k_ref, v_ref, near_ref, far_ref, lq1_ref, lk1_ref, lq2_ref, lk2_ref, subln_ref, o_ref, s_scr):
    qi = pl.program_id(2)
    dh = HEAD_DIM
    cfar = far_ref[0][:, :1]
    v_rows = lambda n, w: v_ref[_block_rows(n, w), :]
    near = lambda i: near_ref[0, i]

    def stream(j):
        cols = slice(j * dh, (j + 1) * dh)
        return AttnStream(q_ref[:, cols], lambda n, w: k_ref[_block_rows(n, w), cols], v_rows, near, cfar,
                          None, s_scr.at[j])

    (acc1, l1), (acc2, l2) = _causal_attention([stream(0), stream(1)], qi)
    lam = (jnp.exp(jnp.sum(lq1_ref[...] * lk1_ref[...], axis=1, keepdims=True))
           - jnp.exp(jnp.sum(lq2_ref[...] * lk2_ref[...], axis=1, keepdims=True)) + LAMBDA_INIT)
    o = acc1 / l1 - lam * (acc2 / l2)
    o_ref[...] = (_rms(o, subln_ref[...]) * (1.0 - LAMBDA_INIT)).astype(o_ref.dtype)


def diff_attention(y, near, far, lq1, lk1, lq2, lk2, subln, batch, seq, col0):
    t = ATTN_TILE
    nq = seq // t
    h = DIFF_HEADS
    w = 2 * HEAD_DIM
    base = col0 // w
    vec = lambda a: a.reshape(1, -1).astype(F32)
    small = lambda n: pl.BlockSpec((1, n), lambda b, hh, i: (0, 0))
    return pl.pallas_call(
        _diff_body,
        grid=(batch, h, nq),
        in_specs=[pl.BlockSpec((t, w), lambda b, hh, i: (b * nq + i, base + hh)),
                  pl.BlockSpec((seq, w), lambda b, hh, i: (b, base + h + hh)),
                  pl.BlockSpec((seq, w), lambda b, hh, i: (b, base + 2 * h + hh)),
                  pl.BlockSpec((1, 2, t, t), lambda b, hh, i: (hh, 0, 0, 0)),
                  pl.BlockSpec((1, 1, t), lambda b, hh, i: (hh, 0, 0)),
                  small(HEAD_DIM), small(HEAD_DIM), small(HEAD_DIM), small(HEAD_DIM), small(w)],
        out_specs=pl.BlockSpec((t, w), lambda b, hh, i: (b * nq + i, hh)),
        out_shape=jax.ShapeDtypeStruct((batch * seq, h * w), BF16),
        scratch_shapes=[pltpu.VMEM((2, nq + 1, t, t), F32)],
        compiler_params=_params(3),
        name="diff_attention",
    )(y, y, y, near, far, vec(lq1), vec(lk1), vec(lq2), vec(lk2), vec(subln))


def _branch_body(om_ref, od_ref, wm_ref, wd_ref, ga_ref, gb_ref, o_ref):
    a = jnp.dot(om_ref[...], wm_ref[...], preferred_element_type=F32)
    b = jnp.dot(od_ref[...], wd_ref[...], preferred_element_type=F32)
    o_ref[...] = (_sigmoid(ga_ref[...].astype(F32)) * a + _sigmoid(gb_ref[...].astype(F32)) * b).astype(o_ref.dtype)


def branch_merge(o_moba, o_diff, w_m, w_d, y, gate_col0, tm=1024, tn=1024):
    m, k = o_moba.shape
    n = w_m.shape[1]
    g0 = gate_col0 // tn
    nj = n // tn
    return pl.pallas_call(
        _branch_body,
        grid=(nj, m // tm),
        in_specs=[pl.BlockSpec((tm, k), lambda j, i: (i, 0)),
                  pl.BlockSpec((tm, k), lambda j, i: (i, 0)),
                  pl.BlockSpec((k, tn), lambda j, i: (0, j)),
                  pl.BlockSpec((k, tn), lambda j, i: (0, j)),
                  pl.BlockSpec((tm, tn), lambda j, i: (i, g0 + j)),
                  pl.BlockSpec((tm, tn), lambda j, i: (i, g0 + nj + j))],
        out_specs=pl.BlockSpec((tm, tn), lambda j, i: (i, j)),
        out_shape=jax.ShapeDtypeStruct((m, n), BF16),
        compiler_params=_params(2),
        name="branch_merge",
    )(o_moba, o_diff, w_m, w_d, y, y)


def _mixout_body(a_ref, w_ref, x_ref, g_ref, x1_ref, h_ref):
    x1 = x_ref[...] + jnp.dot(a_ref[...], w_ref[...], preferred_element_type=F32)
    x1_ref[...] = x1
    h_ref[...] = _rms(x1, g_ref[...]).astype(h_ref.dtype)


def mixout(merged, w, x, gain, tm=512):
    m, k = merged.shape
    n = w.shape[1]
    return pl.pallas_call(
        _mixout_body,
        grid=(m // tm,),
        in_specs=[pl.BlockSpec((tm, k), lambda i: (i, 0)),
                  pl.BlockSpec((k, n), lambda i: (0, 0)),
                  pl.BlockSpec((tm, n), lambda i: (i, 0)),
                  pl.BlockSpec((1, n), lambda i: (0, 0))],
        out_specs=[pl.BlockSpec((tm, n), lambda i: (i, 0)),
                   pl.BlockSpec((tm, n), lambda i: (i, 0))],
        out_shape=[jax.ShapeDtypeStruct((m, n), F32), jax.ShapeDtypeStruct((m, n), BF16)],
        compiler_params=_params(1),
        name="mixout",
    )(merged, w, x, gain.reshape(1, n))


def _memkv_body(mem_ref, g_ref, wk_ref, wv_ref, k_ref, v_ref):
    mn = _rms(mem_ref[...], g_ref[...]).astype(BF16)
    k_ref[...] = jnp.dot(mn, wk_ref[...], preferred_element_type=F32).astype(k_ref.dtype)
    v_ref[...] = jnp.dot(mn, wv_ref[...], preferred_element_type=F32).astype(v_ref.dtype)


def memory_kv(mem2d, gain, w_k, w_v, rows):
    m, d = mem2d.shape
    n = w_k.shape[1]
    return pl.pallas_call(
        _memkv_body,
        grid=(m // rows,),
        in_specs=[pl.BlockSpec((rows, d), lambda i: (i, 0)),
                  pl.BlockSpec((1, d), lambda i: (0, 0)),
                  pl.BlockSpec((d, n), lambda i: (0, 0)),
                  pl.BlockSpec((d, n), lambda i: (0, 0))],
        out_specs=[pl.BlockSpec((rows, n), lambda i: (i, 0)),
                   pl.BlockSpec((rows, n), lambda i: (i, 0))],
        out_shape=[jax.ShapeDtypeStruct((m, n), BF16), jax.ShapeDtypeStruct((m, n), BF16)],
        compiler_params=_params(1),
        name="memory_kv",
    )(mem2d, gain.reshape(1, d), w_k, w_v)


def _xattn_body(h_ref, x1_ref, k_ref, v_ref, wq_ref, wo_ref, g_ref, wr_ref, br_ref,
                x2_ref, h2_ref, idx_ref, wgt_ref):
    q = jnp.dot(h_ref[...], wq_ref[...], preferred_element_type=F32).astype(BF16)
    outs = []
    for hh in range(XATTN_HEADS):
        sl = slice(hh * HEAD_DIM, (hh + 1) * HEAD_DIM)
        s = _dot_nt(q[:, sl], k_ref[:, sl]) * ATTN_SCALE
        p = jnp.exp(s - jnp.max(s, axis=1, keepdims=True))
        o = jnp.dot(p.astype(BF16), v_ref[:, sl], preferred_element_type=F32)
        outs.append((o / jnp.sum(p, axis=1, keepdims=True)).astype(BF16))
    o = jnp.concatenate(outs, axis=1)
    x2 = x1_ref[...] + jnp.dot(o, wo_ref[...], preferred_element_type=F32)
    x2_ref[...] = x2
    h2 = _rms(x2, g_ref[...])
    h2_ref[...] = h2

    h_hi = h2.astype(BF16)
    h_lo = (h2 - h_hi.astype(F32)).astype(BF16)
    w_hi = wr_ref[...].astype(BF16)
    w_lo = (wr_ref[...] - w_hi.astype(F32)).astype(BF16)
    logits = (jnp.dot(h_hi, w_hi, preferred_element_type=F32) + jnp.dot(h_lo, w_hi, preferred_element_type=F32)
              + jnp.dot(h_hi, w_lo, preferred_element_type=F32) + br_ref[...])
    lane = lax.broadcasted_iota(jnp.int32, logits.shape, 1)
    out_lane = lax.broadcasted_iota(jnp.int32, idx_ref.shape, 1)
    idx_out = jnp.zeros(idx_ref.shape, jnp.int32)
    exp_out = jnp.zeros(wgt_ref.shape, F32)
    denom = jnp.zeros((logits.shape[0], 1), F32)
    top0 = None
    for kk in range(TOP_K):
        top = jnp.max(logits, axis=1, keepdims=True)
        arg = jnp.min(jnp.where(logits == top, lane, N_EXPERTS), axis=1, keepdims=True)
        logits = jnp.where(lane == arg, NEG_INF, logits)
        top0 = top if top0 is None else top0
        e = jnp.exp(top - top0)
        denom = denom + e
        idx_out = jnp.where(out_lane == kk, arg, idx_out)
        exp_out = jnp.where(out_lane == kk, e, exp_out)
    idx_ref[...] = idx_out
    wgt_ref[...] = exp_out / denom


def cross_attention_router(hx, x1, k_mem, v_mem, w_q, w_o, gain, w_router, b_router, seq, tm=512):
    m, d = hx.shape
    mem_len = k_mem.shape[0] // (m // seq)
    n = w_q.shape[1]
    per_b = seq // tm
    const = lambda shape: pl.BlockSpec(shape, lambda i: (0,) * len(shape))
    rows = lambda cols: pl.BlockSpec((tm, cols), lambda i: (i, 0))
    return pl.pallas_call(
        _xattn_body,
        grid=(m // tm,),
        in_specs=[rows(d), rows(d),
                  pl.BlockSpec((mem_len, n), lambda i: (i // per_b, 0)),
                  pl.BlockSpec((mem_len, n), lambda i: (i // per_b, 0)),
                  const((d, n)), const((n, d)), const((1, d)), const((d, N_EXPERTS)), const((1, N_EXPERTS))],
        out_specs=[rows(d), rows(d), rows(LANES), rows(LANES)],
        out_shape=[jax.ShapeDtypeStruct((m, d), F32), jax.ShapeDtypeStruct((m, d), F32),
                   jax.ShapeDtypeStruct((m, LANES), jnp.int32), jax.ShapeDtypeStruct((m, LANES), F32)],
        compiler_params=_params(1),
        name="cross_attention_router",
    )(hx, x1, k_mem, v_mem, w_q, w_o, gain.reshape(1, d), w_router.astype(F32), b_router.reshape(1, -1).astype(F32))


def _routing_plan(top_idx, n_tokens):
    rt = EXPERT_ROW_TILE
    tiles_per_unit = EXPERT_UNIT_ROWS // rt
    slot_onehot = (top_idx[:, :, None] == jnp.arange(N_EXPERTS)[None, None, :]).astype(jnp.int32)
    onehot = slot_onehot.sum(axis=1)
    before = jnp.cumsum(onehot, axis=0) - onehot
    count = onehot.sum(axis=0)
    tiles = (count + rt - 1) // rt
    tile_start = jnp.cumsum(tiles) - tiles
    pos = (slot_onehot * (tile_start * rt + before)[:, None, :]).sum(axis=-1)

    units = (tiles + tiles_per_unit - 1) // tiles_per_unit
    unit_first = jnp.cumsum(units) - units
    n_units = units.sum()
    max_units = N_EXPERTS + (n_tokens * TOP_K) // EXPERT_UNIT_ROWS
    uid = jnp.arange(max_units)
    e_of = jnp.clip(jnp.searchsorted(jnp.cumsum(units), uid, side="right"), 0, N_EXPERTS - 1)
    k_in = uid - unit_first[e_of]
    live = uid < n_units
    last_e = e_of[jnp.maximum(n_units - 1, 0)]
    unit_expert = jnp.where(live, e_of, last_e).astype(jnp.int32)
    unit_start = jnp.where(live, (tile_start[e_of] + k_in * tiles_per_unit) * rt, 0).astype(jnp.int32)
    unit_tiles = jnp.where(live, jnp.minimum(tiles[e_of] - k_in * tiles_per_unit, tiles_per_unit), 0).astype(jnp.int32)
    totals = jnp.stack([n_units, tiles.sum()]).astype(jnp.int32)
    return pos.astype(jnp.int32), unit_expert, unit_start, unit_tiles, totals


def _dispatch_body(pos_hbm, h_ref, xs_in, xs_hbm, pos_smem, sem):
    del xs_in
    i = pl.program_id(0)
    groups, sub, _ = h_ref.shape
    idx_copy = pltpu.make_async_copy(pos_hbm.at[i], pos_smem, sem.at[0])
    idx_copy.start()
    idx_copy.wait()

    def send(g, carry):
        for s in range(sub):
            for kk in range(TOP_K):
                r = pos_smem[g * (sub * TOP_K) + s * TOP_K + kk]
                pltpu.make_async_copy(h_ref.at[g, pl.ds(s, 1), :], xs_hbm.at[pl.ds(r, 1), :], sem.at[1]).start()
        return carry

    lax.fori_loop(0, groups, send, 0)

    def drain(g, carry):
        pltpu.make_async_copy(h_ref.at[0], xs_hbm.at[pl.ds(0, sub), :], sem.at[1]).wait()
        return carry

    lax.fori_loop(0, groups * TOP_K, drain, 0)


def dispatch_rows(h2, pos, p_rows):
    t, d = h2.shape
    tm = GATHER_TOKENS
    sub = 8
    return pl.pallas_call(
        _dispatch_body,
        grid=(t // tm,),
        in_specs=[pl.BlockSpec(memory_space=pl.ANY),
                  pl.BlockSpec((tm // sub, sub, d), lambda i: (i, 0, 0)),
                  pl.BlockSpec(memory_space=pl.ANY)],
        out_specs=pl.BlockSpec(memory_space=pl.ANY),
        out_shape=jax.ShapeDtypeStruct((p_rows, d), h2.dtype),
        scratch_shapes=[pltpu.SMEM((tm * TOP_K,), jnp.int32), pltpu.SemaphoreType.DMA((2,))],
        input_output_aliases={2: 0},
        compiler_params=_params(1),
        name="dispatch_rows",
    )(pos.reshape(t // tm, tm * TOP_K), h2.reshape(t // sub, sub, d), jnp.zeros((p_rows, d), h2.dtype))


def _expert_body(ue_ref, us_ref, un_ref, nu_ref,
                 x_hbm, wgu_ref, bgu_ref, wd_ref, bd_ref, y_hbm,
                 xbuf, actbuf, gubuf, ystage, wgu_bf, wd_f32, wd_bf, sem_x, sem_y, *, n_up, n_down):
    u = pl.program_id(0)
    c = pl.program_id(1)
    rt = EXPERT_ROW_TILE
    chunk = EXPERT_COL_CHUNK
    half = chunk // 2
    quarter = half // 2
    n_live = nu_ref[0]
    live = u < n_live
    start = us_ref[u]
    n_tiles = un_ref[u]

    def tile_rows(j):
        return pl.ds(pl.multiple_of(j * rt, rt), rt)

    def x_copy(unit, j):
        rows = pl.ds(pl.multiple_of(us_ref[unit] + j * rt, rt), rt)
        return pltpu.make_async_copy(x_hbm.at[rows, :], xbuf.at[tile_rows(j), :], sem_x.at[0])

    def fetch_rows(unit):
        def body(j, carry):
            x_copy(unit, j).start()
            return carry
        lax.fori_loop(0, un_ref[unit], body, 0)

    @pl.when((u == 0) & (c == 0) & live)
    def _():
        fetch_rows(0)

    @pl.when(live & (c == 0))
    def _():
        def body(j, carry):
            x_copy(u, j).wait()
            return carry
        lax.fori_loop(0, n_tiles, body, 0)

    @pl.when((c == n_up) & (u + 1 < n_live))
    def _():
        fetch_rows(u + 1)

    @pl.when(live & (c < n_up))
    def _():
        wgu_bf[...] = wgu_ref[0].astype(BF16)
        bias = bgu_ref[0]
        even = (lax.broadcasted_iota(jnp.int32, (rt, half), 1) % 2) == 0

        def project(j):
            x = xbuf[tile_rows(j), :].astype(BF16)
            gubuf[j % 2] = jnp.dot(x, wgu_bf[...], preferred_element_type=F32) + bias

        def activate(j):
            gu = gubuf[j % 2]
            lo = gu[:, :half]
            hi = gu[:, half:]
            gate = jnp.where(even, lo, pltpu.roll(hi, 1, axis=1))
            up = jnp.where(even, pltpu.roll(lo, half - 1, axis=1), hi)
            gate = jnp.minimum(gate, SWIGLU_LIMIT)
            up = jnp.clip(up, -SWIGLU_LIMIT, SWIGLU_LIMIT)
            act = (up + 1.0) * gate * _sigmoid(SWIGLU_ALPHA * gate)
            actbuf[c, tile_rows(j), :] = act.astype(BF16)

        project(0)

        def body(j, carry):
            activate(j - 1)
            project(j)
            return carry

        lax.fori_loop(1, n_tiles, body, 0)
        activate(n_tiles - 1)

    @pl.when(live & (c >= n_up))
    def _():
        cd = c - n_up
        for g in range(chunk // LANES):
            lanes = slice(g * LANES, (g + 1) * LANES)
            for f in range(n_up):
                base = f * half
                wd_f32[g, pl.ds(base, quarter, stride=2), :] = wd_ref[0, base:base + quarter, lanes]
                wd_f32[g, pl.ds(base + 1, quarter, stride=2), :] = wd_ref[0, base + quarter:base + half, lanes]
            wd_bf[:, lanes] = wd_f32[g].astype(BF16)
        bias = bd_ref[0]

        def y_copy(j, slot):
            rows = pl.ds(pl.multiple_of(start + j * rt, rt), rt)
            cols = pl.ds(pl.multiple_of(cd * chunk, chunk), chunk)
            return pltpu.make_async_copy(ystage.at[slot], y_hbm.at[rows, cols], sem_y.at[slot])

        def tile(j, carry):
            slot = j % 2

            @pl.when(j >= 2)
            def _():
                y_copy(j - 2, slot).wait()

            act = jnp.concatenate([actbuf[f, tile_rows(j), :] for f in range(n_up)], axis=1)
            ystage[slot] = jnp.dot(act, wd_bf[...], preferred_element_type=F32) + bias
            y_copy(j, slot).start()
            return carry

        lax.fori_loop(0, n_tiles, tile, 0)

        @pl.when(n_tiles >= 2)
        def _():
            y_copy(n_tiles - 2, n_tiles % 2).wait()

        y_copy(n_tiles - 1, (n_tiles - 1) % 2).wait()

    @pl.when((u == pl.num_programs(0) - 1) & (c == n_up + n_down - 1))
    def _():
        zero_rows = xbuf.at[:rt, :]
        zero_rows[...] = jnp.zeros(zero_rows.shape, F32)
        used_tiles = nu_ref[1]

        def pad_copy(j):
            return pltpu.make_async_copy(zero_rows, y_hbm.at[pl.ds(pl.multiple_of(j * rt, rt), rt), :], sem_x.at[0])

        def pad_start(j, carry):
            pad_copy(j).start()
            return carry

        def pad_wait(j, carry):
            pad_copy(j).wait()
            return carry

        lax.fori_loop(used_tiles, y_hbm.shape[0] // rt, pad_start, 0)
        lax.fori_loop(used_tiles, y_hbm.shape[0] // rt, pad_wait, 0)


def expert_ffn(x_sorted, w_gate_up, b_gate_up, w_down, b_down, unit_expert, unit_start, unit_tiles, totals):
    p_rows, d = x_sorted.shape
    n_exp, _, two_ff = w_gate_up.shape
    d_ff = two_ff // 2
    chunk = EXPERT_COL_CHUNK
    n_up = two_ff // chunk
    n_down = d // chunk
    n_steps = n_up + n_down
    max_units = unit_expert.shape[0]
    rt = EXPERT_ROW_TILE

    def up_idx(u, c, nu):
        return jnp.where(u < nu[0], jnp.minimum(c, n_up - 1), n_up - 1)

    def down_idx(u, c, nu):
        return jnp.where(u < nu[0], jnp.maximum(c - n_up, 0), n_down - 1)

    grid_spec = pltpu.PrefetchScalarGridSpec(
        num_scalar_prefetch=4,
        grid=(max_units, n_steps),
        in_specs=[pl.BlockSpec(memory_space=pl.ANY),
                  pl.BlockSpec((1, d, chunk), lambda u, c, ue, us, un, nu: (ue[u], 0, up_idx(u, c, nu))),
                  pl.BlockSpec((1, 1, chunk), lambda u, c, ue, us, un, nu: (ue[u], 0, up_idx(u, c, nu))),
                  pl.BlockSpec((1, d_ff, chunk), lambda u, c, ue, us, un, nu: (ue[u], 0, down_idx(u, c, nu))),
                  pl.BlockSpec((1, 1, chunk), lambda u, c, ue, us, un, nu: (ue[u], 0, down_idx(u, c, nu)))],
        out_specs=pl.BlockSpec(memory_space=pl.ANY),
        scratch_shapes=[pltpu.VMEM((EXPERT_UNIT_ROWS, d), F32),
                        pltpu.VMEM((n_up, EXPERT_UNIT_ROWS, chunk // 2), BF16),
                        pltpu.VMEM((2, rt, chunk), F32),
                        pltpu.VMEM((2, rt, chunk), F32),
                        pltpu.VMEM((d, chunk), BF16),
                        pltpu.VMEM((chunk // LANES, d_ff, LANES), F32),
                        pltpu.VMEM((d_ff, chunk), BF16),
                        pltpu.SemaphoreType.DMA((1,)),
                        pltpu.SemaphoreType.DMA((2,))],
    )
    return pl.pallas_call(
        functools.partial(_expert_body, n_up=n_up, n_down=n_down),
        grid_spec=grid_spec,
        out_shape=jax.ShapeDtypeStruct((p_rows, d), F32),
        compiler_params=_params(2),
        name="expert_ffn",
    )(unit_expert, unit_start, unit_tiles, totals,
      x_sorted, w_gate_up, b_gate_up.reshape(n_exp, 1, two_ff), w_down, b_down.reshape(n_exp, 1, d))


def _combine_body(pos_hbm, x2_ref, w_ref, g_ref, y_hbm, o_ref, pos_smem, ybuf, sem, *, final_norm):
    i = pl.program_id(0)
    n = pl.num_programs(0)
    groups, sub, _ = x2_ref.shape
    slot = i % 2

    def fetch(step, into):
        per_step = groups * sub * TOP_K
        base = pl.multiple_of(into * per_step, per_step)
        idx_copy = pltpu.make_async_copy(pos_hbm.at[step], pos_smem.at[pl.ds(base, per_step)], sem.at[2])
        idx_copy.start()
        idx_copy.wait()

        def recv(g, carry):
            for s in range(sub):
                for kk in range(TOP_K):
                    r = pos_smem[base + g * (sub * TOP_K) + s * TOP_K + kk]
                    pltpu.make_async_copy(y_hbm.at[pl.ds(r, 1), :], ybuf.at[into, kk, g, pl.ds(s, 1), :],
                                          sem.at[into]).start()
            return carry

        lax.fori_loop(0, groups, recv, 0)

    @pl.when(i == 0)
    def _():
        fetch(0, 0)

    @pl.when(i + 1 < n)
    def _():
        fetch(i + 1, 1 - slot)

    def drain(g, carry):
        pltpu.make_async_copy(y_hbm.at[pl.ds(0, sub), :], ybuf.at[slot, 0, 0], sem.at[slot]).wait()
        return carry

    lax.fori_loop(0, groups * TOP_K, drain, 0)

    x3 = x2_ref[...]
    for kk in range(TOP_K):
        x3 = x3 + w_ref[:, :, kk:kk + 1] * ybuf[slot, kk]
    o_ref[...] = _rms(x3, g_ref[...]) if final_norm else x3


def combine(x2, y_sorted, pos, weights, gain):
    m, d = x2.shape
    tm = GATHER_TOKENS
    sub = 8
    final_norm = gain is not None
    gain = gain if final_norm else jnp.ones((d,), F32)
    rows = lambda width: pl.BlockSpec((tm // sub, sub, width), lambda i: (i, 0, 0))
    out = pl.pallas_call(
        functools.partial(_combine_body, final_norm=final_norm),
        grid=(m // tm,),
        in_specs=[pl.BlockSpec(memory_space=pl.ANY),
                  rows(d), rows(LANES),
                  pl.BlockSpec((1, d), lambda i: (0, 0)),
                  pl.BlockSpec(memory_space=pl.ANY)],
        out_specs=rows(d),
        out_shape=jax.ShapeDtypeStruct((m // sub, sub, d), F32),
        scratch_shapes=[pltpu.SMEM((2 * tm * TOP_K,), jnp.int32),
                        pltpu.VMEM((2, TOP_K, tm // sub, sub, d), F32),
                        pltpu.SemaphoreType.DMA((3,))],
        compiler_params=_params(1),
        name="combine",
    )(pos.reshape(m // tm, tm * TOP_K), x2.reshape(m // sub, sub, d), weights.reshape(m // sub, sub, LANES),
      gain.reshape(1, d), y_sorted)
    return out.reshape(m, d)


def kernel(x, mem, rel_bias_table, mix_norm, w_in, diff_lambda_q1, diff_lambda_k1, diff_lambda_q2, diff_lambda_k2, diff_subln, w_branch_moba, w_branch_diff, w_mix_out, xattn_norm, mem_norm, w_xq, w_xk, w_xv, w_xo, ffn_norm, w_router, b_router, w_gate_up, b_gate_up, w_down, b_down, final_norm):
    batch, seq, d = x.shape
    n_tok = batch * seq
    x2d = x.reshape(n_tok, d)
    near, far = _bias_tiles(rel_bias_table)
    diff_col0 = 3 * MOBA_WIDTH
    gate_col0 = diff_col0 + 3 * DIFF_WIDTH
    cols = jnp.arange(w_in.shape[2])
    is_q = (cols < MOBA_WIDTH) | ((cols >= diff_col0) & (cols < diff_col0 + DIFF_WIDTH))
    col_scale = jnp.where(is_q, ATTN_SCALE * LOG2E, 1.0)
    p_rows = n_tok * TOP_K + N_EXPERTS * EXPERT_ROW_TILE
    for l in range(w_in.shape[0]):
        h = rmsnorm_rows(x2d, mix_norm[l])
        y = matmul_colscale(h, w_in[l].astype(BF16), col_scale)
        o_moba = moba_attention(y, near[:MOBA_HEADS], far[:MOBA_HEADS], batch, seq)
        o_diff = diff_attention(y, near[MOBA_HEADS:], far[MOBA_HEADS:], diff_lambda_q1[l], diff_lambda_k1[l],
                                diff_lambda_q2[l], diff_lambda_k2[l], diff_subln[l], batch, seq, diff_col0)
        merged = branch_merge(o_moba, o_diff, w_branch_moba[l].astype(BF16), w_branch_diff[l].astype(BF16),
                              y, gate_col0)
        x1, hx = mixout(merged, w_mix_out[l].astype(BF16), x2d, xattn_norm[l])
        k_mem, v_mem = memory_kv(mem.reshape(-1, d), mem_norm[l], w_xk[l].astype(BF16), w_xv[l].astype(BF16),
                                 mem.shape[1])
        x2, h2, idx_pad, wgt_pad = cross_attention_router(hx, x1, k_mem, v_mem, w_xq[l].astype(BF16),
                                                          w_xo[l].astype(BF16), ffn_norm[l], w_router[l],
                                                          b_router[l], seq)
        pos, unit_expert, unit_start, unit_tiles, totals = _routing_plan(idx_pad[:, :TOP_K], n_tok)
        x_sorted = dispatch_rows(h2, pos, p_rows)
        y_sorted = expert_ffn(x_sorted, w_gate_up[l], b_gate_up[l], w_down[l], b_down[l],
                              unit_expert, unit_start, unit_tiles, totals)
        last = l == w_in.shape[0] - 1
        x2d = combine(x2, y_sorted, pos, wgt_pad, final_norm if last else None)
    return x2d.reshape(batch, seq, d)
```

```python
import collections
import functools
import math

import jax
import jax.numpy as jnp
from jax import lax
from jax.experimental import pallas as pl
from jax.experimental.pallas import tpu as pltpu

F32 = jnp.float32
BF16 = jnp.bfloat16
NEG_INF = float("-inf")

D_MODEL = 2048
HEAD_DIM = 128
MOBA_HEADS = 8
MOBA_WIDTH = MOBA_HEADS * HEAD_DIM
MOBA_BLOCK = 256
MOBA_TOPK = 3
DIFF_HEADS = 4
DIFF_WIDTH = DIFF_HEADS * 2 * HEAD_DIM
REL_BUCKETS = 32
REL_MAX_DISTANCE = 128
XATTN_HEADS = 4
N_EXPERTS = 32
TOP_K = 4
SWIGLU_LIMIT = 7.0
SWIGLU_ALPHA = 1.702
NORM_EPS = 1e-5
LAMBDA_INIT = 0.8 - 0.6 * math.exp(-0.3 * 0)
ATTN_SCALE = HEAD_DIM ** -0.5
LOG2E = math.log2(math.e)

ATTN_TILE = MOBA_BLOCK
MOBA_HEADS_PER_STEP = 4
DIFF_HEADS_PER_STEP = 2
LANES = 128
EXPERT_ROW_TILE = 256
EXPERT_UNIT_ROWS = 1536
EXPERT_COL_CHUNK = 512
GATHER_TOKENS = 256
VMEM_LIMIT = 56 * 1024 * 1024


def _params(n_axes):
    return pltpu.CompilerParams(dimension_semantics=("arbitrary",) * n_axes,
                                vmem_limit_bytes=VMEM_LIMIT)


def _rms(x, gain):
    return x * lax.rsqrt(jnp.mean(x * x, axis=-1, keepdims=True) + NORM_EPS) * gain


def _sigmoid(x):
    return 1.0 / (1.0 + jnp.exp(-x))


def _dot_nt(a, b):
    return lax.dot_general(a, b, (((1,), (1,)), ((), ())), preferred_element_type=F32)


def _rmsnorm_body(x_ref, g_ref, o_ref):
    o_ref[...] = _rms(x_ref[...], g_ref[...]).astype(o_ref.dtype)


def rmsnorm_rows(x, gain, tm=512):
    t, d = x.shape
    return pl.pallas_call(
        _rmsnorm_body,
        grid=(t // tm,),
        in_specs=[pl.BlockSpec((tm, d), lambda i: (i, 0)),
                  pl.BlockSpec((1, d), lambda i: (0, 0))],
        out_specs=pl.BlockSpec((tm, d), lambda i: (i, 0)),
        out_shape=jax.ShapeDtypeStruct((t, d), BF16),
        compiler_params=_params(1),
        name="rmsnorm_rows",
    )(x, gain.reshape(1, d))


def _matmul_body(a_ref, w_ref, cs_ref, o_ref, w_bf):
    @pl.when(pl.program_id(1) == 0)
    def _():
        w_bf[...] = w_ref[...].astype(BF16)

    acc = jnp.dot(a_ref[...], w_bf[...], preferred_element_type=F32)
    o_ref[...] = (acc * cs_ref[...]).astype(o_ref.dtype)


def matmul_colscale(a, w, col_scale, tm=1024, tn=1024, out_dtype=BF16):
    m, k = a.shape
    n = w.shape[1]
    tm, tn = min(tm, m), min(tn, n)
    return pl.pallas_call(
        _matmul_body,
        grid=(n // tn, m // tm),
        in_specs=[pl.BlockSpec((tm, k), lambda j, i: (i, 0)),
                  pl.BlockSpec((k, tn), lambda j, i: (0, j)),
                  pl.BlockSpec((1, tn), lambda j, i: (0, j))],
        out_specs=pl.BlockSpec((tm, tn), lambda j, i: (i, j)),
        out_shape=jax.ShapeDtypeStruct((m, n), out_dtype),
        scratch_shapes=[pltpu.VMEM((k, tn), BF16)],
        compiler_params=_params(2),
        name="matmul",
    )(a, w, col_scale.reshape(1, n).astype(F32))


def _rel_bucket(dist):
    n = jnp.maximum(dist, 0)
    max_exact = REL_BUCKETS // 2
    nf = jnp.maximum(n, max_exact).astype(F32)
    large = max_exact + (jnp.log(nf / max_exact) / math.log(REL_MAX_DISTANCE / max_exact)
                         * (REL_BUCKETS - max_exact)).astype(jnp.int32)
    return jnp.where(n < max_exact, n, jnp.minimum(large, REL_BUCKETS - 1))


def _bias_tiles(table):
    t = ATTN_TILE
    r = jnp.arange(t)[:, None]
    c = jnp.arange(t)[None, :]
    dist = jnp.stack([r - c, t + r - c])
    onehot = (_rel_bucket(dist)[..., None] == jnp.arange(REL_BUCKETS)).astype(F32)
    near = jnp.einsum("irck,kh->hirc", onehot, table.astype(F32), precision=lax.Precision.HIGHEST)
    far = jnp.broadcast_to(table[REL_BUCKETS - 1][:, None, None], (table.shape[1], 1, t))
    return near * LOG2E, far.astype(F32) * LOG2E


def _lane_halves(x, op):
    return op(x[:, :LANES], x[:, LANES:])


AttnStream = collections.namedtuple("AttnStream", "q k_rows v_rows near cfar pen_at s_scr acc lsum mpast")


def _causal_attention(streams, qi):
    t = ATTN_TILE
    own_slot = streams[0].s_scr.shape[0] - 2
    prev_slot = own_slot + 1
    prev = jnp.maximum(qi - 1, 0)
    n_far = prev
    n_pairs = (n_far + 1) // 2
    row = lax.broadcasted_iota(jnp.int32, (t, t), 0)
    col = lax.broadcasted_iota(jnp.int32, (t, t), 1)
    has_prev = jnp.where(qi >= 1, 0.0, NEG_INF)

    def masked(st, s, n):
        return s if st.pen_at is None else s + st.pen_at(n)

    def tile_max(m, s):
        return jnp.maximum(m, _lane_halves(s, jnp.maximum))

    def tile_sum(l, p):
        return l + _lane_halves(p, jnp.add)

    m_near = []
    for st in streams:
        s2 = _dot_nt(st.q, jnp.concatenate([st.k_rows(qi, 1), st.k_rows(prev, 1)], axis=0))
        s_own = jnp.where(col <= row, s2[:, :t] + st.near(0), NEG_INF)
        s_prev = masked(st, s2[:, t:] + st.near(1) + has_prev, prev)
        st.s_scr[own_slot] = s_own
        st.s_scr[prev_slot] = s_prev - st.cfar
        m_near.append(tile_max(_lane_halves(s_own, jnp.maximum), s_prev))

    def pair_scores(i, m_far):
        second_is_far = jnp.where(2 * i + 1 < n_far, 0.0, NEG_INF)
        out = []
        for st, m in zip(streams, m_far):
            s = _dot_nt(st.q, st.k_rows(2 * i, 2))
            for half in range(2):
                sh = masked(st, s[:, half * t:(half + 1) * t], 2 * i + half)
                if half == 1:
                    sh = sh + second_is_far
                st.s_scr[2 * i + half] = sh
                m = tile_max(m, sh)
            out.append(m)
        return tuple(out)

    m_far = lax.fori_loop(0, n_pairs, pair_scores, tuple(jnp.full((t, LANES), NEG_INF, F32) for _ in streams))

    for st, mn, mf in zip(streams, m_near, m_far):
        m_row = jnp.maximum(jnp.max(mn, axis=1, keepdims=True),
                            jnp.max(mf, axis=1, keepdims=True) + st.cfar)
        mp = m_row - st.cfar
        p_own = jnp.exp2(st.s_scr[own_slot] - m_row)
        p_prev = jnp.exp2(st.s_scr[prev_slot] - mp)
        st.mpast[...] = jnp.broadcast_to(mp, st.mpast.shape)
        st.lsum[...] = tile_sum(_lane_halves(p_own, jnp.add), p_prev)
        st.acc[...] = jnp.dot(jnp.concatenate([p_own, p_prev], axis=1).astype(BF16),
                              jnp.concatenate([st.v_rows(qi, 1), st.v_rows(prev, 1)], axis=0),
                              preferred_element_type=F32)

    def pair_weights(i, carry):
        for st in streams:
            mp = jnp.concatenate([st.mpast[...]] * (2 * t // LANES), axis=1)
            p = jnp.exp2(jnp.concatenate([st.s_scr[2 * i], st.s_scr[2 * i + 1]], axis=1) - mp)
            st.lsum[...] = tile_sum(tile_sum(st.lsum[...], p[:, :t]), p[:, t:])
            st.acc[...] += jnp.dot(p.astype(BF16), st.v_rows(2 * i, 2), preferred_element_type=F32)
        return carry

    lax.fori_loop(0, n_pairs, pair_weights, 0)
    return [(st.acc[...], jnp.sum(st.lsum[...], axis=1, keepdims=True)) for st in streams]


def _block_rows(n, w=1):
    return pl.ds(pl.multiple_of(n * ATTN_TILE, ATTN_TILE), w * ATTN_TILE)


def _moba_body(q_ref, k_ref, v_ref, near_ref, far_ref, o_ref, kmean_ref, s_scr, acc_scr, lsum_scr, mpast_scr, *,
               n_blocks):
    qi = pl.program_id(2)
    t = ATTN_TILE
    dh = HEAD_DIM
    heads = MOBA_HEADS_PER_STEP

    @pl.when(qi == 0)
    def _():
        for j in range(heads):
            for n in range(n_blocks):
                kmean_ref[j, n:n + 1, :] = jnp.mean(k_ref[n * t:(n + 1) * t, j * dh:(j + 1) * dh].astype(F32),
                                                    axis=0, keepdims=True)

    def stream(j):
        cols = slice(j * dh, (j + 1) * dh)
        q = q_ref[:, cols]
        gate = lax.dot_general(kmean_ref[j], q.astype(F32), (((1,), (1,)), ((), ())),
                               precision=lax.Precision.HIGHEST, preferred_element_type=F32)
        valid = lax.broadcasted_iota(jnp.int32, gate.shape, 0) < qi
        g = jnp.where(valid, gate, NEG_INF)
        kth = g
        for _ in range(MOBA_TOPK - 1):
            top = jnp.max(kth, axis=0, keepdims=True)
            kth = jnp.where(kth == top, NEG_INF, kth)
        third = jnp.max(kth, axis=0, keepdims=True)
        pen_t = jnp.where(valid & (g >= third), 0.0, NEG_INF)
        pen_t = jnp.concatenate([pen_t, jnp.full((LANES - n_blocks, t), NEG_INF, F32)], axis=0)
        pen = pen_t.T
        blk = lax.broadcasted_iota(jnp.int32, pen.shape, 1)

        def pen_at(n):
            return jnp.max(jnp.where(blk == n, pen, NEG_INF), axis=1, keepdims=True)

        return AttnStream(q, lambda n, w: k_ref[_block_rows(n, w), cols], lambda n, w: v_ref[_block_rows(n, w), cols],
                          lambda i: near_ref[j, i], far_ref[j][:, :1], pen_at, s_scr.at[j],
                          acc_scr.at[j], lsum_scr.at[j], mpast_scr.at[j])

    results = _causal_attention([stream(j) for j in range(heads)], qi)
    for j, (acc, l) in enumerate(results):
        o_ref[:, j * dh:(j + 1) * dh] = (acc / l).astype(o_ref.dtype)


def moba_attention(y, near, far, batch, seq):
    t = ATTN_TILE
    nq = seq // t
    hs = MOBA_HEADS_PER_STEP
    groups = MOBA_HEADS // hs
    w = hs * HEAD_DIM
    return pl.pallas_call(
        functools.partial(_moba_body, n_blocks=nq),
        grid=(batch, groups, nq),
        in_specs=[pl.BlockSpec((t, w), lambda b, g, i: (b * nq + i, g)),
                  pl.BlockSpec((seq, w), lambda b, g, i: (b, groups + g)),
                  pl.BlockSpec((seq, w), lambda b, g, i: (b, 2 * groups + g)),
                  pl.BlockSpec((hs, 2, t, t), lambda b, g, i: (g, 0, 0, 0)),
                  pl.BlockSpec((hs, 1, t), lambda b, g, i: (g, 0, 0))],
        out_specs=pl.BlockSpec((t, w), lambda b, g, i: (b * nq + i, g)),
        out_shape=jax.ShapeDtypeStruct((batch * seq, MOBA_WIDTH), BF16),
        scratch_shapes=[pltpu.VMEM((hs, nq, HEAD_DIM), F32),
                        pltpu.VMEM((hs, nq + 2, t, t), F32),
                        pltpu.VMEM((hs, t, HEAD_DIM), F32),
                        pltpu.VMEM((hs, t, LANES), F32),
                        pltpu.VMEM((hs, t, LANES), F32)],
        compiler_params=_params(3),
        name="moba_attention",
    )(y, y, y, near, far)


def _diff_body(q_ref, k_ref, v_ref, near_ref, far_ref, lq1_ref, lk1_ref, lq2_ref, lk2_ref, subln_ref, o_ref,
               s_scr, acc_scr, lsum_scr, mpast_scr):
    qi = pl.program_id(2)
    dh = HEAD_DIM
    w = 2 * dh
    heads = DIFF_HEADS_PER_STEP

    def stream(hh, j):
        qk_cols = slice(hh * w + j * dh, hh * w + (j + 1) * dh)
        v_cols = slice(hh * w, (hh + 1) * w)
        return AttnStream(q_ref[:, qk_cols], lambda n, nb: k_ref[_block_rows(n, nb), qk_cols],
                          lambda n, nb: v_ref[_block_rows(n, nb), v_cols], lambda i: near_ref[hh, i],
                          far_ref[hh][:, :1], None, s_scr.at[2 * hh + j],
                          acc_scr.at[2 * hh + j], lsum_scr.at[2 * hh + j], mpast_scr.at[2 * hh + j])

    results = _causal_attention([stream(hh, j) for hh in range(heads) for j in range(2)], qi)
    lam = (jnp.exp(jnp.sum(lq1_ref[...] * lk1_ref[...], axis=1, keepdims=True))
           - jnp.exp(jnp.sum(lq2_ref[...] * lk2_ref[...], axis=1, keepdims=True)) + LAMBDA_INIT)
    for hh in range(heads):
        (acc1, l1), (acc2, l2) = results[2 * hh], results[2 * hh + 1]
        o = acc1 / l1 - lam * (acc2 / l2)
        o_ref[:, hh * w:(hh + 1) * w] = (_rms(o, subln_ref[...]) * (1.0 - LAMBDA_INIT)).astype(o_ref.dtype)


def diff_attention(y, near, far, lq1, lk1, lq2, lk2, subln, batch, seq, col0):
    t = ATTN_TILE
    nq = seq // t
    hs = DIFF_HEADS_PER_STEP
    groups = DIFF_HEADS // hs
    w = hs * 2 * HEAD_DIM
    base = col0 // w
    vec = lambda a: a.reshape(1, -1).astype(F32)
    small = lambda n: pl.BlockSpec((1, n), lambda b, g, i: (0, 0))
    return pl.pallas_call(
        _diff_body,
        grid=(batch, groups, nq),
        in_specs=[pl.BlockSpec((t, w), lambda b, g, i: (b * nq + i, base + g)),
                  pl.BlockSpec((seq, w), lambda b, g, i: (b, base + groups + g)),
                  pl.BlockSpec((seq, w), lambda b, g, i: (b, base + 2 * groups + g)),
                  pl.BlockSpec((hs, 2, t, t), lambda b, g, i: (g, 0, 0, 0)),
                  pl.BlockSpec((hs, 1, t), lambda b, g, i: (g, 0, 0)),
                  small(HEAD_DIM), small(HEAD_DIM), small(HEAD_DIM), small(HEAD_DIM), small(2 * HEAD_DIM)],
        out_specs=pl.BlockSpec((t, w), lambda b, g, i: (b * nq + i, g)),
        out_shape=jax.ShapeDtypeStruct((batch * seq, DIFF_WIDTH), BF16),
        scratch_shapes=[pltpu.VMEM((2 * hs, nq + 2, t, t), F32),
                        pltpu.VMEM((2 * hs, t, 2 * HEAD_DIM), F32),
                        pltpu.VMEM((2 * hs, t, LANES), F32),
                        pltpu.VMEM((2 * hs, t, LANES), F32)],
        compiler_params=_params(3),
        name="diff_attention",
    )(y, y, y, near, far, vec(lq1), vec(lk1), vec(lq2), vec(lk2), vec(subln))


def _branch_body(om_ref, od_ref, wm_ref, wd_ref, ga_ref, gb_ref, o_ref):
    a = jnp.dot(om_ref[...], wm_ref[...], preferred_element_type=F32)
    b = jnp.dot(od_ref[...], wd_ref[...], preferred_element_type=F32)
    o_ref[...] = (_sigmoid(ga_ref[...].astype(F32)) * a + _sigmoid(gb_ref[...].astype(F32)) * b).astype(o_ref.dtype)


def branch_merge(o_moba, o_diff, w_m, w_d, y, gate_col0, tm=1024, tn=1024):
    m, k = o_moba.shape
    n = w_m.shape[1]
    g0 = gate_col0 // tn
    nj = n // tn
    return pl.pallas_call(
        _branch_body,
        grid=(nj, m // tm),
        in_specs=[pl.BlockSpec((tm, k), lambda j, i: (i, 0)),
                  pl.BlockSpec((tm, k), lambda j, i: (i, 0)),
                  pl.BlockSpec((k, tn), lambda j, i: (0, j)),
                  pl.BlockSpec((k, tn), lambda j, i: (0, j)),
                  pl.BlockSpec((tm, tn), lambda j, i: (i, g0 + j)),
                  pl.BlockSpec((tm, tn), lambda j, i: (i, g0 + nj + j))],
        out_specs=pl.BlockSpec((tm, tn), lambda j, i: (i, j)),
        out_shape=jax.ShapeDtypeStruct((m, n), BF16),
        compiler_params=_params(2),
        name="branch_merge",
    )(o_moba, o_diff, w_m, w_d, y, y)


def _mixout_body(a_ref, w_ref, x_ref, g_ref, x1_ref, h_ref):
    x1 = x_ref[...] + jnp.dot(a_ref[...], w_ref[...], preferred_element_type=F32)
    x1_ref[...] = x1
    h_ref[...] = _rms(x1, g_ref[...]).astype(h_ref.dtype)


def mixout(merged, w, x, gain, tm=512):
    m, k = merged.shape
    n = w.shape[1]
    return pl.pallas_call(
        _mixout_body,
        grid=(m // tm,),
        in_specs=[pl.BlockSpec((tm, k), lambda i: (i, 0)),
                  pl.BlockSpec((k, n), lambda i: (0, 0)),
                  pl.BlockSpec((tm, n), lambda i: (i, 0)),
                  pl.BlockSpec((1, n), lambda i: (0, 0))],
        out_specs=[pl.BlockSpec((tm, n), lambda i: (i, 0)),
                   pl.BlockSpec((tm, n), lambda i: (i, 0))],
        out_shape=[jax.ShapeDtypeStruct((m, n), F32), jax.ShapeDtypeStruct((m, n), BF16)],
        compiler_params=_params(1),
        name="mixout",
    )(merged, w, x, gain.reshape(1, n))


def _memkv_body(mem_ref, g_ref, wk_ref, wv_ref, k_ref, v_ref):
    mn = _rms(mem_ref[...], g_ref[...]).astype(BF16)
    k_ref[...] = jnp.dot(mn, wk_ref[...], preferred_element_type=F32).astype(k_ref.dtype)
    v_ref[...] = jnp.dot(mn, wv_ref[...], preferred_element_type=F32).astype(v_ref.dtype)


def memory_kv(mem2d, gain, w_k, w_v, rows):
    m, d = mem2d.shape
    n = w_k.shape[1]
    return pl.pallas_call(
        _memkv_body,
        grid=(m // rows,),
        in_specs=[pl.BlockSpec((rows, d), lambda i: (i, 0)),
                  pl.BlockSpec((1, d), lambda i: (0, 0)),
                  pl.BlockSpec((d, n), lambda i: (0, 0)),
                  pl.BlockSpec((d, n), lambda i: (0, 0))],
        out_specs=[pl.BlockSpec((rows, n), lambda i: (i, 0)),
                   pl.BlockSpec((rows, n), lambda i: (i, 0))],
        out_shape=[jax.ShapeDtypeStruct((m, n), BF16), jax.ShapeDtypeStruct((m, n), BF16)],
        compiler_params=_params(1),
        name="memory_kv",
    )(mem2d, gain.reshape(1, d), w_k, w_v)


def _xattn_body(h_ref, x1_ref, k_ref, v_ref, wq_ref, wo_ref, g_ref, wr_ref, br_ref,
                x2_ref, h2_ref, idx_ref, wgt_ref):
    q = jnp.dot(h_ref[...], wq_ref[...], preferred_element_type=F32).astype(BF16)
    outs = []
    for hh in range(XATTN_HEADS):
        sl = slice(hh * HEAD_DIM, (hh + 1) * HEAD_DIM)
        s = _dot_nt(q[:, sl], k_ref[:, sl]) * ATTN_SCALE
        p = jnp.exp(s - jnp.max(s, axis=1, keepdims=True))
        o = jnp.dot(p.astype(BF16), v_ref[:, sl], preferred_element_type=F32)
        outs.append((o / jnp.sum(p, axis=1, keepdims=True)).astype(BF16))
    o = jnp.concatenate(outs, axis=1)
    x2 = x1_ref[...] + jnp.dot(o, wo_ref[...], preferred_element_type=F32)
    x2_ref[...] = x2
    h2 = _rms(x2, g_ref[...])
    h2_ref[...] = h2

    h_hi = h2.astype(BF16)
    h_lo = (h2 - h_hi.astype(F32)).astype(BF16)
    w_hi = wr_ref[...].astype(BF16)
    w_lo = (wr_ref[...] - w_hi.astype(F32)).astype(BF16)
    logits = (jnp.dot(h_hi, w_hi, preferred_element_type=F32) + jnp.dot(h_lo, w_hi, preferred_element_type=F32)
              + jnp.dot(h_hi, w_lo, preferred_element_type=F32) + br_ref[...])
    lane = lax.broadcasted_iota(jnp.int32, logits.shape, 1)
    out_lane = lax.broadcasted_iota(jnp.int32, idx_ref.shape, 1)
    idx_out = jnp.zeros(idx_ref.shape, jnp.int32)
    exp_out = jnp.zeros(wgt_ref.shape, F32)
    denom = jnp.zeros((logits.shape[0], 1), F32)
    top0 = None
    for kk in range(TOP_K):
        top = jnp.max(logits, axis=1, keepdims=True)
        arg = jnp.min(jnp.where(logits == top, lane, N_EXPERTS), axis=1, keepdims=True)
        logits = jnp.where(lane == arg, NEG_INF, logits)
        top0 = top if top0 is None else top0
        e = jnp.exp(top - top0)
        denom = denom + e
        idx_out = jnp.where(out_lane == kk, arg, idx_out)
        exp_out = jnp.where(out_lane == kk, e, exp_out)
    idx_ref[...] = idx_out
    wgt_ref[...] = exp_out / denom


def cross_attention_router(hx, x1, k_mem, v_mem, w_q, w_o, gain, w_router, b_router, seq, tm=512):
    m, d = hx.shape
    mem_len = k_mem.shape[0] // (m // seq)
    n = w_q.shape[1]
    per_b = seq // tm
    const = lambda shape: pl.BlockSpec(shape, lambda i: (0,) * len(shape))
    rows = lambda cols: pl.BlockSpec((tm, cols), lambda i: (i, 0))
    return pl.pallas_call(
        _xattn_body,
        grid=(m // tm,),
        in_specs=[rows(d), rows(d),
                  pl.BlockSpec((mem_len, n), lambda i: (i // per_b, 0)),
                  pl.BlockSpec((mem_len, n), lambda i: (i // per_b, 0)),
                  const((d, n)), const((n, d)), const((1, d)), const((d, N_EXPERTS)), const((1, N_EXPERTS))],
        out_specs=[rows(d), rows(d), rows(LANES), rows(LANES)],
        out_shape=[jax.ShapeDtypeStruct((m, d), F32), jax.ShapeDtypeStruct((m, d), F32),
                   jax.ShapeDtypeStruct((m, LANES), jnp.int32), jax.ShapeDtypeStruct((m, LANES), F32)],
        compiler_params=_params(1),
        name="cross_attention_router",
    )(hx, x1, k_mem, v_mem, w_q, w_o, gain.reshape(1, d), w_router.astype(F32), b_router.reshape(1, -1).astype(F32))


def _routing_plan(top_idx, n_tokens):
    rt = EXPERT_ROW_TILE
    tiles_per_unit = EXPERT_UNIT_ROWS // rt
    slot_onehot = (top_idx[:, :, None] == jnp.arange(N_EXPERTS)[None, None, :]).astype(jnp.int32)
    onehot = slot_onehot.sum(axis=1)
    before = jnp.cumsum(onehot, axis=0) - onehot
    count = onehot.sum(axis=0)
    tiles = (count + rt - 1) // rt
    tile_start = jnp.cumsum(tiles) - tiles
    pos = (slot_onehot * (tile_start * rt + before)[:, None, :]).sum(axis=-1)

    units = (tiles + tiles_per_unit - 1) // tiles_per_unit
    unit_first = jnp.cumsum(units) - units
    n_units = units.sum()
    max_units = N_EXPERTS + (n_tokens * TOP_K) // EXPERT_UNIT_ROWS
    uid = jnp.arange(max_units)
    e_of = jnp.clip(jnp.searchsorted(jnp.cumsum(units), uid, side="right"), 0, N_EXPERTS - 1)
    k_in = uid - unit_first[e_of]
    live = uid < n_units
    last_e = e_of[jnp.maximum(n_units - 1, 0)]
    unit_expert = jnp.where(live, e_of, last_e).astype(jnp.int32)
    unit_start = jnp.where(live, (tile_start[e_of] + k_in * tiles_per_unit) * rt, 0).astype(jnp.int32)
    unit_tiles = jnp.where(live, jnp.minimum(tiles[e_of] - k_in * tiles_per_unit, tiles_per_unit), 0).astype(jnp.int32)
    totals = jnp.stack([n_units, tiles.sum()]).astype(jnp.int32)
    return pos.astype(jnp.int32), unit_expert, unit_start, unit_tiles, totals


def _dispatch_body(pos_hbm, h_ref, xs_in, xs_hbm, pos_smem, sem):
    del xs_in
    i = pl.program_id(0)
    groups, sub, _ = h_ref.shape
    idx_copy = pltpu.make_async_copy(pos_hbm.at[i], pos_smem, sem.at[0])
    idx_copy.start()
    idx_copy.wait()

    def send(g, carry):
        for s in range(sub):
            for kk in range(TOP_K):
                r = pos_smem[g * (sub * TOP_K) + s * TOP_K + kk]
                pltpu.make_async_copy(h_ref.at[g, pl.ds(s, 1), :], xs_hbm.at[pl.ds(r, 1), :], sem.at[1]).start()
        return carry

    lax.fori_loop(0, groups, send, 0)

    def drain(g, carry):
        pltpu.make_async_copy(h_ref.at[0], xs_hbm.at[pl.ds(0, sub), :], sem.at[1]).wait()
        return carry

    lax.fori_loop(0, groups * TOP_K, drain, 0)


def dispatch_rows(h2, pos, p_rows):
    t, d = h2.shape
    tm = GATHER_TOKENS
    sub = 8
    return pl.pallas_call(
        _dispatch_body,
        grid=(t // tm,),
        in_specs=[pl.BlockSpec(memory_space=pl.ANY),
                  pl.BlockSpec((tm // sub, sub, d), lambda i: (i, 0, 0)),
                  pl.BlockSpec(memory_space=pl.ANY)],
        out_specs=pl.BlockSpec(memory_space=pl.ANY),
        out_shape=jax.ShapeDtypeStruct((p_rows, d), h2.dtype),
        scratch_shapes=[pltpu.SMEM((tm * TOP_K,), jnp.int32), pltpu.SemaphoreType.DMA((2,))],
        input_output_aliases={2: 0},
        compiler_params=_params(1),
        name="dispatch_rows",
    )(pos.reshape(t // tm, tm * TOP_K), h2.reshape(t // sub, sub, d), jnp.zeros((p_rows, d), h2.dtype))


def _expert_body(ue_ref, us_ref, un_ref, nu_ref,
                 x_hbm, wgu_ref, bgu_ref, wd_ref, bd_ref, y_hbm,
                 xbuf, actbuf, gubuf, ystage, wgu_bf, wd_f32, wd_bf, sem_x, sem_y, *, n_up, n_down):
    u = pl.program_id(0)
    c = pl.program_id(1)
    rt = EXPERT_ROW_TILE
    chunk = EXPERT_COL_CHUNK
    half = chunk // 2
    quarter = half // 2
    n_live = nu_ref[0]
    live = u < n_live
    start = us_ref[u]
    n_tiles = un_ref[u]

    def tile_rows(j):
        return pl.ds(pl.multiple_of(j * rt, rt), rt)

    def x_copy(unit, j):
        rows = pl.ds(pl.multiple_of(us_ref[unit] + j * rt, rt), rt)
        return pltpu.make_async_copy(x_hbm.at[rows, :], xbuf.at[tile_rows(j), :], sem_x.at[0])

    def fetch_rows(unit):
        def body(j, carry):
            x_copy(unit, j).start()
            return carry
        lax.fori_loop(0, un_ref[unit], body, 0)

    @pl.when((u == 0) & (c == 0) & live)
    def _():
        fetch_rows(0)

    @pl.when(live & (c == 0))
    def _():
        def body(j, carry):
            x_copy(u, j).wait()
            return carry
        lax.fori_loop(0, n_tiles, body, 0)

    @pl.when((c == n_up) & (u + 1 < n_live))
    def _():
        fetch_rows(u + 1)

    @pl.when(live & (c < n_up))
    def _():
        wgu_bf[...] = wgu_ref[0].astype(BF16)
        bias = bgu_ref[0]
        even = (lax.broadcasted_iota(jnp.int32, (rt, half), 1) % 2) == 0

        def project(j):
            x = xbuf[tile_rows(j), :].astype(BF16)
            gubuf[j % 2] = jnp.dot(x, wgu_bf[...], preferred_element_type=F32) + bias

        def activate(j):
            gu = gubuf[j % 2]
            lo = gu[:, :half]
            hi = gu[:, half:]
            gate = jnp.where(even, lo, pltpu.roll(hi, 1, axis=1))
            up = jnp.where(even, pltpu.roll(lo, half - 1, axis=1), hi)
            gate = jnp.minimum(gate, SWIGLU_LIMIT)
            up = jnp.clip(up, -SWIGLU_LIMIT, SWIGLU_LIMIT)
            act = (up + 1.0) * gate * _sigmoid(SWIGLU_ALPHA * gate)
            actbuf[c, tile_rows(j), :] = act.astype(BF16)

        project(0)

        def body(j, carry):
            activate(j - 1)
            project(j)
            return carry

        lax.fori_loop(1, n_tiles, body, 0)
        activate(n_tiles - 1)

    @pl.when(live & (c >= n_up))
    def _():
        cd = c - n_up
        for g in range(chunk // LANES):
            lanes = slice(g * LANES, (g + 1) * LANES)
            for f in range(n_up):
                base = f * half
                wd_f32[g, pl.ds(base, quarter, stride=2), :] = wd_ref[0, base:base + quarter, lanes]
                wd_f32[g, pl.ds(base + 1, quarter, stride=2), :] = wd_ref[0, base + quarter:base + half, lanes]
            wd_bf[:, lanes] = wd_f32[g].astype(BF16)
        bias = bd_ref[0]

        def y_copy(j, slot):
            rows = pl.ds(pl.multiple_of(start + j * rt, rt), rt)
            cols = pl.ds(pl.multiple_of(cd * chunk, chunk), chunk)
            return pltpu.make_async_copy(ystage.at[slot], y_hbm.at[rows, cols], sem_y.at[slot])

        def tile(j, carry):
            slot = j % 2

            @pl.when(j >= 2)
            def _():
                y_copy(j - 2, slot).wait()

            act = jnp.concatenate([actbuf[f, tile_rows(j), :] for f in range(n_up)], axis=1)
            ystage[slot] = jnp.dot(act, wd_bf[...], preferred_element_type=F32) + bias
            y_copy(j, slot).start()
            return carry

        lax.fori_loop(0, n_tiles, tile, 0)

        @pl.when(n_tiles >= 2)
        def _():
            y_copy(n_tiles - 2, n_tiles % 2).wait()

        y_copy(n_tiles - 1, (n_tiles - 1) % 2).wait()

    @pl.when((u == pl.num_programs(0) - 1) & (c == n_up + n_down - 1))
    def _():
        zero_rows = xbuf.at[:rt, :]
        zero_rows[...] = jnp.zeros(zero_rows.shape, F32)
        used_tiles = nu_ref[1]

        def pad_copy(j):
            return pltpu.make_async_copy(zero_rows, y_hbm.at[pl.ds(pl.multiple_of(j * rt, rt), rt), :], sem_x.at[0])

        def pad_start(j, carry):
            pad_copy(j).start()
            return carry

        def pad_wait(j, carry):
            pad_copy(j).wait()
            return carry

        lax.fori_loop(used_tiles, y_hbm.shape[0] // rt, pad_start, 0)
        lax.fori_loop(used_tiles, y_hbm.shape[0] // rt, pad_wait, 0)


def expert_ffn(x_sorted, w_gate_up, b_gate_up, w_down, b_down, unit_expert, unit_start, unit_tiles, totals):
    p_rows, d = x_sorted.shape
    n_exp, _, two_ff = w_gate_up.shape
    d_ff = two_ff // 2
    chunk = EXPERT_COL_CHUNK
    n_up = two_ff // chunk
    n_down = d // chunk
    n_steps = n_up + n_down
    max_units = unit_expert.shape[0]
    rt = EXPERT_ROW_TILE

    def up_idx(u, c, nu):
        return jnp.where(u < nu[0], jnp.minimum(c, n_up - 1), n_up - 1)

    def down_idx(u, c, nu):
        return jnp.where(u < nu[0], jnp.maximum(c - n_up, 0), n_down - 1)

    grid_spec = pltpu.PrefetchScalarGridSpec(
        num_scalar_prefetch=4,
        grid=(max_units, n_steps),
        in_specs=[pl.BlockSpec(memory_space=pl.ANY),
                  pl.BlockSpec((1, d, chunk), lambda u, c, ue, us, un, nu: (ue[u], 0, up_idx(u, c, nu))),
                  pl.BlockSpec((1, 1, chunk), lambda u, c, ue, us, un, nu: (ue[u], 0, up_idx(u, c, nu))),
                  pl.BlockSpec((1, d_ff, chunk), lambda u, c, ue, us, un, nu: (ue[u], 0, down_idx(u, c, nu))),
                  pl.BlockSpec((1, 1, chunk), lambda u, c, ue, us, un, nu: (ue[u], 0, down_idx(u, c, nu)))],
        out_specs=pl.BlockSpec(memory_space=pl.ANY),
        scratch_shapes=[pltpu.VMEM((EXPERT_UNIT_ROWS, d), F32),
                        pltpu.VMEM((n_up, EXPERT_UNIT_ROWS, chunk // 2), BF16),
                        pltpu.VMEM((2, rt, chunk), F32),
                        pltpu.VMEM((2, rt, chunk), F32),
                        pltpu.VMEM((d, chunk), BF16),
                        pltpu.VMEM((chunk // LANES, d_ff, LANES), F32),
                        pltpu.VMEM((d_ff, chunk), BF16),
                        pltpu.SemaphoreType.DMA((1,)),
                        pltpu.SemaphoreType.DMA((2,))],
    )
    return pl.pallas_call(
        functools.partial(_expert_body, n_up=n_up, n_down=n_down),
        grid_spec=grid_spec,
        out_shape=jax.ShapeDtypeStruct((p_rows, d), F32),
        compiler_params=_params(2),
        name="expert_ffn",
    )(unit_expert, unit_start, unit_tiles, totals,
      x_sorted, w_gate_up, b_gate_up.reshape(n_exp, 1, two_ff), w_down, b_down.reshape(n_exp, 1, d))


def _combine_body(pos_hbm, x2_ref, w_ref, g_ref, y_hbm, o_ref, pos_smem, ybuf, sem, *, final_norm):
    i = pl.program_id(0)
    n = pl.num_programs(0)
    groups, sub, _ = x2_ref.shape
    slot = i % 2

    def fetch(step, into):
        per_step = groups * sub * TOP_K
        base = pl.multiple_of(into * per_step, per_step)
        idx_copy = pltpu.make_async_copy(pos_hbm.at[step], pos_smem.at[pl.ds(base, per_step)], sem.at[2])
        idx_copy.start()
        idx_copy.wait()

        def recv(g, carry):
            for s in range(sub):
                for kk in range(TOP_K):
                    r = pos_smem[base + g * (sub * TOP_K) + s * TOP_K + kk]
                    pltpu.make_async_copy(y_hbm.at[pl.ds(r, 1), :], ybuf.at[into, kk, g, pl.ds(s, 1), :],
                                          sem.at[into]).start()
            return carry

        lax.fori_loop(0, groups, recv, 0)

    @pl.when(i == 0)
    def _():
        fetch(0, 0)

    @pl.when(i + 1 < n)
    def _():
        fetch(i + 1, 1 - slot)

    def drain(g, carry):
        pltpu.make_async_copy(y_hbm.at[pl.ds(0, sub), :], ybuf.at[slot, 0, 0], sem.at[slot]).wait()
        return carry

    lax.fori_loop(0, groups * TOP_K, drain, 0)

    x3 = x2_ref[...]
    for kk in range(TOP_K):
        x3 = x3 + w_ref[:, :, kk:kk + 1] * ybuf[slot, kk]
    o_ref[...] = _rms(x3, g_ref[...]) if final_norm else x3


def combine(x2, y_sorted, pos, weights, gain):
    m, d = x2.shape
    tm = GATHER_TOKENS
    sub = 8
    final_norm = gain is not None
    gain = gain if final_norm else jnp.ones((d,), F32)
    rows = lambda width: pl.BlockSpec((tm // sub, sub, width), lambda i: (i, 0, 0))
    out = pl.pallas_call(
        functools.partial(_combine_body, final_norm=final_norm),
        grid=(m // tm,),
        in_specs=[pl.BlockSpec(memory_space=pl.ANY),
                  rows(d), rows(LANES),
                  pl.BlockSpec((1, d), lambda i: (0, 0)),
                  pl.BlockSpec(memory_space=pl.ANY)],
        out_specs=rows(d),
        out_shape=jax.ShapeDtypeStruct((m // sub, sub, d), F32),
        scratch_shapes=[pltpu.SMEM((2 * tm * TOP_K,), jnp.int32),
                        pltpu.VMEM((2, TOP_K, tm // sub, sub, d), F32),
                        pltpu.SemaphoreType.DMA((3,))],
        compiler_params=_params(1),
        name="combine",
    )(pos.reshape(m // tm, tm * TOP_K), x2.reshape(m // sub, sub, d), weights.reshape(m // sub, sub, LANES),
      gain.reshape(1, d), y_sorted)
    return out.reshape(m, d)


def kernel(x, mem, rel_bias_table, mix_norm, w_in, diff_lambda_q1, diff_lambda_k1, diff_lambda_q2, diff_lambda_k2, diff_subln, w_branch_moba, w_branch_diff, w_mix_out, xattn_norm, mem_norm, w_xq, w_xk, w_xv, w_xo, ffn_norm, w_router, b_router, w_gate_up, b_gate_up, w_down, b_down, final_norm):
    batch, seq, d = x.shape
    n_tok = batch * seq
    x2d = x.reshape(n_tok, d)
    near, far = _bias_tiles(rel_bias_table)
    diff_col0 = 3 * MOBA_WIDTH
    gate_col0 = diff_col0 + 3 * DIFF_WIDTH
    cols = jnp.arange(w_in.shape[2])
    is_q = (cols < MOBA_WIDTH) | ((cols >= diff_col0) & (cols < diff_col0 + DIFF_WIDTH))
    col_scale = jnp.where(is_q, ATTN_SCALE * LOG2E, 1.0)
    p_rows = n_tok * TOP_K + N_EXPERTS * EXPERT_ROW_TILE
    for l in range(w_in.shape[0]):
        h = rmsnorm_rows(x2d, mix_norm[l])
        y = matmul_colscale(h, w_in[l], col_scale)
        o_moba = moba_attention(y, near[:MOBA_HEADS], far[:MOBA_HEADS], batch, seq)
        o_diff = diff_attention(y, near[MOBA_HEADS:], far[MOBA_HEADS:], diff_lambda_q1[l], diff_lambda_k1[l],
                                diff_lambda_q2[l], diff_lambda_k2[l], diff_subln[l], batch, seq, diff_col0)
        merged = branch_merge(o_moba, o_diff, w_branch_moba[l].astype(BF16), w_branch_diff[l].astype(BF16),
                              y, gate_col0)
        x1, hx = mixout(merged, w_mix_out[l].astype(BF16), x2d, xattn_norm[l])
        k_mem, v_mem = memory_kv(mem.reshape(-1, d), mem_norm[l], w_xk[l].astype(BF16), w_xv[l].astype(BF16),
                                 mem.shape[1])
        x2, h2, idx_pad, wgt_pad = cross_attention_router(hx, x1, k_mem, v_mem, w_xq[l].astype(BF16),
                                                          w_xo[l].astype(BF16), ffn_norm[l], w_router[l],
                                                          b_router[l], seq)
        pos, unit_expert, unit_start, unit_tiles, totals = _routing_plan(idx_pad[:, :TOP_K], n_tok)
        x_sorted = dispatch_rows(h2, pos, p_rows)
        y_sorted = expert_ffn(x_sorted, w_gate_up[l], b_gate_up[l], w_down[l], b_down[l],
                              unit_expert, unit_start, unit_tiles, totals)
        last = l == w_in.shape[0] - 1
        x2d = combine(x2, y_sorted, pos, wgt_pad, final_norm if last else None)
    return x2d.reshape(batch, seq, d)
```

```python
import collections
import functools
import math

import jax
import jax.numpy as jnp
from jax import lax
from jax.experimental import pallas as pl
from jax.experimental.pallas import tpu as pltpu

F32 = jnp.float32
BF16 = jnp.bfloat16
NEG_INF = float("-inf")

D_MODEL = 2048
HEAD_DIM = 128
MOBA_HEADS = 8
MOBA_WIDTH = MOBA_HEADS * HEAD_DIM
MOBA_BLOCK = 256
MOBA_TOPK = 3
DIFF_HEADS = 4
DIFF_WIDTH = DIFF_HEADS * 2 * HEAD_DIM
REL_BUCKETS = 32
REL_MAX_DISTANCE = 128
XATTN_HEADS = 4
N_EXPERTS = 32
TOP_K = 4
SWIGLU_LIMIT = 7.0
SWIGLU_ALPHA = 1.702
NORM_EPS = 1e-5
LAMBDA_INIT = 0.8 - 0.6 * math.exp(-0.3 * 0)
ATTN_SCALE = HEAD_DIM ** -0.5
LOG2E = math.log2(math.e)

ATTN_TILE = MOBA_BLOCK
MOBA_HEADS_PER_STEP = 4
DIFF_HEADS_PER_STEP = 2
LANES = 128
EXPERT_ROW_TILE = 256
EXPERT_UNIT_ROWS = 1536
EXPERT_COL_CHUNK = 512
GATHER_TOKENS = 256
VMEM_LIMIT = 56 * 1024 * 1024


def _params(n_axes):
    return pltpu.CompilerParams(dimension_semantics=("arbitrary",) * n_axes,
                                vmem_limit_bytes=VMEM_LIMIT)


def _rms(x, gain):
    return x * lax.rsqrt(jnp.mean(x * x, axis=-1, keepdims=True) + NORM_EPS) * gain


def _sigmoid(x):
    return 1.0 / (1.0 + jnp.exp(-x))


def _dot_nt(a, b):
    return lax.dot_general(a, b, (((1,), (1,)), ((), ())), preferred_element_type=F32)


def _rmsnorm_body(x_ref, g_ref, o_ref):
    o_ref[...] = _rms(x_ref[...], g_ref[...]).astype(o_ref.dtype)


def rmsnorm_rows(x, gain, tm=512):
    t, d = x.shape
    return pl.pallas_call(
        _rmsnorm_body,
        grid=(t // tm,),
        in_specs=[pl.BlockSpec((tm, d), lambda i: (i, 0)),
                  pl.BlockSpec((1, d), lambda i: (0, 0))],
        out_specs=pl.BlockSpec((tm, d), lambda i: (i, 0)),
        out_shape=jax.ShapeDtypeStruct((t, d), BF16),
        compiler_params=_params(1),
        name="rmsnorm_rows",
    )(x, gain.reshape(1, d))


def _matmul_body(a_ref, w_ref, cs_ref, o_ref, w_bf):
    @pl.when(pl.program_id(1) == 0)
    def _():
        w_bf[...] = w_ref[...].astype(BF16)

    acc = jnp.dot(a_ref[...], w_bf[...], preferred_element_type=F32)
    o_ref[...] = (acc * cs_ref[...]).astype(o_ref.dtype)


def matmul_colscale(a, w, col_scale, tm=1024, tn=1024, out_dtype=BF16):
    m, k = a.shape
    n = w.shape[1]
    tm, tn = min(tm, m), min(tn, n)
    return pl.pallas_call(
        _matmul_body,
        grid=(n // tn, m // tm),
        in_specs=[pl.BlockSpec((tm, k), lambda j, i: (i, 0)),
                  pl.BlockSpec((k, tn), lambda j, i: (0, j)),
                  pl.BlockSpec((1, tn), lambda j, i: (0, j))],
        out_specs=pl.BlockSpec((tm, tn), lambda j, i: (i, j)),
        out_shape=jax.ShapeDtypeStruct((m, n), out_dtype),
        scratch_shapes=[pltpu.VMEM((k, tn), BF16)],
        compiler_params=_params(2),
        name="matmul",
    )(a, w, col_scale.reshape(1, n).astype(F32))


def _rel_bucket(dist):
    n = jnp.maximum(dist, 0)
    max_exact = REL_BUCKETS // 2
    nf = jnp.maximum(n, max_exact).astype(F32)
    large = max_exact + (jnp.log(nf / max_exact) / math.log(REL_MAX_DISTANCE / max_exact)
                         * (REL_BUCKETS - max_exact)).astype(jnp.int32)
    return jnp.where(n < max_exact, n, jnp.minimum(large, REL_BUCKETS - 1))


def _bias_tiles(table):
    t = ATTN_TILE
    r = jnp.arange(t)[:, None]
    c = jnp.arange(t)[None, :]
    dist = jnp.stack([r - c, t + r - c])
    onehot = (_rel_bucket(dist)[..., None] == jnp.arange(REL_BUCKETS)).astype(F32)
    near = jnp.einsum("irck,kh->hirc", onehot, table.astype(F32), precision=lax.Precision.HIGHEST)
    far = jnp.broadcast_to(table[REL_BUCKETS - 1][:, None, None], (table.shape[1], 1, t))
    return near * LOG2E, far.astype(F32) * LOG2E


def _lane_halves(x, op):
    return op(x[:, :LANES], x[:, LANES:])


AttnStream = collections.namedtuple("AttnStream", "q k_rows v_rows near cfar pen_at s_scr acc lsum mpast")


def _causal_attention(streams, qi):
    t = ATTN_TILE
    own_slot = streams[0].s_scr.shape[0] - 2
    prev_slot = own_slot + 1
    prev = jnp.maximum(qi - 1, 0)
    n_far = prev
    n_pairs = (n_far + 1) // 2
    row = lax.broadcasted_iota(jnp.int32, (t, t), 0)
    col = lax.broadcasted_iota(jnp.int32, (t, t), 1)
    has_prev = jnp.where(qi >= 1, 0.0, NEG_INF)

    def masked(st, s, n):
        return s if st.pen_at is None else s + st.pen_at(n)

    def tile_max(m, s):
        return jnp.maximum(m, _lane_halves(s, jnp.maximum))

    def tile_sum(l, p):
        return l + _lane_halves(p, jnp.add)

    m_near = []
    for st in streams:
        s2 = _dot_nt(st.q, jnp.concatenate([st.k_rows(qi, 1), st.k_rows(prev, 1)], axis=0))
        s_own = jnp.where(col <= row, s2[:, :t] + st.near(0), NEG_INF)
        s_prev = masked(st, s2[:, t:] + st.near(1) + has_prev, prev)
        st.s_scr[own_slot] = s_own
        st.s_scr[prev_slot] = s_prev - st.cfar
        m_near.append(tile_max(_lane_halves(s_own, jnp.maximum), s_prev))

    def pair_scores(i, m_far):
        second_is_far = jnp.where(2 * i + 1 < n_far, 0.0, NEG_INF)
        out = []
        for st, m in zip(streams, m_far):
            s = _dot_nt(st.q, st.k_rows(2 * i, 2))
            for half in range(2):
                sh = masked(st, s[:, half * t:(half + 1) * t], 2 * i + half)
                if half == 1:
                    sh = sh + second_is_far
                st.s_scr[2 * i + half] = sh
                m = tile_max(m, sh)
            out.append(m)
        return tuple(out)

    m_far = lax.fori_loop(0, n_pairs, pair_scores, tuple(jnp.full((t, LANES), NEG_INF, F32) for _ in streams))

    for st, mn, mf in zip(streams, m_near, m_far):
        m_row = jnp.maximum(jnp.max(mn, axis=1, keepdims=True),
                            jnp.max(mf, axis=1, keepdims=True) + st.cfar)
        mp = m_row - st.cfar
        p_own = jnp.exp2(st.s_scr[own_slot] - m_row)
        p_prev = jnp.exp2(st.s_scr[prev_slot] - mp)
        st.mpast[...] = jnp.broadcast_to(mp, st.mpast.shape)
        st.lsum[...] = tile_sum(_lane_halves(p_own, jnp.add), p_prev)
        st.acc[...] = jnp.dot(jnp.concatenate([p_own, p_prev], axis=1).astype(BF16),
                              jnp.concatenate([st.v_rows(qi, 1), st.v_rows(prev, 1)], axis=0),
                              preferred_element_type=F32)

    def pair_weights(i, carry):
        for st in streams:
            mp = jnp.concatenate([st.mpast[...]] * (2 * t // LANES), axis=1)
            p = jnp.exp2(jnp.concatenate([st.s_scr[2 * i], st.s_scr[2 * i + 1]], axis=1) - mp)
            st.lsum[...] = tile_sum(tile_sum(st.lsum[...], p[:, :t]), p[:, t:])
            st.acc[...] += jnp.dot(p.astype(BF16), st.v_rows(2 * i, 2), preferred_element_type=F32)
        return carry

    lax.fori_loop(0, n_pairs, pair_weights, 0)
    return [(st.acc[...], jnp.sum(st.lsum[...], axis=1, keepdims=True)) for st in streams]


def _block_rows(n, w=1):
    return pl.ds(pl.multiple_of(n * ATTN_TILE, ATTN_TILE), w * ATTN_TILE)


def _moba_body(q_ref, k_ref, v_ref, near_ref, far_ref, o_ref, kmean_ref, s_scr, acc_scr, lsum_scr, mpast_scr, *,
               n_blocks):
    qi = pl.program_id(2)
    t = ATTN_TILE
    dh = HEAD_DIM
    heads = MOBA_HEADS_PER_STEP

    @pl.when(qi == 0)
    def _():
        for j in range(heads):
            for n in range(n_blocks):
                kmean_ref[j, n:n + 1, :] = jnp.mean(k_ref[n * t:(n + 1) * t, j * dh:(j + 1) * dh].astype(F32),
                                                    axis=0, keepdims=True)

    def stream(j):
        cols = slice(j * dh, (j + 1) * dh)
        q = q_ref[:, cols]
        gate = lax.dot_general(kmean_ref[j], q.astype(F32), (((1,), (1,)), ((), ())),
                               precision=lax.Precision.HIGHEST, preferred_element_type=F32)
        valid = lax.broadcasted_iota(jnp.int32, gate.shape, 0) < qi
        g = jnp.where(valid, gate, NEG_INF)
        kth = g
        for _ in range(MOBA_TOPK - 1):
            top = jnp.max(kth, axis=0, keepdims=True)
            kth = jnp.where(kth == top, NEG_INF, kth)
        third = jnp.max(kth, axis=0, keepdims=True)
        pen_t = jnp.where(valid & (g >= third), 0.0, NEG_INF)
        pen_t = jnp.concatenate([pen_t, jnp.full((LANES - n_blocks, t), NEG_INF, F32)], axis=0)
        pen = pen_t.T
        blk = lax.broadcasted_iota(jnp.int32, pen.shape, 1)

        def pen_at(n):
            return jnp.max(jnp.where(blk == n, pen, NEG_INF), axis=1, keepdims=True)

        return AttnStream(q, lambda n, w: k_ref[_block_rows(n, w), cols], lambda n, w: v_ref[_block_rows(n, w), cols],
                          lambda i: near_ref[j, i], far_ref[j][:, :1], pen_at, s_scr.at[j],
                          acc_scr.at[j], lsum_scr.at[j], mpast_scr.at[j])

    results = _causal_attention([stream(j) for j in range(heads)], qi)
    for j, (acc, l) in enumerate(results):
        o_ref[:, j * dh:(j + 1) * dh] = (acc / l).astype(o_ref.dtype)


def moba_attention(y, near, far, batch, seq):
    t = ATTN_TILE
    nq = seq // t
    hs = MOBA_HEADS_PER_STEP
    groups = MOBA_HEADS // hs
    w = hs * HEAD_DIM
    return pl.pallas_call(
        functools.partial(_moba_body, n_blocks=nq),
        grid=(batch, groups, nq),
        in_specs=[pl.BlockSpec((t, w), lambda b, g, i: (b * nq + i, g)),
                  pl.BlockSpec((seq, w), lambda b, g, i: (b, groups + g)),
                  pl.BlockSpec((seq, w), lambda b, g, i: (b, 2 * groups + g)),
                  pl.BlockSpec((hs, 2, t, t), lambda b, g, i: (g, 0, 0, 0)),
                  pl.BlockSpec((hs, 1, t), lambda b, g, i: (g, 0, 0))],
        out_specs=pl.BlockSpec((t, w), lambda b, g, i: (b * nq + i, g)),
        out_shape=jax.ShapeDtypeStruct((batch * seq, MOBA_WIDTH), BF16),
        scratch_shapes=[pltpu.VMEM((hs, nq, HEAD_DIM), F32),
                        pltpu.VMEM((hs, nq + 2, t, t), F32),
                        pltpu.VMEM((hs, t, HEAD_DIM), F32),
                        pltpu.VMEM((hs, t, LANES), F32),
                        pltpu.VMEM((hs, t, LANES), F32)],
        compiler_params=_params(3),
        name="moba_attention",
    )(y, y, y, near, far)


def _diff_body(q_ref, k_ref, v_ref, near_ref, far_ref, lq1_ref, lk1_ref, lq2_ref, lk2_ref, subln_ref, o_ref,
               s_scr, acc_scr, lsum_scr, mpast_scr):
    qi = pl.program_id(2)
    dh = HEAD_DIM
    w = 2 * dh
    heads = DIFF_HEADS_PER_STEP

    def stream(hh, j):
        qk_cols = slice(hh * w + j * dh, hh * w + (j + 1) * dh)
        v_cols = slice(hh * w, (hh + 1) * w)
        return AttnStream(q_ref[:, qk_cols], lambda n, nb: k_ref[_block_rows(n, nb), qk_cols],
                          lambda n, nb: v_ref[_block_rows(n, nb), v_cols], lambda i: near_ref[hh, i],
                          far_ref[hh][:, :1], None, s_scr.at[2 * hh + j],
                          acc_scr.at[2 * hh + j], lsum_scr.at[2 * hh + j], mpast_scr.at[2 * hh + j])

    results = _causal_attention([stream(hh, j) for hh in range(heads) for j in range(2)], qi)
    lam = (jnp.exp(jnp.sum(lq1_ref[...] * lk1_ref[...], axis=1, keepdims=True))
           - jnp.exp(jnp.sum(lq2_ref[...] * lk2_ref[...], axis=1, keepdims=True)) + LAMBDA_INIT)
    for hh in range(heads):
        (acc1, l1), (acc2, l2) = results[2 * hh], results[2 * hh + 1]
        o = acc1 / l1 - lam * (acc2 / l2)
        o_ref[:, hh * w:(hh + 1) * w] = (_rms(o, subln_ref[...]) * (1.0 - LAMBDA_INIT)).astype(o_ref.dtype)


def diff_attention(y, near, far, lq1, lk1, lq2, lk2, subln, batch, seq, col0):
    t = ATTN_TILE
    nq = seq // t
    hs = DIFF_HEADS_PER_STEP
    groups = DIFF_HEADS // hs
    w = hs * 2 * HEAD_DIM
    base = col0 // w
    vec = lambda a: a.reshape(1, -1).astype(F32)
    small = lambda n: pl.BlockSpec((1, n), lambda b, g, i: (0, 0))
    return pl.pallas_call(
        _diff_body,
        grid=(batch, groups, nq),
        in_specs=[pl.BlockSpec((t, w), lambda b, g, i: (b * nq + i, base + g)),
                  pl.BlockSpec((seq, w), lambda b, g, i: (b, base + groups + g)),
                  pl.BlockSpec((seq, w), lambda b, g, i: (b, base + 2 * groups + g)),
                  pl.BlockSpec((hs, 2, t, t), lambda b, g, i: (g, 0, 0, 0)),
                  pl.BlockSpec((hs, 1, t), lambda b, g, i: (g, 0, 0)),
                  small(HEAD_DIM), small(HEAD_DIM), small(HEAD_DIM), small(HEAD_DIM), small(2 * HEAD_DIM)],
        out_specs=pl.BlockSpec((t, w), lambda b, g, i: (b * nq + i, g)),
        out_shape=jax.ShapeDtypeStruct((batch * seq, DIFF_WIDTH), BF16),
        scratch_shapes=[pltpu.VMEM((2 * hs, nq + 2, t, t), F32),
                        pltpu.VMEM((2 * hs, t, 2 * HEAD_DIM), F32),
                        pltpu.VMEM((2 * hs, t, LANES), F32),
                        pltpu.VMEM((2 * hs, t, LANES), F32)],
        compiler_params=_params(3),
        name="diff_attention",
    )(y, y, y, near, far, vec(lq1), vec(lk1), vec(lq2), vec(lk2), vec(subln))


def _branch_body(om_ref, od_ref, wm_ref, wd_ref, ga_ref, gb_ref, o_ref):
    a = jnp.dot(om_ref[...], wm_ref[...], preferred_element_type=F32)
    b = jnp.dot(od_ref[...], wd_ref[...], preferred_element_type=F32)
    o_ref[...] = (_sigmoid(ga_ref[...].astype(F32)) * a + _sigmoid(gb_ref[...].astype(F32)) * b).astype(o_ref.dtype)


def branch_merge(o_moba, o_diff, w_m, w_d, y, gate_col0, tm=1024, tn=1024):
    m, k = o_moba.shape
    n = w_m.shape[1]
    g0 = gate_col0 // tn
    nj = n // tn
    return pl.pallas_call(
        _branch_body,
        grid=(nj, m // tm),
        in_specs=[pl.BlockSpec((tm, k), lambda j, i: (i, 0)),
                  pl.BlockSpec((tm, k), lambda j, i: (i, 0)),
                  pl.BlockSpec((k, tn), lambda j, i: (0, j)),
                  pl.BlockSpec((k, tn), lambda j, i: (0, j)),
                  pl.BlockSpec((tm, tn), lambda j, i: (i, g0 + j)),
                  pl.BlockSpec((tm, tn), lambda j, i: (i, g0 + nj + j))],
        out_specs=pl.BlockSpec((tm, tn), lambda j, i: (i, j)),
        out_shape=jax.ShapeDtypeStruct((m, n), BF16),
        compiler_params=_params(2),
        name="branch_merge",
    )(o_moba, o_diff, w_m, w_d, y, y)


def _mixout_body(a_ref, w_ref, x_ref, g_ref, x1_ref, h_ref):
    x1 = x_ref[...] + jnp.dot(a_ref[...], w_ref[...], preferred_element_type=F32)
    x1_ref[...] = x1
    h_ref[...] = _rms(x1, g_ref[...]).astype(h_ref.dtype)


def mixout(merged, w, x, gain, tm=512):
    m, k = merged.shape
    n = w.shape[1]
    return pl.pallas_call(
        _mixout_body,
        grid=(m // tm,),
        in_specs=[pl.BlockSpec((tm, k), lambda i: (i, 0)),
                  pl.BlockSpec((k, n), lambda i: (0, 0)),
                  pl.BlockSpec((tm, n), lambda i: (i, 0)),
                  pl.BlockSpec((1, n), lambda i: (0, 0))],
        out_specs=[pl.BlockSpec((tm, n), lambda i: (i, 0)),
                   pl.BlockSpec((tm, n), lambda i: (i, 0))],
        out_shape=[jax.ShapeDtypeStruct((m, n), F32), jax.ShapeDtypeStruct((m, n), BF16)],
        compiler_params=_params(1),
        name="mixout",
    )(merged, w, x, gain.reshape(1, n))


def _memkv_body(mem_ref, g_ref, wk_ref, wv_ref, k_ref, v_ref):
    mn = _rms(mem_ref[...], g_ref[...]).astype(BF16)
    k_ref[...] = jnp.dot(mn, wk_ref[...], preferred_element_type=F32).astype(k_ref.dtype)
    v_ref[...] = jnp.dot(mn, wv_ref[...], preferred_element_type=F32).astype(v_ref.dtype)


def memory_kv(mem2d, gain, w_k, w_v, rows):
    m, d = mem2d.shape
    n = w_k.shape[1]
    return pl.pallas_call(
        _memkv_body,
        grid=(m // rows,),
        in_specs=[pl.BlockSpec((rows, d), lambda i: (i, 0)),
                  pl.BlockSpec((1, d), lambda i: (0, 0)),
                  pl.BlockSpec((d, n), lambda i: (0, 0)),
                  pl.BlockSpec((d, n), lambda i: (0, 0))],
        out_specs=[pl.BlockSpec((rows, n), lambda i: (i, 0)),
                   pl.BlockSpec((rows, n), lambda i: (i, 0))],
        out_shape=[jax.ShapeDtypeStruct((m, n), BF16), jax.ShapeDtypeStruct((m, n), BF16)],
        compiler_params=_params(1),
        name="memory_kv",
    )(mem2d, gain.reshape(1, d), w_k, w_v)


def _xattn_body(h_ref, x1_ref, k_ref, v_ref, wq_ref, wo_ref, g_ref, wr_ref, br_ref,
                x2_ref, h2_ref, idx_ref, wgt_ref):
    q = jnp.dot(h_ref[...], wq_ref[...], preferred_element_type=F32).astype(BF16)
    outs = []
    for hh in range(XATTN_HEADS):
        sl = slice(hh * HEAD_DIM, (hh + 1) * HEAD_DIM)
        s = _dot_nt(q[:, sl], k_ref[:, sl]) * ATTN_SCALE
        p = jnp.exp(s - jnp.max(s, axis=1, keepdims=True))
        o = jnp.dot(p.astype(BF16), v_ref[:, sl], preferred_element_type=F32)
        outs.append((o / jnp.sum(p, axis=1, keepdims=True)).astype(BF16))
    o = jnp.concatenate(outs, axis=1)
    x2 = x1_ref[...] + jnp.dot(o, wo_ref[...], preferred_element_type=F32)
    x2_ref[...] = x2
    h2 = _rms(x2, g_ref[...])
    h2_ref[...] = h2

    h_hi = h2.astype(BF16)
    h_lo = (h2 - h_hi.astype(F32)).astype(BF16)
    w_hi = wr_ref[...].astype(BF16)
    w_lo = (wr_ref[...] - w_hi.astype(F32)).astype(BF16)
    logits = (jnp.dot(h_hi, w_hi, preferred_element_type=F32) + jnp.dot(h_lo, w_hi, preferred_element_type=F32)
              + jnp.dot(h_hi, w_lo, preferred_element_type=F32) + br_ref[...])
    lane = lax.broadcasted_iota(jnp.int32, logits.shape, 1)
    out_lane = lax.broadcasted_iota(jnp.int32, idx_ref.shape, 1)
    idx_out = jnp.zeros(idx_ref.shape, jnp.int32)
    exp_out = jnp.zeros(wgt_ref.shape, F32)
    denom = jnp.zeros((logits.shape[0], 1), F32)
    top0 = None
    for kk in range(TOP_K):
        top = jnp.max(logits, axis=1, keepdims=True)
        arg = jnp.min(jnp.where(logits == top, lane, N_EXPERTS), axis=1, keepdims=True)
        logits = jnp.where(lane == arg, NEG_INF, logits)
        top0 = top if top0 is None else top0
        e = jnp.exp(top - top0)
        denom = denom + e
        idx_out = jnp.where(out_lane == kk, arg, idx_out)
        exp_out = jnp.where(out_lane == kk, e, exp_out)
    idx_ref[...] = idx_out
    wgt_ref[...] = exp_out / denom


def cross_attention_router(hx, x1, k_mem, v_mem, w_q, w_o, gain, w_router, b_router, seq, tm=512):
    m, d = hx.shape
    mem_len = k_mem.shape[0] // (m // seq)
    n = w_q.shape[1]
    per_b = seq // tm
    const = lambda shape: pl.BlockSpec(shape, lambda i: (0,) * len(shape))
    rows = lambda cols: pl.BlockSpec((tm, cols), lambda i: (i, 0))
    return pl.pallas_call(
        _xattn_body,
        grid=(m // tm,),
        in_specs=[rows(d), rows(d),
                  pl.BlockSpec((mem_len, n), lambda i: (i // per_b, 0)),
                  pl.BlockSpec((mem_len, n), lambda i: (i // per_b, 0)),
                  const((d, n)), const((n, d)), const((1, d)), const((d, N_EXPERTS)), const((1, N_EXPERTS))],
        out_specs=[rows(d), rows(d), rows(LANES), rows(LANES)],
        out_shape=[jax.ShapeDtypeStruct((m, d), F32), jax.ShapeDtypeStruct((m, d), F32),
                   jax.ShapeDtypeStruct((m, LANES), jnp.int32), jax.ShapeDtypeStruct((m, LANES), F32)],
        compiler_params=_params(1),
        name="cross_attention_router",
    )(hx, x1, k_mem, v_mem, w_q, w_o, gain.reshape(1, d), w_router.astype(F32), b_router.reshape(1, -1).astype(F32))


def _routing_plan(top_idx, n_tokens):
    rt = EXPERT_ROW_TILE
    tiles_per_unit = EXPERT_UNIT_ROWS // rt
    slot_onehot = (top_idx[:, :, None] == jnp.arange(N_EXPERTS)[None, None, :]).astype(jnp.int32)
    onehot = slot_onehot.sum(axis=1)
    before = jnp.cumsum(onehot, axis=0) - onehot
    count = onehot.sum(axis=0)
    tiles = (count + rt - 1) // rt
    tile_start = jnp.cumsum(tiles) - tiles
    pos = (slot_onehot * (tile_start * rt + before)[:, None, :]).sum(axis=-1)

    units = (tiles + tiles_per_unit - 1) // tiles_per_unit
    unit_first = jnp.cumsum(units) - units
    n_units = units.sum()
    max_units = N_EXPERTS + (n_tokens * TOP_K) // EXPERT_UNIT_ROWS
    uid = jnp.arange(max_units)
    e_of = jnp.clip(jnp.searchsorted(jnp.cumsum(units), uid, side="right"), 0, N_EXPERTS - 1)
    k_in = uid - unit_first[e_of]
    live = uid < n_units
    last_e = e_of[jnp.maximum(n_units - 1, 0)]
    unit_expert = jnp.where(live, e_of, last_e).astype(jnp.int32)
    unit_start = jnp.where(live, (tile_start[e_of] + k_in * tiles_per_unit) * rt, 0).astype(jnp.int32)
    unit_tiles = jnp.where(live, jnp.minimum(tiles[e_of] - k_in * tiles_per_unit, tiles_per_unit), 0).astype(jnp.int32)
    totals = jnp.stack([n_units, tiles.sum()]).astype(jnp.int32)
    return pos.astype(jnp.int32), unit_expert, unit_start, unit_tiles, totals


def _dispatch_body(pos_hbm, h_ref, xs_in, xs_hbm, pos_smem, sem):
    del xs_in
    i = pl.program_id(0)
    groups, sub, _ = h_ref.shape
    idx_copy = pltpu.make_async_copy(pos_hbm.at[i], pos_smem, sem.at[0])
    idx_copy.start()
    idx_copy.wait()

    def send(g, carry):
        for s in range(sub):
            for kk in range(TOP_K):
                r = pos_smem[g * (sub * TOP_K) + s * TOP_K + kk]
                pltpu.make_async_copy(h_ref.at[g, pl.ds(s, 1), :], xs_hbm.at[pl.ds(r, 1), :], sem.at[1]).start()
        return carry

    lax.fori_loop(0, groups, send, 0)

    def drain(g, carry):
        pltpu.make_async_copy(h_ref.at[0], xs_hbm.at[pl.ds(0, sub), :], sem.at[1]).wait()
        return carry

    lax.fori_loop(0, groups * TOP_K, drain, 0)


def dispatch_rows(h2, pos, p_rows):
    t, d = h2.shape
    tm = GATHER_TOKENS
    sub = 8
    return pl.pallas_call(
        _dispatch_body,
        grid=(t // tm,),
        in_specs=[pl.BlockSpec(memory_space=pl.ANY),
                  pl.BlockSpec((tm // sub, sub, d), lambda i: (i, 0, 0)),
                  pl.BlockSpec(memory_space=pl.ANY)],
        out_specs=pl.BlockSpec(memory_space=pl.ANY),
        out_shape=jax.ShapeDtypeStruct((p_rows, d), h2.dtype),
        scratch_shapes=[pltpu.SMEM((tm * TOP_K,), jnp.int32), pltpu.SemaphoreType.DMA((2,))],
        input_output_aliases={2: 0},
        compiler_params=_params(1),
        name="dispatch_rows",
    )(pos.reshape(t // tm, tm * TOP_K), h2.reshape(t // sub, sub, d), jnp.zeros((p_rows, d), h2.dtype))


def _expert_body(ue_ref, us_ref, un_ref, nu_ref,
                 x_hbm, wgu_ref, bgu_ref, wd_ref, bd_ref, y_hbm,
                 xbuf, actbuf, gubuf, ystage, wgu_bf, wd_f32, wd_bf, sem_x, sem_y, *, n_up, n_down):
    u = pl.program_id(0)
    c = pl.program_id(1)
    rt = EXPERT_ROW_TILE
    chunk = EXPERT_COL_CHUNK
    half = chunk // 2
    quarter = half // 2
    n_live = nu_ref[0]
    live = u < n_live
    start = us_ref[u]
    n_tiles = un_ref[u]

    n_pairs = n_tiles // 2
    odd = n_tiles % 2 == 1

    def span_rows(j, tiles):
        return pl.ds(pl.multiple_of(j * rt, rt), tiles * rt)

    def tile_rows(j):
        return span_rows(j, 1)

    def x_copy(unit, j):
        rows = pl.ds(pl.multiple_of(us_ref[unit] + j * rt, rt), rt)
        return pltpu.make_async_copy(x_hbm.at[rows, :], xbuf.at[tile_rows(j), :], sem_x.at[0])

    def fetch_rows(unit):
        def body(j, carry):
            x_copy(unit, j).start()
            return carry
        lax.fori_loop(0, un_ref[unit], body, 0)

    @pl.when((u == 0) & (c == 0) & live)
    def _():
        fetch_rows(0)

    @pl.when(live & (c == 0))
    def _():
        def body(j, carry):
            x_copy(u, j).wait()
            return carry
        lax.fori_loop(0, n_tiles, body, 0)

    @pl.when((c == n_up) & (u + 1 < n_live))
    def _():
        fetch_rows(u + 1)

    @pl.when(live & (c < n_up))
    def _():
        wgu_bf[...] = wgu_ref[0].astype(BF16)
        bias = bgu_ref[0]

        def project(j, tiles, slot):
            x = xbuf[span_rows(j, tiles), :].astype(BF16)
            gubuf[slot, :tiles * rt, :] = jnp.dot(x, wgu_bf[...], preferred_element_type=F32) + bias

        def activate(j, tiles, slot):
            gu = gubuf[slot, :tiles * rt, :]
            even = (lax.broadcasted_iota(jnp.int32, (tiles * rt, half), 1) % 2) == 0
            lo = gu[:, :half]
            hi = gu[:, half:]
            gate = jnp.where(even, lo, pltpu.roll(hi, 1, axis=1))
            up = jnp.where(even, pltpu.roll(lo, half - 1, axis=1), hi)
            gate = jnp.minimum(gate, SWIGLU_LIMIT)
            up = jnp.clip(up, -SWIGLU_LIMIT, SWIGLU_LIMIT)
            act = (up + 1.0) * gate * _sigmoid(SWIGLU_ALPHA * gate)
            actbuf[c, span_rows(j, tiles), :] = act.astype(BF16)

        @pl.when(n_pairs >= 1)
        def _():
            project(0, 2, 0)

            def body(i, carry):
                activate(2 * (i - 1), 2, (i - 1) % 2)
                project(2 * i, 2, i % 2)
                return carry

            lax.fori_loop(1, n_pairs, body, 0)

        @pl.when((n_pairs >= 1) & odd)
        def _():
            activate(2 * (n_pairs - 1), 2, (n_pairs - 1) % 2)
            project(2 * n_pairs, 1, n_pairs % 2)

        @pl.when((n_pairs >= 1) & jnp.logical_not(odd))
        def _():
            activate(2 * (n_pairs - 1), 2, (n_pairs - 1) % 2)

        @pl.when(n_pairs == 0)
        def _():
            project(0, 1, 0)

        @pl.when(odd)
        def _():
            activate(2 * n_pairs, 1, n_pairs % 2)

    @pl.when(live & (c >= n_up))
    def _():
        cd = c - n_up
        for g in range(chunk // LANES):
            lanes = slice(g * LANES, (g + 1) * LANES)
            for f in range(n_up):
                base = f * half
                wd_f32[g, pl.ds(base, quarter, stride=2), :] = wd_ref[0, base:base + quarter, lanes]
                wd_f32[g, pl.ds(base + 1, quarter, stride=2), :] = wd_ref[0, base + quarter:base + half, lanes]
            wd_bf[:, lanes] = wd_f32[g].astype(BF16)
        bias = bd_ref[0]

        def y_copy(j, tiles, slot):
            rows = pl.ds(pl.multiple_of(start + j * rt, rt), tiles * rt)
            cols = pl.ds(pl.multiple_of(cd * chunk, chunk), chunk)
            return pltpu.make_async_copy(ystage.at[slot, :tiles * rt, :], y_hbm.at[rows, cols], sem_y.at[slot])

        def emit(j, tiles, slot):
            act = jnp.concatenate([actbuf[f, span_rows(j, tiles), :] for f in range(n_up)], axis=1)
            ystage[slot, :tiles * rt, :] = jnp.dot(act, wd_bf[...], preferred_element_type=F32) + bias
            y_copy(j, tiles, slot).start()

        def pair(i, carry):
            @pl.when(i >= 2)
            def _():
                y_copy(2 * (i - 2), 2, i % 2).wait()

            emit(2 * i, 2, i % 2)
            return carry

        lax.fori_loop(0, n_pairs, pair, 0)

        @pl.when(n_pairs >= 2)
        def _():
            y_copy(2 * (n_pairs - 2), 2, n_pairs % 2).wait()

        @pl.when(odd)
        def _():
            emit(2 * n_pairs, 1, n_pairs % 2)

        @pl.when(n_pairs >= 1)
        def _():
            y_copy(2 * (n_pairs - 1), 2, (n_pairs - 1) % 2).wait()

        @pl.when(odd)
        def _():
            y_copy(2 * n_pairs, 1, n_pairs % 2).wait()

    @pl.when((u == pl.num_programs(0) - 1) & (c == n_up + n_down - 1))
    def _():
        zero_rows = xbuf.at[:rt, :]
        zero_rows[...] = jnp.zeros(zero_rows.shape, F32)
        used_tiles = nu_ref[1]

        def pad_copy(j):
            return pltpu.make_async_copy(zero_rows, y_hbm.at[pl.ds(pl.multiple_of(j * rt, rt), rt), :], sem_x.at[0])

        def pad_start(j, carry):
            pad_copy(j).start()
            return carry

        def pad_wait(j, carry):
            pad_copy(j).wait()
            return carry

        lax.fori_loop(used_tiles, y_hbm.shape[0] // rt, pad_start, 0)
        lax.fori_loop(used_tiles, y_hbm.shape[0] // rt, pad_wait, 0)


def expert_ffn(x_sorted, w_gate_up, b_gate_up, w_down, b_down, unit_expert, unit_start, unit_tiles, totals):
    p_rows, d = x_sorted.shape
    n_exp, _, two_ff = w_gate_up.shape
    d_ff = two_ff // 2
    chunk = EXPERT_COL_CHUNK
    n_up = two_ff // chunk
    n_down = d // chunk
    n_steps = n_up + n_down
    max_units = unit_expert.shape[0]
    rt = EXPERT_ROW_TILE

    def up_idx(u, c, nu):
        return jnp.where(u < nu[0], jnp.minimum(c, n_up - 1), n_up - 1)

    def down_idx(u, c, nu):
        return jnp.where(u < nu[0], jnp.maximum(c - n_up, 0), n_down - 1)

    grid_spec = pltpu.PrefetchScalarGridSpec(
        num_scalar_prefetch=4,
        grid=(totals[0], n_steps),
        in_specs=[pl.BlockSpec(memory_space=pl.ANY),
                  pl.BlockSpec((1, d, chunk), lambda u, c, ue, us, un, nu: (ue[u], 0, up_idx(u, c, nu))),
                  pl.BlockSpec((1, 1, chunk), lambda u, c, ue, us, un, nu: (ue[u], 0, up_idx(u, c, nu))),
                  pl.BlockSpec((1, d_ff, chunk), lambda u, c, ue, us, un, nu: (ue[u], 0, down_idx(u, c, nu))),
                  pl.BlockSpec((1, 1, chunk), lambda u, c, ue, us, un, nu: (ue[u], 0, down_idx(u, c, nu)))],
        out_specs=pl.BlockSpec(memory_space=pl.ANY),
        scratch_shapes=[pltpu.VMEM((EXPERT_UNIT_ROWS, d), F32),
                        pltpu.VMEM((n_up, EXPERT_UNIT_ROWS, chunk // 2), BF16),
                        pltpu.VMEM((2, 2 * rt, chunk), F32),
                        pltpu.VMEM((2, 2 * rt, chunk), F32),
                        pltpu.VMEM((d, chunk), BF16),
                        pltpu.VMEM((chunk // LANES, d_ff, LANES), F32),
                        pltpu.VMEM((d_ff, chunk), BF16),
                        pltpu.SemaphoreType.DMA((1,)),
                        pltpu.SemaphoreType.DMA((2,))],
    )
    return pl.pallas_call(
        functools.partial(_expert_body, n_up=n_up, n_down=n_down),
        grid_spec=grid_spec,
        out_shape=jax.ShapeDtypeStruct((p_rows, d), F32),
        compiler_params=_params(2),
        name="expert_ffn",
    )(unit_expert, unit_start, unit_tiles, totals,
      x_sorted, w_gate_up, b_gate_up.reshape(n_exp, 1, two_ff), w_down, b_down.reshape(n_exp, 1, d))


def _combine_body(pos_hbm, x2_ref, w_ref, g_ref, y_hbm, o_ref, pos_smem, ybuf, sem, *, final_norm):
    i = pl.program_id(0)
    n = pl.num_programs(0)
    groups, sub, _ = x2_ref.shape
    slot = i % 2

    def fetch(step, into):
        per_step = groups * sub * TOP_K
        base = pl.multiple_of(into * per_step, per_step)
        idx_copy = pltpu.make_async_copy(pos_hbm.at[step], pos_smem.at[pl.ds(base, per_step)], sem.at[2])
        idx_copy.start()
        idx_copy.wait()

        def recv(g, carry):
            for s in range(sub):
                for kk in range(TOP_K):
                    r = pos_smem[base + g * (sub * TOP_K) + s * TOP_K + kk]
                    pltpu.make_async_copy(y_hbm.at[pl.ds(r, 1), :], ybuf.at[into, kk, g, pl.ds(s, 1), :],
                                          sem.at[into]).start()
            return carry

        lax.fori_loop(0, groups, recv, 0)

    @pl.when(i == 0)
    def _():
        fetch(0, 0)

    @pl.when(i + 1 < n)
    def _():
        fetch(i + 1, 1 - slot)

    def drain(g, carry):
        pltpu.make_async_copy(y_hbm.at[pl.ds(0, sub), :], ybuf.at[slot, 0, 0], sem.at[slot]).wait()
        return carry

    lax.fori_loop(0, groups * TOP_K, drain, 0)

    x3 = x2_ref[...]
    for kk in range(TOP_K):
        x3 = x3 + w_ref[:, :, kk:kk + 1] * ybuf[slot, kk]
    o_ref[...] = _rms(x3, g_ref[...]) if final_norm else x3


def combine(x2, y_sorted, pos, weights, gain):
    m, d = x2.shape
    tm = GATHER_TOKENS
    sub = 8
    final_norm = gain is not None
    gain = gain if final_norm else jnp.ones((d,), F32)
    rows = lambda width: pl.BlockSpec((tm // sub, sub, width), lambda i: (i, 0, 0))
    out = pl.pallas_call(
        functools.partial(_combine_body, final_norm=final_norm),
        grid=(m // tm,),
        in_specs=[pl.BlockSpec(memory_space=pl.ANY),
                  rows(d), rows(LANES),
                  pl.BlockSpec((1, d), lambda i: (0, 0)),
                  pl.BlockSpec(memory_space=pl.ANY)],
        out_specs=rows(d),
        out_shape=jax.ShapeDtypeStruct((m // sub, sub, d), F32),
        scratch_shapes=[pltpu.SMEM((2 * tm * TOP_K,), jnp.int32),
                        pltpu.VMEM((2, TOP_K, tm // sub, sub, d), F32),
                        pltpu.SemaphoreType.DMA((3,))],
        compiler_params=_params(1),
        name="combine",
    )(pos.reshape(m // tm, tm * TOP_K), x2.reshape(m // sub, sub, d), weights.reshape(m // sub, sub, LANES),
      gain.reshape(1, d), y_sorted)
    return out.reshape(m, d)


def kernel(x, mem, rel_bias_table, mix_norm, w_in, diff_lambda_q1, diff_lambda_k1, diff_lambda_q2, diff_lambda_k2, diff_subln, w_branch_moba, w_branch_diff, w_mix_out, xattn_norm, mem_norm, w_xq, w_xk, w_xv, w_xo, ffn_norm, w_router, b_router, w_gate_up, b_gate_up, w_down, b_down, final_norm):
    batch, seq, d = x.shape
    n_tok = batch * seq
    x2d = x.reshape(n_tok, d)
    near, far = _bias_tiles(rel_bias_table)
    diff_col0 = 3 * MOBA_WIDTH
    gate_col0 = diff_col0 + 3 * DIFF_WIDTH
    cols = jnp.arange(w_in.shape[2])
    is_q = (cols < MOBA_WIDTH) | ((cols >= diff_col0) & (cols < diff_col0 + DIFF_WIDTH))
    col_scale = jnp.where(is_q, ATTN_SCALE * LOG2E, 1.0)
    p_rows = n_tok * TOP_K + N_EXPERTS * EXPERT_ROW_TILE
    for l in range(w_in.shape[0]):
        h = rmsnorm_rows(x2d, mix_norm[l])
        y = matmul_colscale(h, w_in[l], col_scale)
        o_moba = moba_attention(y, near[:MOBA_HEADS], far[:MOBA_HEADS], batch, seq)
        o_diff = diff_attention(y, near[MOBA_HEADS:], far[MOBA_HEADS:], diff_lambda_q1[l], diff_lambda_k1[l],
                                diff_lambda_q2[l], diff_lambda_k2[l], diff_subln[l], batch, seq, diff_col0)
        merged = branch_merge(o_moba, o_diff, w_branch_moba[l].astype(BF16), w_branch_diff[l].astype(BF16),
                              y, gate_col0)
        x1, hx = mixout(merged, w_mix_out[l].astype(BF16), x2d, xattn_norm[l])
        k_mem, v_mem = memory_kv(mem.reshape(-1, d), mem_norm[l], w_xk[l].astype(BF16), w_xv[l].astype(BF16),
                                 mem.shape[1])
        x2, h2, idx_pad, wgt_pad = cross_attention_router(hx, x1, k_mem, v_mem, w_xq[l].astype(BF16),
                                                          w_xo[l].astype(BF16), ffn_norm[l], w_router[l],
                                                          b_router[l], seq)
        pos, unit_expert, unit_start, unit_tiles, totals = _routing_plan(idx_pad[:, :TOP_K], n_tok)
        x_sorted = dispatch_rows(h2, pos, p_rows)
        y_sorted = expert_ffn(x_sorted, w_gate_up[l], b_gate_up[l], w_down[l], b_down[l],
                              unit_expert, unit_start, unit_tiles, totals)
        last = l == w_in.shape[0] - 1
        x2d = combine(x2, y_sorted, pos, wgt_pad, final_norm if last else None)
    return x2d.reshape(batch, seq, d)
```

```python
import collections
import functools
import math

import jax
import jax.numpy as jnp
from jax import lax
from jax.experimental import pallas as pl
from jax.experimental.pallas import tpu as pltpu

F32 = jnp.float32
BF16 = jnp.bfloat16
NEG_INF = float("-inf")

D_MODEL = 2048
HEAD_DIM = 128
MOBA_HEADS = 8
MOBA_WIDTH = MOBA_HEADS * HEAD_DIM
MOBA_BLOCK = 256
MOBA_TOPK = 3
DIFF_HEADS = 4
DIFF_WIDTH = DIFF_HEADS * 2 * HEAD_DIM
REL_BUCKETS = 32
REL_MAX_DISTANCE = 128
XATTN_HEADS = 4
N_EXPERTS = 32
TOP_K = 4
SWIGLU_LIMIT = 7.0
SWIGLU_ALPHA = 1.702
NORM_EPS = 1e-5
LAMBDA_INIT = 0.8 - 0.6 * math.exp(-0.3 * 0)
ATTN_SCALE = HEAD_DIM ** -0.5
LOG2E = math.log2(math.e)

ATTN_TILE = MOBA_BLOCK
MOBA_HEADS_PER_STEP = 4
DIFF_HEADS_PER_STEP = 2
LANES = 128
EXPERT_ROW_TILE = 256
EXPERT_UNIT_ROWS = 1536
EXPERT_COL_CHUNK = 512
GATHER_TOKENS = 256
VMEM_LIMIT = 56 * 1024 * 1024


def _params(n_axes):
    return pltpu.CompilerParams(dimension_semantics=("arbitrary",) * n_axes,
                                vmem_limit_bytes=VMEM_LIMIT)


def _rms(x, gain):
    return x * lax.rsqrt(jnp.mean(x * x, axis=-1, keepdims=True) + NORM_EPS) * gain


def _sigmoid(x):
    return 1.0 / (1.0 + jnp.exp(-x))


def _dot_nt(a, b):
    return lax.dot_general(a, b, (((1,), (1,)), ((), ())), preferred_element_type=F32)


def _rmsnorm_body(x_ref, g_ref, o_ref):
    o_ref[...] = _rms(x_ref[...], g_ref[...]).astype(o_ref.dtype)


def rmsnorm_rows(x, gain, tm=512):
    t, d = x.shape
    return pl.pallas_call(
        _rmsnorm_body,
        grid=(t // tm,),
        in_specs=[pl.BlockSpec((tm, d), lambda i: (i, 0)),
                  pl.BlockSpec((1, d), lambda i: (0, 0))],
        out_specs=pl.BlockSpec((tm, d), lambda i: (i, 0)),
        out_shape=jax.ShapeDtypeStruct((t, d), BF16),
        compiler_params=_params(1),
        name="rmsnorm_rows",
    )(x, gain.reshape(1, d))


def _matmul_body(a_ref, w_ref, cs_ref, o_ref, w_bf):
    @pl.when(pl.program_id(1) == 0)
    def _():
        w_bf[...] = w_ref[...].astype(BF16)

    acc = jnp.dot(a_ref[...], w_bf[...], preferred_element_type=F32)
    o_ref[...] = (acc * cs_ref[...]).astype(o_ref.dtype)


def matmul_colscale(a, w, col_scale, tm=1024, tn=1024, out_dtype=BF16):
    m, k = a.shape
    n = w.shape[1]
    tm, tn = min(tm, m), min(tn, n)
    return pl.pallas_call(
        _matmul_body,
        grid=(n // tn, m // tm),
        in_specs=[pl.BlockSpec((tm, k), lambda j, i: (i, 0)),
                  pl.BlockSpec((k, tn), lambda j, i: (0, j)),
                  pl.BlockSpec((1, tn), lambda j, i: (0, j))],
        out_specs=pl.BlockSpec((tm, tn), lambda j, i: (i, j)),
        out_shape=jax.ShapeDtypeStruct((m, n), out_dtype),
        scratch_shapes=[pltpu.VMEM((k, tn), BF16)],
        compiler_params=_params(2),
        name="matmul",
    )(a, w, col_scale.reshape(1, n).astype(F32))


def _rel_bucket(dist):
    n = jnp.maximum(dist, 0)
    max_exact = REL_BUCKETS // 2
    nf = jnp.maximum(n, max_exact).astype(F32)
    large = max_exact + (jnp.log(nf / max_exact) / math.log(REL_MAX_DISTANCE / max_exact)
                         * (REL_BUCKETS - max_exact)).astype(jnp.int32)
    return jnp.where(n < max_exact, n, jnp.minimum(large, REL_BUCKETS - 1))


def _bias_tiles(table):
    t = ATTN_TILE
    r = jnp.arange(t)[:, None]
    c = jnp.arange(t)[None, :]
    dist = jnp.stack([r - c, t + r - c])
    onehot = (_rel_bucket(dist)[..., None] == jnp.arange(REL_BUCKETS)).astype(F32)
    near = jnp.einsum("irck,kh->hirc", onehot, table.astype(F32), precision=lax.Precision.HIGHEST)
    far = jnp.broadcast_to(table[REL_BUCKETS - 1][:, None, None], (table.shape[1], 1, t))
    return near * LOG2E, far.astype(F32) * LOG2E


def _lane_halves(x, op):
    return op(x[:, :LANES], x[:, LANES:])


AttnStream = collections.namedtuple("AttnStream", "q k_rows v_rows near cfar pen_at s_scr acc lsum mpast")


def _causal_attention(streams, qi):
    t = ATTN_TILE
    own_slot = streams[0].s_scr.shape[0] - 2
    prev_slot = own_slot + 1
    prev = jnp.maximum(qi - 1, 0)
    n_far = prev
    n_pairs = (n_far + 1) // 2
    row = lax.broadcasted_iota(jnp.int32, (t, t), 0)
    col = lax.broadcasted_iota(jnp.int32, (t, t), 1)
    has_prev = jnp.where(qi >= 1, 0.0, NEG_INF)

    def masked(st, s, n):
        return s if st.pen_at is None else s + st.pen_at(n)

    def tile_max(m, s):
        return jnp.maximum(m, _lane_halves(s, jnp.maximum))

    def tile_sum(l, p):
        return l + _lane_halves(p, jnp.add)

    m_near = []
    for st in streams:
        s2 = _dot_nt(st.q, jnp.concatenate([st.k_rows(qi, 1), st.k_rows(prev, 1)], axis=0))
        s_own = jnp.where(col <= row, s2[:, :t] + st.near(0), NEG_INF)
        s_prev = masked(st, s2[:, t:] + st.near(1) + has_prev, prev)
        st.s_scr[own_slot] = s_own
        st.s_scr[prev_slot] = s_prev - st.cfar
        m_near.append(tile_max(_lane_halves(s_own, jnp.maximum), s_prev))

    def pair_scores(i, m_far):
        second_is_far = jnp.where(2 * i + 1 < n_far, 0.0, NEG_INF)
        out = []
        for st, m in zip(streams, m_far):
            s = _dot_nt(st.q, st.k_rows(2 * i, 2))
            for half in range(2):
                sh = masked(st, s[:, half * t:(half + 1) * t], 2 * i + half)
                if half == 1:
                    sh = sh + second_is_far
                st.s_scr[2 * i + half] = sh
                m = tile_max(m, sh)
            out.append(m)
        return tuple(out)

    m_far = lax.fori_loop(0, n_pairs, pair_scores, tuple(jnp.full((t, LANES), NEG_INF, F32) for _ in streams))

    for st, mn, mf in zip(streams, m_near, m_far):
        m_row = jnp.maximum(jnp.max(mn, axis=1, keepdims=True),
                            jnp.max(mf, axis=1, keepdims=True) + st.cfar)
        mp = m_row - st.cfar
        p_own = jnp.exp2(st.s_scr[own_slot] - m_row)
        p_prev = jnp.exp2(st.s_scr[prev_slot] - mp)
        st.mpast[...] = jnp.broadcast_to(mp, st.mpast.shape)
        st.lsum[...] = tile_sum(_lane_halves(p_own, jnp.add), p_prev)
        st.acc[...] = jnp.dot(jnp.concatenate([p_own, p_prev], axis=1).astype(BF16),
                              jnp.concatenate([st.v_rows(qi, 1), st.v_rows(prev, 1)], axis=0),
                              preferred_element_type=F32)

    def pair_weights(i, carry):
        for st in streams:
            mp = jnp.concatenate([st.mpast[...]] * (2 * t // LANES), axis=1)
            p = jnp.exp2(jnp.concatenate([st.s_scr[2 * i], st.s_scr[2 * i + 1]], axis=1) - mp)
            st.lsum[...] = tile_sum(tile_sum(st.lsum[...], p[:, :t]), p[:, t:])
            st.acc[...] += jnp.dot(p.astype(BF16), st.v_rows(2 * i, 2), preferred_element_type=F32)
        return carry

    lax.fori_loop(0, n_pairs, pair_weights, 0)
    return [(st.acc[...], jnp.sum(st.lsum[...], axis=1, keepdims=True)) for st in streams]


def _block_rows(n, w=1):
    return pl.ds(pl.multiple_of(n * ATTN_TILE, ATTN_TILE), w * ATTN_TILE)


def _moba_body(q_ref, k_ref, v_ref, near_ref, far_ref, o_ref, kmean_ref, s_scr, acc_scr, lsum_scr, mpast_scr, *,
               n_blocks):
    qi = pl.program_id(2)
    t = ATTN_TILE
    dh = HEAD_DIM
    heads = MOBA_HEADS_PER_STEP

    @pl.when(qi == 0)
    def _():
        for j in range(heads):
            for n in range(n_blocks):
                kmean_ref[j, n:n + 1, :] = jnp.mean(k_ref[n * t:(n + 1) * t, j * dh:(j + 1) * dh].astype(F32),
                                                    axis=0, keepdims=True)

    def stream(j):
        cols = slice(j * dh, (j + 1) * dh)
        q = q_ref[:, cols]
        gate = lax.dot_general(kmean_ref[j], q.astype(F32), (((1,), (1,)), ((), ())),
                               precision=lax.Precision.HIGHEST, preferred_element_type=F32)
        valid = lax.broadcasted_iota(jnp.int32, gate.shape, 0) < qi
        g = jnp.where(valid, gate, NEG_INF)
        kth = g
        for _ in range(MOBA_TOPK - 1):
            top = jnp.max(kth, axis=0, keepdims=True)
            kth = jnp.where(kth == top, NEG_INF, kth)
        third = jnp.max(kth, axis=0, keepdims=True)
        pen_t = jnp.where(valid & (g >= third), 0.0, NEG_INF)
        pen_t = jnp.concatenate([pen_t, jnp.full((LANES - n_blocks, t), NEG_INF, F32)], axis=0)
        pen = pen_t.T
        blk = lax.broadcasted_iota(jnp.int32, pen.shape, 1)

        def pen_at(n):
            return jnp.max(jnp.where(blk == n, pen, NEG_INF), axis=1, keepdims=True)

        return AttnStream(q, lambda n, w: k_ref[_block_rows(n, w), cols], lambda n, w: v_ref[_block_rows(n, w), cols],
                          lambda i: near_ref[j, i], far_ref[j][:, :1], pen_at, s_scr.at[j],
                          acc_scr.at[j], lsum_scr.at[j], mpast_scr.at[j])

    results = _causal_attention([stream(j) for j in range(heads)], qi)
    for j, (acc, l) in enumerate(results):
        o_ref[:, j * dh:(j + 1) * dh] = (acc / l).astype(o_ref.dtype)


def moba_attention(y, near, far, batch, seq):
    t = ATTN_TILE
    nq = seq // t
    hs = MOBA_HEADS_PER_STEP
    groups = MOBA_HEADS // hs
    w = hs * HEAD_DIM
    return pl.pallas_call(
        functools.partial(_moba_body, n_blocks=nq),
        grid=(batch, groups, nq),
        in_specs=[pl.BlockSpec((t, w), lambda b, g, i: (b * nq + i, g)),
                  pl.BlockSpec((seq, w), lambda b, g, i: (b, groups + g)),
                  pl.BlockSpec((seq, w), lambda b, g, i: (b, 2 * groups + g)),
                  pl.BlockSpec((hs, 2, t, t), lambda b, g, i: (g, 0, 0, 0)),
                  pl.BlockSpec((hs, 1, t), lambda b, g, i: (g, 0, 0))],
        out_specs=pl.BlockSpec((t, w), lambda b, g, i: (b * nq + i, g)),
        out_shape=jax.ShapeDtypeStruct((batch * seq, MOBA_WIDTH), BF16),
        scratch_shapes=[pltpu.VMEM((hs, nq, HEAD_DIM), F32),
                        pltpu.VMEM((hs, nq + 2, t, t), F32),
                        pltpu.VMEM((hs, t, HEAD_DIM), F32),
                        pltpu.VMEM((hs, t, LANES), F32),
                        pltpu.VMEM((hs, t, LANES), F32)],
        compiler_params=_params(3),
        name="moba_attention",
    )(y, y, y, near, far)


def _diff_body(q_ref, k_ref, v_ref, near_ref, far_ref, lq1_ref, lk1_ref, lq2_ref, lk2_ref, subln_ref, o_ref,
               s_scr, acc_scr, lsum_scr, mpast_scr):
    qi = pl.program_id(2)
    dh = HEAD_DIM
    w = 2 * dh
    heads = DIFF_HEADS_PER_STEP

    def stream(hh, j):
        qk_cols = slice(hh * w + j * dh, hh * w + (j + 1) * dh)
        v_cols = slice(hh * w, (hh + 1) * w)
        return AttnStream(q_ref[:, qk_cols], lambda n, nb: k_ref[_block_rows(n, nb), qk_cols],
                          lambda n, nb: v_ref[_block_rows(n, nb), v_cols], lambda i: near_ref[hh, i],
                          far_ref[hh][:, :1], None, s_scr.at[2 * hh + j],
                          acc_scr.at[2 * hh + j], lsum_scr.at[2 * hh + j], mpast_scr.at[2 * hh + j])

    results = _causal_attention([stream(hh, j) for hh in range(heads) for j in range(2)], qi)
    lam = (jnp.exp(jnp.sum(lq1_ref[...] * lk1_ref[...], axis=1, keepdims=True))
           - jnp.exp(jnp.sum(lq2_ref[...] * lk2_ref[...], axis=1, keepdims=True)) + LAMBDA_INIT)
    for hh in range(heads):
        (acc1, l1), (acc2, l2) = results[2 * hh], results[2 * hh + 1]
        o = acc1 / l1 - lam * (acc2 / l2)
        o_ref[:, hh * w:(hh + 1) * w] = (_rms(o, subln_ref[...]) * (1.0 - LAMBDA_INIT)).astype(o_ref.dtype)


def diff_attention(y, near, far, lq1, lk1, lq2, lk2, subln, batch, seq, col0):
    t = ATTN_TILE
    nq = seq // t
    hs = DIFF_HEADS_PER_STEP
    groups = DIFF_HEADS // hs
    w = hs * 2 * HEAD_DIM
    base = col0 // w
    vec = lambda a: a.reshape(1, -1).astype(F32)
    small = lambda n: pl.BlockSpec((1, n), lambda b, g, i: (0, 0))
    return pl.pallas_call(
        _diff_body,
        grid=(batch, groups, nq),
        in_specs=[pl.BlockSpec((t, w), lambda b, g, i: (b * nq + i, base + g)),
                  pl.BlockSpec((seq, w), lambda b, g, i: (b, base + groups + g)),
                  pl.BlockSpec((seq, w), lambda b, g, i: (b, base + 2 * groups + g)),
                  pl.BlockSpec((hs, 2, t, t), lambda b, g, i: (g, 0, 0, 0)),
                  pl.BlockSpec((hs, 1, t), lambda b, g, i: (g, 0, 0)),
                  small(HEAD_DIM), small(HEAD_DIM), small(HEAD_DIM), small(HEAD_DIM), small(2 * HEAD_DIM)],
        out_specs=pl.BlockSpec((t, w), lambda b, g, i: (b * nq + i, g)),
        out_shape=jax.ShapeDtypeStruct((batch * seq, DIFF_WIDTH), BF16),
        scratch_shapes=[pltpu.VMEM((2 * hs, nq + 2, t, t), F32),
                        pltpu.VMEM((2 * hs, t, 2 * HEAD_DIM), F32),
                        pltpu.VMEM((2 * hs, t, LANES), F32),
                        pltpu.VMEM((2 * hs, t, LANES), F32)],
        compiler_params=_params(3),
        name="diff_attention",
    )(y, y, y, near, far, vec(lq1), vec(lk1), vec(lq2), vec(lk2), vec(subln))


def _branch_body(om_ref, od_ref, wm_ref, wd_ref, ga_ref, gb_ref, o_ref):
    a = jnp.dot(om_ref[...], wm_ref[...], preferred_element_type=F32)
    b = jnp.dot(od_ref[...], wd_ref[...], preferred_element_type=F32)
    o_ref[...] = (_sigmoid(ga_ref[...].astype(F32)) * a + _sigmoid(gb_ref[...].astype(F32)) * b).astype(o_ref.dtype)


def branch_merge(o_moba, o_diff, w_m, w_d, y, gate_col0, tm=1024, tn=1024):
    m, k = o_moba.shape
    n = w_m.shape[1]
    g0 = gate_col0 // tn
    nj = n // tn
    return pl.pallas_call(
        _branch_body,
        grid=(nj, m // tm),
        in_specs=[pl.BlockSpec((tm, k), lambda j, i: (i, 0)),
                  pl.BlockSpec((tm, k), lambda j, i: (i, 0)),
                  pl.BlockSpec((k, tn), lambda j, i: (0, j)),
                  pl.BlockSpec((k, tn), lambda j, i: (0, j)),
                  pl.BlockSpec((tm, tn), lambda j, i: (i, g0 + j)),
                  pl.BlockSpec((tm, tn), lambda j, i: (i, g0 + nj + j))],
        out_specs=pl.BlockSpec((tm, tn), lambda j, i: (i, j)),
        out_shape=jax.ShapeDtypeStruct((m, n), BF16),
        compiler_params=_params(2),
        name="branch_merge",
    )(o_moba, o_diff, w_m, w_d, y, y)


def _mixout_body(a_ref, w_ref, x_ref, g_ref, x1_ref, h_ref):
    x1 = x_ref[...] + jnp.dot(a_ref[...], w_ref[...], preferred_element_type=F32)
    x1_ref[...] = x1
    h_ref[...] = _rms(x1, g_ref[...]).astype(h_ref.dtype)


def mixout(merged, w, x, gain, tm=512):
    m, k = merged.shape
    n = w.shape[1]
    return pl.pallas_call(
        _mixout_body,
        grid=(m // tm,),
        in_specs=[pl.BlockSpec((tm, k), lambda i: (i, 0)),
                  pl.BlockSpec((k, n), lambda i: (0, 0)),
                  pl.BlockSpec((tm, n), lambda i: (i, 0)),
                  pl.BlockSpec((1, n), lambda i: (0, 0))],
        out_specs=[pl.BlockSpec((tm, n), lambda i: (i, 0)),
                   pl.BlockSpec((tm, n), lambda i: (i, 0))],
        out_shape=[jax.ShapeDtypeStruct((m, n), F32), jax.ShapeDtypeStruct((m, n), BF16)],
        compiler_params=_params(1),
        name="mixout",
    )(merged, w, x, gain.reshape(1, n))


def _memkv_body(mem_ref, g_ref, wk_ref, wv_ref, k_ref, v_ref):
    mn = _rms(mem_ref[...], g_ref[...]).astype(BF16)
    k_ref[...] = jnp.dot(mn, wk_ref[...], preferred_element_type=F32).astype(k_ref.dtype)
    v_ref[...] = jnp.dot(mn, wv_ref[...], preferred_element_type=F32).astype(v_ref.dtype)


def memory_kv(mem2d, gain, w_k, w_v, rows):
    m, d = mem2d.shape
    n = w_k.shape[1]
    return pl.pallas_call(
        _memkv_body,
        grid=(m // rows,),
        in_specs=[pl.BlockSpec((rows, d), lambda i: (i, 0)),
                  pl.BlockSpec((1, d), lambda i: (0, 0)),
                  pl.BlockSpec((d, n), lambda i: (0, 0)),
                  pl.BlockSpec((d, n), lambda i: (0, 0))],
        out_specs=[pl.BlockSpec((rows, n), lambda i: (i, 0)),
                   pl.BlockSpec((rows, n), lambda i: (i, 0))],
        out_shape=[jax.ShapeDtypeStruct((m, n), BF16), jax.ShapeDtypeStruct((m, n), BF16)],
        compiler_params=_params(1),
        name="memory_kv",
    )(mem2d, gain.reshape(1, d), w_k, w_v)


def _xattn_body(h_ref, x1_ref, k_ref, v_ref, wq_ref, wo_ref, g_ref, wr_ref, br_ref,
                x2_ref, h2_ref, idx_ref, wgt_ref):
    q = jnp.dot(h_ref[...], wq_ref[...], preferred_element_type=F32).astype(BF16)
    outs = []
    for hh in range(XATTN_HEADS):
        sl = slice(hh * HEAD_DIM, (hh + 1) * HEAD_DIM)
        s = _dot_nt(q[:, sl], k_ref[:, sl]) * ATTN_SCALE
        p = jnp.exp(s - jnp.max(s, axis=1, keepdims=True))
        o = jnp.dot(p.astype(BF16), v_ref[:, sl], preferred_element_type=F32)
        outs.append((o / jnp.sum(p, axis=1, keepdims=True)).astype(BF16))
    o = jnp.concatenate(outs, axis=1)
    x2 = x1_ref[...] + jnp.dot(o, wo_ref[...], preferred_element_type=F32)
    x2_ref[...] = x2
    h2 = _rms(x2, g_ref[...])
    h2_ref[...] = h2

    h_hi = h2.astype(BF16)
    h_lo = (h2 - h_hi.astype(F32)).astype(BF16)
    w_hi = wr_ref[...].astype(BF16)
    w_lo = (wr_ref[...] - w_hi.astype(F32)).astype(BF16)
    logits = (jnp.dot(h_hi, w_hi, preferred_element_type=F32) + jnp.dot(h_lo, w_hi, preferred_element_type=F32)
              + jnp.dot(h_hi, w_lo, preferred_element_type=F32) + br_ref[...])
    lane = lax.broadcasted_iota(jnp.int32, logits.shape, 1)
    out_lane = lax.broadcasted_iota(jnp.int32, idx_ref.shape, 1)
    idx_out = jnp.zeros(idx_ref.shape, jnp.int32)
    exp_out = jnp.zeros(wgt_ref.shape, F32)
    denom = jnp.zeros((logits.shape[0], 1), F32)
    top0 = None
    for kk in range(TOP_K):
        top = jnp.max(logits, axis=1, keepdims=True)
        arg = jnp.min(jnp.where(logits == top, lane, N_EXPERTS), axis=1, keepdims=True)
        logits = jnp.where(lane == arg, NEG_INF, logits)
        top0 = top if top0 is None else top0
        e = jnp.exp(top - top0)
        denom = denom + e
        idx_out = jnp.where(out_lane == kk, arg, idx_out)
        exp_out = jnp.where(out_lane == kk, e, exp_out)
    idx_ref[...] = idx_out
    wgt_ref[...] = exp_out / denom


def cross_attention_router(hx, x1, k_mem, v_mem, w_q, w_o, gain, w_router, b_router, seq, tm=512):
    m, d = hx.shape
    mem_len = k_mem.shape[0] // (m // seq)
    n = w_q.shape[1]
    per_b = seq // tm
    const = lambda shape: pl.BlockSpec(shape, lambda i: (0,) * len(shape))
    rows = lambda cols: pl.BlockSpec((tm, cols), lambda i: (i, 0))
    return pl.pallas_call(
        _xattn_body,
        grid=(m // tm,),
        in_specs=[rows(d), rows(d),
                  pl.BlockSpec((mem_len, n), lambda i: (i // per_b, 0)),
                  pl.BlockSpec((mem_len, n), lambda i: (i // per_b, 0)),
                  const((d, n)), const((n, d)), const((1, d)), const((d, N_EXPERTS)), const((1, N_EXPERTS))],
        out_specs=[rows(d), rows(d), rows(LANES), rows(LANES)],
        out_shape=[jax.ShapeDtypeStruct((m, d), F32), jax.ShapeDtypeStruct((m, d), F32),
                   jax.ShapeDtypeStruct((m, LANES), jnp.int32), jax.ShapeDtypeStruct((m, LANES), F32)],
        compiler_params=_params(1),
        name="cross_attention_router",
    )(hx, x1, k_mem, v_mem, w_q, w_o, gain.reshape(1, d), w_router.astype(F32), b_router.reshape(1, -1).astype(F32))


def _routing_plan(top_idx, n_tokens):
    rt = EXPERT_ROW_TILE
    tiles_per_unit = EXPERT_UNIT_ROWS // rt
    slot_onehot = (top_idx[:, :, None] == jnp.arange(N_EXPERTS)[None, None, :]).astype(jnp.int32)
    onehot = slot_onehot.sum(axis=1)
    before = jnp.cumsum(onehot, axis=0) - onehot
    count = onehot.sum(axis=0)
    tiles = (count + rt - 1) // rt
    tile_start = jnp.cumsum(tiles) - tiles
    pos = (slot_onehot * (tile_start * rt + before)[:, None, :]).sum(axis=-1)

    units = (tiles + tiles_per_unit - 1) // tiles_per_unit
    unit_first = jnp.cumsum(units) - units
    n_units = units.sum()
    max_units = N_EXPERTS + (n_tokens * TOP_K) // EXPERT_UNIT_ROWS
    uid = jnp.arange(max_units)
    e_of = jnp.clip(jnp.searchsorted(jnp.cumsum(units), uid, side="right"), 0, N_EXPERTS - 1)
    k_in = uid - unit_first[e_of]
    live = uid < n_units
    last_e = e_of[jnp.maximum(n_units - 1, 0)]
    unit_expert = jnp.where(live, e_of, last_e).astype(jnp.int32)
    unit_start = jnp.where(live, (tile_start[e_of] + k_in * tiles_per_unit) * rt, 0).astype(jnp.int32)
    unit_tiles = jnp.where(live, jnp.minimum(tiles[e_of] - k_in * tiles_per_unit, tiles_per_unit), 0).astype(jnp.int32)
    totals = jnp.stack([n_units, tiles.sum()]).astype(jnp.int32)
    return pos.astype(jnp.int32), unit_expert, unit_start, unit_tiles, totals


def _dispatch_body(pos_hbm, h_ref, xs_in, xs_hbm, pos_smem, sem):
    del xs_in
    i = pl.program_id(0)
    n = pl.num_programs(0)
    groups, sub, _ = h_ref.shape
    per_step = groups * sub * TOP_K

    def idx_copy(step):
        half = pl.ds(pl.multiple_of((step % 2) * per_step, per_step), per_step)
        return pltpu.make_async_copy(pos_hbm.at[step], pos_smem.at[half], sem.at[0])

    @pl.when(i == 0)
    def _():
        idx_copy(0).start()
        idx_copy(0).wait()

    @pl.when(i + 1 < n)
    def _():
        idx_copy(i + 1).start()

    base = (i % 2) * per_step

    def send(g, carry):
        for s in range(sub):
            for kk in range(TOP_K):
                r = pos_smem[base + g * (sub * TOP_K) + s * TOP_K + kk]
                pltpu.make_async_copy(h_ref.at[g, pl.ds(s, 1), :], xs_hbm.at[pl.ds(r, 1), :], sem.at[1]).start()
        return carry

    lax.fori_loop(0, groups, send, 0)

    def drain(g, carry):
        pltpu.make_async_copy(h_ref.at[0], xs_hbm.at[pl.ds(0, sub), :], sem.at[1]).wait()
        return carry

    lax.fori_loop(0, groups * TOP_K, drain, 0)

    @pl.when(i + 1 < n)
    def _():
        idx_copy(i + 1).wait()


def dispatch_rows(h2, pos, p_rows):
    t, d = h2.shape
    tm = GATHER_TOKENS
    sub = 8
    return pl.pallas_call(
        _dispatch_body,
        grid=(t // tm,),
        in_specs=[pl.BlockSpec(memory_space=pl.ANY),
                  pl.BlockSpec((tm // sub, sub, d), lambda i: (i, 0, 0)),
                  pl.BlockSpec(memory_space=pl.ANY)],
        out_specs=pl.BlockSpec(memory_space=pl.ANY),
        out_shape=jax.ShapeDtypeStruct((p_rows, d), h2.dtype),
        scratch_shapes=[pltpu.SMEM((2 * tm * TOP_K,), jnp.int32), pltpu.SemaphoreType.DMA((2,))],
        input_output_aliases={2: 0},
        compiler_params=_params(1),
        name="dispatch_rows",
    )(pos.reshape(t // tm, tm * TOP_K), h2.reshape(t // sub, sub, d), jnp.zeros((p_rows, d), h2.dtype))


def _expert_body(ue_ref, us_ref, un_ref, nu_ref,
                 x_hbm, wgu_ref, bgu_ref, wd_ref, bd_ref, y_hbm,
                 xbuf, actbuf, gubuf, ystage, wd_f32, sem_x, sem_y, *, n_up, n_down):
    u = pl.program_id(0)
    c = pl.program_id(1)
    rt = EXPERT_ROW_TILE
    chunk = EXPERT_COL_CHUNK
    half = chunk // 2
    quarter = half // 2
    n_live = nu_ref[0]
    live = u < n_live
    start = us_ref[u]
    n_tiles = un_ref[u]

    n_pairs = n_tiles // 2
    odd = n_tiles % 2 == 1

    def span_rows(j, tiles):
        return pl.ds(pl.multiple_of(j * rt, rt), tiles * rt)

    def tile_rows(j):
        return span_rows(j, 1)

    def x_copy(unit, j):
        rows = pl.ds(pl.multiple_of(us_ref[unit] + j * rt, rt), rt)
        return pltpu.make_async_copy(x_hbm.at[rows, :], xbuf.at[tile_rows(j), :], sem_x.at[0])

    def fetch_rows(unit):
        def body(j, carry):
            x_copy(unit, j).start()
            return carry
        lax.fori_loop(0, un_ref[unit], body, 0)

    @pl.when((u == 0) & (c == 0) & live)
    def _():
        fetch_rows(0)

    @pl.when(live & (c == 0))
    def _():
        def body(j, carry):
            x_copy(u, j).wait()
            return carry
        lax.fori_loop(0, n_tiles, body, 0)

    @pl.when((c == n_up) & (u + 1 < n_live))
    def _():
        fetch_rows(u + 1)

    @pl.when(live & (c < n_up))
    def _():
        bias = bgu_ref[0]

        def project(j, tiles, slot):
            x = xbuf[span_rows(j, tiles), :].astype(BF16)
            gubuf[slot, :tiles * rt, :] = jnp.dot(x, wgu_ref[0].astype(BF16), preferred_element_type=F32) + bias

        def activate(j, tiles, slot):
            gu = gubuf[slot, :tiles * rt, :]
            even = (lax.broadcasted_iota(jnp.int32, (tiles * rt, half), 1) % 2) == 0
            lo = gu[:, :half]
            hi = gu[:, half:]
            gate = jnp.where(even, lo, pltpu.roll(hi, 1, axis=1))
            up = jnp.where(even, pltpu.roll(lo, half - 1, axis=1), hi)
            gate = jnp.minimum(gate, SWIGLU_LIMIT)
            up = jnp.clip(up, -SWIGLU_LIMIT, SWIGLU_LIMIT)
            act = (up + 1.0) * gate * _sigmoid(SWIGLU_ALPHA * gate)
            actbuf[c, span_rows(j, tiles), :] = act.astype(BF16)

        @pl.when(n_pairs >= 1)
        def _():
            project(0, 2, 0)

            def body(i, carry):
                activate(2 * (i - 1), 2, (i - 1) % 2)
                project(2 * i, 2, i % 2)
                return carry

            lax.fori_loop(1, n_pairs, body, 0)

        @pl.when((n_pairs >= 1) & odd)
        def _():
            activate(2 * (n_pairs - 1), 2, (n_pairs - 1) % 2)
            project(2 * n_pairs, 1, n_pairs % 2)

        @pl.when((n_pairs >= 1) & jnp.logical_not(odd))
        def _():
            activate(2 * (n_pairs - 1), 2, (n_pairs - 1) % 2)

        @pl.when(n_pairs == 0)
        def _():
            project(0, 1, 0)

        @pl.when(odd)
        def _():
            activate(2 * n_pairs, 1, n_pairs % 2)

    @pl.when(live & (c >= n_up))
    def _():
        cd = c - n_up
        for g in range(chunk // LANES):
            lanes = slice(g * LANES, (g + 1) * LANES)
            for f in range(n_up):
                base = f * half
                wd_f32[g, pl.ds(base, quarter, stride=2), :] = wd_ref[0, base:base + quarter, lanes]
                wd_f32[g, pl.ds(base + 1, quarter, stride=2), :] = wd_ref[0, base + quarter:base + half, lanes]
        bias = bd_ref[0]

        def y_copy(j, tiles, slot):
            rows = pl.ds(pl.multiple_of(start + j * rt, rt), tiles * rt)
            cols = pl.ds(pl.multiple_of(cd * chunk, chunk), chunk)
            return pltpu.make_async_copy(ystage.at[slot, :tiles * rt, :], y_hbm.at[rows, cols], sem_y.at[slot])

        def emit(j, tiles, slot):
            act = jnp.concatenate([actbuf[f, span_rows(j, tiles), :] for f in range(n_up)], axis=1)
            w = jnp.concatenate([wd_f32[g].astype(BF16) for g in range(chunk // LANES)], axis=1)
            ystage[slot, :tiles * rt, :] = jnp.dot(act, w, preferred_element_type=F32) + bias
            y_copy(j, tiles, slot).start()

        def pair(i, carry):
            @pl.when(i >= 2)
            def _():
                y_copy(2 * (i - 2), 2, i % 2).wait()

            emit(2 * i, 2, i % 2)
            return carry

        lax.fori_loop(0, n_pairs, pair, 0)

        @pl.when(n_pairs >= 2)
        def _():
            y_copy(2 * (n_pairs - 2), 2, n_pairs % 2).wait()

        @pl.when(odd)
        def _():
            emit(2 * n_pairs, 1, n_pairs % 2)

        @pl.when(n_pairs >= 1)
        def _():
            y_copy(2 * (n_pairs - 1), 2, (n_pairs - 1) % 2).wait()

        @pl.when(odd)
        def _():
            y_copy(2 * n_pairs, 1, n_pairs % 2).wait()

    @pl.when((u == pl.num_programs(0) - 1) & (c == n_up + n_down - 1))
    def _():
        zero_rows = xbuf.at[:rt, :]
        zero_rows[...] = jnp.zeros(zero_rows.shape, F32)
        used_tiles = nu_ref[1]

        def pad_copy(j):
            return pltpu.make_async_copy(zero_rows, y_hbm.at[pl.ds(pl.multiple_of(j * rt, rt), rt), :], sem_x.at[0])

        def pad_start(j, carry):
            pad_copy(j).start()
            return carry

        def pad_wait(j, carry):
            pad_copy(j).wait()
            return carry

        lax.fori_loop(used_tiles, y_hbm.shape[0] // rt, pad_start, 0)
        lax.fori_loop(used_tiles, y_hbm.shape[0] // rt, pad_wait, 0)


def expert_ffn(x_sorted, w_gate_up, b_gate_up, w_down, b_down, unit_expert, unit_start, unit_tiles, totals):
    p_rows, d = x_sorted.shape
    n_exp, _, two_ff = w_gate_up.shape
    d_ff = two_ff // 2
    chunk = EXPERT_COL_CHUNK
    n_up = two_ff // chunk
    n_down = d // chunk
    n_steps = n_up + n_down
    max_units = unit_expert.shape[0]
    rt = EXPERT_ROW_TILE

    def up_idx(u, c, nu):
        return jnp.where(u < nu[0], jnp.minimum(c, n_up - 1), n_up - 1)

    def down_idx(u, c, nu):
        return jnp.where(u < nu[0], jnp.maximum(c - n_up, 0), n_down - 1)

    grid_spec = pltpu.PrefetchScalarGridSpec(
        num_scalar_prefetch=4,
        grid=(totals[0], n_steps),
        in_specs=[pl.BlockSpec(memory_space=pl.ANY),
                  pl.BlockSpec((1, d, chunk), lambda u, c, ue, us, un, nu: (ue[u], 0, up_idx(u, c, nu))),
                  pl.BlockSpec((1, 1, chunk), lambda u, c, ue, us, un, nu: (ue[u], 0, up_idx(u, c, nu))),
                  pl.BlockSpec((1, d_ff, chunk), lambda u, c, ue, us, un, nu: (ue[u], 0, down_idx(u, c, nu))),
                  pl.BlockSpec((1, 1, chunk), lambda u, c, ue, us, un, nu: (ue[u], 0, down_idx(u, c, nu)))],
        out_specs=pl.BlockSpec(memory_space=pl.ANY),
        scratch_shapes=[pltpu.VMEM((EXPERT_UNIT_ROWS, d), F32),
                        pltpu.VMEM((n_up, EXPERT_UNIT_ROWS, chunk // 2), BF16),
                        pltpu.VMEM((2, 2 * rt, chunk), F32),
                        pltpu.VMEM((2, 2 * rt, chunk), F32),
                        pltpu.VMEM((chunk // LANES, d_ff, LANES), F32),
                        pltpu.SemaphoreType.DMA((1,)),
                        pltpu.SemaphoreType.DMA((2,))],
    )
    return pl.pallas_call(
        functools.partial(_expert_body, n_up=n_up, n_down=n_down),
        grid_spec=grid_spec,
        out_shape=jax.ShapeDtypeStruct((p_rows, d), F32),
        compiler_params=_params(2),
        name="expert_ffn",
    )(unit_expert, unit_start, unit_tiles, totals,
      x_sorted, w_gate_up, b_gate_up.reshape(n_exp, 1, two_ff), w_down, b_down.reshape(n_exp, 1, d))


def _combine_body(pos_hbm, x2_ref, w_ref, g_ref, y_hbm, o_ref, pos_smem, ybuf, sem, *, final_norm):
    i = pl.program_id(0)
    n = pl.num_programs(0)
    groups, sub, _ = x2_ref.shape
    slot = i % 2
    per_step = groups * sub * TOP_K

    def idx_copy(step):
        half = pl.ds(pl.multiple_of((step % 2) * per_step, per_step), per_step)
        return pltpu.make_async_copy(pos_hbm.at[step], pos_smem.at[half], sem.at[2])

    def fetch_rows(step):
        into = step % 2
        base = into * per_step

        def recv(g, carry):
            for s in range(sub):
                for kk in range(TOP_K):
                    r = pos_smem[base + g * (sub * TOP_K) + s * TOP_K + kk]
                    pltpu.make_async_copy(y_hbm.at[pl.ds(r, 1), :], ybuf.at[into, kk, g, pl.ds(s, 1), :],
                                          sem.at[into]).start()
            return carry

        lax.fori_loop(0, groups, recv, 0)

    @pl.when(i == 0)
    def _():
        idx_copy(0).start()
        idx_copy(0).wait()
        fetch_rows(0)

        @pl.when(n > 1)
        def _():
            idx_copy(1).start()

    @pl.when(i + 1 < n)
    def _():
        idx_copy(i + 1).wait()
        fetch_rows(i + 1)

    @pl.when(i + 2 < n)
    def _():
        idx_copy(i + 2).start()

    def drain(g, carry):
        pltpu.make_async_copy(y_hbm.at[pl.ds(0, sub), :], ybuf.at[slot, 0, 0], sem.at[slot]).wait()
        return carry

    lax.fori_loop(0, groups * TOP_K, drain, 0)

    x3 = x2_ref[...]
    for kk in range(TOP_K):
        x3 = x3 + w_ref[:, :, kk:kk + 1] * ybuf[slot, kk]
    o_ref[...] = _rms(x3, g_ref[...]) if final_norm else x3


def combine(x2, y_sorted, pos, weights, gain):
    m, d = x2.shape
    tm = GATHER_TOKENS
    sub = 8
    final_norm = gain is not None
    gain = gain if final_norm else jnp.ones((d,), F32)
    rows = lambda width: pl.BlockSpec((tm // sub, sub, width), lambda i: (i, 0, 0))
    out = pl.pallas_call(
        functools.partial(_combine_body, final_norm=final_norm),
        grid=(m // tm,),
        in_specs=[pl.BlockSpec(memory_space=pl.ANY),
                  rows(d), rows(LANES),
                  pl.BlockSpec((1, d), lambda i: (0, 0)),
                  pl.BlockSpec(memory_space=pl.ANY)],
        out_specs=rows(d),
        out_shape=jax.ShapeDtypeStruct((m // sub, sub, d), F32),
        scratch_shapes=[pltpu.SMEM((2 * tm * TOP_K,), jnp.int32),
                        pltpu.VMEM((2, TOP_K, tm // sub, sub, d), F32),
                        pltpu.SemaphoreType.DMA((3,))],
        compiler_params=_params(1),
        name="combine",
    )(pos.reshape(m // tm, tm * TOP_K), x2.reshape(m // sub, sub, d), weights.reshape(m // sub, sub, LANES),
      gain.reshape(1, d), y_sorted)
    return out.reshape(m, d)


def kernel(x, mem, rel_bias_table, mix_norm, w_in, diff_lambda_q1, diff_lambda_k1, diff_lambda_q2, diff_lambda_k2, diff_subln, w_branch_moba, w_branch_diff, w_mix_out, xattn_norm, mem_norm, w_xq, w_xk, w_xv, w_xo, ffn_norm, w_router, b_router, w_gate_up, b_gate_up, w_down, b_down, final_norm):
    batch, seq, d = x.shape
    n_tok = batch * seq
    x2d = x.reshape(n_tok, d)
    near, far = _bias_tiles(rel_bias_table)
    diff_col0 = 3 * MOBA_WIDTH
    gate_col0 = diff_col0 + 3 * DIFF_WIDTH
    cols = jnp.arange(w_in.shape[2])
    is_q = (cols < MOBA_WIDTH) | ((cols >= diff_col0) & (cols < diff_col0 + DIFF_WIDTH))
    col_scale = jnp.where(is_q, ATTN_SCALE * LOG2E, 1.0)
    p_rows = n_tok * TOP_K + N_EXPERTS * EXPERT_ROW_TILE
    for l in range(w_in.shape[0]):
        h = rmsnorm_rows(x2d, mix_norm[l])
        y = matmul_colscale(h, w_in[l], col_scale)
        o_moba = moba_attention(y, near[:MOBA_HEADS], far[:MOBA_HEADS], batch, seq)
        o_diff = diff_attention(y, near[MOBA_HEADS:], far[MOBA_HEADS:], diff_lambda_q1[l], diff_lambda_k1[l],
                                diff_lambda_q2[l], diff_lambda_k2[l], diff_subln[l], batch, seq, diff_col0)
        merged = branch_merge(o_moba, o_diff, w_branch_moba[l].astype(BF16), w_branch_diff[l].astype(BF16),
                              y, gate_col0)
        x1, hx = mixout(merged, w_mix_out[l].astype(BF16), x2d, xattn_norm[l])
        k_mem, v_mem = memory_kv(mem.reshape(-1, d), mem_norm[l], w_xk[l].astype(BF16), w_xv[l].astype(BF16),
                                 mem.shape[1])
        x2, h2, idx_pad, wgt_pad = cross_attention_router(hx, x1, k_mem, v_mem, w_xq[l].astype(BF16),
                                                          w_xo[l].astype(BF16), ffn_norm[l], w_router[l],
                                                          b_router[l], seq)
        pos, unit_expert, unit_start, unit_tiles, totals = _routing_plan(idx_pad[:, :TOP_K], n_tok)
        x_sorted = dispatch_rows(h2, pos, p_rows)
        y_sorted = expert_ffn(x_sorted, w_gate_up[l], b_gate_up[l], w_down[l], b_down[l],
                              unit_expert, unit_start, unit_tiles, totals)
        last = l == w_in.shape[0] - 1
        x2d = combine(x2, y_sorted, pos, wgt_pad, final_norm if last else None)
    return x2d.reshape(batch, seq, d)
```

```python
import collections
import functools
import math

import jax
import jax.numpy as jnp
from jax import lax
from jax.experimental import pallas as pl
from jax.experimental.pallas import tpu as pltpu

F32 = jnp.float32
BF16 = jnp.bfloat16
NEG_INF = float("-inf")

D_MODEL = 2048
HEAD_DIM = 128
MOBA_HEADS = 8
MOBA_WIDTH = MOBA_HEADS * HEAD_DIM
MOBA_BLOCK = 256
MOBA_TOPK = 3
DIFF_HEADS = 4
DIFF_WIDTH = DIFF_HEADS * 2 * HEAD_DIM
REL_BUCKETS = 32
REL_MAX_DISTANCE = 128
XATTN_HEADS = 4
N_EXPERTS = 32
TOP_K = 4
SWIGLU_LIMIT = 7.0
SWIGLU_ALPHA = 1.702
NORM_EPS = 1e-5
LAMBDA_INIT = 0.8 - 0.6 * math.exp(-0.3 * 0)
ATTN_SCALE = HEAD_DIM ** -0.5
LOG2E = math.log2(math.e)

ATTN_TILE = MOBA_BLOCK
MOBA_HEADS_PER_STEP = 4
DIFF_HEADS_PER_STEP = 2
LANES = 128
EXPERT_ROW_TILE = 256
EXPERT_UNIT_ROWS = 1536
EXPERT_UP_CHUNK = 512
EXPERT_DOWN_CHUNK = 512
GATHER_TOKENS = 256
VMEM_LIMIT = 56 * 1024 * 1024


def _params(n_axes):
    return pltpu.CompilerParams(dimension_semantics=("arbitrary",) * n_axes,
                                vmem_limit_bytes=VMEM_LIMIT)


def _rms(x, gain):
    return x * lax.rsqrt(jnp.mean(x * x, axis=-1, keepdims=True) + NORM_EPS) * gain


def _sigmoid(x):
    return 1.0 / (1.0 + jnp.exp(-x))


def _dot_nt(a, b):
    return lax.dot_general(a, b, (((1,), (1,)), ((), ())), preferred_element_type=F32)


def _rmsnorm_body(x_ref, g_ref, o_ref):
    o_ref[...] = _rms(x_ref[...], g_ref[...]).astype(o_ref.dtype)


def rmsnorm_rows(x, gain, tm=512):
    t, d = x.shape
    return pl.pallas_call(
        _rmsnorm_body,
        grid=(t // tm,),
        in_specs=[pl.BlockSpec((tm, d), lambda i: (i, 0)),
                  pl.BlockSpec((1, d), lambda i: (0, 0))],
        out_specs=pl.BlockSpec((tm, d), lambda i: (i, 0)),
        out_shape=jax.ShapeDtypeStruct((t, d), BF16),
        compiler_params=_params(1),
        name="rmsnorm_rows",
    )(x, gain.reshape(1, d))


def _matmul_body(a_ref, w_ref, cs_ref, o_ref, w_bf):
    @pl.when(pl.program_id(1) == 0)
    def _():
        w_bf[...] = w_ref[...].astype(BF16)

    acc = jnp.dot(a_ref[...], w_bf[...], preferred_element_type=F32)
    o_ref[...] = (acc * cs_ref[...]).astype(o_ref.dtype)


def matmul_colscale(a, w, col_scale, tm=1024, tn=1024, out_dtype=BF16):
    m, k = a.shape
    n = w.shape[1]
    tm, tn = min(tm, m), min(tn, n)
    return pl.pallas_call(
        _matmul_body,
        grid=(n // tn, m // tm),
        in_specs=[pl.BlockSpec((tm, k), lambda j, i: (i, 0)),
                  pl.BlockSpec((k, tn), lambda j, i: (0, j)),
                  pl.BlockSpec((1, tn), lambda j, i: (0, j))],
        out_specs=pl.BlockSpec((tm, tn), lambda j, i: (i, j)),
        out_shape=jax.ShapeDtypeStruct((m, n), out_dtype),
        scratch_shapes=[pltpu.VMEM((k, tn), BF16)],
        compiler_params=_params(2),
        name="matmul",
    )(a, w, col_scale.reshape(1, n).astype(F32))


def _rel_bucket(dist):
    n = jnp.maximum(dist, 0)
    max_exact = REL_BUCKETS // 2
    nf = jnp.maximum(n, max_exact).astype(F32)
    large = max_exact + (jnp.log(nf / max_exact) / math.log(REL_MAX_DISTANCE / max_exact)
                         * (REL_BUCKETS - max_exact)).astype(jnp.int32)
    return jnp.where(n < max_exact, n, jnp.minimum(large, REL_BUCKETS - 1))


def _bias_tiles(table):
    t = ATTN_TILE
    r = jnp.arange(t)[:, None]
    c = jnp.arange(t)[None, :]
    dist = jnp.stack([r - c, t + r - c])
    onehot = (_rel_bucket(dist)[..., None] == jnp.arange(REL_BUCKETS)).astype(F32)
    near = jnp.einsum("irck,kh->hirc", onehot, table.astype(F32), precision=lax.Precision.HIGHEST)
    far = jnp.broadcast_to(table[REL_BUCKETS - 1][:, None, None], (table.shape[1], 1, t))
    return near * LOG2E, far.astype(F32) * LOG2E


def _lane_halves(x, op):
    return op(x[:, :LANES], x[:, LANES:])


AttnStream = collections.namedtuple("AttnStream", "q k_rows v_rows near cfar pen_at s_scr acc lsum mpast")


def _causal_attention(streams, qi):
    t = ATTN_TILE
    own_slot = streams[0].s_scr.shape[0] - 2
    prev_slot = own_slot + 1
    prev = jnp.maximum(qi - 1, 0)
    n_far = prev
    n_pairs = (n_far + 1) // 2
    row = lax.broadcasted_iota(jnp.int32, (t, t), 0)
    col = lax.broadcasted_iota(jnp.int32, (t, t), 1)
    has_prev = jnp.where(qi >= 1, 0.0, NEG_INF)

    def masked(st, s, n):
        return s if st.pen_at is None else s + st.pen_at(n)

    def tile_max(m, s):
        return jnp.maximum(m, _lane_halves(s, jnp.maximum))

    def tile_sum(l, p):
        return l + _lane_halves(p, jnp.add)

    m_near = []
    for st in streams:
        s2 = _dot_nt(st.q, jnp.concatenate([st.k_rows(qi, 1), st.k_rows(prev, 1)], axis=0))
        s_own = jnp.where(col <= row, s2[:, :t] + st.near(0), NEG_INF)
        s_prev = masked(st, s2[:, t:] + st.near(1) + has_prev, prev)
        st.s_scr[own_slot] = s_own
        st.s_scr[prev_slot] = s_prev - st.cfar
        m_near.append(tile_max(_lane_halves(s_own, jnp.maximum), s_prev))

    def pair_scores(i, m_far):
        second_is_far = jnp.where(2 * i + 1 < n_far, 0.0, NEG_INF)
        out = []
        for st, m in zip(streams, m_far):
            s = _dot_nt(st.q, st.k_rows(2 * i, 2))
            for half in range(2):
                sh = masked(st, s[:, half * t:(half + 1) * t], 2 * i + half)
                if half == 1:
                    sh = sh + second_is_far
                st.s_scr[2 * i + half] = sh
                m = tile_max(m, sh)
            out.append(m)
        return tuple(out)

    m_far = lax.fori_loop(0, n_pairs, pair_scores, tuple(jnp.full((t, LANES), NEG_INF, F32) for _ in streams))

    for st, mn, mf in zip(streams, m_near, m_far):
        m_row = jnp.maximum(jnp.max(mn, axis=1, keepdims=True),
                            jnp.max(mf, axis=1, keepdims=True) + st.cfar)
        mp = m_row - st.cfar
        p_own = jnp.exp2(st.s_scr[own_slot] - m_row)
        p_prev = jnp.exp2(st.s_scr[prev_slot] - mp)
        st.mpast[...] = jnp.broadcast_to(mp, st.mpast.shape)
        st.lsum[...] = tile_sum(_lane_halves(p_own, jnp.add), p_prev)
        st.acc[...] = jnp.dot(jnp.concatenate([p_own, p_prev], axis=1).astype(BF16),
                              jnp.concatenate([st.v_rows(qi, 1), st.v_rows(prev, 1)], axis=0),
                              preferred_element_type=F32)

    def pair_weights(i, carry):
        for st in streams:
            mp = jnp.concatenate([st.mpast[...]] * (2 * t // LANES), axis=1)
            p = jnp.exp2(jnp.concatenate([st.s_scr[2 * i], st.s_scr[2 * i + 1]], axis=1) - mp)
            st.lsum[...] = tile_sum(tile_sum(st.lsum[...], p[:, :t]), p[:, t:])
            st.acc[...] += jnp.dot(p.astype(BF16), st.v_rows(2 * i, 2), preferred_element_type=F32)
        return carry

    lax.fori_loop(0, n_pairs, pair_weights, 0)
    return [(st.acc[...], jnp.sum(st.lsum[...], axis=1, keepdims=True)) for st in streams]


def _block_rows(n, w=1):
    return pl.ds(pl.multiple_of(n * ATTN_TILE, ATTN_TILE), w * ATTN_TILE)


def _moba_body(q_ref, k_ref, v_ref, near_ref, far_ref, o_ref, kmean_ref, s_scr, acc_scr, lsum_scr, mpast_scr, *,
               n_blocks):
    qi = pl.program_id(2)
    t = ATTN_TILE
    dh = HEAD_DIM
    heads = MOBA_HEADS_PER_STEP

    @pl.when(qi == 0)
    def _():
        for j in range(heads):
            for n in range(n_blocks):
                kmean_ref[j, n:n + 1, :] = jnp.mean(k_ref[n * t:(n + 1) * t, j * dh:(j + 1) * dh].astype(F32),
                                                    axis=0, keepdims=True)

    def stream(j):
        cols = slice(j * dh, (j + 1) * dh)
        q = q_ref[:, cols]
        gate = lax.dot_general(kmean_ref[j], q.astype(F32), (((1,), (1,)), ((), ())),
                               precision=lax.Precision.HIGHEST, preferred_element_type=F32)
        valid = lax.broadcasted_iota(jnp.int32, gate.shape, 0) < qi
        g = jnp.where(valid, gate, NEG_INF)
        kth = g
        for _ in range(MOBA_TOPK - 1):
            top = jnp.max(kth, axis=0, keepdims=True)
            kth = jnp.where(kth == top, NEG_INF, kth)
        third = jnp.max(kth, axis=0, keepdims=True)
        pen_t = jnp.where(valid & (g >= third), 0.0, NEG_INF)
        pen_t = jnp.concatenate([pen_t, jnp.full((LANES - n_blocks, t), NEG_INF, F32)], axis=0)
        pen = pen_t.T
        blk = lax.broadcasted_iota(jnp.int32, pen.shape, 1)

        def pen_at(n):
            return jnp.max(jnp.where(blk == n, pen, NEG_INF), axis=1, keepdims=True)

        return AttnStream(q, lambda n, w: k_ref[_block_rows(n, w), cols], lambda n, w: v_ref[_block_rows(n, w), cols],
                          lambda i: near_ref[j, i], far_ref[j][:, :1], pen_at, s_scr.at[j],
                          acc_scr.at[j], lsum_scr.at[j], mpast_scr.at[j])

    results = _causal_attention([stream(j) for j in range(heads)], qi)
    for j, (acc, l) in enumerate(results):
        o_ref[:, j * dh:(j + 1) * dh] = (acc / l).astype(o_ref.dtype)


def moba_attention(y, near, far, batch, seq):
    t = ATTN_TILE
    nq = seq // t
    hs = MOBA_HEADS_PER_STEP
    groups = MOBA_HEADS // hs
    w = hs * HEAD_DIM
    return pl.pallas_call(
        functools.partial(_moba_body, n_blocks=nq),
        grid=(batch, groups, nq),
        in_specs=[pl.BlockSpec((t, w), lambda b, g, i: (b * nq + i, g)),
                  pl.BlockSpec((seq, w), lambda b, g, i: (b, groups + g)),
                  pl.BlockSpec((seq, w), lambda b, g, i: (b, 2 * groups + g)),
                  pl.BlockSpec((hs, 2, t, t), lambda b, g, i: (g, 0, 0, 0)),
                  pl.BlockSpec((hs, 1, t), lambda b, g, i: (g, 0, 0))],
        out_specs=pl.BlockSpec((t, w), lambda b, g, i: (b * nq + i, g)),
        out_shape=jax.ShapeDtypeStruct((batch * seq, MOBA_WIDTH), BF16),
        scratch_shapes=[pltpu.VMEM((hs, nq, HEAD_DIM), F32),
                        pltpu.VMEM((hs, nq + 2, t, t), F32),
                        pltpu.VMEM((hs, t, HEAD_DIM), F32),
                        pltpu.VMEM((hs, t, LANES), F32),
                        pltpu.VMEM((hs, t, LANES), F32)],
        compiler_params=_params(3),
        name="moba_attention",
    )(y, y, y, near, far)


def _diff_body(q_ref, k_ref, v_ref, near_ref, far_ref, lq1_ref, lk1_ref, lq2_ref, lk2_ref, subln_ref, o_ref,
               s_scr, acc_scr, lsum_scr, mpast_scr):
    qi = pl.program_id(2)
    dh = HEAD_DIM
    w = 2 * dh
    heads = DIFF_HEADS_PER_STEP

    def stream(hh, j):
        qk_cols = slice(hh * w + j * dh, hh * w + (j + 1) * dh)
        v_cols = slice(hh * w, (hh + 1) * w)
        return AttnStream(q_ref[:, qk_cols], lambda n, nb: k_ref[_block_rows(n, nb), qk_cols],
                          lambda n, nb: v_ref[_block_rows(n, nb), v_cols], lambda i: near_ref[hh, i],
                          far_ref[hh][:, :1], None, s_scr.at[2 * hh + j],
                          acc_scr.at[2 * hh + j], lsum_scr.at[2 * hh + j], mpast_scr.at[2 * hh + j])

    results = _causal_attention([stream(hh, j) for hh in range(heads) for j in range(2)], qi)
    lam = (jnp.exp(jnp.sum(lq1_ref[...] * lk1_ref[...], axis=1, keepdims=True))
           - jnp.exp(jnp.sum(lq2_ref[...] * lk2_ref[...], axis=1, keepdims=True)) + LAMBDA_INIT)
    for hh in range(heads):
        (acc1, l1), (acc2, l2) = results[2 * hh], results[2 * hh + 1]
        o = acc1 / l1 - lam * (acc2 / l2)
        o_ref[:, hh * w:(hh + 1) * w] = (_rms(o, subln_ref[...]) * (1.0 - LAMBDA_INIT)).astype(o_ref.dtype)


def diff_attention(y, near, far, lq1, lk1, lq2, lk2, subln, batch, seq, col0):
    t = ATTN_TILE
    nq = seq // t
    hs = DIFF_HEADS_PER_STEP
    groups = DIFF_HEADS // hs
    w = hs * 2 * HEAD_DIM
    base = col0 // w
    vec = lambda a: a.reshape(1, -1).astype(F32)
    small = lambda n: pl.BlockSpec((1, n), lambda b, g, i: (0, 0))
    return pl.pallas_call(
        _diff_body,
        grid=(batch, groups, nq),
        in_specs=[pl.BlockSpec((t, w), lambda b, g, i: (b * nq + i, base + g)),
                  pl.BlockSpec((seq, w), lambda b, g, i: (b, base + groups + g)),
                  pl.BlockSpec((seq, w), lambda b, g, i: (b, base + 2 * groups + g)),
                  pl.BlockSpec((hs, 2, t, t), lambda b, g, i: (g, 0, 0, 0)),
                  pl.BlockSpec((hs, 1, t), lambda b, g, i: (g, 0, 0)),
                  small(HEAD_DIM), small(HEAD_DIM), small(HEAD_DIM), small(HEAD_DIM), small(2 * HEAD_DIM)],
        out_specs=pl.BlockSpec((t, w), lambda b, g, i: (b * nq + i, g)),
        out_shape=jax.ShapeDtypeStruct((batch * seq, DIFF_WIDTH), BF16),
        scratch_shapes=[pltpu.VMEM((2 * hs, nq + 2, t, t), F32),
                        pltpu.VMEM((2 * hs, t, 2 * HEAD_DIM), F32),
                        pltpu.VMEM((2 * hs, t, LANES), F32),
                        pltpu.VMEM((2 * hs, t, LANES), F32)],
        compiler_params=_params(3),
        name="diff_attention",
    )(y, y, y, near, far, vec(lq1), vec(lk1), vec(lq2), vec(lk2), vec(subln))


def _branch_body(om_ref, od_ref, wm_ref, wd_ref, ga_ref, gb_ref, o_ref):
    a = jnp.dot(om_ref[...], wm_ref[...], preferred_element_type=F32)
    b = jnp.dot(od_ref[...], wd_ref[...], preferred_element_type=F32)
    o_ref[...] = (_sigmoid(ga_ref[...].astype(F32)) * a + _sigmoid(gb_ref[...].astype(F32)) * b).astype(o_ref.dtype)


def branch_merge(o_moba, o_diff, w_m, w_d, y, gate_col0, tm=1024, tn=1024):
    m, k = o_moba.shape
    n = w_m.shape[1]
    g0 = gate_col0 // tn
    nj = n // tn
    return pl.pallas_call(
        _branch_body,
        grid=(nj, m // tm),
        in_specs=[pl.BlockSpec((tm, k), lambda j, i: (i, 0)),
                  pl.BlockSpec((tm, k), lambda j, i: (i, 0)),
                  pl.BlockSpec((k, tn), lambda j, i: (0, j)),
                  pl.BlockSpec((k, tn), lambda j, i: (0, j)),
                  pl.BlockSpec((tm, tn), lambda j, i: (i, g0 + j)),
                  pl.BlockSpec((tm, tn), lambda j, i: (i, g0 + nj + j))],
        out_specs=pl.BlockSpec((tm, tn), lambda j, i: (i, j)),
        out_shape=jax.ShapeDtypeStruct((m, n), BF16),
        compiler_params=_params(2),
        name="branch_merge",
    )(o_moba, o_diff, w_m, w_d, y, y)


def _mixout_body(a_ref, w_ref, x_ref, g_ref, x1_ref, h_ref):
    x1 = x_ref[...] + jnp.dot(a_ref[...], w_ref[...], preferred_element_type=F32)
    x1_ref[...] = x1
    h_ref[...] = _rms(x1, g_ref[...]).astype(h_ref.dtype)


def mixout(merged, w, x, gain, tm=512):
    m, k = merged.shape
    n = w.shape[1]
    return pl.pallas_call(
        _mixout_body,
        grid=(m // tm,),
        in_specs=[pl.BlockSpec((tm, k), lambda i: (i, 0)),
                  pl.BlockSpec((k, n), lambda i: (0, 0)),
                  pl.BlockSpec((tm, n), lambda i: (i, 0)),
                  pl.BlockSpec((1, n), lambda i: (0, 0))],
        out_specs=[pl.BlockSpec((tm, n), lambda i: (i, 0)),
                   pl.BlockSpec((tm, n), lambda i: (i, 0))],
        out_shape=[jax.ShapeDtypeStruct((m, n), F32), jax.ShapeDtypeStruct((m, n), BF16)],
        compiler_params=_params(1),
        name="mixout",
    )(merged, w, x, gain.reshape(1, n))


def _memkv_body(mem_ref, g_ref, wk_ref, wv_ref, k_ref, v_ref):
    mn = _rms(mem_ref[...], g_ref[...]).astype(BF16)
    k_ref[...] = jnp.dot(mn, wk_ref[...], preferred_element_type=F32).astype(k_ref.dtype)
    v_ref[...] = jnp.dot(mn, wv_ref[...], preferred_element_type=F32).astype(v_ref.dtype)


def memory_kv(mem2d, gain, w_k, w_v, rows):
    m, d = mem2d.shape
    n = w_k.shape[1]
    return pl.pallas_call(
        _memkv_body,
        grid=(m // rows,),
        in_specs=[pl.BlockSpec((rows, d), lambda i: (i, 0)),
                  pl.BlockSpec((1, d), lambda i: (0, 0)),
                  pl.BlockSpec((d, n), lambda i: (0, 0)),
                  pl.BlockSpec((d, n), lambda i: (0, 0))],
        out_specs=[pl.BlockSpec((rows, n), lambda i: (i, 0)),
                   pl.BlockSpec((rows, n), lambda i: (i, 0))],
        out_shape=[jax.ShapeDtypeStruct((m, n), BF16), jax.ShapeDtypeStruct((m, n), BF16)],
        compiler_params=_params(1),
        name="memory_kv",
    )(mem2d, gain.reshape(1, d), w_k, w_v)


def _xattn_body(h_ref, x1_ref, k_ref, v_ref, wq_ref, wo_ref, g_ref, wr_ref, br_ref,
                x2_ref, h2_ref, idx_ref, wgt_ref):
    q = jnp.dot(h_ref[...], wq_ref[...], preferred_element_type=F32).astype(BF16)
    outs = []
    for hh in range(XATTN_HEADS):
        sl = slice(hh * HEAD_DIM, (hh + 1) * HEAD_DIM)
        s = _dot_nt(q[:, sl], k_ref[:, sl]) * ATTN_SCALE
        p = jnp.exp(s - jnp.max(s, axis=1, keepdims=True))
        o = jnp.dot(p.astype(BF16), v_ref[:, sl], preferred_element_type=F32)
        outs.append((o / jnp.sum(p, axis=1, keepdims=True)).astype(BF16))
    o = jnp.concatenate(outs, axis=1)
    x2 = x1_ref[...] + jnp.dot(o, wo_ref[...], preferred_element_type=F32)
    x2_ref[...] = x2
    h2 = _rms(x2, g_ref[...])
    h2_ref[...] = h2

    h_hi = h2.astype(BF16)
    h_lo = (h2 - h_hi.astype(F32)).astype(BF16)
    w_hi = wr_ref[...].astype(BF16)
    w_lo = (wr_ref[...] - w_hi.astype(F32)).astype(BF16)
    logits = (jnp.dot(h_hi, w_hi, preferred_element_type=F32) + jnp.dot(h_lo, w_hi, preferred_element_type=F32)
              + jnp.dot(h_hi, w_lo, preferred_element_type=F32) + br_ref[...])
    lane = lax.broadcasted_iota(jnp.int32, logits.shape, 1)
    out_lane = lax.broadcasted_iota(jnp.int32, idx_ref.shape, 1)
    idx_out = jnp.zeros(idx_ref.shape, jnp.int32)
    exp_out = jnp.zeros(wgt_ref.shape, F32)
    denom = jnp.zeros((logits.shape[0], 1), F32)
    top0 = None
    for kk in range(TOP_K):
        top = jnp.max(logits, axis=1, keepdims=True)
        arg = jnp.min(jnp.where(logits == top, lane, N_EXPERTS), axis=1, keepdims=True)
        logits = jnp.where(lane == arg, NEG_INF, logits)
        top0 = top if top0 is None else top0
        e = jnp.exp(top - top0)
        denom = denom + e
        idx_out = jnp.where(out_lane == kk, arg, idx_out)
        exp_out = jnp.where(out_lane == kk, e, exp_out)
    idx_ref[...] = idx_out
    wgt_ref[...] = exp_out / denom


def cross_attention_router(hx, x1, k_mem, v_mem, w_q, w_o, gain, w_router, b_router, seq, tm=512):
    m, d = hx.shape
    mem_len = k_mem.shape[0] // (m // seq)
    n = w_q.shape[1]
    per_b = seq // tm
    const = lambda shape: pl.BlockSpec(shape, lambda i: (0,) * len(shape))
    rows = lambda cols: pl.BlockSpec((tm, cols), lambda i: (i, 0))
    return pl.pallas_call(
        _xattn_body,
        grid=(m // tm,),
        in_specs=[rows(d), rows(d),
                  pl.BlockSpec((mem_len, n), lambda i: (i // per_b, 0)),
                  pl.BlockSpec((mem_len, n), lambda i: (i // per_b, 0)),
                  const((d, n)), const((n, d)), const((1, d)), const((d, N_EXPERTS)), const((1, N_EXPERTS))],
        out_specs=[rows(d), rows(d), rows(LANES), rows(LANES)],
        out_shape=[jax.ShapeDtypeStruct((m, d), F32), jax.ShapeDtypeStruct((m, d), F32),
                   jax.ShapeDtypeStruct((m, LANES), jnp.int32), jax.ShapeDtypeStruct((m, LANES), F32)],
        compiler_params=_params(1),
        name="cross_attention_router",
    )(hx, x1, k_mem, v_mem, w_q, w_o, gain.reshape(1, d), w_router.astype(F32), b_router.reshape(1, -1).astype(F32))


def _routing_plan(top_idx, n_tokens):
    rt = EXPERT_ROW_TILE
    tiles_per_unit = EXPERT_UNIT_ROWS // rt
    slot_onehot = (top_idx[:, :, None] == jnp.arange(N_EXPERTS)[None, None, :]).astype(jnp.int32)
    onehot = slot_onehot.sum(axis=1)
    before = jnp.cumsum(onehot, axis=0) - onehot
    count = onehot.sum(axis=0)
    tiles = (count + rt - 1) // rt
    tile_start = jnp.cumsum(tiles) - tiles
    pos = (slot_onehot * (tile_start * rt + before)[:, None, :]).sum(axis=-1)

    units = (tiles + tiles_per_unit - 1) // tiles_per_unit
    unit_first = jnp.cumsum(units) - units
    n_units = units.sum()
    max_units = N_EXPERTS + (n_tokens * TOP_K) // EXPERT_UNIT_ROWS
    uid = jnp.arange(max_units)
    e_of = jnp.clip(jnp.searchsorted(jnp.cumsum(units), uid, side="right"), 0, N_EXPERTS - 1)
    k_in = uid - unit_first[e_of]
    live = uid < n_units
    last_e = e_of[jnp.maximum(n_units - 1, 0)]
    unit_expert = jnp.where(live, e_of, last_e).astype(jnp.int32)
    unit_start = jnp.where(live, (tile_start[e_of] + k_in * tiles_per_unit) * rt, 0).astype(jnp.int32)
    unit_tiles = jnp.where(live, jnp.minimum(tiles[e_of] - k_in * tiles_per_unit, tiles_per_unit), 0).astype(jnp.int32)
    totals = jnp.stack([n_units, tiles.sum()]).astype(jnp.int32)
    pad_plan = jnp.concatenate([tile_start * rt + count, (tile_start + tiles) * rt, tiles.sum()[None]]).astype(jnp.int32)
    return pos.astype(jnp.int32), unit_expert, unit_start, unit_tiles, totals, pad_plan


def _dispatch_body(pos_hbm, pad_ref, h_ref, xs_hbm, pos_smem, zeros, sem):
    i = pl.program_id(0)
    n = pl.num_programs(0)
    groups, sub, _ = h_ref.shape
    per_step = groups * sub * TOP_K
    rt = zeros.shape[0]

    def idx_copy(step):
        half = pl.ds(pl.multiple_of((step % 2) * per_step, per_step), per_step)
        return pltpu.make_async_copy(pos_hbm.at[step], pos_smem.at[half], sem.at[0])

    @pl.when(i == 0)
    def _():
        idx_copy(0).start()
        idx_copy(0).wait()

    @pl.when(i + 1 < n)
    def _():
        idx_copy(i + 1).start()

    base = (i % 2) * per_step

    def send(g, carry):
        for s in range(sub):
            for kk in range(TOP_K):
                r = pos_smem[base + g * (sub * TOP_K) + s * TOP_K + kk]
                pltpu.make_async_copy(h_ref.at[g, pl.ds(s, 1), :], xs_hbm.at[pl.ds(r, 1), :], sem.at[1]).start()
        return carry

    lax.fori_loop(0, groups, send, 0)

    def drain(g, carry):
        pltpu.make_async_copy(h_ref.at[0], xs_hbm.at[pl.ds(0, sub), :], sem.at[1]).wait()
        return carry

    lax.fori_loop(0, groups * TOP_K, drain, 0)

    @pl.when(i + 1 < n)
    def _():
        idx_copy(i + 1).wait()

    @pl.when(i == n - 1)
    def _():
        zeros[...] = jnp.zeros(zeros.shape, zeros.dtype)

        def pad_row(r):
            return pltpu.make_async_copy(zeros.at[pl.ds(0, 1), :], xs_hbm.at[pl.ds(r, 1), :], sem.at[1])

        def pad_tile(j):
            return pltpu.make_async_copy(zeros, xs_hbm.at[pl.ds(pl.multiple_of(j * rt, rt), rt), :], sem.at[1])

        def each_pad(fn):
            def expert(e, carry):
                def row(r, carry):
                    fn(pad_row(r))
                    return carry
                return lax.fori_loop(pad_ref[e], pad_ref[N_EXPERTS + e], row, carry)
            lax.fori_loop(0, N_EXPERTS, expert, 0)

            def tile(j, carry):
                fn(pad_tile(j))
                return carry
            lax.fori_loop(pad_ref[2 * N_EXPERTS], xs_hbm.shape[0] // rt, tile, 0)

        each_pad(lambda copy: copy.start())
        each_pad(lambda copy: copy.wait())


def dispatch_rows(h2, pos, pad_plan, p_rows):
    t, d = h2.shape
    tm = GATHER_TOKENS
    sub = 8
    return pl.pallas_call(
        _dispatch_body,
        grid=(t // tm,),
        in_specs=[pl.BlockSpec(memory_space=pl.ANY),
                  pl.BlockSpec(memory_space=pltpu.SMEM),
                  pl.BlockSpec((tm // sub, sub, d), lambda i: (i, 0, 0))],
        out_specs=pl.BlockSpec(memory_space=pl.ANY),
        out_shape=jax.ShapeDtypeStruct((p_rows, d), h2.dtype),
        scratch_shapes=[pltpu.SMEM((2 * tm * TOP_K,), jnp.int32),
                        pltpu.VMEM((EXPERT_ROW_TILE, d), h2.dtype),
                        pltpu.SemaphoreType.DMA((2,))],
        compiler_params=_params(1),
        name="dispatch_rows",
    )(pos.reshape(t // tm, tm * TOP_K), pad_plan, h2.reshape(t // sub, sub, d))


def _expert_body(ue_ref, us_ref, un_ref, nu_ref,
                 x_hbm, wgu_ref, bgu_ref, wd_ref, bd_ref, y_hbm,
                 xbuf, actbuf, gubuf, ystage, wd_f32, sem_x, sem_y, *, n_up, n_down):
    u = pl.program_id(0)
    c = pl.program_id(1)
    rt = EXPERT_ROW_TILE
    half = EXPERT_UP_CHUNK // 2
    quarter = half // 2
    chunk = EXPERT_DOWN_CHUNK
    n_live = nu_ref[0]
    live = u < n_live
    start = us_ref[u]
    n_tiles = un_ref[u]

    n_pairs = n_tiles // 2
    odd = n_tiles % 2 == 1

    def span_rows(j, tiles):
        return pl.ds(pl.multiple_of(j * rt, rt), tiles * rt)

    def tile_rows(j):
        return span_rows(j, 1)

    def x_copy(unit, j):
        rows = pl.ds(pl.multiple_of(us_ref[unit] + j * rt, rt), rt)
        return pltpu.make_async_copy(x_hbm.at[rows, :], xbuf.at[tile_rows(j), :], sem_x.at[0])

    def fetch_rows(unit):
        def body(j, carry):
            x_copy(unit, j).start()
            return carry
        lax.fori_loop(0, un_ref[unit], body, 0)

    @pl.when((u == 0) & (c == 0) & live)
    def _():
        fetch_rows(0)

    @pl.when(live & (c == 0))
    def _():
        def body(j, carry):
            x_copy(u, j).wait()
            return carry
        lax.fori_loop(0, n_tiles, body, 0)

    @pl.when((c == n_up) & (u + 1 < n_live))
    def _():
        fetch_rows(u + 1)

    @pl.when(live & (c < n_up))
    def _():
        bias = bgu_ref[ue_ref[u], pl.ds(c, 1), :]

        def project(j, tiles, slot):
            x = xbuf[span_rows(j, tiles), :].astype(BF16)
            gubuf[slot, :tiles * rt, :] = jnp.dot(x, wgu_ref[0].astype(BF16), preferred_element_type=F32) + bias

        def activate(j, tiles, slot):
            gu = gubuf[slot, :tiles * rt, :]
            even = (lax.broadcasted_iota(jnp.int32, (tiles * rt, half), 1) % 2) == 0
            lo = gu[:, :half]
            hi = gu[:, half:]
            gate = jnp.where(even, lo, pltpu.roll(hi, 1, axis=1))
            up = jnp.where(even, pltpu.roll(lo, half - 1, axis=1), hi)
            gate = jnp.minimum(gate, SWIGLU_LIMIT)
            up = jnp.clip(up, -SWIGLU_LIMIT, SWIGLU_LIMIT)
            act = (up + 1.0) * gate * _sigmoid(SWIGLU_ALPHA * gate)
            actbuf[c, span_rows(j, tiles), :] = act.astype(BF16)

        @pl.when(n_pairs >= 1)
        def _():
            project(0, 2, 0)

            def body(i, carry):
                activate(2 * (i - 1), 2, (i - 1) % 2)
                project(2 * i, 2, i % 2)
                return carry

            lax.fori_loop(1, n_pairs, body, 0)

        @pl.when((n_pairs >= 1) & odd)
        def _():
            activate(2 * (n_pairs - 1), 2, (n_pairs - 1) % 2)
            project(2 * n_pairs, 1, n_pairs % 2)

        @pl.when((n_pairs >= 1) & jnp.logical_not(odd))
        def _():
            activate(2 * (n_pairs - 1), 2, (n_pairs - 1) % 2)

        @pl.when(n_pairs == 0)
        def _():
            project(0, 1, 0)

        @pl.when(odd)
        def _():
            activate(2 * n_pairs, 1, n_pairs % 2)

    @pl.when(live & (c >= n_up))
    def _():
        cd = c - n_up
        for g in range(chunk // LANES):
            lanes = slice(g * LANES, (g + 1) * LANES)
            for f in range(n_up):
                base = f * half
                wd_f32[g, pl.ds(base, quarter, stride=2), :] = wd_ref[0, base:base + quarter, lanes]
                wd_f32[g, pl.ds(base + 1, quarter, stride=2), :] = wd_ref[0, base + quarter:base + half, lanes]
        bias = bd_ref[ue_ref[u], pl.ds(cd, 1), :]

        def y_copy(j, tiles, slot):
            rows = pl.ds(pl.multiple_of(start + j * rt, rt), tiles * rt)
            cols = pl.ds(pl.multiple_of(cd * chunk, chunk), chunk)
            return pltpu.make_async_copy(ystage.at[slot, :tiles * rt, :], y_hbm.at[rows, cols], sem_y.at[slot])

        def emit(j, tiles, slot):
            act = jnp.concatenate([actbuf[f, span_rows(j, tiles), :] for f in range(n_up)], axis=1)
            w = jnp.concatenate([wd_f32[g].astype(BF16) for g in range(chunk // LANES)], axis=1)
            ystage[slot, :tiles * rt, :] = jnp.dot(act, w, preferred_element_type=F32) + bias
            y_copy(j, tiles, slot).start()

        def pair(i, carry):
            @pl.when(i >= 2)
            def _():
                y_copy(2 * (i - 2), 2, i % 2).wait()

            emit(2 * i, 2, i % 2)
            return carry

        lax.fori_loop(0, n_pairs, pair, 0)

        @pl.when(n_pairs >= 2)
        def _():
            y_copy(2 * (n_pairs - 2), 2, n_pairs % 2).wait()

        @pl.when(odd)
        def _():
            emit(2 * n_pairs, 1, n_pairs % 2)

        @pl.when(n_pairs >= 1)
        def _():
            y_copy(2 * (n_pairs - 1), 2, (n_pairs - 1) % 2).wait()

        @pl.when(odd)
        def _():
            y_copy(2 * n_pairs, 1, n_pairs % 2).wait()

    @pl.when((u == pl.num_programs(0) - 1) & (c == n_up + n_down - 1))
    def _():
        zero_rows = xbuf.at[:rt, :]
        zero_rows[...] = jnp.zeros(zero_rows.shape, F32)
        used_tiles = nu_ref[1]

        def pad_copy(j):
            return pltpu.make_async_copy(zero_rows, y_hbm.at[pl.ds(pl.multiple_of(j * rt, rt), rt), :], sem_x.at[0])

        def pad_start(j, carry):
            pad_copy(j).start()
            return carry

        def pad_wait(j, carry):
            pad_copy(j).wait()
            return carry

        lax.fori_loop(used_tiles, y_hbm.shape[0] // rt, pad_start, 0)
        lax.fori_loop(used_tiles, y_hbm.shape[0] // rt, pad_wait, 0)


def expert_ffn(x_sorted, w_gate_up, b_gate_up, w_down, b_down, unit_expert, unit_start, unit_tiles, totals):
    p_rows, d = x_sorted.shape
    n_exp, _, two_ff = w_gate_up.shape
    d_ff = two_ff // 2
    up_chunk = EXPERT_UP_CHUNK
    chunk = EXPERT_DOWN_CHUNK
    n_up = two_ff // up_chunk
    n_down = d // chunk
    n_steps = n_up + n_down
    max_units = unit_expert.shape[0]
    rt = EXPERT_ROW_TILE

    def up_idx(u, c, nu):
        return jnp.where(u < nu[0], jnp.minimum(c, n_up - 1), n_up - 1)

    def down_idx(u, c, nu):
        return jnp.where(u < nu[0], jnp.maximum(c - n_up, 0), n_down - 1)

    grid_spec = pltpu.PrefetchScalarGridSpec(
        num_scalar_prefetch=4,
        grid=(totals[0], n_steps),
        in_specs=[pl.BlockSpec(memory_space=pl.ANY),
                  pl.BlockSpec((1, d, up_chunk), lambda u, c, ue, us, un, nu: (ue[u], 0, up_idx(u, c, nu))),
                  pl.BlockSpec((n_exp, n_up, up_chunk), lambda u, c, ue, us, un, nu: (0, 0, 0)),
                  pl.BlockSpec((1, d_ff, chunk), lambda u, c, ue, us, un, nu: (ue[u], 0, down_idx(u, c, nu))),
                  pl.BlockSpec((n_exp, n_down, chunk), lambda u, c, ue, us, un, nu: (0, 0, 0))],
        out_specs=pl.BlockSpec(memory_space=pl.ANY),
        scratch_shapes=[pltpu.VMEM((EXPERT_UNIT_ROWS, d), F32),
                        pltpu.VMEM((n_up, EXPERT_UNIT_ROWS, up_chunk // 2), BF16),
                        pltpu.VMEM((2, 2 * rt, up_chunk), F32),
                        pltpu.VMEM((2, 2 * rt, chunk), F32),
                        pltpu.VMEM((chunk // LANES, d_ff, LANES), F32),
                        pltpu.SemaphoreType.DMA((1,)),
                        pltpu.SemaphoreType.DMA((2,))],
    )
    return pl.pallas_call(
        functools.partial(_expert_body, n_up=n_up, n_down=n_down),
        grid_spec=grid_spec,
        out_shape=jax.ShapeDtypeStruct((p_rows, d), F32),
        compiler_params=_params(2),
        name="expert_ffn",
    )(unit_expert, unit_start, unit_tiles, totals,
      x_sorted, w_gate_up, b_gate_up.reshape(n_exp, n_up, up_chunk), w_down,
      b_down.reshape(n_exp, n_down, chunk))


def _combine_body(pos_hbm, x2_ref, w_ref, g_ref, y_hbm, o_ref, pos_smem, ybuf, sem, *, final_norm):
    i = pl.program_id(0)
    n = pl.num_programs(0)
    groups, sub, _ = x2_ref.shape
    slot = i % 2
    per_step = groups * sub * TOP_K

    def idx_copy(step):
        half = pl.ds(pl.multiple_of((step % 2) * per_step, per_step), per_step)
        return pltpu.make_async_copy(pos_hbm.at[step], pos_smem.at[half], sem.at[2])

    def fetch_rows(step):
        into = step % 2
        base = into * per_step

        def recv(g, carry):
            for s in range(sub):
                for kk in range(TOP_K):
                    r = pos_smem[base + g * (sub * TOP_K) + s * TOP_K + kk]
                    pltpu.make_async_copy(y_hbm.at[pl.ds(r, 1), :], ybuf.at[into, kk, g, pl.ds(s, 1), :],
                                          sem.at[into]).start()
            return carry

        lax.fori_loop(0, groups, recv, 0)

    @pl.when(i == 0)
    def _():
        idx_copy(0).start()
        idx_copy(0).wait()
        fetch_rows(0)

        @pl.when(n > 1)
        def _():
            idx_copy(1).start()

    @pl.when(i + 1 < n)
    def _():
        idx_copy(i + 1).wait()
        fetch_rows(i + 1)

    @pl.when(i + 2 < n)
    def _():
        idx_copy(i + 2).start()

    def drain(g, carry):
        pltpu.make_async_copy(y_hbm.at[pl.ds(0, sub), :], ybuf.at[slot, 0, 0], sem.at[slot]).wait()
        return carry

    lax.fori_loop(0, groups * TOP_K, drain, 0)

    x3 = x2_ref[...]
    for kk in range(TOP_K):
        x3 = x3 + w_ref[:, :, kk:kk + 1] * ybuf[slot, kk]
    o_ref[...] = _rms(x3, g_ref[...]) if final_norm else x3


def combine(x2, y_sorted, pos, weights, gain):
    m, d = x2.shape
    tm = GATHER_TOKENS
    sub = 8
    final_norm = gain is not None
    gain = gain if final_norm else jnp.ones((d,), F32)
    rows = lambda width: pl.BlockSpec((tm // sub, sub, width), lambda i: (i, 0, 0))
    out = pl.pallas_call(
        functools.partial(_combine_body, final_norm=final_norm),
        grid=(m // tm,),
        in_specs=[pl.BlockSpec(memory_space=pl.ANY),
                  rows(d), rows(LANES),
                  pl.BlockSpec((1, d), lambda i: (0, 0)),
                  pl.BlockSpec(memory_space=pl.ANY)],
        out_specs=rows(d),
        out_shape=jax.ShapeDtypeStruct((m // sub, sub, d), F32),
        scratch_shapes=[pltpu.SMEM((2 * tm * TOP_K,), jnp.int32),
                        pltpu.VMEM((2, TOP_K, tm // sub, sub, d), F32),
                        pltpu.SemaphoreType.DMA((3,))],
        compiler_params=_params(1),
        name="combine",
    )(pos.reshape(m // tm, tm * TOP_K), x2.reshape(m // sub, sub, d), weights.reshape(m // sub, sub, LANES),
      gain.reshape(1, d), y_sorted)
    return out.reshape(m, d)


def kernel(x, mem, rel_bias_table, mix_norm, w_in, diff_lambda_q1, diff_lambda_k1, diff_lambda_q2, diff_lambda_k2, diff_subln, w_branch_moba, w_branch_diff, w_mix_out, xattn_norm, mem_norm, w_xq, w_xk, w_xv, w_xo, ffn_norm, w_router, b_router, w_gate_up, b_gate_up, w_down, b_down, final_norm):
    batch, seq, d = x.shape
    n_tok = batch * seq
    x2d = x.reshape(n_tok, d)
    near, far = _bias_tiles(rel_bias_table)
    diff_col0 = 3 * MOBA_WIDTH
    gate_col0 = diff_col0 + 3 * DIFF_WIDTH
    cols = jnp.arange(w_in.shape[2])
    is_q = (cols < MOBA_WIDTH) | ((cols >= diff_col0) & (cols < diff_col0 + DIFF_WIDTH))
    col_scale = jnp.where(is_q, ATTN_SCALE * LOG2E, 1.0)
    p_rows = n_tok * TOP_K + N_EXPERTS * EXPERT_ROW_TILE
    for l in range(w_in.shape[0]):
        h = rmsnorm_rows(x2d, mix_norm[l])
        y = matmul_colscale(h, w_in[l], col_scale)
        o_moba = moba_attention(y, near[:MOBA_HEADS], far[:MOBA_HEADS], batch, seq)
        o_diff = diff_attention(y, near[MOBA_HEADS:], far[MOBA_HEADS:], diff_lambda_q1[l], diff_lambda_k1[l],
                                diff_lambda_q2[l], diff_lambda_k2[l], diff_subln[l], batch, seq, diff_col0)
        merged = branch_merge(o_moba, o_diff, w_branch_moba[l].astype(BF16), w_branch_diff[l].astype(BF16),
                              y, gate_col0)
        x1, hx = mixout(merged, w_mix_out[l].astype(BF16), x2d, xattn_norm[l])
        k_mem, v_mem = memory_kv(mem.reshape(-1, d), mem_norm[l], w_xk[l].astype(BF16), w_xv[l].astype(BF16),
                                 mem.shape[1])
        x2, h2, idx_pad, wgt_pad = cross_attention_router(hx, x1, k_mem, v_mem, w_xq[l].astype(BF16),
                                                          w_xo[l].astype(BF16), ffn_norm[l], w_router[l],
                                                          b_router[l], seq)
        pos, unit_expert, unit_start, unit_tiles, totals, pad_plan = _routing_plan(idx_pad[:, :TOP_K], n_tok)
        x_sorted = dispatch_rows(h2, pos, pad_plan, p_rows)
        y_sorted = expert_ffn(x_sorted, w_gate_up[l], b_gate_up[l], w_down[l], b_down[l],
                              unit_expert, unit_start, unit_tiles, totals)
        last = l == w_in.shape[0] - 1
        x2d = combine(x2, y_sorted, pos, wgt_pad, final_norm if last else None)
    return x2d.reshape(batch, seq, d)
```

```python
import collections
import functools
import math

import jax
import jax.numpy as jnp
from jax import lax
from jax.experimental import pallas as pl
from jax.experimental.pallas import tpu as pltpu

F32 = jnp.float32
BF16 = jnp.bfloat16
NEG_INF = float("-inf")

D_MODEL = 2048
HEAD_DIM = 128
MOBA_HEADS = 8
MOBA_WIDTH = MOBA_HEADS * HEAD_DIM
MOBA_BLOCK = 256
MOBA_TOPK = 3
DIFF_HEADS = 4
DIFF_WIDTH = DIFF_HEADS * 2 * HEAD_DIM
REL_BUCKETS = 32
REL_MAX_DISTANCE = 128
XATTN_HEADS = 4
N_EXPERTS = 32
TOP_K = 4
SWIGLU_LIMIT = 7.0
SWIGLU_ALPHA = 1.702
NORM_EPS = 1e-5
LAMBDA_INIT = 0.8 - 0.6 * math.exp(-0.3 * 0)
ATTN_SCALE = HEAD_DIM ** -0.5
LOG2E = math.log2(math.e)

ATTN_TILE = MOBA_BLOCK
MOBA_HEADS_PER_STEP = 4
DIFF_HEADS_PER_STEP = 2
LANES = 128
EXPERT_ROW_TILE = 256
EXPERT_UNIT_ROWS = 1280
EXPERT_UP_BLOCK = 1024
EXPERT_UP_CHUNK = 512
EXPERT_DOWN_CHUNK = 512
GATHER_TOKENS = 256
VMEM_LIMIT = 56 * 1024 * 1024


def _params(n_axes):
    return pltpu.CompilerParams(dimension_semantics=("arbitrary",) * n_axes,
                                vmem_limit_bytes=VMEM_LIMIT)


def _rms(x, gain):
    return x * lax.rsqrt(jnp.mean(x * x, axis=-1, keepdims=True) + NORM_EPS) * gain


def _sigmoid(x):
    return 1.0 / (1.0 + jnp.exp(-x))


def _dot_nt(a, b):
    return lax.dot_general(a, b, (((1,), (1,)), ((), ())), preferred_element_type=F32)


def _rmsnorm_body(x_ref, g_ref, o_ref):
    o_ref[...] = _rms(x_ref[...], g_ref[...]).astype(o_ref.dtype)


def rmsnorm_rows(x, gain, tm=512):
    t, d = x.shape
    return pl.pallas_call(
        _rmsnorm_body,
        grid=(t // tm,),
        in_specs=[pl.BlockSpec((tm, d), lambda i: (i, 0)),
                  pl.BlockSpec((1, d), lambda i: (0, 0))],
        out_specs=pl.BlockSpec((tm, d), lambda i: (i, 0)),
        out_shape=jax.ShapeDtypeStruct((t, d), BF16),
        compiler_params=_params(1),
        name="rmsnorm_rows",
    )(x, gain.reshape(1, d))


def _matmul_body(a_ref, w_ref, cs_ref, o_ref, w_bf):
    @pl.when(pl.program_id(1) == 0)
    def _():
        w_bf[...] = w_ref[...].astype(BF16)

    acc = jnp.dot(a_ref[...], w_bf[...], preferred_element_type=F32)
    o_ref[...] = (acc * cs_ref[...]).astype(o_ref.dtype)


def matmul_colscale(a, w, col_scale, tm=1024, tn=1024, out_dtype=BF16):
    m, k = a.shape
    n = w.shape[1]
    tm, tn = min(tm, m), min(tn, n)
    return pl.pallas_call(
        _matmul_body,
        grid=(n // tn, m // tm),
        in_specs=[pl.BlockSpec((tm, k), lambda j, i: (i, 0)),
                  pl.BlockSpec((k, tn), lambda j, i: (0, j)),
                  pl.BlockSpec((1, tn), lambda j, i: (0, j))],
        out_specs=pl.BlockSpec((tm, tn), lambda j, i: (i, j)),
        out_shape=jax.ShapeDtypeStruct((m, n), out_dtype),
        scratch_shapes=[pltpu.VMEM((k, tn), BF16)],
        compiler_params=_params(2),
        name="matmul",
    )(a, w, col_scale.reshape(1, n).astype(F32))


def _rel_bucket(dist):
    n = jnp.maximum(dist, 0)
    max_exact = REL_BUCKETS // 2
    nf = jnp.maximum(n, max_exact).astype(F32)
    large = max_exact + (jnp.log(nf / max_exact) / math.log(REL_MAX_DISTANCE / max_exact)
                         * (REL_BUCKETS - max_exact)).astype(jnp.int32)
    return jnp.where(n < max_exact, n, jnp.minimum(large, REL_BUCKETS - 1))


def _bias_tiles(table):
    t = ATTN_TILE
    r = jnp.arange(t)[:, None]
    c = jnp.arange(t)[None, :]
    dist = jnp.stack([r - c, t + r - c])
    onehot = (_rel_bucket(dist)[..., None] == jnp.arange(REL_BUCKETS)).astype(F32)
    near = jnp.einsum("irck,kh->hirc", onehot, table.astype(F32), precision=lax.Precision.HIGHEST)
    far = jnp.broadcast_to(table[REL_BUCKETS - 1][:, None, None], (table.shape[1], 1, t))
    return near * LOG2E, far.astype(F32) * LOG2E


def _lane_halves(x, op):
    return op(x[:, :LANES], x[:, LANES:])


AttnStream = collections.namedtuple("AttnStream", "q k_rows v_rows near cfar pen_at s_scr acc lsum mpast")


def _causal_attention(streams, qi):
    t = ATTN_TILE
    own_slot = streams[0].s_scr.shape[0] - 2
    prev_slot = own_slot + 1
    prev = jnp.maximum(qi - 1, 0)
    n_far = prev
    n_pairs = (n_far + 1) // 2
    row = lax.broadcasted_iota(jnp.int32, (t, t), 0)
    col = lax.broadcasted_iota(jnp.int32, (t, t), 1)
    has_prev = jnp.where(qi >= 1, 0.0, NEG_INF)

    def masked(st, s, n):
        return s if st.pen_at is None else s + st.pen_at(n)

    def tile_max(m, s):
        return jnp.maximum(m, _lane_halves(s, jnp.maximum))

    def tile_sum(l, p):
        return l + _lane_halves(p, jnp.add)

    m_near = []
    for st in streams:
        s2 = _dot_nt(st.q, jnp.concatenate([st.k_rows(qi, 1), st.k_rows(prev, 1)], axis=0))
        s_own = jnp.where(col <= row, s2[:, :t] + st.near(0), NEG_INF)
        s_prev = masked(st, s2[:, t:] + st.near(1) + has_prev, prev)
        st.s_scr[own_slot] = s_own
        st.s_scr[prev_slot] = s_prev - st.cfar
        m_near.append(tile_max(_lane_halves(s_own, jnp.maximum), s_prev))

    def pair_scores(i, m_far):
        second_is_far = jnp.where(2 * i + 1 < n_far, 0.0, NEG_INF)
        out = []
        for st, m in zip(streams, m_far):
            s = _dot_nt(st.q, st.k_rows(2 * i, 2))
            for half in range(2):
                sh = masked(st, s[:, half * t:(half + 1) * t], 2 * i + half)
                if half == 1:
                    sh = sh + second_is_far
                st.s_scr[2 * i + half] = sh
                m = tile_max(m, sh)
            out.append(m)
        return tuple(out)

    m_far = lax.fori_loop(0, n_pairs, pair_scores, tuple(jnp.full((t, LANES), NEG_INF, F32) for _ in streams))

    for st, mn, mf in zip(streams, m_near, m_far):
        m_row = jnp.maximum(jnp.max(mn, axis=1, keepdims=True),
                            jnp.max(mf, axis=1, keepdims=True) + st.cfar)
        mp = m_row - st.cfar
        p_own = jnp.exp2(st.s_scr[own_slot] - m_row)
        p_prev = jnp.exp2(st.s_scr[prev_slot] - mp)
        st.mpast[...] = jnp.broadcast_to(mp, st.mpast.shape)
        st.lsum[...] = tile_sum(_lane_halves(p_own, jnp.add), p_prev)
        st.acc[...] = jnp.dot(jnp.concatenate([p_own, p_prev], axis=1).astype(BF16),
                              jnp.concatenate([st.v_rows(qi, 1), st.v_rows(prev, 1)], axis=0),
                              preferred_element_type=F32)

    def pair_weights(i, carry):
        for st in streams:
            mp = jnp.concatenate([st.mpast[...]] * (2 * t // LANES), axis=1)
            p = jnp.exp2(jnp.concatenate([st.s_scr[2 * i], st.s_scr[2 * i + 1]], axis=1) - mp)
            st.lsum[...] = tile_sum(tile_sum(st.lsum[...], p[:, :t]), p[:, t:])
            st.acc[...] += jnp.dot(p.astype(BF16), st.v_rows(2 * i, 2), preferred_element_type=F32)
        return carry

    lax.fori_loop(0, n_pairs, pair_weights, 0)
    return [(st.acc[...], jnp.sum(st.lsum[...], axis=1, keepdims=True)) for st in streams]


def _block_rows(n, w=1):
    return pl.ds(pl.multiple_of(n * ATTN_TILE, ATTN_TILE), w * ATTN_TILE)


def _moba_body(q_ref, k_ref, v_ref, near_ref, far_ref, o_ref, kmean_ref, s_scr, acc_scr, lsum_scr, mpast_scr, *,
               n_blocks):
    qi = pl.program_id(2)
    t = ATTN_TILE
    dh = HEAD_DIM
    heads = MOBA_HEADS_PER_STEP

    @pl.when(qi == 0)
    def _():
        for j in range(heads):
            for n in range(n_blocks):
                kmean_ref[j, n:n + 1, :] = jnp.mean(k_ref[n * t:(n + 1) * t, j * dh:(j + 1) * dh].astype(F32),
                                                    axis=0, keepdims=True)

    def stream(j):
        cols = slice(j * dh, (j + 1) * dh)
        q = q_ref[:, cols]
        gate = lax.dot_general(kmean_ref[j], q.astype(F32), (((1,), (1,)), ((), ())),
                               precision=lax.Precision.HIGHEST, preferred_element_type=F32)
        valid = lax.broadcasted_iota(jnp.int32, gate.shape, 0) < qi
        g = jnp.where(valid, gate, NEG_INF)
        kth = g
        for _ in range(MOBA_TOPK - 1):
            top = jnp.max(kth, axis=0, keepdims=True)
            kth = jnp.where(kth == top, NEG_INF, kth)
        third = jnp.max(kth, axis=0, keepdims=True)
        pen_t = jnp.where(valid & (g >= third), 0.0, NEG_INF)
        pen_t = jnp.concatenate([pen_t, jnp.full((LANES - n_blocks, t), NEG_INF, F32)], axis=0)
        pen = pen_t.T
        blk = lax.broadcasted_iota(jnp.int32, pen.shape, 1)

        def pen_at(n):
            return jnp.max(jnp.where(blk == n, pen, NEG_INF), axis=1, keepdims=True)

        return AttnStream(q, lambda n, w: k_ref[_block_rows(n, w), cols], lambda n, w: v_ref[_block_rows(n, w), cols],
                          lambda i: near_ref[j, i], far_ref[j][:, :1], pen_at, s_scr.at[j],
                          acc_scr.at[j], lsum_scr.at[j], mpast_scr.at[j])

    results = _causal_attention([stream(j) for j in range(heads)], qi)
    for j, (acc, l) in enumerate(results):
        o_ref[:, j * dh:(j + 1) * dh] = (acc / l).astype(o_ref.dtype)


def moba_attention(y, near, far, batch, seq):
    t = ATTN_TILE
    nq = seq // t
    hs = MOBA_HEADS_PER_STEP
    groups = MOBA_HEADS // hs
    w = hs * HEAD_DIM
    return pl.pallas_call(
        functools.partial(_moba_body, n_blocks=nq),
        grid=(batch, groups, nq),
        in_specs=[pl.BlockSpec((t, w), lambda b, g, i: (b * nq + i, g)),
                  pl.BlockSpec((seq, w), lambda b, g, i: (b, groups + g)),
                  pl.BlockSpec((seq, w), lambda b, g, i: (b, 2 * groups + g)),
                  pl.BlockSpec((hs, 2, t, t), lambda b, g, i: (g, 0, 0, 0)),
                  pl.BlockSpec((hs, 1, t), lambda b, g, i: (g, 0, 0))],
        out_specs=pl.BlockSpec((t, w), lambda b, g, i: (b * nq + i, g)),
        out_shape=jax.ShapeDtypeStruct((batch * seq, MOBA_WIDTH), BF16),
        scratch_shapes=[pltpu.VMEM((hs, nq, HEAD_DIM), F32),
                        pltpu.VMEM((hs, nq + 2, t, t), F32),
                        pltpu.VMEM((hs, t, HEAD_DIM), F32),
                        pltpu.VMEM((hs, t, LANES), F32),
                        pltpu.VMEM((hs, t, LANES), F32)],
        compiler_params=_params(3),
        name="moba_attention",
    )(y, y, y, near, far)


def _diff_body(q_ref, k_ref, v_ref, near_ref, far_ref, lq1_ref, lk1_ref, lq2_ref, lk2_ref, subln_ref, o_ref,
               s_scr, acc_scr, lsum_scr, mpast_scr):
    qi = pl.program_id(2)
    dh = HEAD_DIM
    w = 2 * dh
    heads = DIFF_HEADS_PER_STEP

    def stream(hh, j):
        qk_cols = slice(hh * w + j * dh, hh * w + (j + 1) * dh)
        v_cols = slice(hh * w, (hh + 1) * w)
        return AttnStream(q_ref[:, qk_cols], lambda n, nb: k_ref[_block_rows(n, nb), qk_cols],
                          lambda n, nb: v_ref[_block_rows(n, nb), v_cols], lambda i: near_ref[hh, i],
                          far_ref[hh][:, :1], None, s_scr.at[2 * hh + j],
                          acc_scr.at[2 * hh + j], lsum_scr.at[2 * hh + j], mpast_scr.at[2 * hh + j])

    results = _causal_attention([stream(hh, j) for hh in range(heads) for j in range(2)], qi)
    lam = (jnp.exp(jnp.sum(lq1_ref[...] * lk1_ref[...], axis=1, keepdims=True))
           - jnp.exp(jnp.sum(lq2_ref[...] * lk2_ref[...], axis=1, keepdims=True)) + LAMBDA_INIT)
    for hh in range(heads):
        (acc1, l1), (acc2, l2) = results[2 * hh], results[2 * hh + 1]
        o = acc1 / l1 - lam * (acc2 / l2)
        o_ref[:, hh * w:(hh + 1) * w] = (_rms(o, subln_ref[...]) * (1.0 - LAMBDA_INIT)).astype(o_ref.dtype)


def diff_attention(y, near, far, lq1, lk1, lq2, lk2, subln, batch, seq, col0):
    t = ATTN_TILE
    nq = seq // t
    hs = DIFF_HEADS_PER_STEP
    groups = DIFF_HEADS // hs
    w = hs * 2 * HEAD_DIM
    base = col0 // w
    vec = lambda a: a.reshape(1, -1).astype(F32)
    small = lambda n: pl.BlockSpec((1, n), lambda b, g, i: (0, 0))
    return pl.pallas_call(
        _diff_body,
        grid=(batch, groups, nq),
        in_specs=[pl.BlockSpec((t, w), lambda b, g, i: (b * nq + i, base + g)),
                  pl.BlockSpec((seq, w), lambda b, g, i: (b, base + groups + g)),
                  pl.BlockSpec((seq, w), lambda b, g, i: (b, base + 2 * groups + g)),
                  pl.BlockSpec((hs, 2, t, t), lambda b, g, i: (g, 0, 0, 0)),
                  pl.BlockSpec((hs, 1, t), lambda b, g, i: (g, 0, 0)),
                  small(HEAD_DIM), small(HEAD_DIM), small(HEAD_DIM), small(HEAD_DIM), small(2 * HEAD_DIM)],
        out_specs=pl.BlockSpec((t, w), lambda b, g, i: (b * nq + i, g)),
        out_shape=jax.ShapeDtypeStruct((batch * seq, DIFF_WIDTH), BF16),
        scratch_shapes=[pltpu.VMEM((2 * hs, nq + 2, t, t), F32),
                        pltpu.VMEM((2 * hs, t, 2 * HEAD_DIM), F32),
                        pltpu.VMEM((2 * hs, t, LANES), F32),
                        pltpu.VMEM((2 * hs, t, LANES), F32)],
        compiler_params=_params(3),
        name="diff_attention",
    )(y, y, y, near, far, vec(lq1), vec(lk1), vec(lq2), vec(lk2), vec(subln))


def _branch_body(om_ref, od_ref, wm_ref, wd_ref, ga_ref, gb_ref, o_ref):
    a = jnp.dot(om_ref[...], wm_ref[...], preferred_element_type=F32)
    b = jnp.dot(od_ref[...], wd_ref[...], preferred_element_type=F32)
    o_ref[...] = (_sigmoid(ga_ref[...].astype(F32)) * a + _sigmoid(gb_ref[...].astype(F32)) * b).astype(o_ref.dtype)


def branch_merge(o_moba, o_diff, w_m, w_d, y, gate_col0, tm=1024, tn=1024):
    m, k = o_moba.shape
    n = w_m.shape[1]
    g0 = gate_col0 // tn
    nj = n // tn
    return pl.pallas_call(
        _branch_body,
        grid=(nj, m // tm),
        in_specs=[pl.BlockSpec((tm, k), lambda j, i: (i, 0)),
                  pl.BlockSpec((tm, k), lambda j, i: (i, 0)),
                  pl.BlockSpec((k, tn), lambda j, i: (0, j)),
                  pl.BlockSpec((k, tn), lambda j, i: (0, j)),
                  pl.BlockSpec((tm, tn), lambda j, i: (i, g0 + j)),
                  pl.BlockSpec((tm, tn), lambda j, i: (i, g0 + nj + j))],
        out_specs=pl.BlockSpec((tm, tn), lambda j, i: (i, j)),
        out_shape=jax.ShapeDtypeStruct((m, n), BF16),
        compiler_params=_params(2),
        name="branch_merge",
    )(o_moba, o_diff, w_m, w_d, y, y)


def _mixout_body(a_ref, w_ref, x_ref, g_ref, x1_ref, h_ref):
    x1 = x_ref[...] + jnp.dot(a_ref[...], w_ref[...], preferred_element_type=F32)
    x1_ref[...] = x1
    h_ref[...] = _rms(x1, g_ref[...]).astype(h_ref.dtype)


def mixout(merged, w, x, gain, tm=512):
    m, k = merged.shape
    n = w.shape[1]
    return pl.pallas_call(
        _mixout_body,
        grid=(m // tm,),
        in_specs=[pl.BlockSpec((tm, k), lambda i: (i, 0)),
                  pl.BlockSpec((k, n), lambda i: (0, 0)),
                  pl.BlockSpec((tm, n), lambda i: (i, 0)),
                  pl.BlockSpec((1, n), lambda i: (0, 0))],
        out_specs=[pl.BlockSpec((tm, n), lambda i: (i, 0)),
                   pl.BlockSpec((tm, n), lambda i: (i, 0))],
        out_shape=[jax.ShapeDtypeStruct((m, n), F32), jax.ShapeDtypeStruct((m, n), BF16)],
        compiler_params=_params(1),
        name="mixout",
    )(merged, w, x, gain.reshape(1, n))


def _memkv_body(mem_ref, g_ref, wk_ref, wv_ref, k_ref, v_ref):
    mn = _rms(mem_ref[...], g_ref[...]).astype(BF16)
    k_ref[...] = jnp.dot(mn, wk_ref[...], preferred_element_type=F32).astype(k_ref.dtype)
    v_ref[...] = jnp.dot(mn, wv_ref[...], preferred_element_type=F32).astype(v_ref.dtype)


def memory_kv(mem2d, gain, w_k, w_v, rows):
    m, d = mem2d.shape
    n = w_k.shape[1]
    return pl.pallas_call(
        _memkv_body,
        grid=(m // rows,),
        in_specs=[pl.BlockSpec((rows, d), lambda i: (i, 0)),
                  pl.BlockSpec((1, d), lambda i: (0, 0)),
                  pl.BlockSpec((d, n), lambda i: (0, 0)),
                  pl.BlockSpec((d, n), lambda i: (0, 0))],
        out_specs=[pl.BlockSpec((rows, n), lambda i: (i, 0)),
                   pl.BlockSpec((rows, n), lambda i: (i, 0))],
        out_shape=[jax.ShapeDtypeStruct((m, n), BF16), jax.ShapeDtypeStruct((m, n), BF16)],
        compiler_params=_params(1),
        name="memory_kv",
    )(mem2d, gain.reshape(1, d), w_k, w_v)


def _xattn_body(h_ref, x1_ref, k_ref, v_ref, wq_ref, wo_ref, g_ref, wr_ref, br_ref,
                x2_ref, h2_ref, idx_ref, wgt_ref):
    q = jnp.dot(h_ref[...], wq_ref[...], preferred_element_type=F32).astype(BF16)
    outs = []
    for hh in range(XATTN_HEADS):
        sl = slice(hh * HEAD_DIM, (hh + 1) * HEAD_DIM)
        s = _dot_nt(q[:, sl], k_ref[:, sl]) * ATTN_SCALE
        p = jnp.exp(s - jnp.max(s, axis=1, keepdims=True))
        o = jnp.dot(p.astype(BF16), v_ref[:, sl], preferred_element_type=F32)
        outs.append((o / jnp.sum(p, axis=1, keepdims=True)).astype(BF16))
    o = jnp.concatenate(outs, axis=1)
    x2 = x1_ref[...] + jnp.dot(o, wo_ref[...], preferred_element_type=F32)
    x2_ref[...] = x2
    h2 = _rms(x2, g_ref[...])
    h2_ref[...] = h2

    h_hi = h2.astype(BF16)
    h_lo = (h2 - h_hi.astype(F32)).astype(BF16)
    w_hi = wr_ref[...].astype(BF16)
    w_lo = (wr_ref[...] - w_hi.astype(F32)).astype(BF16)
    logits = (jnp.dot(h_hi, w_hi, preferred_element_type=F32) + jnp.dot(h_lo, w_hi, preferred_element_type=F32)
              + jnp.dot(h_hi, w_lo, preferred_element_type=F32) + br_ref[...])
    lane = lax.broadcasted_iota(jnp.int32, logits.shape, 1)
    out_lane = lax.broadcasted_iota(jnp.int32, idx_ref.shape, 1)
    idx_out = jnp.zeros(idx_ref.shape, jnp.int32)
    exp_out = jnp.zeros(wgt_ref.shape, F32)
    denom = jnp.zeros((logits.shape[0], 1), F32)
    top0 = None
    for kk in range(TOP_K):
        top = jnp.max(logits, axis=1, keepdims=True)
        arg = jnp.min(jnp.where(logits == top, lane, N_EXPERTS), axis=1, keepdims=True)
        logits = jnp.where(lane == arg, NEG_INF, logits)
        top0 = top if top0 is None else top0
        e = jnp.exp(top - top0)
        denom = denom + e
        idx_out = jnp.where(out_lane == kk, arg, idx_out)
        exp_out = jnp.where(out_lane == kk, e, exp_out)
    idx_ref[...] = idx_out
    wgt_ref[...] = exp_out / denom


def cross_attention_router(hx, x1, k_mem, v_mem, w_q, w_o, gain, w_router, b_router, seq, tm=512):
    m, d = hx.shape
    mem_len = k_mem.shape[0] // (m // seq)
    n = w_q.shape[1]
    per_b = seq // tm
    const = lambda shape: pl.BlockSpec(shape, lambda i: (0,) * len(shape))
    rows = lambda cols: pl.BlockSpec((tm, cols), lambda i: (i, 0))
    return pl.pallas_call(
        _xattn_body,
        grid=(m // tm,),
        in_specs=[rows(d), rows(d),
                  pl.BlockSpec((mem_len, n), lambda i: (i // per_b, 0)),
                  pl.BlockSpec((mem_len, n), lambda i: (i // per_b, 0)),
                  const((d, n)), const((n, d)), const((1, d)), const((d, N_EXPERTS)), const((1, N_EXPERTS))],
        out_specs=[rows(d), rows(d), rows(LANES), rows(LANES)],
        out_shape=[jax.ShapeDtypeStruct((m, d), F32), jax.ShapeDtypeStruct((m, d), F32),
                   jax.ShapeDtypeStruct((m, LANES), jnp.int32), jax.ShapeDtypeStruct((m, LANES), F32)],
        compiler_params=_params(1),
        name="cross_attention_router",
    )(hx, x1, k_mem, v_mem, w_q, w_o, gain.reshape(1, d), w_router.astype(F32), b_router.reshape(1, -1).astype(F32))


def _routing_plan(top_idx, n_tokens):
    rt = EXPERT_ROW_TILE
    tiles_per_unit = EXPERT_UNIT_ROWS // rt
    slot_onehot = (top_idx[:, :, None] == jnp.arange(N_EXPERTS)[None, None, :]).astype(jnp.int32)
    onehot = slot_onehot.sum(axis=1)
    before = jnp.cumsum(onehot, axis=0) - onehot
    count = onehot.sum(axis=0)
    tiles = (count + rt - 1) // rt
    tile_start = jnp.cumsum(tiles) - tiles
    pos = (slot_onehot * (tile_start * rt + before)[:, None, :]).sum(axis=-1)

    units = (tiles + tiles_per_unit - 1) // tiles_per_unit
    unit_first = jnp.cumsum(units) - units
    n_units = units.sum()
    max_units = N_EXPERTS + (n_tokens * TOP_K) // EXPERT_UNIT_ROWS
    uid = jnp.arange(max_units)
    e_of = jnp.clip(jnp.searchsorted(jnp.cumsum(units), uid, side="right"), 0, N_EXPERTS - 1)
    k_in = uid - unit_first[e_of]
    live = uid < n_units
    last_e = e_of[jnp.maximum(n_units - 1, 0)]
    unit_expert = jnp.where(live, e_of, last_e).astype(jnp.int32)
    unit_start = jnp.where(live, (tile_start[e_of] + k_in * tiles_per_unit) * rt, 0).astype(jnp.int32)
    unit_tiles = jnp.where(live, jnp.minimum(tiles[e_of] - k_in * tiles_per_unit, tiles_per_unit), 0).astype(jnp.int32)
    totals = jnp.stack([n_units, tiles.sum()]).astype(jnp.int32)
    pad_plan = jnp.concatenate([tile_start * rt + count, (tile_start + tiles) * rt, tiles.sum()[None]]).astype(jnp.int32)
    return pos.astype(jnp.int32), unit_expert, unit_start, unit_tiles, totals, pad_plan


def _dispatch_body(pos_hbm, pad_ref, h_ref, xs_hbm, pos_smem, zeros, sem):
    i = pl.program_id(0)
    n = pl.num_programs(0)
    groups, sub, _ = h_ref.shape
    per_step = groups * sub * TOP_K
    rt = zeros.shape[0]

    def idx_copy(step):
        half = pl.ds(pl.multiple_of((step % 2) * per_step, per_step), per_step)
        return pltpu.make_async_copy(pos_hbm.at[step], pos_smem.at[half], sem.at[0])

    @pl.when(i == 0)
    def _():
        idx_copy(0).start()
        idx_copy(0).wait()

    @pl.when(i + 1 < n)
    def _():
        idx_copy(i + 1).start()

    base = (i % 2) * per_step

    def send(g, carry):
        for s in range(sub):
            for kk in range(TOP_K):
                r = pos_smem[base + g * (sub * TOP_K) + s * TOP_K + kk]
                pltpu.make_async_copy(h_ref.at[g, pl.ds(s, 1), :], xs_hbm.at[pl.ds(r, 1), :], sem.at[1]).start()
        return carry

    lax.fori_loop(0, groups, send, 0)

    def drain(g, carry):
        pltpu.make_async_copy(h_ref.at[0], xs_hbm.at[pl.ds(0, sub), :], sem.at[1]).wait()
        return carry

    lax.fori_loop(0, groups * TOP_K, drain, 0)

    @pl.when(i + 1 < n)
    def _():
        idx_copy(i + 1).wait()

    @pl.when(i == n - 1)
    def _():
        zeros[...] = jnp.zeros(zeros.shape, zeros.dtype)

        def pad_rows(r, size):
            return pltpu.make_async_copy(zeros.at[pl.ds(0, size), :], xs_hbm.at[pl.ds(r, size), :], sem.at[1])

        def pad_tile(j):
            return pltpu.make_async_copy(zeros, xs_hbm.at[pl.ds(pl.multiple_of(j * rt, rt), rt), :], sem.at[1])

        def each_pad(fn):
            def expert(e, carry):
                first = pad_ref[e]
                end = pad_ref[N_EXPERTS + e]
                aligned = jnp.minimum((first + sub - 1) // sub * sub, end)

                def row(r, carry):
                    fn(pad_rows(r, 1))
                    return carry

                lax.fori_loop(first, aligned, row, 0)
                length = end - aligned
                size = sub
                while size < rt:
                    @pl.when((length & size) != 0)
                    def _(size=size):
                        fn(pad_rows(pl.multiple_of(aligned + (length & (size - 1)), sub), size))
                    size *= 2
                return carry
            lax.fori_loop(0, N_EXPERTS, expert, 0)

            def tile(j, carry):
                fn(pad_tile(j))
                return carry
            lax.fori_loop(pad_ref[2 * N_EXPERTS], xs_hbm.shape[0] // rt, tile, 0)

        each_pad(lambda copy: copy.start())
        each_pad(lambda copy: copy.wait())


def dispatch_rows(h2, pos, pad_plan, p_rows):
    t, d = h2.shape
    tm = GATHER_TOKENS
    sub = 8
    return pl.pallas_call(
        _dispatch_body,
        grid=(t // tm,),
        in_specs=[pl.BlockSpec(memory_space=pl.ANY),
                  pl.BlockSpec(memory_space=pltpu.SMEM),
                  pl.BlockSpec((tm // sub, sub, d), lambda i: (i, 0, 0))],
        out_specs=pl.BlockSpec(memory_space=pl.ANY),
        out_shape=jax.ShapeDtypeStruct((p_rows, d), h2.dtype),
        scratch_shapes=[pltpu.SMEM((2 * tm * TOP_K,), jnp.int32),
                        pltpu.VMEM((EXPERT_ROW_TILE, d), h2.dtype),
                        pltpu.SemaphoreType.DMA((2,))],
        compiler_params=_params(1),
        name="dispatch_rows",
    )(pos.reshape(t // tm, tm * TOP_K), pad_plan, h2.reshape(t // sub, sub, d))


def _expert_body(ue_ref, us_ref, un_ref, nu_ref,
                 x_hbm, wgu_ref, bgu_ref, wd_ref, bd_ref, y_hbm,
                 xbuf, actbuf, gubuf, ystage, wd_f32, sem_x, sem_y, *, n_up, n_down):
    u = pl.program_id(0)
    c = pl.program_id(1)
    rt = EXPERT_ROW_TILE
    half = EXPERT_UP_CHUNK // 2
    quarter = half // 2
    chunk = EXPERT_DOWN_CHUNK
    n_live = nu_ref[0]
    live = u < n_live
    start = us_ref[u]
    n_tiles = un_ref[u]

    n_pairs = n_tiles // 2
    odd = n_tiles % 2 == 1

    def span_rows(j, tiles):
        return pl.ds(pl.multiple_of(j * rt, rt), tiles * rt)

    def tile_rows(j):
        return span_rows(j, 1)

    def x_copy(unit, j):
        rows = pl.ds(pl.multiple_of(us_ref[unit] + j * rt, rt), rt)
        return pltpu.make_async_copy(x_hbm.at[rows, :], xbuf.at[tile_rows(j), :], sem_x.at[0])

    def fetch_rows(unit):
        def body(j, carry):
            x_copy(unit, j).start()
            return carry
        lax.fori_loop(0, un_ref[unit], body, 0)

    @pl.when((u == 0) & (c == 0) & live)
    def _():
        fetch_rows(0)

    @pl.when(live & (c == 0))
    def _():
        def body(j, carry):
            x_copy(u, j).wait()
            return carry
        lax.fori_loop(0, n_tiles, body, 0)

    @pl.when((c == n_up) & (u + 1 < n_live))
    def _():
        fetch_rows(u + 1)

    @pl.when(live & (c < n_up))
    def _():
        for sub_chunk in range(EXPERT_UP_BLOCK // EXPERT_UP_CHUNK):
            up_cols = slice(sub_chunk * EXPERT_UP_CHUNK, (sub_chunk + 1) * EXPERT_UP_CHUNK)
            act_group = c * (EXPERT_UP_BLOCK // EXPERT_UP_CHUNK) + sub_chunk
            bias = bgu_ref[ue_ref[u], pl.ds(c, 1), up_cols]

            def project(j, tiles, slot):
                x = xbuf[span_rows(j, tiles), :].astype(BF16)
                gubuf[slot, :tiles * rt, :] = jnp.dot(x, wgu_ref[0, :, up_cols].astype(BF16),
                                                      preferred_element_type=F32) + bias

            def activate(j, tiles, slot):
                gu = gubuf[slot, :tiles * rt, :]
                even = (lax.broadcasted_iota(jnp.int32, (tiles * rt, half), 1) % 2) == 0
                lo = gu[:, :half]
                hi = gu[:, half:]
                gate = jnp.where(even, lo, pltpu.roll(hi, 1, axis=1))
                up = jnp.where(even, pltpu.roll(lo, half - 1, axis=1), hi)
                gate = jnp.minimum(gate, SWIGLU_LIMIT)
                up = jnp.clip(up, -SWIGLU_LIMIT, SWIGLU_LIMIT)
                act = (up + 1.0) * gate * _sigmoid(SWIGLU_ALPHA * gate)
                actbuf[act_group, span_rows(j, tiles), :] = act.astype(BF16)

            @pl.when(n_pairs >= 1)
            def _():
                project(0, 2, 0)

                def body(i, carry):
                    activate(2 * (i - 1), 2, (i - 1) % 2)
                    project(2 * i, 2, i % 2)
                    return carry

                lax.fori_loop(1, n_pairs, body, 0)

            @pl.when((n_pairs >= 1) & odd)
            def _():
                activate(2 * (n_pairs - 1), 2, (n_pairs - 1) % 2)
                project(2 * n_pairs, 1, n_pairs % 2)

            @pl.when((n_pairs >= 1) & jnp.logical_not(odd))
            def _():
                activate(2 * (n_pairs - 1), 2, (n_pairs - 1) % 2)

            @pl.when(n_pairs == 0)
            def _():
                project(0, 1, 0)

            @pl.when(odd)
            def _():
                activate(2 * n_pairs, 1, n_pairs % 2)

    @pl.when(live & (c >= n_up))
    def _():
        cd = c - n_up
        for g in range(chunk // LANES):
            lanes = slice(g * LANES, (g + 1) * LANES)
            for f in range(actbuf.shape[0]):
                base = f * half
                wd_f32[g, pl.ds(base, quarter, stride=2), :] = wd_ref[0, base:base + quarter, lanes]
                wd_f32[g, pl.ds(base + 1, quarter, stride=2), :] = wd_ref[0, base + quarter:base + half, lanes]
        bias = bd_ref[ue_ref[u], pl.ds(cd, 1), :]

        def y_copy(j, tiles, slot):
            rows = pl.ds(pl.multiple_of(start + j * rt, rt), tiles * rt)
            cols = pl.ds(pl.multiple_of(cd * chunk, chunk), chunk)
            return pltpu.make_async_copy(ystage.at[slot, :tiles * rt, :], y_hbm.at[rows, cols], sem_y.at[slot])

        def emit(j, tiles, slot):
            act = jnp.concatenate([actbuf[f, span_rows(j, tiles), :] for f in range(actbuf.shape[0])], axis=1)
            w = jnp.concatenate([wd_f32[g].astype(BF16) for g in range(chunk // LANES)], axis=1)
            ystage[slot, :tiles * rt, :] = jnp.dot(act, w, preferred_element_type=F32) + bias
            y_copy(j, tiles, slot).start()

        def pair(i, carry):
            @pl.when(i >= 2)
            def _():
                y_copy(2 * (i - 2), 2, i % 2).wait()

            emit(2 * i, 2, i % 2)
            return carry

        lax.fori_loop(0, n_pairs, pair, 0)

        @pl.when(n_pairs >= 2)
        def _():
            y_copy(2 * (n_pairs - 2), 2, n_pairs % 2).wait()

        @pl.when(odd)
        def _():
            emit(2 * n_pairs, 1, n_pairs % 2)

        @pl.when(n_pairs >= 1)
        def _():
            y_copy(2 * (n_pairs - 1), 2, (n_pairs - 1) % 2).wait()

        @pl.when(odd)
        def _():
            y_copy(2 * n_pairs, 1, n_pairs % 2).wait()

    @pl.when((u == pl.num_programs(0) - 1) & (c == n_up + n_down - 1))
    def _():
        zero_rows = xbuf.at[:rt, :]
        zero_rows[...] = jnp.zeros(zero_rows.shape, F32)
        used_tiles = nu_ref[1]

        def pad_copy(j):
            return pltpu.make_async_copy(zero_rows, y_hbm.at[pl.ds(pl.multiple_of(j * rt, rt), rt), :], sem_x.at[0])

        def pad_start(j, carry):
            pad_copy(j).start()
            return carry

        def pad_wait(j, carry):
            pad_copy(j).wait()
            return carry

        lax.fori_loop(used_tiles, y_hbm.shape[0] // rt, pad_start, 0)
        lax.fori_loop(used_tiles, y_hbm.shape[0] // rt, pad_wait, 0)


def expert_ffn(x_sorted, w_gate_up, b_gate_up, w_down, b_down, unit_expert, unit_start, unit_tiles, totals):
    p_rows, d = x_sorted.shape
    n_exp, _, two_ff = w_gate_up.shape
    d_ff = two_ff // 2
    up_chunk = EXPERT_UP_BLOCK
    chunk = EXPERT_DOWN_CHUNK
    n_up = two_ff // up_chunk
    n_down = d // chunk
    n_steps = n_up + n_down
    max_units = unit_expert.shape[0]
    rt = EXPERT_ROW_TILE

    def up_idx(u, c, nu):
        return jnp.where(u < nu[0], jnp.minimum(c, n_up - 1), n_up - 1)

    def down_idx(u, c, nu):
        return jnp.where(u < nu[0], jnp.maximum(c - n_up, 0), n_down - 1)

    grid_spec = pltpu.PrefetchScalarGridSpec(
        num_scalar_prefetch=4,
        grid=(totals[0], n_steps),
        in_specs=[pl.BlockSpec(memory_space=pl.ANY),
                  pl.BlockSpec((1, d, up_chunk), lambda u, c, ue, us, un, nu: (ue[u], 0, up_idx(u, c, nu))),
                  pl.BlockSpec((n_exp, n_up, up_chunk), lambda u, c, ue, us, un, nu: (0, 0, 0)),
                  pl.BlockSpec((1, d_ff, chunk), lambda u, c, ue, us, un, nu: (ue[u], 0, down_idx(u, c, nu))),
                  pl.BlockSpec((n_exp, n_down, chunk), lambda u, c, ue, us, un, nu: (0, 0, 0))],
        out_specs=pl.BlockSpec(memory_space=pl.ANY),
        scratch_shapes=[pltpu.VMEM((EXPERT_UNIT_ROWS, d), F32),
                        pltpu.VMEM((two_ff // EXPERT_UP_CHUNK, EXPERT_UNIT_ROWS, EXPERT_UP_CHUNK // 2), BF16),
                        pltpu.VMEM((2, 2 * rt, EXPERT_UP_CHUNK), F32),
                        pltpu.VMEM((2, 2 * rt, chunk), F32),
                        pltpu.VMEM((chunk // LANES, d_ff, LANES), F32),
                        pltpu.SemaphoreType.DMA((1,)),
                        pltpu.SemaphoreType.DMA((2,))],
    )
    return pl.pallas_call(
        functools.partial(_expert_body, n_up=n_up, n_down=n_down),
        grid_spec=grid_spec,
        out_shape=jax.ShapeDtypeStruct((p_rows, d), F32),
        compiler_params=_params(2),
        name="expert_ffn",
    )(unit_expert, unit_start, unit_tiles, totals,
      x_sorted, w_gate_up, b_gate_up.reshape(n_exp, n_up, up_chunk), w_down,
      b_down.reshape(n_exp, n_down, chunk))


def _combine_body(pos_hbm, x2_ref, w_ref, g_ref, y_hbm, o_ref, pos_smem, ybuf, sem, *, final_norm):
    i = pl.program_id(0)
    n = pl.num_programs(0)
    groups, sub, _ = x2_ref.shape
    slot = i % 2
    per_step = groups * sub * TOP_K

    def idx_copy(step):
        half = pl.ds(pl.multiple_of((step % 2) * per_step, per_step), per_step)
        return pltpu.make_async_copy(pos_hbm.at[step], pos_smem.at[half], sem.at[2])

    def fetch_rows(step):
        into = step % 2
        base = into * per_step

        def recv(g, carry):
            for s in range(sub):
                for kk in range(TOP_K):
                    r = pos_smem[base + g * (sub * TOP_K) + s * TOP_K + kk]
                    pltpu.make_async_copy(y_hbm.at[pl.ds(r, 1), :], ybuf.at[into, kk, g, pl.ds(s, 1), :],
                                          sem.at[into]).start()
            return carry

        lax.fori_loop(0, groups, recv, 0)

    @pl.when(i == 0)
    def _():
        idx_copy(0).start()
        idx_copy(0).wait()
        fetch_rows(0)

        @pl.when(n > 1)
        def _():
            idx_copy(1).start()

    @pl.when(i + 1 < n)
    def _():
        idx_copy(i + 1).wait()
        fetch_rows(i + 1)

    @pl.when(i + 2 < n)
    def _():
        idx_copy(i + 2).start()

    def drain(g, carry):
        pltpu.make_async_copy(y_hbm.at[pl.ds(0, sub), :], ybuf.at[slot, 0, 0], sem.at[slot]).wait()
        return carry

    lax.fori_loop(0, groups * TOP_K, drain, 0)

    x3 = x2_ref[...]
    for kk in range(TOP_K):
        x3 = x3 + w_ref[:, :, kk:kk + 1] * ybuf[slot, kk]
    o_ref[...] = _rms(x3, g_ref[...]) if final_norm else x3


def combine(x2, y_sorted, pos, weights, gain):
    m, d = x2.shape
    tm = GATHER_TOKENS
    sub = 8
    final_norm = gain is not None
    gain = gain if final_norm else jnp.ones((d,), F32)
    rows = lambda width: pl.BlockSpec((tm // sub, sub, width), lambda i: (i, 0, 0))
    out = pl.pallas_call(
        functools.partial(_combine_body, final_norm=final_norm),
        grid=(m // tm,),
        in_specs=[pl.BlockSpec(memory_space=pl.ANY),
                  rows(d), rows(LANES),
                  pl.BlockSpec((1, d), lambda i: (0, 0)),
                  pl.BlockSpec(memory_space=pl.ANY)],
        out_specs=rows(d),
        out_shape=jax.ShapeDtypeStruct((m // sub, sub, d), F32),
        scratch_shapes=[pltpu.SMEM((2 * tm * TOP_K,), jnp.int32),
                        pltpu.VMEM((2, TOP_K, tm // sub, sub, d), F32),
                        pltpu.SemaphoreType.DMA((3,))],
        compiler_params=_params(1),
        name="combine",
    )(pos.reshape(m // tm, tm * TOP_K), x2.reshape(m // sub, sub, d), weights.reshape(m // sub, sub, LANES),
      gain.reshape(1, d), y_sorted)
    return out.reshape(m, d)


def kernel(x, mem, rel_bias_table, mix_norm, w_in, diff_lambda_q1, diff_lambda_k1, diff_lambda_q2, diff_lambda_k2, diff_subln, w_branch_moba, w_branch_diff, w_mix_out, xattn_norm, mem_norm, w_xq, w_xk, w_xv, w_xo, ffn_norm, w_router, b_router, w_gate_up, b_gate_up, w_down, b_down, final_norm):
    batch, seq, d = x.shape
    n_tok = batch * seq
    x2d = x.reshape(n_tok, d)
    near, far = _bias_tiles(rel_bias_table)
    diff_col0 = 3 * MOBA_WIDTH
    gate_col0 = diff_col0 + 3 * DIFF_WIDTH
    cols = jnp.arange(w_in.shape[2])
    is_q = (cols < MOBA_WIDTH) | ((cols >= diff_col0) & (cols < diff_col0 + DIFF_WIDTH))
    col_scale = jnp.where(is_q, ATTN_SCALE * LOG2E, 1.0)
    p_rows = n_tok * TOP_K + N_EXPERTS * EXPERT_ROW_TILE
    for l in range(w_in.shape[0]):
        h = rmsnorm_rows(x2d, mix_norm[l])
        y = matmul_colscale(h, w_in[l], col_scale, tm=2048)
        o_moba = moba_attention(y, near[:MOBA_HEADS], far[:MOBA_HEADS], batch, seq)
        o_diff = diff_attention(y, near[MOBA_HEADS:], far[MOBA_HEADS:], diff_lambda_q1[l], diff_lambda_k1[l],
                                diff_lambda_q2[l], diff_lambda_k2[l], diff_subln[l], batch, seq, diff_col0)
        merged = branch_merge(o_moba, o_diff, w_branch_moba[l].astype(BF16), w_branch_diff[l].astype(BF16),
                              y, gate_col0)
        x1, hx = mixout(merged, w_mix_out[l].astype(BF16), x2d, xattn_norm[l])
        k_mem, v_mem = memory_kv(mem.reshape(-1, d), mem_norm[l], w_xk[l].astype(BF16), w_xv[l].astype(BF16),
                                 mem.shape[1])
        x2, h2, idx_pad, wgt_pad = cross_attention_router(hx, x1, k_mem, v_mem, w_xq[l].astype(BF16),
                                                          w_xo[l].astype(BF16), ffn_norm[l], w_router[l],
                                                          b_router[l], seq)
        pos, unit_expert, unit_start, unit_tiles, totals, pad_plan = _routing_plan(idx_pad[:, :TOP_K], n_tok)
        x_sorted = dispatch_rows(h2, pos, pad_plan, p_rows)
        y_sorted = expert_ffn(x_sorted, w_gate_up[l], b_gate_up[l], w_down[l], b_down[l],
                              unit_expert, unit_start, unit_tiles, totals)
        last = l == w_in.shape[0] - 1
        x2d = combine(x2, y_sorted, pos, wgt_pad, final_norm if last else None)
    return x2d.reshape(batch, seq, d)
```

```python
import collections
import functools
import math

import jax
import jax.numpy as jnp
from jax import lax
from jax.experimental import pallas as pl
from jax.experimental.pallas import tpu as pltpu

F32 = jnp.float32
BF16 = jnp.bfloat16
NEG_INF = float("-inf")

D_MODEL = 2048
HEAD_DIM = 128
MOBA_HEADS = 8
MOBA_WIDTH = MOBA_HEADS * HEAD_DIM
MOBA_BLOCK = 256
MOBA_TOPK = 3
DIFF_HEADS = 4
DIFF_WIDTH = DIFF_HEADS * 2 * HEAD_DIM
REL_BUCKETS = 32
REL_MAX_DISTANCE = 128
XATTN_HEADS = 4
N_EXPERTS = 32
TOP_K = 4
SWIGLU_LIMIT = 7.0
SWIGLU_ALPHA = 1.702
NORM_EPS = 1e-5
LAMBDA_INIT = 0.8 - 0.6 * math.exp(-0.3 * 0)
ATTN_SCALE = HEAD_DIM ** -0.5
LOG2E = math.log2(math.e)

ATTN_TILE = MOBA_BLOCK
MOBA_HEADS_PER_STEP = 4
DIFF_HEADS_PER_STEP = 2
LANES = 128
EXPERT_ROW_TILE = 256
EXPERT_UNIT_ROWS = 1536
EXPERT_UP_BLOCK = 512
EXPERT_UP_CHUNK = 512
EXPERT_DOWN_CHUNK = 512
GATHER_TOKENS = 256
VMEM_LIMIT = 56 * 1024 * 1024


def _params(n_axes):
    return pltpu.CompilerParams(dimension_semantics=("arbitrary",) * n_axes,
                                vmem_limit_bytes=VMEM_LIMIT)


def _rms(x, gain):
    return x * lax.rsqrt(jnp.mean(x * x, axis=-1, keepdims=True) + NORM_EPS) * gain


def _sigmoid(x):
    return 1.0 / (1.0 + jnp.exp(-x))


def _dot_nt(a, b):
    return lax.dot_general(a, b, (((1,), (1,)), ((), ())), preferred_element_type=F32)


def _rmsnorm_body(x_ref, g_ref, o_ref):
    o_ref[...] = _rms(x_ref[...], g_ref[...]).astype(o_ref.dtype)


def rmsnorm_rows(x, gain, tm=512):
    t, d = x.shape
    return pl.pallas_call(
        _rmsnorm_body,
        grid=(t // tm,),
        in_specs=[pl.BlockSpec((tm, d), lambda i: (i, 0)),
                  pl.BlockSpec((1, d), lambda i: (0, 0))],
        out_specs=pl.BlockSpec((tm, d), lambda i: (i, 0)),
        out_shape=jax.ShapeDtypeStruct((t, d), BF16),
        compiler_params=_params(1),
        name="rmsnorm_rows",
    )(x, gain.reshape(1, d))


def _matmul_body(a_ref, w_ref, cs_ref, o_ref, w_bf):
    @pl.when(pl.program_id(1) == 0)
    def _():
        w_bf[...] = w_ref[...].astype(BF16)

    acc = jnp.dot(a_ref[...], w_bf[...], preferred_element_type=F32)
    o_ref[...] = (acc * cs_ref[...]).astype(o_ref.dtype)


def matmul_colscale(a, w, col_scale, tm=1024, tn=1024, out_dtype=BF16):
    m, k = a.shape
    n = w.shape[1]
    tm, tn = min(tm, m), min(tn, n)
    return pl.pallas_call(
        _matmul_body,
        grid=(n // tn, m // tm),
        in_specs=[pl.BlockSpec((tm, k), lambda j, i: (i, 0)),
                  pl.BlockSpec((k, tn), lambda j, i: (0, j)),
                  pl.BlockSpec((1, tn), lambda j, i: (0, j))],
        out_specs=pl.BlockSpec((tm, tn), lambda j, i: (i, j)),
        out_shape=jax.ShapeDtypeStruct((m, n), out_dtype),
        scratch_shapes=[pltpu.VMEM((k, tn), BF16)],
        compiler_params=_params(2),
        name="matmul",
    )(a, w, col_scale.reshape(1, n).astype(F32))


def _rel_bucket(dist):
    n = jnp.maximum(dist, 0)
    max_exact = REL_BUCKETS // 2
    nf = jnp.maximum(n, max_exact).astype(F32)
    large = max_exact + (jnp.log(nf / max_exact) / math.log(REL_MAX_DISTANCE / max_exact)
                         * (REL_BUCKETS - max_exact)).astype(jnp.int32)
    return jnp.where(n < max_exact, n, jnp.minimum(large, REL_BUCKETS - 1))


def _bias_tiles(table):
    t = ATTN_TILE
    r = jnp.arange(t)[:, None]
    c = jnp.arange(t)[None, :]
    dist = jnp.stack([r - c, t + r - c])
    onehot = (_rel_bucket(dist)[..., None] == jnp.arange(REL_BUCKETS)).astype(F32)
    near = jnp.einsum("irck,kh->hirc", onehot, table.astype(F32), precision=lax.Precision.HIGHEST)
    far = jnp.broadcast_to(table[REL_BUCKETS - 1][:, None, None], (table.shape[1], 1, t))
    return near * LOG2E, far.astype(F32) * LOG2E


def _lane_halves(x, op):
    return op(x[:, :LANES], x[:, LANES:])


AttnStream = collections.namedtuple("AttnStream", "q k_rows v_rows near cfar pen_at s_scr acc lsum mpast")


def _causal_attention(streams, qi):
    t = ATTN_TILE
    own_slot = streams[0].s_scr.shape[0] - 2
    prev_slot = own_slot + 1
    prev = jnp.maximum(qi - 1, 0)
    n_far = prev
    n_pairs = (n_far + 1) // 2
    row = lax.broadcasted_iota(jnp.int32, (t, t), 0)
    col = lax.broadcasted_iota(jnp.int32, (t, t), 1)
    has_prev = jnp.where(qi >= 1, 0.0, NEG_INF)

    def masked(st, s, n):
        return s if st.pen_at is None else s + st.pen_at(n)

    def tile_max(m, s):
        return jnp.maximum(m, _lane_halves(s, jnp.maximum))

    def tile_sum(l, p):
        return l + _lane_halves(p, jnp.add)

    m_near = []
    for st in streams:
        s2 = _dot_nt(st.q, jnp.concatenate([st.k_rows(qi, 1), st.k_rows(prev, 1)], axis=0))
        s_own = jnp.where(col <= row, s2[:, :t] + st.near(0), NEG_INF)
        s_prev = masked(st, s2[:, t:] + st.near(1) + has_prev, prev)
        st.s_scr[own_slot] = s_own
        st.s_scr[prev_slot] = s_prev - st.cfar
        m_near.append(tile_max(_lane_halves(s_own, jnp.maximum), s_prev))

    def pair_scores(i, m_far):
        second_is_far = jnp.where(2 * i + 1 < n_far, 0.0, NEG_INF)
        out = []
        for st, m in zip(streams, m_far):
            s = _dot_nt(st.q, st.k_rows(2 * i, 2))
            for half in range(2):
                sh = masked(st, s[:, half * t:(half + 1) * t], 2 * i + half)
                if half == 1:
                    sh = sh + second_is_far
                st.s_scr[2 * i + half] = sh
                m = tile_max(m, sh)
            out.append(m)
        return tuple(out)

    m_far = lax.fori_loop(0, n_pairs, pair_scores, tuple(jnp.full((t, LANES), NEG_INF, F32) for _ in streams))

    for st, mn, mf in zip(streams, m_near, m_far):
        m_row = jnp.max(jnp.maximum(mn, mf + st.cfar), axis=1, keepdims=True)
        mp = m_row - st.cfar
        p_own = jnp.exp2(st.s_scr[own_slot] - m_row)
        p_prev = jnp.exp2(st.s_scr[prev_slot] - mp)
        st.mpast[...] = jnp.broadcast_to(mp, st.mpast.shape)
        st.lsum[...] = tile_sum(_lane_halves(p_own, jnp.add), p_prev)
        st.acc[...] = jnp.dot(jnp.concatenate([p_own, p_prev], axis=1).astype(BF16),
                              jnp.concatenate([st.v_rows(qi, 1), st.v_rows(prev, 1)], axis=0),
                              preferred_element_type=F32)

    def pair_weights(i, carry):
        for st in streams:
            mp = jnp.concatenate([st.mpast[...]] * (2 * t // LANES), axis=1)
            p = jnp.exp2(jnp.concatenate([st.s_scr[2 * i], st.s_scr[2 * i + 1]], axis=1) - mp)
            st.lsum[...] = tile_sum(tile_sum(st.lsum[...], p[:, :t]), p[:, t:])
            st.acc[...] += jnp.dot(p.astype(BF16), st.v_rows(2 * i, 2), preferred_element_type=F32)
        return carry

    lax.fori_loop(0, n_pairs, pair_weights, 0)
    return [(st.acc[...], jnp.sum(st.lsum[...], axis=1, keepdims=True)) for st in streams]


def _block_rows(n, w=1):
    return pl.ds(pl.multiple_of(n * ATTN_TILE, ATTN_TILE), w * ATTN_TILE)


def _moba_body(q_ref, k_ref, v_ref, near_ref, far_ref, o_ref, kmean_ref, s_scr, acc_scr, lsum_scr, mpast_scr, *,
               n_blocks):
    qi = pl.program_id(2)
    t = ATTN_TILE
    dh = HEAD_DIM
    heads = MOBA_HEADS_PER_STEP

    @pl.when(qi == 0)
    def _():
        for j in range(heads):
            for n in range(n_blocks):
                kmean_ref[j, n:n + 1, :] = jnp.mean(k_ref[n * t:(n + 1) * t, j * dh:(j + 1) * dh].astype(F32),
                                                    axis=0, keepdims=True)

    def stream(j):
        cols = slice(j * dh, (j + 1) * dh)
        q = q_ref[:, cols]
        gate = lax.dot_general(kmean_ref[j], q.astype(F32), (((1,), (1,)), ((), ())),
                               precision=lax.Precision.HIGHEST, preferred_element_type=F32)
        valid = lax.broadcasted_iota(jnp.int32, gate.shape, 0) < qi
        g = jnp.where(valid, gate, NEG_INF)
        kth = g
        for _ in range(MOBA_TOPK - 1):
            top = jnp.max(kth, axis=0, keepdims=True)
            kth = jnp.where(kth == top, NEG_INF, kth)
        third = jnp.max(kth, axis=0, keepdims=True)
        pen_t = jnp.where(valid & (g >= third), 0.0, NEG_INF)
        pen_t = jnp.concatenate([pen_t, jnp.full((LANES - n_blocks, t), NEG_INF, F32)], axis=0)
        pen = pen_t.T
        blk = lax.broadcasted_iota(jnp.int32, pen.shape, 1)

        def pen_at(n):
            return jnp.max(jnp.where(blk == n, pen, NEG_INF), axis=1, keepdims=True)

        return AttnStream(q, lambda n, w: k_ref[_block_rows(n, w), cols], lambda n, w: v_ref[_block_rows(n, w), cols],
                          lambda i: near_ref[j, i], far_ref[j][:, :1], pen_at, s_scr.at[j],
                          acc_scr.at[j], lsum_scr.at[j], mpast_scr.at[j])

    results = _causal_attention([stream(j) for j in range(heads)], qi)
    for j, (acc, l) in enumerate(results):
        o_ref[:, j * dh:(j + 1) * dh] = (acc / l).astype(o_ref.dtype)


def moba_attention(y, near, far, batch, seq):
    t = ATTN_TILE
    nq = seq // t
    hs = MOBA_HEADS_PER_STEP
    groups = MOBA_HEADS // hs
    w = hs * HEAD_DIM
    return pl.pallas_call(
        functools.partial(_moba_body, n_blocks=nq),
        grid=(batch, groups, nq),
        in_specs=[pl.BlockSpec((t, w), lambda b, g, i: (b * nq + i, g)),
                  pl.BlockSpec((seq, w), lambda b, g, i: (b, groups + g)),
                  pl.BlockSpec((seq, w), lambda b, g, i: (b, 2 * groups + g)),
                  pl.BlockSpec((hs, 2, t, t), lambda b, g, i: (g, 0, 0, 0)),
                  pl.BlockSpec((hs, 1, t), lambda b, g, i: (g, 0, 0))],
        out_specs=pl.BlockSpec((t, w), lambda b, g, i: (b * nq + i, g)),
        out_shape=jax.ShapeDtypeStruct((batch * seq, MOBA_WIDTH), BF16),
        scratch_shapes=[pltpu.VMEM((hs, nq, HEAD_DIM), F32),
                        pltpu.VMEM((hs, nq + 2, t, t), F32),
                        pltpu.VMEM((hs, t, HEAD_DIM), F32),
                        pltpu.VMEM((hs, t, LANES), F32),
                        pltpu.VMEM((hs, t, LANES), F32)],
        compiler_params=_params(3),
        name="moba_attention",
    )(y, y, y, near, far)


def _diff_body(q_ref, k_ref, v_ref, near_ref, far_ref, lq1_ref, lk1_ref, lq2_ref, lk2_ref, subln_ref, o_ref,
               s_scr, acc_scr, lsum_scr, mpast_scr):
    qi = pl.program_id(2)
    dh = HEAD_DIM
    w = 2 * dh
    heads = DIFF_HEADS_PER_STEP

    def stream(hh, j):
        qk_cols = slice(hh * w + j * dh, hh * w + (j + 1) * dh)
        v_cols = slice(hh * w, (hh + 1) * w)
        return AttnStream(q_ref[:, qk_cols], lambda n, nb: k_ref[_block_rows(n, nb), qk_cols],
                          lambda n, nb: v_ref[_block_rows(n, nb), v_cols], lambda i: near_ref[hh, i],
                          far_ref[hh][:, :1], None, s_scr.at[2 * hh + j],
                          acc_scr.at[2 * hh + j], lsum_scr.at[2 * hh + j], mpast_scr.at[2 * hh + j])

    results = _causal_attention([stream(hh, j) for hh in range(heads) for j in range(2)], qi)
    lam = (jnp.exp(jnp.sum(lq1_ref[...] * lk1_ref[...], axis=1, keepdims=True))
           - jnp.exp(jnp.sum(lq2_ref[...] * lk2_ref[...], axis=1, keepdims=True)) + LAMBDA_INIT)
    for hh in range(heads):
        (acc1, l1), (acc2, l2) = results[2 * hh], results[2 * hh + 1]
        o = acc1 / l1 - lam * (acc2 / l2)
        o_ref[:, hh * w:(hh + 1) * w] = (_rms(o, subln_ref[...]) * (1.0 - LAMBDA_INIT)).astype(o_ref.dtype)


def diff_attention(y, near, far, lq1, lk1, lq2, lk2, subln, batch, seq, col0):
    t = ATTN_TILE
    nq = seq // t
    hs = DIFF_HEADS_PER_STEP
    groups = DIFF_HEADS // hs
    w = hs * 2 * HEAD_DIM
    base = col0 // w
    vec = lambda a: a.reshape(1, -1).astype(F32)
    small = lambda n: pl.BlockSpec((1, n), lambda b, g, i: (0, 0))
    return pl.pallas_call(
        _diff_body,
        grid=(batch, groups, nq),
        in_specs=[pl.BlockSpec((t, w), lambda b, g, i: (b * nq + i, base + g)),
                  pl.BlockSpec((seq, w), lambda b, g, i: (b, base + groups + g)),
                  pl.BlockSpec((seq, w), lambda b, g, i: (b, base + 2 * groups + g)),
                  pl.BlockSpec((hs, 2, t, t), lambda b, g, i: (g, 0, 0, 0)),
                  pl.BlockSpec((hs, 1, t), lambda b, g, i: (g, 0, 0)),
                  small(HEAD_DIM), small(HEAD_DIM), small(HEAD_DIM), small(HEAD_DIM), small(2 * HEAD_DIM)],
        out_specs=pl.BlockSpec((t, w), lambda b, g, i: (b * nq + i, g)),
        out_shape=jax.ShapeDtypeStruct((batch * seq, DIFF_WIDTH), BF16),
        scratch_shapes=[pltpu.VMEM((2 * hs, nq + 2, t, t), F32),
                        pltpu.VMEM((2 * hs, t, 2 * HEAD_DIM), F32),
                        pltpu.VMEM((2 * hs, t, LANES), F32),
                        pltpu.VMEM((2 * hs, t, LANES), F32)],
        compiler_params=_params(3),
        name="diff_attention",
    )(y, y, y, near, far, vec(lq1), vec(lk1), vec(lq2), vec(lk2), vec(subln))


def _branch_body(om_ref, od_ref, wm_ref, wd_ref, ga_ref, gb_ref, o_ref):
    a = jnp.dot(om_ref[...], wm_ref[...], preferred_element_type=F32)
    b = jnp.dot(od_ref[...], wd_ref[...], preferred_element_type=F32)
    o_ref[...] = (_sigmoid(ga_ref[...].astype(F32)) * a + _sigmoid(gb_ref[...].astype(F32)) * b).astype(o_ref.dtype)


def branch_merge(o_moba, o_diff, w_m, w_d, y, gate_col0, tm=1024, tn=1024):
    m, k = o_moba.shape
    n = w_m.shape[1]
    g0 = gate_col0 // tn
    nj = n // tn
    return pl.pallas_call(
        _branch_body,
        grid=(nj, m // tm),
        in_specs=[pl.BlockSpec((tm, k), lambda j, i: (i, 0)),
                  pl.BlockSpec((tm, k), lambda j, i: (i, 0)),
                  pl.BlockSpec((k, tn), lambda j, i: (0, j)),
                  pl.BlockSpec((k, tn), lambda j, i: (0, j)),
                  pl.BlockSpec((tm, tn), lambda j, i: (i, g0 + j)),
                  pl.BlockSpec((tm, tn), lambda j, i: (i, g0 + nj + j))],
        out_specs=pl.BlockSpec((tm, tn), lambda j, i: (i, j)),
        out_shape=jax.ShapeDtypeStruct((m, n), BF16),
        compiler_params=_params(2),
        name="branch_merge",
    )(o_moba, o_diff, w_m, w_d, y, y)


def _mixout_body(a_ref, w_ref, x_ref, g_ref, x1_ref, h_ref):
    x1 = x_ref[...] + jnp.dot(a_ref[...], w_ref[...], preferred_element_type=F32)
    x1_ref[...] = x1
    h_ref[...] = _rms(x1, g_ref[...]).astype(h_ref.dtype)


def mixout(merged, w, x, gain, tm=512):
    m, k = merged.shape
    n = w.shape[1]
    return pl.pallas_call(
        _mixout_body,
        grid=(m // tm,),
        in_specs=[pl.BlockSpec((tm, k), lambda i: (i, 0)),
                  pl.BlockSpec((k, n), lambda i: (0, 0)),
                  pl.BlockSpec((tm, n), lambda i: (i, 0)),
                  pl.BlockSpec((1, n), lambda i: (0, 0))],
        out_specs=[pl.BlockSpec((tm, n), lambda i: (i, 0)),
                   pl.BlockSpec((tm, n), lambda i: (i, 0))],
        out_shape=[jax.ShapeDtypeStruct((m, n), F32), jax.ShapeDtypeStruct((m, n), BF16)],
        compiler_params=_params(1),
        name="mixout",
    )(merged, w, x, gain.reshape(1, n))


def _memkv_body(mem_ref, g_ref, wk_ref, wv_ref, k_ref, v_ref):
    mn = _rms(mem_ref[...], g_ref[...]).astype(BF16)
    k_ref[...] = jnp.dot(mn, wk_ref[...], preferred_element_type=F32).astype(k_ref.dtype)
    v_ref[...] = jnp.dot(mn, wv_ref[...], preferred_element_type=F32).astype(v_ref.dtype)


def memory_kv(mem2d, gain, w_k, w_v, rows):
    m, d = mem2d.shape
    n = w_k.shape[1]
    return pl.pallas_call(
        _memkv_body,
        grid=(m // rows,),
        in_specs=[pl.BlockSpec((rows, d), lambda i: (i, 0)),
                  pl.BlockSpec((1, d), lambda i: (0, 0)),
                  pl.BlockSpec((d, n), lambda i: (0, 0)),
                  pl.BlockSpec((d, n), lambda i: (0, 0))],
        out_specs=[pl.BlockSpec((rows, n), lambda i: (i, 0)),
                   pl.BlockSpec((rows, n), lambda i: (i, 0))],
        out_shape=[jax.ShapeDtypeStruct((m, n), BF16), jax.ShapeDtypeStruct((m, n), BF16)],
        compiler_params=_params(1),
        name="memory_kv",
    )(mem2d, gain.reshape(1, d), w_k, w_v)


def _xattn_body(h_ref, x1_ref, k_ref, v_ref, wq_ref, wo_ref, g_ref, wr_ref, br_ref,
                x2_ref, h2_ref, idx_ref, wgt_ref):
    q = jnp.dot(h_ref[...], wq_ref[...], preferred_element_type=F32).astype(BF16)
    outs = []
    for hh in range(XATTN_HEADS):
        sl = slice(hh * HEAD_DIM, (hh + 1) * HEAD_DIM)
        s = _dot_nt(q[:, sl], k_ref[:, sl]) * ATTN_SCALE
        p = jnp.exp(s - jnp.max(s, axis=1, keepdims=True))
        o = jnp.dot(p.astype(BF16), v_ref[:, sl], preferred_element_type=F32)
        outs.append((o / jnp.sum(p, axis=1, keepdims=True)).astype(BF16))
    o = jnp.concatenate(outs, axis=1)
    x2 = x1_ref[...] + jnp.dot(o, wo_ref[...], preferred_element_type=F32)
    x2_ref[...] = x2
    h2 = _rms(x2, g_ref[...])
    h2_ref[...] = h2

    h_hi = h2.astype(BF16)
    h_lo = (h2 - h_hi.astype(F32)).astype(BF16)
    w_hi = wr_ref[...].astype(BF16)
    w_lo = (wr_ref[...] - w_hi.astype(F32)).astype(BF16)
    logits = (jnp.dot(h_hi, w_hi, preferred_element_type=F32) + jnp.dot(h_lo, w_hi, preferred_element_type=F32)
              + jnp.dot(h_hi, w_lo, preferred_element_type=F32) + br_ref[...])
    lane = lax.broadcasted_iota(jnp.int32, logits.shape, 1)
    out_lane = lax.broadcasted_iota(jnp.int32, idx_ref.shape, 1)
    idx_out = jnp.zeros(idx_ref.shape, jnp.int32)
    exp_out = jnp.zeros(wgt_ref.shape, F32)
    denom = jnp.zeros((logits.shape[0], 1), F32)
    top0 = None
    for kk in range(TOP_K):
        top = jnp.max(logits, axis=1, keepdims=True)
        arg = jnp.min(jnp.where(logits == top, lane, N_EXPERTS), axis=1, keepdims=True)
        logits = jnp.where(lane == arg, NEG_INF, logits)
        top0 = top if top0 is None else top0
        e = jnp.exp(top - top0)
        denom = denom + e
        idx_out = jnp.where(out_lane == kk, arg, idx_out)
        exp_out = jnp.where(out_lane == kk, e, exp_out)
    idx_ref[...] = idx_out
    wgt_ref[...] = exp_out / denom


def cross_attention_router(hx, x1, k_mem, v_mem, w_q, w_o, gain, w_router, b_router, seq, tm=512):
    m, d = hx.shape
    mem_len = k_mem.shape[0] // (m // seq)
    n = w_q.shape[1]
    per_b = seq // tm
    const = lambda shape: pl.BlockSpec(shape, lambda i: (0,) * len(shape))
    rows = lambda cols: pl.BlockSpec((tm, cols), lambda i: (i, 0))
    return pl.pallas_call(
        _xattn_body,
        grid=(m // tm,),
        in_specs=[rows(d), rows(d),
                  pl.BlockSpec((mem_len, n), lambda i: (i // per_b, 0)),
                  pl.BlockSpec((mem_len, n), lambda i: (i // per_b, 0)),
                  const((d, n)), const((n, d)), const((1, d)), const((d, N_EXPERTS)), const((1, N_EXPERTS))],
        out_specs=[rows(d), rows(d), rows(LANES), rows(LANES)],
        out_shape=[jax.ShapeDtypeStruct((m, d), F32), jax.ShapeDtypeStruct((m, d), F32),
                   jax.ShapeDtypeStruct((m, LANES), jnp.int32), jax.ShapeDtypeStruct((m, LANES), F32)],
        compiler_params=_params(1),
        name="cross_attention_router",
    )(hx, x1, k_mem, v_mem, w_q, w_o, gain.reshape(1, d), w_router.astype(F32), b_router.reshape(1, -1).astype(F32))


def _routing_plan(top_idx, n_tokens):
    rt = EXPERT_ROW_TILE
    tiles_per_unit = EXPERT_UNIT_ROWS // rt
    slot_onehot = (top_idx[:, :, None] == jnp.arange(N_EXPERTS)[None, None, :]).astype(jnp.int32)
    onehot = slot_onehot.sum(axis=1)
    before = jnp.cumsum(onehot, axis=0) - onehot
    count = onehot.sum(axis=0)
    tiles = (count + rt - 1) // rt
    tile_start = jnp.cumsum(tiles) - tiles
    pos = (slot_onehot * (tile_start * rt + before)[:, None, :]).sum(axis=-1)

    units = (tiles + tiles_per_unit - 1) // tiles_per_unit
    unit_first = jnp.cumsum(units) - units
    n_units = units.sum()
    max_units = N_EXPERTS + (n_tokens * TOP_K) // EXPERT_UNIT_ROWS
    uid = jnp.arange(max_units)
    e_of = jnp.clip(jnp.searchsorted(jnp.cumsum(units), uid, side="right"), 0, N_EXPERTS - 1)
    k_in = uid - unit_first[e_of]
    live = uid < n_units
    last_e = e_of[jnp.maximum(n_units - 1, 0)]
    unit_expert = jnp.where(live, e_of, last_e).astype(jnp.int32)
    unit_start = jnp.where(live, (tile_start[e_of] + k_in * tiles_per_unit) * rt, 0).astype(jnp.int32)
    unit_tiles = jnp.where(live, jnp.minimum(tiles[e_of] - k_in * tiles_per_unit, tiles_per_unit), 0).astype(jnp.int32)
    totals = jnp.stack([n_units, tiles.sum()]).astype(jnp.int32)
    pad_plan = jnp.concatenate([tile_start * rt + count, (tile_start + tiles) * rt, tiles.sum()[None]]).astype(jnp.int32)
    return pos.astype(jnp.int32), unit_expert, unit_start, unit_tiles, totals, pad_plan


def _dispatch_body(pos_hbm, pad_ref, h_ref, xs_hbm, pos_smem, zeros, sem):
    i = pl.program_id(0)
    n = pl.num_programs(0)
    groups, sub, _ = h_ref.shape
    per_step = groups * sub * TOP_K
    rt = zeros.shape[0]

    def idx_copy(step):
        half = pl.ds(pl.multiple_of((step % 2) * per_step, per_step), per_step)
        return pltpu.make_async_copy(pos_hbm.at[step], pos_smem.at[half], sem.at[0])

    @pl.when(i == 0)
    def _():
        idx_copy(0).start()
        idx_copy(0).wait()

    @pl.when(i + 1 < n)
    def _():
        idx_copy(i + 1).start()

    base = (i % 2) * per_step

    def send(g, carry):
        for s in range(sub):
            for kk in range(TOP_K):
                r = pos_smem[base + g * (sub * TOP_K) + s * TOP_K + kk]
                pltpu.make_async_copy(h_ref.at[g, pl.ds(s, 1), :], xs_hbm.at[pl.ds(r, 1), :], sem.at[1]).start()
        return carry

    lax.fori_loop(0, groups, send, 0)

    def drain(g, carry):
        pltpu.make_async_copy(h_ref.at[0], xs_hbm.at[pl.ds(0, sub), :], sem.at[1]).wait()
        return carry

    lax.fori_loop(0, groups * TOP_K, drain, 0)

    @pl.when(i + 1 < n)
    def _():
        idx_copy(i + 1).wait()

    @pl.when(i == n - 1)
    def _():
        zeros[...] = jnp.zeros(zeros.shape, zeros.dtype)

        def pad_rows(r, size):
            return pltpu.make_async_copy(zeros.at[pl.ds(0, size), :], xs_hbm.at[pl.ds(r, size), :], sem.at[1])

        def pad_tile(j):
            return pltpu.make_async_copy(zeros, xs_hbm.at[pl.ds(pl.multiple_of(j * rt, rt), rt), :], sem.at[1])

        def each_pad(fn):
            def expert(e, carry):
                first = pad_ref[e]
                end = pad_ref[N_EXPERTS + e]
                aligned = jnp.minimum((first + sub - 1) // sub * sub, end)

                def row(r, carry):
                    fn(pad_rows(r, 1))
                    return carry

                lax.fori_loop(first, aligned, row, 0)
                length = end - aligned
                size = sub
                while size < rt:
                    @pl.when((length & size) != 0)
                    def _(size=size):
                        fn(pad_rows(pl.multiple_of(aligned + (length & (size - 1)), sub), size))
                    size *= 2
                return carry
            lax.fori_loop(0, N_EXPERTS, expert, 0)

            def tile(j, carry):
                fn(pad_tile(j))
                return carry
            lax.fori_loop(pad_ref[2 * N_EXPERTS], xs_hbm.shape[0] // rt, tile, 0)

        each_pad(lambda copy: copy.start())
        each_pad(lambda copy: copy.wait())


def dispatch_rows(h2, pos, pad_plan, p_rows):
    t, d = h2.shape
    tm = GATHER_TOKENS
    sub = 8
    return pl.pallas_call(
        _dispatch_body,
        grid=(t // tm,),
        in_specs=[pl.BlockSpec(memory_space=pl.ANY),
                  pl.BlockSpec(memory_space=pltpu.SMEM),
                  pl.BlockSpec((tm // sub, sub, d), lambda i: (i, 0, 0))],
        out_specs=pl.BlockSpec(memory_space=pl.ANY),
        out_shape=jax.ShapeDtypeStruct((p_rows, d), h2.dtype),
        scratch_shapes=[pltpu.SMEM((2 * tm * TOP_K,), jnp.int32),
                        pltpu.VMEM((EXPERT_ROW_TILE, d), h2.dtype),
                        pltpu.SemaphoreType.DMA((2,))],
        compiler_params=_params(1),
        name="dispatch_rows",
    )(pos.reshape(t // tm, tm * TOP_K), pad_plan, h2.reshape(t // sub, sub, d))


def _expert_body(ue_ref, us_ref, un_ref, nu_ref,
                 x_hbm, wgu_ref, bgu_ref, wd_ref, bd_ref, y_hbm,
                 xbuf, actbuf, gubuf, ystage, wd_f32, sem_x, sem_y, *, n_up, n_down):
    u = pl.program_id(0)
    c = pl.program_id(1)
    rt = EXPERT_ROW_TILE
    half = EXPERT_UP_CHUNK // 2
    quarter = half // 2
    chunk = EXPERT_DOWN_CHUNK
    n_live = nu_ref[0]
    live = u < n_live
    start = us_ref[u]
    n_tiles = un_ref[u]

    n_pairs = n_tiles // 2
    odd = n_tiles % 2 == 1

    def span_rows(j, tiles):
        return pl.ds(pl.multiple_of(j * rt, rt), tiles * rt)

    def tile_rows(j):
        return span_rows(j, 1)

    def x_copy(unit, j):
        rows = pl.ds(pl.multiple_of(us_ref[unit] + j * rt, rt), rt)
        return pltpu.make_async_copy(x_hbm.at[rows, :], xbuf.at[tile_rows(j), :], sem_x.at[0])

    def fetch_rows(unit):
        def body(j, carry):
            x_copy(unit, j).start()
            return carry
        lax.fori_loop(0, un_ref[unit], body, 0)

    @pl.when((u == 0) & (c == 0) & live)
    def _():
        fetch_rows(0)

    @pl.when(live & (c == 0))
    def _():
        def body(j, carry):
            x_copy(u, j).wait()
            return carry
        lax.fori_loop(0, n_tiles, body, 0)

    @pl.when((c == n_up) & (u + 1 < n_live))
    def _():
        fetch_rows(u + 1)

    @pl.when(live & (c < n_up))
    def _():
        for sub_chunk in range(EXPERT_UP_BLOCK // EXPERT_UP_CHUNK):
            up_cols = slice(sub_chunk * EXPERT_UP_CHUNK, (sub_chunk + 1) * EXPERT_UP_CHUNK)
            act_group = c * (EXPERT_UP_BLOCK // EXPERT_UP_CHUNK) + sub_chunk
            bias = bgu_ref[ue_ref[u], pl.ds(c, 1), up_cols]

            def project(j, tiles, slot):
                x = xbuf[span_rows(j, tiles), :].astype(BF16)
                gubuf[slot, :tiles * rt, :] = jnp.dot(x, wgu_ref[0, :, up_cols].astype(BF16),
                                                      preferred_element_type=F32) + bias

            def activate(j, tiles, slot):
                gu = gubuf[slot, :tiles * rt, :]
                even = (lax.broadcasted_iota(jnp.int32, (tiles * rt, half), 1) % 2) == 0
                lo = gu[:, :half]
                hi = gu[:, half:]
                gate = jnp.where(even, lo, pltpu.roll(hi, 1, axis=1))
                up = jnp.where(even, pltpu.roll(lo, half - 1, axis=1), hi)
                gate = jnp.minimum(gate, SWIGLU_LIMIT)
                up = jnp.clip(up, -SWIGLU_LIMIT, SWIGLU_LIMIT)
                act = (up + 1.0) * gate * _sigmoid(SWIGLU_ALPHA * gate)
                actbuf[act_group, span_rows(j, tiles), :] = act.astype(BF16)

            @pl.when(n_pairs >= 1)
            def _():
                project(0, 2, 0)

                def body(i, carry):
                    activate(2 * (i - 1), 2, (i - 1) % 2)
                    project(2 * i, 2, i % 2)
                    return carry

                lax.fori_loop(1, n_pairs, body, 0)

            @pl.when((n_pairs >= 1) & odd)
            def _():
                activate(2 * (n_pairs - 1), 2, (n_pairs - 1) % 2)
                project(2 * n_pairs, 1, n_pairs % 2)

            @pl.when((n_pairs >= 1) & jnp.logical_not(odd))
            def _():
                activate(2 * (n_pairs - 1), 2, (n_pairs - 1) % 2)

            @pl.when(n_pairs == 0)
            def _():
                project(0, 1, 0)

            @pl.when(odd)
            def _():
                activate(2 * n_pairs, 1, n_pairs % 2)

    @pl.when(live & (c >= n_up))
    def _():
        cd = c - n_up
        for g in range(chunk // LANES):
            lanes = slice(g * LANES, (g + 1) * LANES)
            for f in range(actbuf.shape[0]):
                base = f * half
                wd_f32[g, pl.ds(base, quarter, stride=2), :] = wd_ref[0, base:base + quarter, lanes]
                wd_f32[g, pl.ds(base + 1, quarter, stride=2), :] = wd_ref[0, base + quarter:base + half, lanes]
        bias = bd_ref[ue_ref[u], pl.ds(cd, 1), :]

        def y_copy(j, tiles, slot):
            rows = pl.ds(pl.multiple_of(start + j * rt, rt), tiles * rt)
            cols = pl.ds(pl.multiple_of(cd * chunk, chunk), chunk)
            return pltpu.make_async_copy(ystage.at[slot, :tiles * rt, :], y_hbm.at[rows, cols], sem_y.at[slot])

        def emit(j, tiles, slot):
            act = jnp.concatenate([actbuf[f, span_rows(j, tiles), :] for f in range(actbuf.shape[0])], axis=1)
            w = jnp.concatenate([wd_f32[g].astype(BF16) for g in range(chunk // LANES)], axis=1)
            ystage[slot, :tiles * rt, :] = jnp.dot(act, w, preferred_element_type=F32) + bias
            y_copy(j, tiles, slot).start()

        def pair(i, carry):
            @pl.when(i >= 2)
            def _():
                y_copy(2 * (i - 2), 2, i % 2).wait()

            emit(2 * i, 2, i % 2)
            return carry

        lax.fori_loop(0, n_pairs, pair, 0)

        @pl.when(n_pairs >= 2)
        def _():
            y_copy(2 * (n_pairs - 2), 2, n_pairs % 2).wait()

        @pl.when(odd)
        def _():
            emit(2 * n_pairs, 1, n_pairs % 2)

        @pl.when(n_pairs >= 1)
        def _():
            y_copy(2 * (n_pairs - 1), 2, (n_pairs - 1) % 2).wait()

        @pl.when(odd)
        def _():
            y_copy(2 * n_pairs, 1, n_pairs % 2).wait()

    @pl.when((u == pl.num_programs(0) - 1) & (c == n_up + n_down - 1))
    def _():
        zero_rows = xbuf.at[:rt, :]
        zero_rows[...] = jnp.zeros(zero_rows.shape, F32)
        used_tiles = nu_ref[1]

        def pad_copy(j):
            return pltpu.make_async_copy(zero_rows, y_hbm.at[pl.ds(pl.multiple_of(j * rt, rt), rt), :], sem_x.at[0])

        def pad_start(j, carry):
            pad_copy(j).start()
            return carry

        def pad_wait(j, carry):
            pad_copy(j).wait()
            return carry

        lax.fori_loop(used_tiles, y_hbm.shape[0] // rt, pad_start, 0)
        lax.fori_loop(used_tiles, y_hbm.shape[0] // rt, pad_wait, 0)


def expert_ffn(x_sorted, w_gate_up, b_gate_up, w_down, b_down, unit_expert, unit_start, unit_tiles, totals):
    p_rows, d = x_sorted.shape
    n_exp, _, two_ff = w_gate_up.shape
    d_ff = two_ff // 2
    up_chunk = EXPERT_UP_BLOCK
    chunk = EXPERT_DOWN_CHUNK
    n_up = two_ff // up_chunk
    n_down = d // chunk
    n_steps = n_up + n_down
    max_units = unit_expert.shape[0]
    rt = EXPERT_ROW_TILE

    def up_idx(u, c, nu):
        return jnp.where(u < nu[0], jnp.minimum(c, n_up - 1), n_up - 1)

    def down_idx(u, c, nu):
        return jnp.where(u < nu[0], jnp.maximum(c - n_up, 0), n_down - 1)

    grid_spec = pltpu.PrefetchScalarGridSpec(
        num_scalar_prefetch=4,
        grid=(totals[0], n_steps),
        in_specs=[pl.BlockSpec(memory_space=pl.ANY),
                  pl.BlockSpec((1, d, up_chunk), lambda u, c, ue, us, un, nu: (ue[u], 0, up_idx(u, c, nu))),
                  pl.BlockSpec((n_exp, n_up, up_chunk), lambda u, c, ue, us, un, nu: (0, 0, 0)),
                  pl.BlockSpec((1, d_ff, chunk), lambda u, c, ue, us, un, nu: (ue[u], 0, down_idx(u, c, nu))),
                  pl.BlockSpec((n_exp, n_down, chunk), lambda u, c, ue, us, un, nu: (0, 0, 0))],
        out_specs=pl.BlockSpec(memory_space=pl.ANY),
        scratch_shapes=[pltpu.VMEM((EXPERT_UNIT_ROWS, d), F32),
                        pltpu.VMEM((two_ff // EXPERT_UP_CHUNK, EXPERT_UNIT_ROWS, EXPERT_UP_CHUNK // 2), BF16),
                        pltpu.VMEM((2, 2 * rt, EXPERT_UP_CHUNK), F32),
                        pltpu.VMEM((2, 2 * rt, chunk), F32),
                        pltpu.VMEM((chunk // LANES, d_ff, LANES), F32),
                        pltpu.SemaphoreType.DMA((1,)),
                        pltpu.SemaphoreType.DMA((2,))],
    )
    return pl.pallas_call(
        functools.partial(_expert_body, n_up=n_up, n_down=n_down),
        grid_spec=grid_spec,
        out_shape=jax.ShapeDtypeStruct((p_rows, d), F32),
        compiler_params=_params(2),
        name="expert_ffn",
    )(unit_expert, unit_start, unit_tiles, totals,
      x_sorted, w_gate_up, b_gate_up.reshape(n_exp, n_up, up_chunk), w_down,
      b_down.reshape(n_exp, n_down, chunk))


def _combine_body(pos_hbm, x2_ref, w_ref, g_ref, y_hbm, o_ref, pos_smem, ybuf, sem, *, final_norm):
    i = pl.program_id(0)
    n = pl.num_programs(0)
    groups, sub, _ = x2_ref.shape
    slot = i % 2
    per_step = groups * sub * TOP_K

    def idx_copy(step):
        half = pl.ds(pl.multiple_of((step % 2) * per_step, per_step), per_step)
        return pltpu.make_async_copy(pos_hbm.at[step], pos_smem.at[half], sem.at[2])

    def fetch_rows(step):
        into = step % 2
        base = into * per_step

        def recv(g, carry):
            for s in range(sub):
                for kk in range(TOP_K):
                    r = pos_smem[base + g * (sub * TOP_K) + s * TOP_K + kk]
                    pltpu.make_async_copy(y_hbm.at[pl.ds(r, 1), :], ybuf.at[into, kk, g, pl.ds(s, 1), :],
                                          sem.at[into]).start()
            return carry

        lax.fori_loop(0, groups, recv, 0)

    @pl.when(i == 0)
    def _():
        idx_copy(0).start()
        idx_copy(0).wait()
        fetch_rows(0)

        @pl.when(n > 1)
        def _():
            idx_copy(1).start()

    @pl.when(i + 1 < n)
    def _():
        idx_copy(i + 1).wait()
        fetch_rows(i + 1)

    @pl.when(i + 2 < n)
    def _():
        idx_copy(i + 2).start()

    def drain(g, carry):
        pltpu.make_async_copy(y_hbm.at[pl.ds(0, sub), :], ybuf.at[slot, 0, 0], sem.at[slot]).wait()
        return carry

    lax.fori_loop(0, groups * TOP_K, drain, 0)

    x3 = x2_ref[...]
    for kk in range(TOP_K):
        x3 = x3 + w_ref[:, :, kk:kk + 1] * ybuf[slot, kk]
    o_ref[...] = _rms(x3, g_ref[...]) if final_norm else x3


def combine(x2, y_sorted, pos, weights, gain):
    m, d = x2.shape
    tm = GATHER_TOKENS
    sub = 8
    final_norm = gain is not None
    gain = gain if final_norm else jnp.ones((d,), F32)
    rows = lambda width: pl.BlockSpec((tm // sub, sub, width), lambda i: (i, 0, 0))
    out = pl.pallas_call(
        functools.partial(_combine_body, final_norm=final_norm),
        grid=(m // tm,),
        in_specs=[pl.BlockSpec(memory_space=pl.ANY),
                  rows(d), rows(LANES),
                  pl.BlockSpec((1, d), lambda i: (0, 0)),
                  pl.BlockSpec(memory_space=pl.ANY)],
        out_specs=rows(d),
        out_shape=jax.ShapeDtypeStruct((m // sub, sub, d), F32),
        scratch_shapes=[pltpu.SMEM((2 * tm * TOP_K,), jnp.int32),
                        pltpu.VMEM((2, TOP_K, tm // sub, sub, d), F32),
                        pltpu.SemaphoreType.DMA((3,))],
        compiler_params=_params(1),
        name="combine",
    )(pos.reshape(m // tm, tm * TOP_K), x2.reshape(m // sub, sub, d), weights.reshape(m // sub, sub, LANES),
      gain.reshape(1, d), y_sorted)
    return out.reshape(m, d)


def kernel(x, mem, rel_bias_table, mix_norm, w_in, diff_lambda_q1, diff_lambda_k1, diff_lambda_q2, diff_lambda_k2, diff_subln, w_branch_moba, w_branch_diff, w_mix_out, xattn_norm, mem_norm, w_xq, w_xk, w_xv, w_xo, ffn_norm, w_router, b_router, w_gate_up, b_gate_up, w_down, b_down, final_norm):
    batch, seq, d = x.shape
    n_tok = batch * seq
    x2d = x.reshape(n_tok, d)
    near, far = _bias_tiles(rel_bias_table)
    diff_col0 = 3 * MOBA_WIDTH
    gate_col0 = diff_col0 + 3 * DIFF_WIDTH
    cols = jnp.arange(w_in.shape[2])
    is_q = (cols < MOBA_WIDTH) | ((cols >= diff_col0) & (cols < diff_col0 + DIFF_WIDTH))
    col_scale = jnp.where(is_q, ATTN_SCALE * LOG2E, 1.0)
    p_rows = n_tok * TOP_K + N_EXPERTS * EXPERT_ROW_TILE
    for l in range(w_in.shape[0]):
        h = rmsnorm_rows(x2d, mix_norm[l])
        y = matmul_colscale(h, w_in[l], col_scale, tm=2048)
        o_moba = moba_attention(y, near[:MOBA_HEADS], far[:MOBA_HEADS], batch, seq)
        o_diff = diff_attention(y, near[MOBA_HEADS:], far[MOBA_HEADS:], diff_lambda_q1[l], diff_lambda_k1[l],
                                diff_lambda_q2[l], diff_lambda_k2[l], diff_subln[l], batch, seq, diff_col0)
        merged = branch_merge(o_moba, o_diff, w_branch_moba[l].astype(BF16), w_branch_diff[l].astype(BF16),
                              y, gate_col0)
        x1, hx = mixout(merged, w_mix_out[l].astype(BF16), x2d, xattn_norm[l])
        k_mem, v_mem = memory_kv(mem.reshape(-1, d), mem_norm[l], w_xk[l].astype(BF16), w_xv[l].astype(BF16),
                                 mem.shape[1])
        x2, h2, idx_pad, wgt_pad = cross_attention_router(hx, x1, k_mem, v_mem, w_xq[l].astype(BF16),
                                                          w_xo[l].astype(BF16), ffn_norm[l], w_router[l],
                                                          b_router[l], seq)
        pos, unit_expert, unit_start, unit_tiles, totals, pad_plan = _routing_plan(idx_pad[:, :TOP_K], n_tok)
        x_sorted = dispatch_rows(h2, pos, pad_plan, p_rows)
        y_sorted = expert_ffn(x_sorted, w_gate_up[l], b_gate_up[l], w_down[l], b_down[l],
                              unit_expert, unit_start, unit_tiles, totals)
        last = l == w_in.shape[0] - 1
        x2d = combine(x2, y_sorted, pos, wgt_pad, final_norm if last else None)
    return x2d.reshape(batch, seq, d)
```

```python
import collections
import functools
import math

import jax
import jax.numpy as jnp
from jax import lax
from jax.experimental import pallas as pl
from jax.experimental.pallas import tpu as pltpu

F32 = jnp.float32
BF16 = jnp.bfloat16
NEG_INF = float("-inf")

D_MODEL = 2048
HEAD_DIM = 128
MOBA_HEADS = 8
MOBA_WIDTH = MOBA_HEADS * HEAD_DIM
MOBA_BLOCK = 256
MOBA_TOPK = 3
DIFF_HEADS = 4
DIFF_WIDTH = DIFF_HEADS * 2 * HEAD_DIM
REL_BUCKETS = 32
REL_MAX_DISTANCE = 128
XATTN_HEADS = 4
N_EXPERTS = 32
TOP_K = 4
SWIGLU_LIMIT = 7.0
SWIGLU_ALPHA = 1.702
NORM_EPS = 1e-5
LAMBDA_INIT = 0.8 - 0.6 * math.exp(-0.3 * 0)
ATTN_SCALE = HEAD_DIM ** -0.5
LOG2E = math.log2(math.e)

ATTN_TILE = MOBA_BLOCK
MOBA_HEADS_PER_STEP = 4
DIFF_HEADS_PER_STEP = 2
LANES = 128
EXPERT_ROW_TILE = 256
EXPERT_UNIT_ROWS = 1536
EXPERT_UP_CHUNK = 512
EXPERT_DOWN_CHUNK = 512
GATHER_TOKENS = 256
VMEM_LIMIT = 56 * 1024 * 1024


def _params(n_axes):
    return pltpu.CompilerParams(dimension_semantics=("arbitrary",) * n_axes,
                                vmem_limit_bytes=VMEM_LIMIT)


def _rms(x, gain):
    return x * lax.rsqrt(jnp.mean(x * x, axis=-1, keepdims=True) + NORM_EPS) * gain


def _sigmoid(x):
    return 1.0 / (1.0 + jnp.exp(-x))


def _dot_nt(a, b):
    return lax.dot_general(a, b, (((1,), (1,)), ((), ())), preferred_element_type=F32)


def _rmsnorm_body(x_ref, g_ref, o_ref):
    o_ref[...] = _rms(x_ref[...], g_ref[...]).astype(o_ref.dtype)


def rmsnorm_rows(x, gain, tm=512):
    t, d = x.shape
    return pl.pallas_call(
        _rmsnorm_body,
        grid=(t // tm,),
        in_specs=[pl.BlockSpec((tm, d), lambda i: (i, 0)),
                  pl.BlockSpec((1, d), lambda i: (0, 0))],
        out_specs=pl.BlockSpec((tm, d), lambda i: (i, 0)),
        out_shape=jax.ShapeDtypeStruct((t, d), BF16),
        compiler_params=_params(1),
        name="rmsnorm_rows",
    )(x, gain.reshape(1, d))


def _matmul_body(a_ref, w_ref, cs_ref, o_ref, w_bf):
    @pl.when(pl.program_id(1) == 0)
    def _():
        w_bf[...] = w_ref[...].astype(BF16)

    acc = jnp.dot(a_ref[...], w_bf[...], preferred_element_type=F32)
    o_ref[...] = (acc * cs_ref[...]).astype(o_ref.dtype)


def matmul_colscale(a, w, col_scale, tm=1024, tn=1024, out_dtype=BF16):
    m, k = a.shape
    n = w.shape[1]
    tm, tn = min(tm, m), min(tn, n)
    return pl.pallas_call(
        _matmul_body,
        grid=(n // tn, m // tm),
        in_specs=[pl.BlockSpec((tm, k), lambda j, i: (i, 0)),
                  pl.BlockSpec((k, tn), lambda j, i: (0, j)),
                  pl.BlockSpec((1, tn), lambda j, i: (0, j))],
        out_specs=pl.BlockSpec((tm, tn), lambda j, i: (i, j)),
        out_shape=jax.ShapeDtypeStruct((m, n), out_dtype),
        scratch_shapes=[pltpu.VMEM((k, tn), BF16)],
        compiler_params=_params(2),
        name="matmul",
    )(a, w, col_scale.reshape(1, n).astype(F32))


def _rel_bucket(dist):
    n = jnp.maximum(dist, 0)
    max_exact = REL_BUCKETS // 2
    nf = jnp.maximum(n, max_exact).astype(F32)
    large = max_exact + (jnp.log(nf / max_exact) / math.log(REL_MAX_DISTANCE / max_exact)
                         * (REL_BUCKETS - max_exact)).astype(jnp.int32)
    return jnp.where(n < max_exact, n, jnp.minimum(large, REL_BUCKETS - 1))


def _bias_tiles(table):
    t = ATTN_TILE
    r = jnp.arange(t)[:, None]
    c = jnp.arange(t)[None, :]
    dist = jnp.stack([r - c, t + r - c])
    onehot = (_rel_bucket(dist)[..., None] == jnp.arange(REL_BUCKETS)).astype(F32)
    near = jnp.einsum("irck,kh->hirc", onehot, table.astype(F32), precision=lax.Precision.HIGHEST)
    far = jnp.broadcast_to(table[REL_BUCKETS - 1][:, None, None], (table.shape[1], 1, t))
    return near * LOG2E, far.astype(F32) * LOG2E


def _lane_halves(x, op):
    return op(x[:, :LANES], x[:, LANES:])


AttnStream = collections.namedtuple("AttnStream", "q k_rows v_rows near cfar pen_at s_scr acc lsum mpast")


def _causal_attention(streams, qi):
    t = ATTN_TILE
    own_slot = streams[0].s_scr.shape[0] - 2
    prev_slot = own_slot + 1
    prev = jnp.maximum(qi - 1, 0)
    n_far = prev
    n_pairs = (n_far + 1) // 2
    row = lax.broadcasted_iota(jnp.int32, (t, t), 0)
    col = lax.broadcasted_iota(jnp.int32, (t, t), 1)
    has_prev = jnp.where(qi >= 1, 0.0, NEG_INF)

    def masked(st, s, n):
        return s if st.pen_at is None else s + st.pen_at(n)

    def tile_max(m, s):
        return jnp.maximum(m, _lane_halves(s, jnp.maximum))

    def tile_sum(l, p):
        return l + _lane_halves(p, jnp.add)

    m_near = []
    for st in streams:
        s2 = _dot_nt(st.q, jnp.concatenate([st.k_rows(qi, 1), st.k_rows(prev, 1)], axis=0))
        s_own = jnp.where(col <= row, s2[:, :t] + st.near(0), NEG_INF)
        s_prev = masked(st, s2[:, t:] + st.near(1) + has_prev, prev)
        st.s_scr[own_slot] = s_own
        st.s_scr[prev_slot] = s_prev - st.cfar
        m_near.append(tile_max(_lane_halves(s_own, jnp.maximum), s_prev))

    def pair_scores(i, m_far):
        second_is_far = jnp.where(2 * i + 1 < n_far, 0.0, NEG_INF)
        out = []
        for st, m in zip(streams, m_far):
            s = _dot_nt(st.q, st.k_rows(2 * i, 2))
            for half in range(2):
                sh = masked(st, s[:, half * t:(half + 1) * t], 2 * i + half)
                if half == 1:
                    sh = sh + second_is_far
                st.s_scr[2 * i + half] = sh
                m = tile_max(m, sh)
            out.append(m)
        return tuple(out)

    m_far = lax.fori_loop(0, n_pairs, pair_scores, tuple(jnp.full((t, LANES), NEG_INF, F32) for _ in streams))

    for st, mn, mf in zip(streams, m_near, m_far):
        m_row = jnp.max(jnp.maximum(mn, mf + st.cfar), axis=1, keepdims=True)
        mp = m_row - st.cfar
        p_own = jnp.exp2(st.s_scr[own_slot] - m_row)
        p_prev = jnp.exp2(st.s_scr[prev_slot] - mp)
        st.mpast[...] = jnp.broadcast_to(mp, st.mpast.shape)
        st.lsum[...] = tile_sum(_lane_halves(p_own, jnp.add), p_prev)
        st.acc[...] = jnp.dot(jnp.concatenate([p_own, p_prev], axis=1).astype(BF16),
                              jnp.concatenate([st.v_rows(qi, 1), st.v_rows(prev, 1)], axis=0),
                              preferred_element_type=F32)

    def pair_weights(i, carry):
        for st in streams:
            mp = jnp.concatenate([st.mpast[...]] * (2 * t // LANES), axis=1)
            p = jnp.exp2(jnp.concatenate([st.s_scr[2 * i], st.s_scr[2 * i + 1]], axis=1) - mp)
            st.lsum[...] = tile_sum(tile_sum(st.lsum[...], p[:, :t]), p[:, t:])
            st.acc[...] += jnp.dot(p.astype(BF16), st.v_rows(2 * i, 2), preferred_element_type=F32)
        return carry

    lax.fori_loop(0, n_pairs, pair_weights, 0)
    return [(st.acc[...], jnp.sum(st.lsum[...], axis=1, keepdims=True)) for st in streams]


def _block_rows(n, w=1):
    return pl.ds(pl.multiple_of(n * ATTN_TILE, ATTN_TILE), w * ATTN_TILE)


def _moba_body(q_ref, k_ref, v_ref, near_ref, far_ref, o_ref, kmean_ref, s_scr, acc_scr, lsum_scr, mpast_scr, *,
               n_blocks):
    qi = pl.program_id(2)
    t = ATTN_TILE
    dh = HEAD_DIM
    heads = MOBA_HEADS_PER_STEP

    @pl.when(qi == 0)
    def _():
        for j in range(heads):
            for n in range(n_blocks):
                kmean_ref[j, n:n + 1, :] = jnp.mean(k_ref[n * t:(n + 1) * t, j * dh:(j + 1) * dh].astype(F32),
                                                    axis=0, keepdims=True)

    def stream(j):
        cols = slice(j * dh, (j + 1) * dh)
        q = q_ref[:, cols]
        gate = lax.dot_general(kmean_ref[j], q.astype(F32), (((1,), (1,)), ((), ())),
                               precision=lax.Precision.HIGHEST, preferred_element_type=F32)
        valid = lax.broadcasted_iota(jnp.int32, gate.shape, 0) < qi
        g = jnp.where(valid, gate, NEG_INF)
        kth = g
        for _ in range(MOBA_TOPK - 1):
            top = jnp.max(kth, axis=0, keepdims=True)
            kth = jnp.where(kth == top, NEG_INF, kth)
        third = jnp.max(kth, axis=0, keepdims=True)
        pen_t = jnp.where(valid & (g >= third), 0.0, NEG_INF)
        pen_t = jnp.concatenate([pen_t, jnp.full((LANES - n_blocks, t), NEG_INF, F32)], axis=0)
        pen = pen_t.T
        blk = lax.broadcasted_iota(jnp.int32, pen.shape, 1)

        def pen_at(n):
            return jnp.max(jnp.where(blk == n, pen, NEG_INF), axis=1, keepdims=True)

        return AttnStream(q, lambda n, w: k_ref[_block_rows(n, w), cols], lambda n, w: v_ref[_block_rows(n, w), cols],
                          lambda i: near_ref[j, i], far_ref[j][:, :1], pen_at, s_scr.at[j],
                          acc_scr.at[j], lsum_scr.at[j], mpast_scr.at[j])

    results = _causal_attention([stream(j) for j in range(heads)], qi)
    for j, (acc, l) in enumerate(results):
        o_ref[:, j * dh:(j + 1) * dh] = (acc / l).astype(o_ref.dtype)


def moba_attention(y, near, far, batch, seq):
    t = ATTN_TILE
    nq = seq // t
    hs = MOBA_HEADS_PER_STEP
    groups = MOBA_HEADS // hs
    w = hs * HEAD_DIM
    return pl.pallas_call(
        functools.partial(_moba_body, n_blocks=nq),
        grid=(batch, groups, nq),
        in_specs=[pl.BlockSpec((t, w), lambda b, g, i: (b * nq + i, g)),
                  pl.BlockSpec((seq, w), lambda b, g, i: (b, groups + g)),
                  pl.BlockSpec((seq, w), lambda b, g, i: (b, 2 * groups + g)),
                  pl.BlockSpec((hs, 2, t, t), lambda b, g, i: (g, 0, 0, 0)),
                  pl.BlockSpec((hs, 1, t), lambda b, g, i: (g, 0, 0))],
        out_specs=pl.BlockSpec((t, w), lambda b, g, i: (b * nq + i, g)),
        out_shape=jax.ShapeDtypeStruct((batch * seq, MOBA_WIDTH), BF16),
        scratch_shapes=[pltpu.VMEM((hs, nq, HEAD_DIM), F32),
                        pltpu.VMEM((hs, nq + 2, t, t), F32),
                        pltpu.VMEM((hs, t, HEAD_DIM), F32),
                        pltpu.VMEM((hs, t, LANES), F32),
                        pltpu.VMEM((hs, t, LANES), F32)],
        compiler_params=_params(3),
        name="moba_attention",
    )(y, y, y, near, far)


def _diff_body(q_ref, k_ref, v_ref, near_ref, far_ref, lq1_ref, lk1_ref, lq2_ref, lk2_ref, subln_ref, o_ref,
               s_scr, acc_scr, lsum_scr, mpast_scr):
    qi = pl.program_id(2)
    dh = HEAD_DIM
    w = 2 * dh
    heads = DIFF_HEADS_PER_STEP

    def stream(hh, j):
        qk_cols = slice(hh * w + j * dh, hh * w + (j + 1) * dh)
        v_cols = slice(hh * w, (hh + 1) * w)
        return AttnStream(q_ref[:, qk_cols], lambda n, nb: k_ref[_block_rows(n, nb), qk_cols],
                          lambda n, nb: v_ref[_block_rows(n, nb), v_cols], lambda i: near_ref[hh, i],
                          far_ref[hh][:, :1], None, s_scr.at[2 * hh + j],
                          acc_scr.at[2 * hh + j], lsum_scr.at[2 * hh + j], mpast_scr.at[2 * hh + j])

    results = _causal_attention([stream(hh, j) for hh in range(heads) for j in range(2)], qi)
    lam = (jnp.exp(jnp.sum(lq1_ref[...] * lk1_ref[...], axis=1, keepdims=True))
           - jnp.exp(jnp.sum(lq2_ref[...] * lk2_ref[...], axis=1, keepdims=True)) + LAMBDA_INIT)
    for hh in range(heads):
        (acc1, l1), (acc2, l2) = results[2 * hh], results[2 * hh + 1]
        o = acc1 / l1 - lam * (acc2 / l2)
        o_ref[:, hh * w:(hh + 1) * w] = (_rms(o, subln_ref[...]) * (1.0 - LAMBDA_INIT)).astype(o_ref.dtype)


def diff_attention(y, near, far, lq1, lk1, lq2, lk2, subln, batch, seq, col0):
    t = ATTN_TILE
    nq = seq // t
    hs = DIFF_HEADS_PER_STEP
    groups = DIFF_HEADS // hs
    w = hs * 2 * HEAD_DIM
    base = col0 // w
    vec = lambda a: a.reshape(1, -1).astype(F32)
    small = lambda n: pl.BlockSpec((1, n), lambda b, g, i: (0, 0))
    return pl.pallas_call(
        _diff_body,
        grid=(batch, groups, nq),
        in_specs=[pl.BlockSpec((t, w), lambda b, g, i: (b * nq + i, base + g)),
                  pl.BlockSpec((seq, w), lambda b, g, i: (b, base + groups + g)),
                  pl.BlockSpec((seq, w), lambda b, g, i: (b, base + 2 * groups + g)),
                  pl.BlockSpec((hs, 2, t, t), lambda b, g, i: (g, 0, 0, 0)),
                  pl.BlockSpec((hs, 1, t), lambda b, g, i: (g, 0, 0)),
                  small(HEAD_DIM), small(HEAD_DIM), small(HEAD_DIM), small(HEAD_DIM), small(2 * HEAD_DIM)],
        out_specs=pl.BlockSpec((t, w), lambda b, g, i: (b * nq + i, g)),
        out_shape=jax.ShapeDtypeStruct((batch * seq, DIFF_WIDTH), BF16),
        scratch_shapes=[pltpu.VMEM((2 * hs, nq + 2, t, t), F32),
                        pltpu.VMEM((2 * hs, t, 2 * HEAD_DIM), F32),
                        pltpu.VMEM((2 * hs, t, LANES), F32),
                        pltpu.VMEM((2 * hs, t, LANES), F32)],
        compiler_params=_params(3),
        name="diff_attention",
    )(y, y, y, near, far, vec(lq1), vec(lk1), vec(lq2), vec(lk2), vec(subln))


def _branch_body(om_ref, od_ref, wm_ref, wd_ref, ga_ref, gb_ref, o_ref):
    a = jnp.dot(om_ref[...], wm_ref[...], preferred_element_type=F32)
    b = jnp.dot(od_ref[...], wd_ref[...], preferred_element_type=F32)
    o_ref[...] = (_sigmoid(ga_ref[...].astype(F32)) * a + _sigmoid(gb_ref[...].astype(F32)) * b).astype(o_ref.dtype)


def branch_merge(o_moba, o_diff, w_m, w_d, y, gate_col0, tm=1024, tn=1024):
    m, k = o_moba.shape
    n = w_m.shape[1]
    g0 = gate_col0 // tn
    nj = n // tn
    return pl.pallas_call(
        _branch_body,
        grid=(nj, m // tm),
        in_specs=[pl.BlockSpec((tm, k), lambda j, i: (i, 0)),
                  pl.BlockSpec((tm, k), lambda j, i: (i, 0)),
                  pl.BlockSpec((k, tn), lambda j, i: (0, j)),
                  pl.BlockSpec((k, tn), lambda j, i: (0, j)),
                  pl.BlockSpec((tm, tn), lambda j, i: (i, g0 + j)),
                  pl.BlockSpec((tm, tn), lambda j, i: (i, g0 + nj + j))],
        out_specs=pl.BlockSpec((tm, tn), lambda j, i: (i, j)),
        out_shape=jax.ShapeDtypeStruct((m, n), BF16),
        compiler_params=_params(2),
        name="branch_merge",
    )(o_moba, o_diff, w_m, w_d, y, y)


def _mixout_body(a_ref, w_ref, x_ref, g_ref, x1_ref, h_ref):
    x1 = x_ref[...] + jnp.dot(a_ref[...], w_ref[...], preferred_element_type=F32)
    x1_ref[...] = x1
    h_ref[...] = _rms(x1, g_ref[...]).astype(h_ref.dtype)


def mixout(merged, w, x, gain, tm=512):
    m, k = merged.shape
    n = w.shape[1]
    return pl.pallas_call(
        _mixout_body,
        grid=(m // tm,),
        in_specs=[pl.BlockSpec((tm, k), lambda i: (i, 0)),
                  pl.BlockSpec((k, n), lambda i: (0, 0)),
                  pl.BlockSpec((tm, n), lambda i: (i, 0)),
                  pl.BlockSpec((1, n), lambda i: (0, 0))],
        out_specs=[pl.BlockSpec((tm, n), lambda i: (i, 0)),
                   pl.BlockSpec((tm, n), lambda i: (i, 0))],
        out_shape=[jax.ShapeDtypeStruct((m, n), F32), jax.ShapeDtypeStruct((m, n), BF16)],
        compiler_params=_params(1),
        name="mixout",
    )(merged, w, x, gain.reshape(1, n))


def _memkv_body(mem_ref, g_ref, wk_ref, wv_ref, k_ref, v_ref):
    mn = _rms(mem_ref[...], g_ref[...]).astype(BF16)
    k_ref[...] = jnp.dot(mn, wk_ref[...], preferred_element_type=F32).astype(k_ref.dtype)
    v_ref[...] = jnp.dot(mn, wv_ref[...], preferred_element_type=F32).astype(v_ref.dtype)


def memory_kv(mem2d, gain, w_k, w_v, rows):
    m, d = mem2d.shape
    n = w_k.shape[1]
    return pl.pallas_call(
        _memkv_body,
        grid=(m // rows,),
        in_specs=[pl.BlockSpec((rows, d), lambda i: (i, 0)),
                  pl.BlockSpec((1, d), lambda i: (0, 0)),
                  pl.BlockSpec((d, n), lambda i: (0, 0)),
                  pl.BlockSpec((d, n), lambda i: (0, 0))],
        out_specs=[pl.BlockSpec((rows, n), lambda i: (i, 0)),
                   pl.BlockSpec((rows, n), lambda i: (i, 0))],
        out_shape=[jax.ShapeDtypeStruct((m, n), BF16), jax.ShapeDtypeStruct((m, n), BF16)],
        compiler_params=_params(1),
        name="memory_kv",
    )(mem2d, gain.reshape(1, d), w_k, w_v)


def _xattn_body(h_ref, x1_ref, k_ref, v_ref, wq_ref, wo_ref, g_ref, wr_ref, br_ref,
                x2_ref, h2_ref, idx_ref, wgt_ref):
    q = jnp.dot(h_ref[...], wq_ref[...], preferred_element_type=F32).astype(BF16)
    outs = []
    for hh in range(XATTN_HEADS):
        sl = slice(hh * HEAD_DIM, (hh + 1) * HEAD_DIM)
        s = _dot_nt(q[:, sl], k_ref[:, sl]) * ATTN_SCALE
        p = jnp.exp(s - jnp.max(s, axis=1, keepdims=True))
        o = jnp.dot(p.astype(BF16), v_ref[:, sl], preferred_element_type=F32)
        outs.append((o / jnp.sum(p, axis=1, keepdims=True)).astype(BF16))
    o = jnp.concatenate(outs, axis=1)
    x2 = x1_ref[...] + jnp.dot(o, wo_ref[...], preferred_element_type=F32)
    x2_ref[...] = x2
    h2 = _rms(x2, g_ref[...])
    h2_ref[...] = h2

    h_hi = h2.astype(BF16)
    h_lo = (h2 - h_hi.astype(F32)).astype(BF16)
    w_hi = wr_ref[...].astype(BF16)
    w_lo = (wr_ref[...] - w_hi.astype(F32)).astype(BF16)
    logits = (jnp.dot(h_hi, w_hi, preferred_element_type=F32) + jnp.dot(h_lo, w_hi, preferred_element_type=F32)
              + jnp.dot(h_hi, w_lo, preferred_element_type=F32) + br_ref[...])
    lane = lax.broadcasted_iota(jnp.int32, logits.shape, 1)
    out_lane = lax.broadcasted_iota(jnp.int32, idx_ref.shape, 1)
    idx_out = jnp.zeros(idx_ref.shape, jnp.int32)
    exp_out = jnp.zeros(wgt_ref.shape, F32)
    denom = jnp.zeros((logits.shape[0], 1), F32)
    top0 = None
    for kk in range(TOP_K):
        top = jnp.max(logits, axis=1, keepdims=True)
        arg = jnp.min(jnp.where(logits == top, lane, N_EXPERTS), axis=1, keepdims=True)
        logits = jnp.where(lane == arg, NEG_INF, logits)
        top0 = top if top0 is None else top0
        e = jnp.exp(top - top0)
        denom = denom + e
        idx_out = jnp.where(out_lane == kk, arg, idx_out)
        exp_out = jnp.where(out_lane == kk, e, exp_out)
    idx_ref[...] = idx_out
    wgt_ref[...] = exp_out / denom


def cross_attention_router(hx, x1, k_mem, v_mem, w_q, w_o, gain, w_router, b_router, seq, tm=512):
    m, d = hx.shape
    mem_len = k_mem.shape[0] // (m // seq)
    n = w_q.shape[1]
    per_b = seq // tm
    const = lambda shape: pl.BlockSpec(shape, lambda i: (0,) * len(shape))
    rows = lambda cols: pl.BlockSpec((tm, cols), lambda i: (i, 0))
    return pl.pallas_call(
        _xattn_body,
        grid=(m // tm,),
        in_specs=[rows(d), rows(d),
                  pl.BlockSpec((mem_len, n), lambda i: (i // per_b, 0)),
                  pl.BlockSpec((mem_len, n), lambda i: (i // per_b, 0)),
                  const((d, n)), const((n, d)), const((1, d)), const((d, N_EXPERTS)), const((1, N_EXPERTS))],
        out_specs=[rows(d), rows(d), rows(LANES), rows(LANES)],
        out_shape=[jax.ShapeDtypeStruct((m, d), F32), jax.ShapeDtypeStruct((m, d), F32),
                   jax.ShapeDtypeStruct((m, LANES), jnp.int32), jax.ShapeDtypeStruct((m, LANES), F32)],
        compiler_params=_params(1),
        name="cross_attention_router",
    )(hx, x1, k_mem, v_mem, w_q, w_o, gain.reshape(1, d), w_router.astype(F32), b_router.reshape(1, -1).astype(F32))


def _routing_plan(top_idx, n_tokens):
    rt = EXPERT_ROW_TILE
    tiles_per_unit = EXPERT_UNIT_ROWS // rt
    slot_onehot = (top_idx[:, :, None] == jnp.arange(N_EXPERTS)[None, None, :]).astype(jnp.int32)
    onehot = slot_onehot.sum(axis=1)
    before = jnp.cumsum(onehot, axis=0) - onehot
    count = onehot.sum(axis=0)
    tiles = (count + rt - 1) // rt
    tile_start = jnp.cumsum(tiles) - tiles
    pos = (slot_onehot * (tile_start * rt + before)[:, None, :]).sum(axis=-1)

    units = (tiles + tiles_per_unit - 1) // tiles_per_unit
    unit_first = jnp.cumsum(units) - units
    n_units = units.sum()
    max_units = N_EXPERTS + (n_tokens * TOP_K) // EXPERT_UNIT_ROWS
    uid = jnp.arange(max_units)
    e_of = jnp.clip(jnp.searchsorted(jnp.cumsum(units), uid, side="right"), 0, N_EXPERTS - 1)
    k_in = uid - unit_first[e_of]
    live = uid < n_units
    last_e = e_of[jnp.maximum(n_units - 1, 0)]
    unit_expert = jnp.where(live, e_of, last_e).astype(jnp.int32)
    unit_start = jnp.where(live, (tile_start[e_of] + k_in * tiles_per_unit) * rt, 0).astype(jnp.int32)
    unit_tiles = jnp.where(live, jnp.minimum(tiles[e_of] - k_in * tiles_per_unit, tiles_per_unit), 0).astype(jnp.int32)
    totals = jnp.stack([n_units, tiles.sum()]).astype(jnp.int32)
    pad_plan = jnp.concatenate([tile_start * rt + count, (tile_start + tiles) * rt, tiles.sum()[None]]).astype(jnp.int32)
    return pos.astype(jnp.int32), unit_expert, unit_start, unit_tiles, totals, pad_plan


def _dispatch_body(pos_hbm, pad_ref, h_ref, xs_hbm, pos_smem, zeros, sem):
    i = pl.program_id(0)
    n = pl.num_programs(0)
    groups, sub, _ = h_ref.shape
    per_step = groups * sub * TOP_K
    rt = zeros.shape[0]

    def idx_copy(step):
        half = pl.ds(pl.multiple_of((step % 2) * per_step, per_step), per_step)
        return pltpu.make_async_copy(pos_hbm.at[step], pos_smem.at[half], sem.at[0])

    @pl.when(i == 0)
    def _():
        idx_copy(0).start()
        idx_copy(0).wait()

    @pl.when(i + 1 < n)
    def _():
        idx_copy(i + 1).start()

    base = (i % 2) * per_step

    def send(g, carry):
        for s in range(sub):
            for kk in range(TOP_K):
                r = pos_smem[base + g * (sub * TOP_K) + s * TOP_K + kk]
                pltpu.make_async_copy(h_ref.at[g, pl.ds(s, 1), :], xs_hbm.at[pl.ds(r, 1), :], sem.at[1]).start()
        return carry

    lax.fori_loop(0, groups, send, 0)

    def drain(g, carry):
        pltpu.make_async_copy(h_ref.at[0], xs_hbm.at[pl.ds(0, sub), :], sem.at[1]).wait()
        return carry

    lax.fori_loop(0, groups * TOP_K, drain, 0)

    @pl.when(i + 1 < n)
    def _():
        idx_copy(i + 1).wait()

    @pl.when(i == n - 1)
    def _():
        zeros[...] = jnp.zeros(zeros.shape, zeros.dtype)

        def pad_rows(r, size):
            return pltpu.make_async_copy(zeros.at[pl.ds(0, size), :], xs_hbm.at[pl.ds(r, size), :], sem.at[1])

        def pad_tile(j):
            return pltpu.make_async_copy(zeros, xs_hbm.at[pl.ds(pl.multiple_of(j * rt, rt), rt), :], sem.at[1])

        def each_pad(fn):
            def expert(e, carry):
                first = pad_ref[e]
                end = pad_ref[N_EXPERTS + e]
                aligned = jnp.minimum((first + sub - 1) // sub * sub, end)

                def row(r, carry):
                    fn(pad_rows(r, 1))
                    return carry

                lax.fori_loop(first, aligned, row, 0)
                length = end - aligned
                size = sub
                while size < rt:
                    @pl.when((length & size) != 0)
                    def _(size=size):
                        fn(pad_rows(pl.multiple_of(aligned + (length & (size - 1)), sub), size))
                    size *= 2
                return carry
            lax.fori_loop(0, N_EXPERTS, expert, 0)

            def tile(j, carry):
                fn(pad_tile(j))
                return carry
            lax.fori_loop(pad_ref[2 * N_EXPERTS], xs_hbm.shape[0] // rt, tile, 0)

        each_pad(lambda copy: copy.start())
        each_pad(lambda copy: copy.wait())


def dispatch_rows(h2, pos, pad_plan, p_rows):
    t, d = h2.shape
    tm = GATHER_TOKENS
    sub = 8
    return pl.pallas_call(
        _dispatch_body,
        grid=(t // tm,),
        in_specs=[pl.BlockSpec(memory_space=pl.ANY),
                  pl.BlockSpec(memory_space=pltpu.SMEM),
                  pl.BlockSpec((tm // sub, sub, d), lambda i: (i, 0, 0))],
        out_specs=pl.BlockSpec(memory_space=pl.ANY),
        out_shape=jax.ShapeDtypeStruct((p_rows, d), h2.dtype),
        scratch_shapes=[pltpu.SMEM((2 * tm * TOP_K,), jnp.int32),
                        pltpu.VMEM((EXPERT_ROW_TILE, d), h2.dtype),
                        pltpu.SemaphoreType.DMA((2,))],
        compiler_params=_params(1),
        name="dispatch_rows",
    )(pos.reshape(t // tm, tm * TOP_K), pad_plan, h2.reshape(t // sub, sub, d))


def _expert_body(ue_ref, us_ref, un_ref, nu_ref,
                 x_hbm, wgu_ref, bgu_ref, wd_ref, bd_ref, y_hbm,
                 xbuf, actbuf, gubuf, ystage, wd_f32, sem_x, sem_y, *, n_up, n_down):
    u = pl.program_id(0)
    c = pl.program_id(1)
    rt = EXPERT_ROW_TILE
    half = EXPERT_UP_CHUNK // 2
    quarter = half // 2
    chunk = EXPERT_DOWN_CHUNK
    n_live = nu_ref[0]
    live = u < n_live
    start = us_ref[u]
    n_tiles = un_ref[u]

    n_pairs = n_tiles // 2
    odd = n_tiles % 2 == 1

    def span_rows(j, tiles):
        return pl.ds(pl.multiple_of(j * rt, rt), tiles * rt)

    def tile_rows(j):
        return span_rows(j, 1)

    def x_copy(unit, j):
        rows = pl.ds(pl.multiple_of(us_ref[unit] + j * rt, rt), rt)
        return pltpu.make_async_copy(x_hbm.at[rows, :], xbuf.at[tile_rows(j), :], sem_x.at[0])

    def fetch_rows(unit):
        def body(j, carry):
            x_copy(unit, j).start()
            return carry
        lax.fori_loop(0, un_ref[unit], body, 0)

    @pl.when((u == 0) & (c == 0) & live)
    def _():
        fetch_rows(0)

    @pl.when(live & (c == 0))
    def _():
        def body(j, carry):
            x_copy(u, j).wait()
            return carry
        lax.fori_loop(0, n_tiles, body, 0)

    @pl.when((c == n_up) & (u + 1 < n_live))
    def _():
        fetch_rows(u + 1)

    @pl.when(live & (c < n_up))
    def _():
        bias = bgu_ref[ue_ref[u], pl.ds(c, 1), :]

        n_spans = n_pairs + n_tiles % 2
        first_slot = (c * n_spans) % 2

        def slot_of(i):
            return (first_slot + i) % 2

        def project(j, tiles, slot):
            x = xbuf[span_rows(j, tiles), :].astype(BF16)
            gubuf[slot, :tiles * rt, :] = jnp.dot(x, wgu_ref[0].astype(BF16), preferred_element_type=F32) + bias

        def activate(group, j, tiles, slot):
            gu = gubuf[slot, :tiles * rt, :]
            even = (lax.broadcasted_iota(jnp.int32, (tiles * rt, half), 1) % 2) == 0
            lo = gu[:, :half]
            hi = gu[:, half:]
            gate = jnp.where(even, lo, pltpu.roll(hi, 1, axis=1))
            up = jnp.where(even, pltpu.roll(lo, half - 1, axis=1), hi)
            gate = jnp.minimum(gate, SWIGLU_LIMIT)
            up = jnp.clip(up, -SWIGLU_LIMIT, SWIGLU_LIMIT)
            act = (up + 1.0) * gate * _sigmoid(SWIGLU_ALPHA * gate)
            actbuf[group, span_rows(j, tiles), :] = act.astype(BF16)

        has_pairs = n_pairs >= 1
        shapes = [
            (has_pairs & odd, 2, 2 * n_pairs, 1),
            (has_pairs & jnp.logical_not(odd), 2, 2 * (n_pairs - 1), 2),
            (jnp.logical_not(has_pairs), 1, 0, 1),
        ]
        for case, first_tiles, last_j, last_tiles in shapes:
            @pl.when(case & (c == 0))
            def _(first_tiles=first_tiles):
                project(0, first_tiles, slot_of(0))

            @pl.when(case & (c > 0))
            def _(first_tiles=first_tiles, last_j=last_j, last_tiles=last_tiles):
                activate(c - 1, last_j, last_tiles, slot_of(1))
                project(0, first_tiles, slot_of(0))

        @pl.when(has_pairs)
        def _():
            def body(i, carry):
                activate(c, 2 * (i - 1), 2, slot_of(i - 1))
                project(2 * i, 2, slot_of(i))
                return carry

            lax.fori_loop(1, n_pairs, body, 0)

        @pl.when(has_pairs & odd)
        def _():
            activate(c, 2 * (n_pairs - 1), 2, slot_of(n_pairs - 1))
            project(2 * n_pairs, 1, slot_of(n_pairs))

        for case, first_tiles, last_j, last_tiles in shapes:
            @pl.when(case & (c == n_up - 1))
            def _(last_j=last_j, last_tiles=last_tiles):
                activate(c, last_j, last_tiles, slot_of(n_spans - 1))

    @pl.when(live & (c >= n_up))
    def _():
        cd = c - n_up
        for g in range(chunk // LANES):
            lanes = slice(g * LANES, (g + 1) * LANES)
            for f in range(actbuf.shape[0]):
                base = f * half
                wd_f32[g, pl.ds(base, quarter, stride=2), :] = wd_ref[0, base:base + quarter, lanes]
                wd_f32[g, pl.ds(base + 1, quarter, stride=2), :] = wd_ref[0, base + quarter:base + half, lanes]
        bias = bd_ref[ue_ref[u], pl.ds(cd, 1), :]

        def y_copy(j, tiles, slot):
            rows = pl.ds(pl.multiple_of(start + j * rt, rt), tiles * rt)
            cols = pl.ds(pl.multiple_of(cd * chunk, chunk), chunk)
            return pltpu.make_async_copy(ystage.at[slot, :tiles * rt, :], y_hbm.at[rows, cols], sem_y.at[slot])

        def emit(j, tiles, slot):
            act = jnp.concatenate([actbuf[f, span_rows(j, tiles), :] for f in range(actbuf.shape[0])], axis=1)
            w = jnp.concatenate([wd_f32[g].astype(BF16) for g in range(chunk // LANES)], axis=1)
            ystage[slot, :tiles * rt, :] = jnp.dot(act, w, preferred_element_type=F32) + bias
            y_copy(j, tiles, slot).start()

        def pair(i, carry):
            @pl.when(i >= 2)
            def _():
                y_copy(2 * (i - 2), 2, i % 2).wait()

            emit(2 * i, 2, i % 2)
            return carry

        lax.fori_loop(0, n_pairs, pair, 0)

        @pl.when(n_pairs >= 2)
        def _():
            y_copy(2 * (n_pairs - 2), 2, n_pairs % 2).wait()

        @pl.when(odd)
        def _():
            emit(2 * n_pairs, 1, n_pairs % 2)

        @pl.when(n_pairs >= 1)
        def _():
            y_copy(2 * (n_pairs - 1), 2, (n_pairs - 1) % 2).wait()

        @pl.when(odd)
        def _():
            y_copy(2 * n_pairs, 1, n_pairs % 2).wait()

    @pl.when((u == pl.num_programs(0) - 1) & (c == n_up + n_down - 1))
    def _():
        zero_rows = xbuf.at[:rt, :]
        zero_rows[...] = jnp.zeros(zero_rows.shape, F32)
        used_tiles = nu_ref[1]

        def pad_copy(j):
            return pltpu.make_async_copy(zero_rows, y_hbm.at[pl.ds(pl.multiple_of(j * rt, rt), rt), :], sem_x.at[0])

        def pad_start(j, carry):
            pad_copy(j).start()
            return carry

        def pad_wait(j, carry):
            pad_copy(j).wait()
            return carry

        lax.fori_loop(used_tiles, y_hbm.shape[0] // rt, pad_start, 0)
        lax.fori_loop(used_tiles, y_hbm.shape[0] // rt, pad_wait, 0)


def expert_ffn(x_sorted, w_gate_up, b_gate_up, w_down, b_down, unit_expert, unit_start, unit_tiles, totals):
    p_rows, d = x_sorted.shape
    n_exp, _, two_ff = w_gate_up.shape
    d_ff = two_ff // 2
    up_chunk = EXPERT_UP_CHUNK
    chunk = EXPERT_DOWN_CHUNK
    n_up = two_ff // up_chunk
    n_down = d // chunk
    n_steps = n_up + n_down
    max_units = unit_expert.shape[0]
    rt = EXPERT_ROW_TILE

    def up_idx(u, c, nu):
        return jnp.where(u < nu[0], jnp.minimum(c, n_up - 1), n_up - 1)

    def down_idx(u, c, nu):
        return jnp.where(u < nu[0], jnp.maximum(c - n_up, 0), n_down - 1)

    grid_spec = pltpu.PrefetchScalarGridSpec(
        num_scalar_prefetch=4,
        grid=(totals[0], n_steps),
        in_specs=[pl.BlockSpec(memory_space=pl.ANY),
                  pl.BlockSpec((1, d, up_chunk), lambda u, c, ue, us, un, nu: (ue[u], 0, up_idx(u, c, nu))),
                  pl.BlockSpec((n_exp, n_up, up_chunk), lambda u, c, ue, us, un, nu: (0, 0, 0)),
                  pl.BlockSpec((1, d_ff, chunk), lambda u, c, ue, us, un, nu: (ue[u], 0, down_idx(u, c, nu))),
                  pl.BlockSpec((n_exp, n_down, chunk), lambda u, c, ue, us, un, nu: (0, 0, 0))],
        out_specs=pl.BlockSpec(memory_space=pl.ANY),
        scratch_shapes=[pltpu.VMEM((EXPERT_UNIT_ROWS, d), F32),
                        pltpu.VMEM((two_ff // EXPERT_UP_CHUNK, EXPERT_UNIT_ROWS, EXPERT_UP_CHUNK // 2), BF16),
                        pltpu.VMEM((2, 2 * rt, EXPERT_UP_CHUNK), F32),
                        pltpu.VMEM((2, 2 * rt, chunk), F32),
                        pltpu.VMEM((chunk // LANES, d_ff, LANES), F32),
                        pltpu.SemaphoreType.DMA((1,)),
                        pltpu.SemaphoreType.DMA((2,))],
    )
    return pl.pallas_call(
        functools.partial(_expert_body, n_up=n_up, n_down=n_down),
        grid_spec=grid_spec,
        out_shape=jax.ShapeDtypeStruct((p_rows, d), F32),
        compiler_params=_params(2),
        name="expert_ffn",
    )(unit_expert, unit_start, unit_tiles, totals,
      x_sorted, w_gate_up, b_gate_up.reshape(n_exp, n_up, up_chunk), w_down,
      b_down.reshape(n_exp, n_down, chunk))


def _combine_body(pos_hbm, x2_ref, w_ref, g_ref, y_hbm, o_ref, pos_smem, ybuf, sem, *, final_norm):
    i = pl.program_id(0)
    n = pl.num_programs(0)
    groups, sub, _ = x2_ref.shape
    slot = i % 2
    per_step = groups * sub * TOP_K

    def idx_copy(step):
        half = pl.ds(pl.multiple_of((step % 2) * per_step, per_step), per_step)
        return pltpu.make_async_copy(pos_hbm.at[step], pos_smem.at[half], sem.at[2])

    def fetch_rows(step):
        into = step % 2
        base = into * per_step

        def recv(g, carry):
            for s in range(sub):
                for kk in range(TOP_K):
                    r = pos_smem[base + g * (sub * TOP_K) + s * TOP_K + kk]
                    pltpu.make_async_copy(y_hbm.at[pl.ds(r, 1), :], ybuf.at[into, kk, g, pl.ds(s, 1), :],
                                          sem.at[into]).start()
            return carry

        lax.fori_loop(0, groups, recv, 0)

    @pl.when(i == 0)
    def _():
        idx_copy(0).start()
        idx_copy(0).wait()
        fetch_rows(0)

        @pl.when(n > 1)
        def _():
            idx_copy(1).start()

    @pl.when(i + 1 < n)
    def _():
        idx_copy(i + 1).wait()
        fetch_rows(i + 1)

    @pl.when(i + 2 < n)
    def _():
        idx_copy(i + 2).start()

    def drain(g, carry):
        pltpu.make_async_copy(y_hbm.at[pl.ds(0, sub), :], ybuf.at[slot, 0, 0], sem.at[slot]).wait()
        return carry

    lax.fori_loop(0, groups * TOP_K, drain, 0)

    x3 = x2_ref[...]
    for kk in range(TOP_K):
        x3 = x3 + w_ref[:, :, kk:kk + 1] * ybuf[slot, kk]
    o_ref[...] = _rms(x3, g_ref[...]) if final_norm else x3


def combine(x2, y_sorted, pos, weights, gain):
    m, d = x2.shape
    tm = GATHER_TOKENS
    sub = 8
    final_norm = gain is not None
    gain = gain if final_norm else jnp.ones((d,), F32)
    rows = lambda width: pl.BlockSpec((tm // sub, sub, width), lambda i: (i, 0, 0))
    out = pl.pallas_call(
        functools.partial(_combine_body, final_norm=final_norm),
        grid=(m // tm,),
        in_specs=[pl.BlockSpec(memory_space=pl.ANY),
                  rows(d), rows(LANES),
                  pl.BlockSpec((1, d), lambda i: (0, 0)),
                  pl.BlockSpec(memory_space=pl.ANY)],
        out_specs=rows(d),
        out_shape=jax.ShapeDtypeStruct((m // sub, sub, d), F32),
        scratch_shapes=[pltpu.SMEM((2 * tm * TOP_K,), jnp.int32),
                        pltpu.VMEM((2, TOP_K, tm // sub, sub, d), F32),
                        pltpu.SemaphoreType.DMA((3,))],
        compiler_params=_params(1),
        name="combine",
    )(pos.reshape(m // tm, tm * TOP_K), x2.reshape(m // sub, sub, d), weights.reshape(m // sub, sub, LANES),
      gain.reshape(1, d), y_sorted)
    return out.reshape(m, d)


def kernel(x, mem, rel_bias_table, mix_norm, w_in, diff_lambda_q1, diff_lambda_k1, diff_lambda_q2, diff_lambda_k2, diff_subln, w_branch_moba, w_branch_diff, w_mix_out, xattn_norm, mem_norm, w_xq, w_xk, w_xv, w_xo, ffn_norm, w_router, b_router, w_gate_up, b_gate_up, w_down, b_down, final_norm):
    batch, seq, d = x.shape
    n_tok = batch * seq
    x2d = x.reshape(n_tok, d)
    near, far = _bias_tiles(rel_bias_table)
    diff_col0 = 3 * MOBA_WIDTH
    gate_col0 = diff_col0 + 3 * DIFF_WIDTH
    cols = jnp.arange(w_in.shape[2])
    is_q = (cols < MOBA_WIDTH) | ((cols >= diff_col0) & (cols < diff_col0 + DIFF_WIDTH))
    col_scale = jnp.where(is_q, ATTN_SCALE * LOG2E, 1.0)
    p_rows = n_tok * TOP_K + N_EXPERTS * EXPERT_ROW_TILE
    for l in range(w_in.shape[0]):
        h = rmsnorm_rows(x2d, mix_norm[l])
        y = matmul_colscale(h, w_in[l], col_scale, tm=2048)
        o_moba = moba_attention(y, near[:MOBA_HEADS], far[:MOBA_HEADS], batch, seq)
        o_diff = diff_attention(y, near[MOBA_HEADS:], far[MOBA_HEADS:], diff_lambda_q1[l], diff_lambda_k1[l],
                                diff_lambda_q2[l], diff_lambda_k2[l], diff_subln[l], batch, seq, diff_col0)
        merged = branch_merge(o_moba, o_diff, w_branch_moba[l].astype(BF16), w_branch_diff[l].astype(BF16),
                              y, gate_col0)
        x1, hx = mixout(merged, w_mix_out[l].astype(BF16), x2d, xattn_norm[l])
        k_mem, v_mem = memory_kv(mem.reshape(-1, d), mem_norm[l], w_xk[l].astype(BF16), w_xv[l].astype(BF16),
                                 mem.shape[1])
        x2, h2, idx_pad, wgt_pad = cross_attention_router(hx, x1, k_mem, v_mem, w_xq[l].astype(BF16),
                                                          w_xo[l].astype(BF16), ffn_norm[l], w_router[l],
                                                          b_router[l], seq)
        pos, unit_expert, unit_start, unit_tiles, totals, pad_plan = _routing_plan(idx_pad[:, :TOP_K], n_tok)
        x_sorted = dispatch_rows(h2, pos, pad_plan, p_rows)
        y_sorted = expert_ffn(x_sorted, w_gate_up[l], b_gate_up[l], w_down[l], b_down[l],
                              unit_expert, unit_start, unit_tiles, totals)
        last = l == w_in.shape[0] - 1
        x2d = combine(x2, y_sorted, pos, wgt_pad, final_norm if last else None)
    return x2d.reshape(batch, seq, d)
```

```python
import collections
import functools
import math

import jax
import jax.numpy as jnp
from jax import lax
from jax.experimental import pallas as pl
from jax.experimental.pallas import tpu as pltpu

F32 = jnp.float32
BF16 = jnp.bfloat16
NEG_INF = float("-inf")

D_MODEL = 2048
HEAD_DIM = 128
MOBA_HEADS = 8
MOBA_WIDTH = MOBA_HEADS * HEAD_DIM
MOBA_BLOCK = 256
MOBA_TOPK = 3
DIFF_HEADS = 4
DIFF_WIDTH = DIFF_HEADS * 2 * HEAD_DIM
REL_BUCKETS = 32
REL_MAX_DISTANCE = 128
XATTN_HEADS = 4
N_EXPERTS = 32
TOP_K = 4
SWIGLU_LIMIT = 7.0
SWIGLU_ALPHA = 1.702
NORM_EPS = 1e-5
LAMBDA_INIT = 0.8 - 0.6 * math.exp(-0.3 * 0)
ATTN_SCALE = HEAD_DIM ** -0.5
LOG2E = math.log2(math.e)

ATTN_TILE = MOBA_BLOCK
MOBA_HEADS_PER_STEP = 4
DIFF_HEADS_PER_STEP = 2
LANES = 128
EXPERT_ROW_TILE = 256
EXPERT_UNIT_ROWS = 1536
EXPERT_UP_CHUNK = 512
EXPERT_DOWN_CHUNK = 512
GATHER_TOKENS = 256
VMEM_LIMIT = 56 * 1024 * 1024


def _params(n_axes):
    return pltpu.CompilerParams(dimension_semantics=("arbitrary",) * n_axes,
                                vmem_limit_bytes=VMEM_LIMIT)


def _rms(x, gain):
    return x * lax.rsqrt(jnp.mean(x * x, axis=-1, keepdims=True) + NORM_EPS) * gain


def _sigmoid(x):
    return 1.0 / (1.0 + jnp.exp(-x))


def _dot_nt(a, b):
    return lax.dot_general(a, b, (((1,), (1,)), ((), ())), preferred_element_type=F32)


def _rmsnorm_body(x_ref, g_ref, o_ref):
    o_ref[...] = _rms(x_ref[...], g_ref[...]).astype(o_ref.dtype)


def rmsnorm_rows(x, gain, tm=512):
    t, d = x.shape
    return pl.pallas_call(
        _rmsnorm_body,
        grid=(t // tm,),
        in_specs=[pl.BlockSpec((tm, d), lambda i: (i, 0)),
                  pl.BlockSpec((1, d), lambda i: (0, 0))],
        out_specs=pl.BlockSpec((tm, d), lambda i: (i, 0)),
        out_shape=jax.ShapeDtypeStruct((t, d), BF16),
        compiler_params=_params(1),
        name="rmsnorm_rows",
    )(x, gain.reshape(1, d))


def _matmul_body(a_ref, w_ref, cs_ref, o_ref, w_bf):
    @pl.when(pl.program_id(1) == 0)
    def _():
        w_bf[...] = w_ref[...].astype(BF16)

    acc = jnp.dot(a_ref[...], w_bf[...], preferred_element_type=F32)
    o_ref[...] = (acc * cs_ref[...]).astype(o_ref.dtype)


def matmul_colscale(a, w, col_scale, tm=1024, tn=1024, out_dtype=BF16):
    m, k = a.shape
    n = w.shape[1]
    tm, tn = min(tm, m), min(tn, n)
    return pl.pallas_call(
        _matmul_body,
        grid=(n // tn, m // tm),
        in_specs=[pl.BlockSpec((tm, k), lambda j, i: (i, 0)),
                  pl.BlockSpec((k, tn), lambda j, i: (0, j)),
                  pl.BlockSpec((1, tn), lambda j, i: (0, j))],
        out_specs=pl.BlockSpec((tm, tn), lambda j, i: (i, j)),
        out_shape=jax.ShapeDtypeStruct((m, n), out_dtype),
        scratch_shapes=[pltpu.VMEM((k, tn), BF16)],
        compiler_params=_params(2),
        name="matmul",
    )(a, w, col_scale.reshape(1, n).astype(F32))


def _rel_bucket(dist):
    n = jnp.maximum(dist, 0)
    max_exact = REL_BUCKETS // 2
    nf = jnp.maximum(n, max_exact).astype(F32)
    large = max_exact + (jnp.log(nf / max_exact) / math.log(REL_MAX_DISTANCE / max_exact)
                         * (REL_BUCKETS - max_exact)).astype(jnp.int32)
    return jnp.where(n < max_exact, n, jnp.minimum(large, REL_BUCKETS - 1))


def _bias_tiles(table):
    t = ATTN_TILE
    r = jnp.arange(t)[:, None]
    c = jnp.arange(t)[None, :]
    dist = jnp.stack([r - c, t + r - c])
    onehot = (_rel_bucket(dist)[..., None] == jnp.arange(REL_BUCKETS)).astype(F32)
    near = jnp.einsum("irck,kh->hirc", onehot, table.astype(F32), precision=lax.Precision.HIGHEST)
    far = jnp.broadcast_to(table[REL_BUCKETS - 1][:, None, None], (table.shape[1], 1, t))
    return near * LOG2E, far.astype(F32) * LOG2E


def _lane_halves(x, op):
    return op(x[:, :LANES], x[:, LANES:])


AttnStream = collections.namedtuple("AttnStream", "q k_rows v_rows near cfar pen_at s_scr acc lsum mpast")


def _causal_attention(streams, qi):
    t = ATTN_TILE
    own_slot = streams[0].s_scr.shape[0] - 2
    prev_slot = own_slot + 1
    prev = jnp.maximum(qi - 1, 0)
    n_far = prev
    n_pairs = (n_far + 1) // 2
    row = lax.broadcasted_iota(jnp.int32, (t, t), 0)
    col = lax.broadcasted_iota(jnp.int32, (t, t), 1)
    has_prev = jnp.where(qi >= 1, 0.0, NEG_INF)

    def masked(st, s, n):
        return s if st.pen_at is None else s + st.pen_at(n)

    def tile_max(m, s):
        return jnp.maximum(m, _lane_halves(s, jnp.maximum))

    def tile_sum(l, p):
        return l + _lane_halves(p, jnp.add)

    m_near = []
    for st in streams:
        s2 = _dot_nt(st.q, jnp.concatenate([st.k_rows(qi, 1), st.k_rows(prev, 1)], axis=0))
        s_own = jnp.where(col <= row, s2[:, :t] + st.near(0), NEG_INF)
        s_prev = masked(st, s2[:, t:] + st.near(1) + has_prev, prev)
        st.s_scr[own_slot] = s_own
        st.s_scr[prev_slot] = s_prev - st.cfar
        m_near.append(tile_max(_lane_halves(s_own, jnp.maximum), s_prev))

    def pair_scores(i, m_far):
        second_is_far = jnp.where(2 * i + 1 < n_far, 0.0, NEG_INF)
        out = []
        for st, m in zip(streams, m_far):
            s = _dot_nt(st.q, st.k_rows(2 * i, 2))
            for half in range(2):
                sh = masked(st, s[:, half * t:(half + 1) * t], 2 * i + half)
                if half == 1:
                    sh = sh + second_is_far
                st.s_scr[2 * i + half] = sh
                m = tile_max(m, sh)
            out.append(m)
        return tuple(out)

    m_far = lax.fori_loop(0, n_pairs, pair_scores, tuple(jnp.full((t, LANES), NEG_INF, F32) for _ in streams))

    for st, mn, mf in zip(streams, m_near, m_far):
        m_row = jnp.max(jnp.maximum(mn, mf + st.cfar), axis=1, keepdims=True)
        mp = m_row - st.cfar
        p_own = jnp.exp2(st.s_scr[own_slot] - m_row)
        p_prev = jnp.exp2(st.s_scr[prev_slot] - mp)
        st.mpast[...] = jnp.broadcast_to(mp, st.mpast.shape)
        st.lsum[...] = tile_sum(_lane_halves(p_own, jnp.add), p_prev)
        st.acc[...] = jnp.dot(jnp.concatenate([p_own, p_prev], axis=1).astype(BF16),
                              jnp.concatenate([st.v_rows(qi, 1), st.v_rows(prev, 1)], axis=0),
                              preferred_element_type=F32)

    def pair_weights(i, carry):
        for st in streams:
            mp = jnp.concatenate([st.mpast[...]] * (2 * t // LANES), axis=1)
            p = jnp.exp2(jnp.concatenate([st.s_scr[2 * i], st.s_scr[2 * i + 1]], axis=1) - mp)
            st.lsum[...] = tile_sum(tile_sum(st.lsum[...], p[:, :t]), p[:, t:])
            st.acc[...] += jnp.dot(p.astype(BF16), st.v_rows(2 * i, 2), preferred_element_type=F32)
        return carry

    lax.fori_loop(0, n_pairs, pair_weights, 0)
    return [(st.acc[...], jnp.sum(st.lsum[...], axis=1, keepdims=True)) for st in streams]


def _block_rows(n, w=1):
    return pl.ds(pl.multiple_of(n * ATTN_TILE, ATTN_TILE), w * ATTN_TILE)


def _moba_body(q_ref, k_ref, v_ref, near_ref, far_ref, o_ref, kmean_ref, s_scr, acc_scr, lsum_scr, mpast_scr, *,
               n_blocks):
    qi = pl.program_id(2)
    t = ATTN_TILE
    dh = HEAD_DIM
    heads = MOBA_HEADS_PER_STEP

    @pl.when(qi == 0)
    def _():
        for j in range(heads):
            for n in range(n_blocks):
                kmean_ref[j, n:n + 1, :] = jnp.mean(k_ref[n * t:(n + 1) * t, j * dh:(j + 1) * dh].astype(F32),
                                                    axis=0, keepdims=True)

    def stream(j):
        cols = slice(j * dh, (j + 1) * dh)
        q = q_ref[:, cols]
        gate = lax.dot_general(kmean_ref[j], q.astype(F32), (((1,), (1,)), ((), ())),
                               precision=lax.Precision.HIGHEST, preferred_element_type=F32)
        valid = lax.broadcasted_iota(jnp.int32, gate.shape, 0) < qi
        g = jnp.where(valid, gate, NEG_INF)
        kth = g
        for _ in range(MOBA_TOPK - 1):
            top = jnp.max(kth, axis=0, keepdims=True)
            kth = jnp.where(kth == top, NEG_INF, kth)
        third = jnp.max(kth, axis=0, keepdims=True)
        pen_t = jnp.where(valid & (g >= third), 0.0, NEG_INF)
        pen_t = jnp.concatenate([pen_t, jnp.full((LANES - n_blocks, t), NEG_INF, F32)], axis=0)
        pen = pen_t.T
        blk = lax.broadcasted_iota(jnp.int32, pen.shape, 1)

        def pen_at(n):
            return jnp.max(jnp.where(blk == n, pen, NEG_INF), axis=1, keepdims=True)

        return AttnStream(q, lambda n, w: k_ref[_block_rows(n, w), cols], lambda n, w: v_ref[_block_rows(n, w), cols],
                          lambda i: near_ref[j, i], far_ref[j][:, :1], pen_at, s_scr.at[j],
                          acc_scr.at[j], lsum_scr.at[j], mpast_scr.at[j])

    results = _causal_attention([stream(j) for j in range(heads)], qi)
    for j, (acc, l) in enumerate(results):
        o_ref[:, j * dh:(j + 1) * dh] = (acc / l).astype(o_ref.dtype)


def moba_attention(y, near, far, batch, seq):
    t = ATTN_TILE
    nq = seq // t
    hs = MOBA_HEADS_PER_STEP
    groups = MOBA_HEADS // hs
    w = hs * HEAD_DIM
    return pl.pallas_call(
        functools.partial(_moba_body, n_blocks=nq),
        grid=(batch, groups, nq),
        in_specs=[pl.BlockSpec((t, w), lambda b, g, i: (b * nq + i, g)),
                  pl.BlockSpec((seq, w), lambda b, g, i: (b, groups + g)),
                  pl.BlockSpec((seq, w), lambda b, g, i: (b, 2 * groups + g)),
                  pl.BlockSpec((hs, 2, t, t), lambda b, g, i: (g, 0, 0, 0)),
                  pl.BlockSpec((hs, 1, t), lambda b, g, i: (g, 0, 0))],
        out_specs=pl.BlockSpec((t, w), lambda b, g, i: (b * nq + i, g)),
        out_shape=jax.ShapeDtypeStruct((batch * seq, MOBA_WIDTH), BF16),
        scratch_shapes=[pltpu.VMEM((hs, nq, HEAD_DIM), F32),
                        pltpu.VMEM((hs, nq + 2, t, t), F32),
                        pltpu.VMEM((hs, t, HEAD_DIM), F32),
                        pltpu.VMEM((hs, t, LANES), F32),
                        pltpu.VMEM((hs, t, LANES), F32)],
        compiler_params=_params(3),
        name="moba_attention",
    )(y, y, y, near, far)


def _diff_body(q_ref, k_ref, v_ref, near_ref, far_ref, lq1_ref, lk1_ref, lq2_ref, lk2_ref, subln_ref, o_ref,
               s_scr, acc_scr, lsum_scr, mpast_scr):
    qi = pl.program_id(2)
    dh = HEAD_DIM
    w = 2 * dh
    heads = DIFF_HEADS_PER_STEP

    def stream(hh, j):
        qk_cols = slice(hh * w + j * dh, hh * w + (j + 1) * dh)
        v_cols = slice(hh * w, (hh + 1) * w)
        return AttnStream(q_ref[:, qk_cols], lambda n, nb: k_ref[_block_rows(n, nb), qk_cols],
                          lambda n, nb: v_ref[_block_rows(n, nb), v_cols], lambda i: near_ref[hh, i],
                          far_ref[hh][:, :1], None, s_scr.at[2 * hh + j],
                          acc_scr.at[2 * hh + j], lsum_scr.at[2 * hh + j], mpast_scr.at[2 * hh + j])

    results = _causal_attention([stream(hh, j) for hh in range(heads) for j in range(2)], qi)
    lam = (jnp.exp(jnp.sum(lq1_ref[...] * lk1_ref[...], axis=1, keepdims=True))
           - jnp.exp(jnp.sum(lq2_ref[...] * lk2_ref[...], axis=1, keepdims=True)) + LAMBDA_INIT)
    for hh in range(heads):
        (acc1, l1), (acc2, l2) = results[2 * hh], results[2 * hh + 1]
        o = acc1 / l1 - lam * (acc2 / l2)
        o_ref[:, hh * w:(hh + 1) * w] = (_rms(o, subln_ref[...]) * (1.0 - LAMBDA_INIT)).astype(o_ref.dtype)


def diff_attention(y, near, far, lq1, lk1, lq2, lk2, subln, batch, seq, col0):
    t = ATTN_TILE
    nq = seq // t
    hs = DIFF_HEADS_PER_STEP
    groups = DIFF_HEADS // hs
    w = hs * 2 * HEAD_DIM
    base = col0 // w
    vec = lambda a: a.reshape(1, -1).astype(F32)
    small = lambda n: pl.BlockSpec((1, n), lambda b, g, i: (0, 0))
    return pl.pallas_call(
        _diff_body,
        grid=(batch, groups, nq),
        in_specs=[pl.BlockSpec((t, w), lambda b, g, i: (b * nq + i, base + g)),
                  pl.BlockSpec((seq, w), lambda b, g, i: (b, base + groups + g)),
                  pl.BlockSpec((seq, w), lambda b, g, i: (b, base + 2 * groups + g)),
                  pl.BlockSpec((hs, 2, t, t), lambda b, g, i: (g, 0, 0, 0)),
                  pl.BlockSpec((hs, 1, t), lambda b, g, i: (g, 0, 0)),
                  small(HEAD_DIM), small(HEAD_DIM), small(HEAD_DIM), small(HEAD_DIM), small(2 * HEAD_DIM)],
        out_specs=pl.BlockSpec((t, w), lambda b, g, i: (b * nq + i, g)),
        out_shape=jax.ShapeDtypeStruct((batch * seq, DIFF_WIDTH), BF16),
        scratch_shapes=[pltpu.VMEM((2 * hs, nq + 2, t, t), F32),
                        pltpu.VMEM((2 * hs, t, 2 * HEAD_DIM), F32),
                        pltpu.VMEM((2 * hs, t, LANES), F32),
                        pltpu.VMEM((2 * hs, t, LANES), F32)],
        compiler_params=_params(3),
        name="diff_attention",
    )(y, y, y, near, far, vec(lq1), vec(lk1), vec(lq2), vec(lk2), vec(subln))


def _mixout_body(om_ref, od_ref, ga_ref, gb_ref, wm_ref, wd_ref, wo_ref, x_ref, g_ref, x1_ref, h_ref):
    a = jnp.dot(om_ref[...], wm_ref[...], preferred_element_type=F32)
    b = jnp.dot(od_ref[...], wd_ref[...], preferred_element_type=F32)
    merged = (_sigmoid(ga_ref[...].astype(F32)) * a + _sigmoid(gb_ref[...].astype(F32)) * b).astype(BF16)
    x1 = x_ref[...] + jnp.dot(merged, wo_ref[...], preferred_element_type=F32)
    x1_ref[...] = x1
    h_ref[...] = _rms(x1, g_ref[...]).astype(h_ref.dtype)


def merge_mixout(o_moba, o_diff, y, gate_col0, w_m, w_d, w_o, x, gain, tm=256):
    m, k = o_moba.shape
    n = w_m.shape[1]
    g0 = gate_col0 // n
    rows = lambda cols, j=0: pl.BlockSpec((tm, cols), lambda i: (i, j))
    const = lambda shape: pl.BlockSpec(shape, lambda i: (0, 0))
    return pl.pallas_call(
        _mixout_body,
        grid=(m // tm,),
        in_specs=[rows(k), rows(k), rows(n, g0), rows(n, g0 + 1),
                  const((k, n)), const((k, n)), const((n, n)), rows(n), const((1, n))],
        out_specs=[rows(n), rows(n)],
        out_shape=[jax.ShapeDtypeStruct((m, n), F32), jax.ShapeDtypeStruct((m, n), BF16)],
        compiler_params=_params(1),
        name="merge_mixout",
    )(o_moba, o_diff, y, y, w_m, w_d, w_o, x, gain.reshape(1, n))


def _memkv_body(mem_ref, g_ref, wk_ref, wv_ref, k_ref, v_ref):
    mn = _rms(mem_ref[...], g_ref[...]).astype(BF16)
    k_ref[...] = jnp.dot(mn, wk_ref[...], preferred_element_type=F32).astype(k_ref.dtype)
    v_ref[...] = jnp.dot(mn, wv_ref[...], preferred_element_type=F32).astype(v_ref.dtype)


def memory_kv(mem2d, gain, w_k, w_v, rows):
    m, d = mem2d.shape
    n = w_k.shape[1]
    return pl.pallas_call(
        _memkv_body,
        grid=(m // rows,),
        in_specs=[pl.BlockSpec((rows, d), lambda i: (i, 0)),
                  pl.BlockSpec((1, d), lambda i: (0, 0)),
                  pl.BlockSpec((d, n), lambda i: (0, 0)),
                  pl.BlockSpec((d, n), lambda i: (0, 0))],
        out_specs=[pl.BlockSpec((rows, n), lambda i: (i, 0)),
                   pl.BlockSpec((rows, n), lambda i: (i, 0))],
        out_shape=[jax.ShapeDtypeStruct((m, n), BF16), jax.ShapeDtypeStruct((m, n), BF16)],
        compiler_params=_params(1),
        name="memory_kv",
    )(mem2d, gain.reshape(1, d), w_k, w_v)


def _xattn_body(h_ref, x1_ref, k_ref, v_ref, wq_ref, wo_ref, g_ref, wr_ref, br_ref,
                x2_ref, h2_ref, idx_ref, wgt_ref):
    q = jnp.dot(h_ref[...], wq_ref[...], preferred_element_type=F32).astype(BF16)
    outs = []
    for hh in range(XATTN_HEADS):
        sl = slice(hh * HEAD_DIM, (hh + 1) * HEAD_DIM)
        s = _dot_nt(q[:, sl], k_ref[:, sl]) * ATTN_SCALE
        p = jnp.exp(s - jnp.max(s, axis=1, keepdims=True))
        o = jnp.dot(p.astype(BF16), v_ref[:, sl], preferred_element_type=F32)
        outs.append((o / jnp.sum(p, axis=1, keepdims=True)).astype(BF16))
    o = jnp.concatenate(outs, axis=1)
    x2 = x1_ref[...] + jnp.dot(o, wo_ref[...], preferred_element_type=F32)
    x2_ref[...] = x2
    h2 = _rms(x2, g_ref[...])
    h2_ref[...] = h2

    h_hi = h2.astype(BF16)
    h_lo = (h2 - h_hi.astype(F32)).astype(BF16)
    w_hi = wr_ref[...].astype(BF16)
    w_lo = (wr_ref[...] - w_hi.astype(F32)).astype(BF16)
    logits = (jnp.dot(h_hi, w_hi, preferred_element_type=F32) + jnp.dot(h_lo, w_hi, preferred_element_type=F32)
              + jnp.dot(h_hi, w_lo, preferred_element_type=F32) + br_ref[...])
    lane = lax.broadcasted_iota(jnp.int32, logits.shape, 1)
    out_lane = lax.broadcasted_iota(jnp.int32, idx_ref.shape, 1)
    idx_out = jnp.zeros(idx_ref.shape, jnp.int32)
    exp_out = jnp.zeros(wgt_ref.shape, F32)
    denom = jnp.zeros((logits.shape[0], 1), F32)
    top0 = None
    for kk in range(TOP_K):
        top = jnp.max(logits, axis=1, keepdims=True)
        arg = jnp.min(jnp.where(logits == top, lane, N_EXPERTS), axis=1, keepdims=True)
        logits = jnp.where(lane == arg, NEG_INF, logits)
        top0 = top if top0 is None else top0
        e = jnp.exp(top - top0)
        denom = denom + e
        idx_out = jnp.where(out_lane == kk, arg, idx_out)
        exp_out = jnp.where(out_lane == kk, e, exp_out)
    idx_ref[...] = idx_out
    wgt_ref[...] = exp_out / denom


def cross_attention_router(hx, x1, k_mem, v_mem, w_q, w_o, gain, w_router, b_router, seq, tm=512):
    m, d = hx.shape
    mem_len = k_mem.shape[0] // (m // seq)
    n = w_q.shape[1]
    per_b = seq // tm
    const = lambda shape: pl.BlockSpec(shape, lambda i: (0,) * len(shape))
    rows = lambda cols: pl.BlockSpec((tm, cols), lambda i: (i, 0))
    return pl.pallas_call(
        _xattn_body,
        grid=(m // tm,),
        in_specs=[rows(d), rows(d),
                  pl.BlockSpec((mem_len, n), lambda i: (i // per_b, 0)),
                  pl.BlockSpec((mem_len, n), lambda i: (i // per_b, 0)),
                  const((d, n)), const((n, d)), const((1, d)), const((d, N_EXPERTS)), const((1, N_EXPERTS))],
        out_specs=[rows(d), rows(d), rows(LANES), rows(LANES)],
        out_shape=[jax.ShapeDtypeStruct((m, d), F32), jax.ShapeDtypeStruct((m, d), F32),
                   jax.ShapeDtypeStruct((m, LANES), jnp.int32), jax.ShapeDtypeStruct((m, LANES), F32)],
        compiler_params=_params(1),
        name="cross_attention_router",
    )(hx, x1, k_mem, v_mem, w_q, w_o, gain.reshape(1, d), w_router.astype(F32), b_router.reshape(1, -1).astype(F32))


def _routing_plan(top_idx, n_tokens):
    rt = EXPERT_ROW_TILE
    tiles_per_unit = EXPERT_UNIT_ROWS // rt
    slot_onehot = (top_idx[:, :, None] == jnp.arange(N_EXPERTS)[None, None, :]).astype(jnp.int32)
    onehot = slot_onehot.sum(axis=1)
    before = jnp.cumsum(onehot, axis=0) - onehot
    count = onehot.sum(axis=0)
    tiles = (count + rt - 1) // rt
    tile_start = jnp.cumsum(tiles) - tiles
    pos = (slot_onehot * (tile_start * rt + before)[:, None, :]).sum(axis=-1)

    units = (tiles + tiles_per_unit - 1) // tiles_per_unit
    unit_first = jnp.cumsum(units) - units
    n_units = units.sum()
    max_units = N_EXPERTS + (n_tokens * TOP_K) // EXPERT_UNIT_ROWS
    uid = jnp.arange(max_units)
    e_of = jnp.clip(jnp.searchsorted(jnp.cumsum(units), uid, side="right"), 0, N_EXPERTS - 1)
    k_in = uid - unit_first[e_of]
    live = uid < n_units
    last_e = e_of[jnp.maximum(n_units - 1, 0)]
    unit_expert = jnp.where(live, e_of, last_e).astype(jnp.int32)
    unit_start = jnp.where(live, (tile_start[e_of] + k_in * tiles_per_unit) * rt, 0).astype(jnp.int32)
    unit_tiles = jnp.where(live, jnp.minimum(tiles[e_of] - k_in * tiles_per_unit, tiles_per_unit), 0).astype(jnp.int32)
    totals = jnp.stack([n_units, tiles.sum()]).astype(jnp.int32)
    pad_plan = jnp.concatenate([tile_start * rt + count, (tile_start + tiles) * rt, tiles.sum()[None]]).astype(jnp.int32)
    return pos.astype(jnp.int32), unit_expert, unit_start, unit_tiles, totals, pad_plan


def _dispatch_body(pos_hbm, pad_ref, h_ref, xs_hbm, pos_smem, zeros, sem):
    i = pl.program_id(0)
    n = pl.num_programs(0)
    groups, sub, _ = h_ref.shape
    per_step = groups * sub * TOP_K
    rt = zeros.shape[0]

    def idx_copy(step):
        half = pl.ds(pl.multiple_of((step % 2) * per_step, per_step), per_step)
        return pltpu.make_async_copy(pos_hbm.at[step], pos_smem.at[half], sem.at[0])

    @pl.when(i == 0)
    def _():
        idx_copy(0).start()
        idx_copy(0).wait()

    @pl.when(i + 1 < n)
    def _():
        idx_copy(i + 1).start()

    base = (i % 2) * per_step

    def send(g, carry):
        for s in range(sub):
            for kk in range(TOP_K):
                r = pos_smem[base + g * (sub * TOP_K) + s * TOP_K + kk]
                pltpu.make_async_copy(h_ref.at[g, pl.ds(s, 1), :], xs_hbm.at[pl.ds(r, 1), :], sem.at[1]).start()
        return carry

    lax.fori_loop(0, groups, send, 0)

    def drain(g, carry):
        pltpu.make_async_copy(h_ref.at[0], xs_hbm.at[pl.ds(0, sub), :], sem.at[1]).wait()
        return carry

    lax.fori_loop(0, groups * TOP_K, drain, 0)

    @pl.when(i + 1 < n)
    def _():
        idx_copy(i + 1).wait()

    @pl.when(i == n - 1)
    def _():
        zeros[...] = jnp.zeros(zeros.shape, zeros.dtype)

        def pad_rows(r, size):
            return pltpu.make_async_copy(zeros.at[pl.ds(0, size), :], xs_hbm.at[pl.ds(r, size), :], sem.at[1])

        def pad_tile(j):
            return pltpu.make_async_copy(zeros, xs_hbm.at[pl.ds(pl.multiple_of(j * rt, rt), rt), :], sem.at[1])

        def each_pad(fn):
            def expert(e, carry):
                first = pad_ref[e]
                end = pad_ref[N_EXPERTS + e]
                aligned = jnp.minimum((first + sub - 1) // sub * sub, end)

                def row(r, carry):
                    fn(pad_rows(r, 1))
                    return carry

                lax.fori_loop(first, aligned, row, 0)
                length = end - aligned
                size = sub
                while size < rt:
                    @pl.when((length & size) != 0)
                    def _(size=size):
                        fn(pad_rows(pl.multiple_of(aligned + (length & (size - 1)), sub), size))
                    size *= 2
                return carry
            lax.fori_loop(0, N_EXPERTS, expert, 0)

            def tile(j, carry):
                fn(pad_tile(j))
                return carry
            lax.fori_loop(pad_ref[2 * N_EXPERTS], xs_hbm.shape[0] // rt, tile, 0)

        each_pad(lambda copy: copy.start())
        each_pad(lambda copy: copy.wait())


def dispatch_rows(h2, pos, pad_plan, p_rows):
    t, d = h2.shape
    tm = GATHER_TOKENS
    sub = 8
    return pl.pallas_call(
        _dispatch_body,
        grid=(t // tm,),
        in_specs=[pl.BlockSpec(memory_space=pl.ANY),
                  pl.BlockSpec(memory_space=pltpu.SMEM),
                  pl.BlockSpec((tm // sub, sub, d), lambda i: (i, 0, 0))],
        out_specs=pl.BlockSpec(memory_space=pl.ANY),
        out_shape=jax.ShapeDtypeStruct((p_rows, d), h2.dtype),
        scratch_shapes=[pltpu.SMEM((2 * tm * TOP_K,), jnp.int32),
                        pltpu.VMEM((EXPERT_ROW_TILE, d), h2.dtype),
                        pltpu.SemaphoreType.DMA((2,))],
        compiler_params=_params(1),
        name="dispatch_rows",
    )(pos.reshape(t // tm, tm * TOP_K), pad_plan, h2.reshape(t // sub, sub, d))


def _expert_body(ue_ref, us_ref, un_ref, nu_ref,
                 x_hbm, wgu_ref, bgu_ref, wd_ref, bd_ref, y_hbm,
                 xbuf, actbuf, gubuf, ystage, wd_f32, sem_x, sem_y, *, n_up, n_down):
    u = pl.program_id(0)
    c = pl.program_id(1)
    rt = EXPERT_ROW_TILE
    half = EXPERT_UP_CHUNK // 2
    quarter = half // 2
    chunk = EXPERT_DOWN_CHUNK
    n_live = nu_ref[0]
    live = u < n_live
    start = us_ref[u]
    n_tiles = un_ref[u]

    n_pairs = n_tiles // 2
    odd = n_tiles % 2 == 1

    def span_rows(j, tiles):
        return pl.ds(pl.multiple_of(j * rt, rt), tiles * rt)

    def tile_rows(j):
        return span_rows(j, 1)

    def x_copy(unit, j):
        rows = pl.ds(pl.multiple_of(us_ref[unit] + j * rt, rt), rt)
        return pltpu.make_async_copy(x_hbm.at[rows, :], xbuf.at[tile_rows(j), :], sem_x.at[0])

    def fetch_rows(unit):
        def body(j, carry):
            x_copy(unit, j).start()
            return carry
        lax.fori_loop(0, un_ref[unit], body, 0)

    @pl.when((u == 0) & (c == 0) & live)
    def _():
        fetch_rows(0)

    @pl.when(live & (c == 0))
    def _():
        def body(j, carry):
            x_copy(u, j).wait()
            return carry
        lax.fori_loop(0, n_tiles, body, 0)

    @pl.when((c == n_up) & (u + 1 < n_live))
    def _():
        fetch_rows(u + 1)

    @pl.when(live & (c < n_up))
    def _():
        bias = bgu_ref[ue_ref[u], pl.ds(c, 1), :]

        n_spans = n_pairs + n_tiles % 2
        first_slot = (c * n_spans) % 2

        def slot_of(i):
            return (first_slot + i) % 2

        def project(j, tiles, slot):
            x = xbuf[span_rows(j, tiles), :].astype(BF16)
            gubuf[slot, :tiles * rt, :] = jnp.dot(x, wgu_ref[0].astype(BF16), preferred_element_type=F32) + bias

        def activate(group, j, tiles, slot):
            gu = gubuf[slot, :tiles * rt, :]
            even = (lax.broadcasted_iota(jnp.int32, (tiles * rt, half), 1) % 2) == 0
            lo = gu[:, :half]
            hi = gu[:, half:]
            gate = jnp.where(even, lo, pltpu.roll(hi, 1, axis=1))
            up = jnp.where(even, pltpu.roll(lo, half - 1, axis=1), hi)
            gate = jnp.minimum(gate, SWIGLU_LIMIT)
            up = jnp.clip(up, -SWIGLU_LIMIT, SWIGLU_LIMIT)
            act = (up + 1.0) * gate * _sigmoid(SWIGLU_ALPHA * gate)
            actbuf[group, span_rows(j, tiles), :] = act.astype(BF16)

        has_pairs = n_pairs >= 1
        shapes = [
            (has_pairs & odd, 2, 2 * n_pairs, 1),
            (has_pairs & jnp.logical_not(odd), 2, 2 * (n_pairs - 1), 2),
            (jnp.logical_not(has_pairs), 1, 0, 1),
        ]
        for case, first_tiles, last_j, last_tiles in shapes:
            @pl.when(case & (c == 0))
            def _(first_tiles=first_tiles):
                project(0, first_tiles, slot_of(0))

            @pl.when(case & (c > 0))
            def _(first_tiles=first_tiles, last_j=last_j, last_tiles=last_tiles):
                activate(c - 1, last_j, last_tiles, slot_of(1))
                project(0, first_tiles, slot_of(0))

        @pl.when(has_pairs)
        def _():
            def body(i, carry):
                activate(c, 2 * (i - 1), 2, slot_of(i - 1))
                project(2 * i, 2, slot_of(i))
                return carry

            lax.fori_loop(1, n_pairs, body, 0)

        @pl.when(has_pairs & odd)
        def _():
            activate(c, 2 * (n_pairs - 1), 2, slot_of(n_pairs - 1))
            project(2 * n_pairs, 1, slot_of(n_pairs))

        for case, first_tiles, last_j, last_tiles in shapes:
            @pl.when(case & (c == n_up - 1))
            def _(last_j=last_j, last_tiles=last_tiles):
                activate(c, last_j, last_tiles, slot_of(n_spans - 1))

    @pl.when(live & (c >= n_up))
    def _():
        cd = c - n_up
        lane_groups = chunk // LANES
        early_groups = lane_groups // 2

        def interleave(groups):
            for g in groups:
                lanes = slice(g * LANES, (g + 1) * LANES)
                for f in range(actbuf.shape[0]):
                    base = f * half
                    wd_f32[g, pl.ds(base, quarter, stride=2), :] = wd_ref[0, base:base + quarter, lanes]
                    wd_f32[g, pl.ds(base + 1, quarter, stride=2), :] = wd_ref[0, base + quarter:base + half, lanes]

        def weights(groups):
            return jnp.concatenate([wd_f32[g].astype(BF16) for g in groups], axis=1)

        interleave(range(early_groups))
        bias = bd_ref[ue_ref[u], pl.ds(cd, 1), :]

        def y_copy(j, tiles, slot):
            rows = pl.ds(pl.multiple_of(start + j * rt, rt), tiles * rt)
            cols = pl.ds(pl.multiple_of(cd * chunk, chunk), chunk)
            return pltpu.make_async_copy(ystage.at[slot, :tiles * rt, :], y_hbm.at[rows, cols], sem_y.at[slot])

        def emit(j, tiles, slot, first=False):
            act = jnp.concatenate([actbuf[f, span_rows(j, tiles), :] for f in range(actbuf.shape[0])], axis=1)
            if first:
                y_early = jnp.dot(act, weights(range(early_groups)), preferred_element_type=F32)
                interleave(range(early_groups, lane_groups))
                y_late = jnp.dot(act, weights(range(early_groups, lane_groups)), preferred_element_type=F32)
                y = jnp.concatenate([y_early, y_late], axis=1)
            else:
                y = jnp.dot(act, weights(range(lane_groups)), preferred_element_type=F32)
            ystage[slot, :tiles * rt, :] = y + bias
            y_copy(j, tiles, slot).start()

        @pl.when(n_pairs >= 1)
        def _():
            emit(0, 2, 0, first=True)

        def pair(i, carry):
            @pl.when(i >= 2)
            def _():
                y_copy(2 * (i - 2), 2, i % 2).wait()

            emit(2 * i, 2, i % 2)
            return carry

        lax.fori_loop(1, n_pairs, pair, 0)

        @pl.when(n_pairs >= 2)
        def _():
            y_copy(2 * (n_pairs - 2), 2, n_pairs % 2).wait()

        @pl.when(odd & (n_pairs >= 1))
        def _():
            emit(2 * n_pairs, 1, n_pairs % 2)

        @pl.when(n_pairs == 0)
        def _():
            emit(0, 1, 0, first=True)

        @pl.when(n_pairs >= 1)
        def _():
            y_copy(2 * (n_pairs - 1), 2, (n_pairs - 1) % 2).wait()

        @pl.when(odd)
        def _():
            y_copy(2 * n_pairs, 1, n_pairs % 2).wait()

    @pl.when((u == pl.num_programs(0) - 1) & (c == n_up + n_down - 1))
    def _():
        zero_rows = xbuf.at[:rt, :]
        zero_rows[...] = jnp.zeros(zero_rows.shape, F32)
        used_tiles = nu_ref[1]

        def pad_copy(j):
            return pltpu.make_async_copy(zero_rows, y_hbm.at[pl.ds(pl.multiple_of(j * rt, rt), rt), :], sem_x.at[0])

        def pad_start(j, carry):
            pad_copy(j).start()
            return carry

        def pad_wait(j, carry):
            pad_copy(j).wait()
            return carry

        lax.fori_loop(used_tiles, y_hbm.shape[0] // rt, pad_start, 0)
        lax.fori_loop(used_tiles, y_hbm.shape[0] // rt, pad_wait, 0)


def expert_ffn(x_sorted, w_gate_up, b_gate_up, w_down, b_down, unit_expert, unit_start, unit_tiles, totals):
    p_rows, d = x_sorted.shape
    n_exp, _, two_ff = w_gate_up.shape
    d_ff = two_ff // 2
    up_chunk = EXPERT_UP_CHUNK
    chunk = EXPERT_DOWN_CHUNK
    n_up = two_ff // up_chunk
    n_down = d // chunk
    n_steps = n_up + n_down
    max_units = unit_expert.shape[0]
    rt = EXPERT_ROW_TILE

    def up_idx(u, c, nu):
        return jnp.where(u < nu[0], jnp.minimum(c, n_up - 1), n_up - 1)

    def down_idx(u, c, nu):
        return jnp.where(u < nu[0], jnp.maximum(c - n_up, 0), n_down - 1)

    grid_spec = pltpu.PrefetchScalarGridSpec(
        num_scalar_prefetch=4,
        grid=(totals[0], n_steps),
        in_specs=[pl.BlockSpec(memory_space=pl.ANY),
                  pl.BlockSpec((1, d, up_chunk), lambda u, c, ue, us, un, nu: (ue[u], 0, up_idx(u, c, nu))),
                  pl.BlockSpec((n_exp, n_up, up_chunk), lambda u, c, ue, us, un, nu: (0, 0, 0)),
                  pl.BlockSpec((1, d_ff, chunk), lambda u, c, ue, us, un, nu: (ue[u], 0, down_idx(u, c, nu))),
                  pl.BlockSpec((n_exp, n_down, chunk), lambda u, c, ue, us, un, nu: (0, 0, 0))],
        out_specs=pl.BlockSpec(memory_space=pl.ANY),
        scratch_shapes=[pltpu.VMEM((EXPERT_UNIT_ROWS, d), F32),
                        pltpu.VMEM((two_ff // EXPERT_UP_CHUNK, EXPERT_UNIT_ROWS, EXPERT_UP_CHUNK // 2), BF16),
                        pltpu.VMEM((2, 2 * rt, EXPERT_UP_CHUNK), F32),
                        pltpu.VMEM((2, 2 * rt, chunk), F32),
                        pltpu.VMEM((chunk // LANES, d_ff, LANES), F32),
                        pltpu.SemaphoreType.DMA((1,)),
                        pltpu.SemaphoreType.DMA((2,))],
    )
    return pl.pallas_call(
        functools.partial(_expert_body, n_up=n_up, n_down=n_down),
        grid_spec=grid_spec,
        out_shape=jax.ShapeDtypeStruct((p_rows, d), F32),
        compiler_params=_params(2),
        name="expert_ffn",
    )(unit_expert, unit_start, unit_tiles, totals,
      x_sorted, w_gate_up, b_gate_up.reshape(n_exp, n_up, up_chunk), w_down,
      b_down.reshape(n_exp, n_down, chunk))


def _combine_body(pos_hbm, x2_ref, w_ref, g_ref, y_hbm, o_ref, pos_smem, ybuf, sem, *, final_norm):
    i = pl.program_id(0)
    n = pl.num_programs(0)
    groups, sub, _ = x2_ref.shape
    slot = i % 2
    per_step = groups * sub * TOP_K

    def idx_copy(step):
        half = pl.ds(pl.multiple_of((step % 2) * per_step, per_step), per_step)
        return pltpu.make_async_copy(pos_hbm.at[step], pos_smem.at[half], sem.at[2])

    def fetch_rows(step):
        into = step % 2
        base = into * per_step

        def recv(g, carry):
            for s in range(sub):
                for kk in range(TOP_K):
                    r = pos_smem[base + g * (sub * TOP_K) + s * TOP_K + kk]
                    pltpu.make_async_copy(y_hbm.at[pl.ds(r, 1), :], ybuf.at[into, kk, g, pl.ds(s, 1), :],
                                          sem.at[into]).start()
            return carry

        lax.fori_loop(0, groups, recv, 0)

    @pl.when(i == 0)
    def _():
        idx_copy(0).start()
        idx_copy(0).wait()
        fetch_rows(0)

        @pl.when(n > 1)
        def _():
            idx_copy(1).start()

    @pl.when(i + 1 < n)
    def _():
        idx_copy(i + 1).wait()
        fetch_rows(i + 1)

    @pl.when(i + 2 < n)
    def _():
        idx_copy(i + 2).start()

    def drain(g, carry):
        pltpu.make_async_copy(y_hbm.at[pl.ds(0, sub), :], ybuf.at[slot, 0, 0], sem.at[slot]).wait()
        return carry

    lax.fori_loop(0, groups * TOP_K, drain, 0)

    x3 = x2_ref[...]
    for kk in range(TOP_K):
        x3 = x3 + w_ref[:, :, kk:kk + 1] * ybuf[slot, kk]
    o_ref[...] = _rms(x3, g_ref[...]) if final_norm else x3


def combine(x2, y_sorted, pos, weights, gain):
    m, d = x2.shape
    tm = GATHER_TOKENS
    sub = 8
    final_norm = gain is not None
    gain = gain if final_norm else jnp.ones((d,), F32)
    rows = lambda width: pl.BlockSpec((tm // sub, sub, width), lambda i: (i, 0, 0))
    out = pl.pallas_call(
        functools.partial(_combine_body, final_norm=final_norm),
        grid=(m // tm,),
        in_specs=[pl.BlockSpec(memory_space=pl.ANY),
                  rows(d), rows(LANES),
                  pl.BlockSpec((1, d), lambda i: (0, 0)),
                  pl.BlockSpec(memory_space=pl.ANY)],
        out_specs=rows(d),
        out_shape=jax.ShapeDtypeStruct((m // sub, sub, d), F32),
        scratch_shapes=[pltpu.SMEM((2 * tm * TOP_K,), jnp.int32),
                        pltpu.VMEM((2, TOP_K, tm // sub, sub, d), F32),
                        pltpu.SemaphoreType.DMA((3,))],
        compiler_params=_params(1),
        name="combine",
    )(pos.reshape(m // tm, tm * TOP_K), x2.reshape(m // sub, sub, d), weights.reshape(m // sub, sub, LANES),
      gain.reshape(1, d), y_sorted)
    return out.reshape(m, d)


def kernel(x, mem, rel_bias_table, mix_norm, w_in, diff_lambda_q1, diff_lambda_k1, diff_lambda_q2, diff_lambda_k2, diff_subln, w_branch_moba, w_branch_diff, w_mix_out, xattn_norm, mem_norm, w_xq, w_xk, w_xv, w_xo, ffn_norm, w_router, b_router, w_gate_up, b_gate_up, w_down, b_down, final_norm):
    batch, seq, d = x.shape
    n_tok = batch * seq
    x2d = x.reshape(n_tok, d)
    near, far = _bias_tiles(rel_bias_table)
    diff_col0 = 3 * MOBA_WIDTH
    gate_col0 = diff_col0 + 3 * DIFF_WIDTH
    cols = jnp.arange(w_in.shape[2])
    is_q = (cols < MOBA_WIDTH) | ((cols >= diff_col0) & (cols < diff_col0 + DIFF_WIDTH))
    col_scale = jnp.where(is_q, ATTN_SCALE * LOG2E, 1.0)
    p_rows = n_tok * TOP_K + N_EXPERTS * EXPERT_ROW_TILE
    for l in range(w_in.shape[0]):
        h = rmsnorm_rows(x2d, mix_norm[l])
        y = matmul_colscale(h, w_in[l], col_scale, tm=2048)
        o_moba = moba_attention(y, near[:MOBA_HEADS], far[:MOBA_HEADS], batch, seq)
        o_diff = diff_attention(y, near[MOBA_HEADS:], far[MOBA_HEADS:], diff_lambda_q1[l], diff_lambda_k1[l],
                                diff_lambda_q2[l], diff_lambda_k2[l], diff_subln[l], batch, seq, diff_col0)
        x1, hx = merge_mixout(o_moba, o_diff, y, gate_col0, w_branch_moba[l].astype(BF16),
                              w_branch_diff[l].astype(BF16), w_mix_out[l].astype(BF16), x2d, xattn_norm[l])
        k_mem, v_mem = memory_kv(mem.reshape(-1, d), mem_norm[l], w_xk[l].astype(BF16), w_xv[l].astype(BF16),
                                 mem.shape[1])
        x2, h2, idx_pad, wgt_pad = cross_attention_router(hx, x1, k_mem, v_mem, w_xq[l].astype(BF16),
                                                          w_xo[l].astype(BF16), ffn_norm[l], w_router[l],
                                                          b_router[l], seq)
        pos, unit_expert, unit_start, unit_tiles, totals, pad_plan = _routing_plan(idx_pad[:, :TOP_K], n_tok)
        x_sorted = dispatch_rows(h2, pos, pad_plan, p_rows)
        y_sorted = expert_ffn(x_sorted, w_gate_up[l], b_gate_up[l], w_down[l], b_down[l],
                              unit_expert, unit_start, unit_tiles, totals)
        last = l == w_in.shape[0] - 1
        x2d = combine(x2, y_sorted, pos, wgt_pad, final_norm if last else None)
    return x2d.reshape(batch, seq, d)
```

```python
import collections
import functools
import math

import jax
import jax.numpy as jnp
from jax import lax
from jax.experimental import pallas as pl
from jax.experimental.pallas import tpu as pltpu

F32 = jnp.float32
BF16 = jnp.bfloat16
NEG_INF = float("-inf")

HEAD_DIM = 128
MOBA_HEADS = 8
MOBA_WIDTH = MOBA_HEADS * HEAD_DIM
MOBA_BLOCK = 256
MOBA_TOPK = 3
DIFF_HEADS = 4
DIFF_WIDTH = DIFF_HEADS * 2 * HEAD_DIM
REL_BUCKETS = 32
REL_MAX_DISTANCE = 128
XATTN_HEADS = 4
N_EXPERTS = 32
TOP_K = 4
SWIGLU_LIMIT = 7.0
SWIGLU_ALPHA = 1.702
NORM_EPS = 1e-5
LAMBDA_INIT = 0.8 - 0.6 * math.exp(-0.3 * 0)
ATTN_SCALE = HEAD_DIM ** -0.5
LOG2E = math.log2(math.e)

ATTN_TILE = MOBA_BLOCK
MOBA_HEADS_PER_STEP = 4
DIFF_HEADS_PER_STEP = 2
LANES = 128
SUBLANES = 8
EXPERT_ROW_TILE = 256
EXPERT_UNIT_ROWS = 1536
EXPERT_UP_CHUNK = 512
EXPERT_DOWN_CHUNK = 512
DISPATCH_TOKENS = 1024
GATHER_TOKENS = 256
IN_PROJ_ROWS = 2048
VMEM_LIMIT = 56 * 1024 * 1024


def _params(n_axes):
    return pltpu.CompilerParams(dimension_semantics=("arbitrary",) * n_axes,
                                vmem_limit_bytes=VMEM_LIMIT)


def _rms(x, gain):
    return x * lax.rsqrt(jnp.mean(x * x, axis=-1, keepdims=True) + NORM_EPS) * gain


def _sigmoid(x):
    return 1.0 / (1.0 + jnp.exp(-x))


def _dot_nt(a, b):
    return lax.dot_general(a, b, (((1,), (1,)), ((), ())), preferred_element_type=F32)


def _rmsnorm_body(x_ref, g_ref, o_ref):
    o_ref[...] = _rms(x_ref[...], g_ref[...]).astype(o_ref.dtype)


def rmsnorm_rows(x, gain, tm=512):
    t, d = x.shape
    return pl.pallas_call(
        _rmsnorm_body,
        grid=(t // tm,),
        in_specs=[pl.BlockSpec((tm, d), lambda i: (i, 0)),
                  pl.BlockSpec((1, d), lambda i: (0, 0))],
        out_specs=pl.BlockSpec((tm, d), lambda i: (i, 0)),
        out_shape=jax.ShapeDtypeStruct((t, d), BF16),
        compiler_params=_params(1),
        name="rmsnorm_rows",
    )(x, gain.reshape(1, d))


def _matmul_body(a_ref, w_ref, cs_ref, o_ref, w_bf):
    @pl.when(pl.program_id(1) == 0)
    def _():
        w_bf[...] = w_ref[...].astype(BF16)

    acc = jnp.dot(a_ref[...], w_bf[...], preferred_element_type=F32)
    o_ref[...] = (acc * cs_ref[...]).astype(o_ref.dtype)


def matmul_colscale(a, w, col_scale, tm=1024, tn=1024, out_dtype=BF16):
    m, k = a.shape
    n = w.shape[1]
    tm, tn = min(tm, m), min(tn, n)
    return pl.pallas_call(
        _matmul_body,
        grid=(n // tn, m // tm),
        in_specs=[pl.BlockSpec((tm, k), lambda j, i: (i, 0)),
                  pl.BlockSpec((k, tn), lambda j, i: (0, j)),
                  pl.BlockSpec((1, tn), lambda j, i: (0, j))],
        out_specs=pl.BlockSpec((tm, tn), lambda j, i: (i, j)),
        out_shape=jax.ShapeDtypeStruct((m, n), out_dtype),
        scratch_shapes=[pltpu.VMEM((k, tn), BF16)],
        compiler_params=_params(2),
        name="matmul",
    )(a, w, col_scale.reshape(1, n).astype(F32))


def _rel_bucket(dist):
    n = jnp.maximum(dist, 0)
    max_exact = REL_BUCKETS // 2
    nf = jnp.maximum(n, max_exact).astype(F32)
    large = max_exact + (jnp.log(nf / max_exact) / math.log(REL_MAX_DISTANCE / max_exact)
                         * (REL_BUCKETS - max_exact)).astype(jnp.int32)
    return jnp.where(n < max_exact, n, jnp.minimum(large, REL_BUCKETS - 1))


def _bias_tiles(table):
    t = ATTN_TILE
    r = jnp.arange(t)[:, None]
    c = jnp.arange(t)[None, :]
    dist = jnp.stack([r - c, t + r - c])
    onehot = (_rel_bucket(dist)[..., None] == jnp.arange(REL_BUCKETS)).astype(F32)
    near = jnp.einsum("irck,kh->hirc", onehot, table.astype(F32), precision=lax.Precision.HIGHEST)
    far = jnp.broadcast_to(table[REL_BUCKETS - 1][:, None, None], (table.shape[1], 1, t))
    return near * LOG2E, far.astype(F32) * LOG2E


def _lane_halves(x, op):
    return op(x[:, :LANES], x[:, LANES:])


AttnStream = collections.namedtuple("AttnStream", "q k_rows v_rows near cfar pen_at s_scr acc lsum mpast")


def _causal_attention(streams, qi):
    t = ATTN_TILE
    own_slot = streams[0].s_scr.shape[0] - 2
    prev_slot = own_slot + 1
    prev = jnp.maximum(qi - 1, 0)
    n_far = prev
    n_pairs = (n_far + 1) // 2
    row = lax.broadcasted_iota(jnp.int32, (t, t), 0)
    col = lax.broadcasted_iota(jnp.int32, (t, t), 1)
    has_prev = jnp.where(qi >= 1, 0.0, NEG_INF)

    def masked(st, s, n):
        return s if st.pen_at is None else s + st.pen_at(n)

    def tile_max(m, s):
        return jnp.maximum(m, _lane_halves(s, jnp.maximum))

    def tile_sum(l, p):
        return l + _lane_halves(p, jnp.add)

    m_near = []
    for st in streams:
        s2 = _dot_nt(st.q, jnp.concatenate([st.k_rows(qi, 1), st.k_rows(prev, 1)], axis=0))
        s_own = jnp.where(col <= row, s2[:, :t] + st.near(0), NEG_INF)
        s_prev = masked(st, s2[:, t:] + st.near(1) + has_prev, prev)
        st.s_scr[own_slot] = s_own
        st.s_scr[prev_slot] = s_prev - st.cfar
        m_near.append(tile_max(_lane_halves(s_own, jnp.maximum), s_prev))

    def pair_scores(i, m_far):
        second_is_far = jnp.where(2 * i + 1 < n_far, 0.0, NEG_INF)
        out = []
        for st, m in zip(streams, m_far):
            s = _dot_nt(st.q, st.k_rows(2 * i, 2))
            for half in range(2):
                sh = masked(st, s[:, half * t:(half + 1) * t], 2 * i + half)
                if half == 1:
                    sh = sh + second_is_far
                st.s_scr[2 * i + half] = sh
                m = tile_max(m, sh)
            out.append(m)
        return tuple(out)

    m_far = lax.fori_loop(0, n_pairs, pair_scores, tuple(jnp.full((t, LANES), NEG_INF, F32) for _ in streams))

    for st, mn, mf in zip(streams, m_near, m_far):
        m_row = jnp.max(jnp.maximum(mn, mf + st.cfar), axis=1, keepdims=True)
        mp = m_row - st.cfar
        p_own = jnp.exp2(st.s_scr[own_slot] - m_row)
        p_prev = jnp.exp2(st.s_scr[prev_slot] - mp)
        st.mpast[...] = jnp.broadcast_to(mp, st.mpast.shape)
        st.lsum[...] = tile_sum(_lane_halves(p_own, jnp.add), p_prev)
        st.acc[...] = jnp.dot(jnp.concatenate([p_own, p_prev], axis=1).astype(BF16),
                              jnp.concatenate([st.v_rows(qi, 1), st.v_rows(prev, 1)], axis=0),
                              preferred_element_type=F32)

    def pair_weights(i, carry):
        for st in streams:
            mp = jnp.concatenate([st.mpast[...]] * (2 * t // LANES), axis=1)
            p = jnp.exp2(jnp.concatenate([st.s_scr[2 * i], st.s_scr[2 * i + 1]], axis=1) - mp)
            st.lsum[...] = tile_sum(tile_sum(st.lsum[...], p[:, :t]), p[:, t:])
            st.acc[...] += jnp.dot(p.astype(BF16), st.v_rows(2 * i, 2), preferred_element_type=F32)
        return carry

    lax.fori_loop(0, n_pairs, pair_weights, 0)
    return [(st.acc[...], jnp.sum(st.lsum[...], axis=1, keepdims=True)) for st in streams]


def _block_rows(n, w=1):
    return pl.ds(pl.multiple_of(n * ATTN_TILE, ATTN_TILE), w * ATTN_TILE)


def _moba_body(q_ref, k_ref, v_ref, near_ref, far_ref, o_ref, kmean_ref, s_scr, acc_scr, lsum_scr, mpast_scr, *,
               n_blocks):
    qi = pl.program_id(2)
    t = ATTN_TILE
    dh = HEAD_DIM
    heads = MOBA_HEADS_PER_STEP

    @pl.when(qi == 0)
    def _():
        for j in range(heads):
            for n in range(n_blocks):
                kmean_ref[j, n:n + 1, :] = jnp.mean(k_ref[n * t:(n + 1) * t, j * dh:(j + 1) * dh].astype(F32),
                                                    axis=0, keepdims=True)

    def stream(j):
        cols = slice(j * dh, (j + 1) * dh)
        q = q_ref[:, cols]
        gate = lax.dot_general(kmean_ref[j], q.astype(F32), (((1,), (1,)), ((), ())),
                               precision=lax.Precision.HIGHEST, preferred_element_type=F32)
        valid = lax.broadcasted_iota(jnp.int32, gate.shape, 0) < qi
        g = jnp.where(valid, gate, NEG_INF)
        kth = g
        for _ in range(MOBA_TOPK - 1):
            top = jnp.max(kth, axis=0, keepdims=True)
            kth = jnp.where(kth == top, NEG_INF, kth)
        third = jnp.max(kth, axis=0, keepdims=True)
        pen_t = jnp.where(valid & (g >= third), 0.0, NEG_INF)
        pen_t = jnp.concatenate([pen_t, jnp.full((LANES - n_blocks, t), NEG_INF, F32)], axis=0)
        pen = pen_t.T
        blk = lax.broadcasted_iota(jnp.int32, pen.shape, 1)

        def pen_at(n):
            return jnp.max(jnp.where(blk == n, pen, NEG_INF), axis=1, keepdims=True)

        return AttnStream(q, lambda n, w: k_ref[_block_rows(n, w), cols], lambda n, w: v_ref[_block_rows(n, w), cols],
                          lambda i: near_ref[j, i], far_ref[j][:, :1], pen_at, s_scr.at[j],
                          acc_scr.at[j], lsum_scr.at[j], mpast_scr.at[j])

    results = _causal_attention([stream(j) for j in range(heads)], qi)
    for j, (acc, l) in enumerate(results):
        o_ref[:, j * dh:(j + 1) * dh] = (acc / l).astype(o_ref.dtype)


def moba_attention(y, near, far, batch, seq):
    t = ATTN_TILE
    nq = seq // t
    hs = MOBA_HEADS_PER_STEP
    groups = MOBA_HEADS // hs
    w = hs * HEAD_DIM
    return pl.pallas_call(
        functools.partial(_moba_body, n_blocks=nq),
        grid=(batch, groups, nq),
        in_specs=[pl.BlockSpec((t, w), lambda b, g, i: (b * nq + i, g)),
                  pl.BlockSpec((seq, w), lambda b, g, i: (b, groups + g)),
                  pl.BlockSpec((seq, w), lambda b, g, i: (b, 2 * groups + g)),
                  pl.BlockSpec((hs, 2, t, t), lambda b, g, i: (g, 0, 0, 0)),
                  pl.BlockSpec((hs, 1, t), lambda b, g, i: (g, 0, 0))],
        out_specs=pl.BlockSpec((t, w), lambda b, g, i: (b * nq + i, g)),
        out_shape=jax.ShapeDtypeStruct((batch * seq, MOBA_WIDTH), BF16),
        scratch_shapes=[pltpu.VMEM((hs, nq, HEAD_DIM), F32),
                        pltpu.VMEM((hs, nq + 2, t, t), F32),
                        pltpu.VMEM((hs, t, HEAD_DIM), F32),
                        pltpu.VMEM((hs, t, LANES), F32),
                        pltpu.VMEM((hs, t, LANES), F32)],
        compiler_params=_params(3),
        name="moba_attention",
    )(y, y, y, near, far)


def _diff_body(q_ref, k_ref, v_ref, near_ref, far_ref, lq1_ref, lk1_ref, lq2_ref, lk2_ref, subln_ref, o_ref,
               s_scr, acc_scr, lsum_scr, mpast_scr):
    qi = pl.program_id(2)
    dh = HEAD_DIM
    w = 2 * dh
    heads = DIFF_HEADS_PER_STEP

    def stream(hh, j):
        qk_cols = slice(hh * w + j * dh, hh * w + (j + 1) * dh)
        v_cols = slice(hh * w, (hh + 1) * w)
        return AttnStream(q_ref[:, qk_cols], lambda n, nb: k_ref[_block_rows(n, nb), qk_cols],
                          lambda n, nb: v_ref[_block_rows(n, nb), v_cols], lambda i: near_ref[hh, i],
                          far_ref[hh][:, :1], None, s_scr.at[2 * hh + j],
                          acc_scr.at[2 * hh + j], lsum_scr.at[2 * hh + j], mpast_scr.at[2 * hh + j])

    results = _causal_attention([stream(hh, j) for hh in range(heads) for j in range(2)], qi)
    lam = (jnp.exp(jnp.sum(lq1_ref[...] * lk1_ref[...], axis=1, keepdims=True))
           - jnp.exp(jnp.sum(lq2_ref[...] * lk2_ref[...], axis=1, keepdims=True)) + LAMBDA_INIT)
    for hh in range(heads):
        (acc1, l1), (acc2, l2) = results[2 * hh], results[2 * hh + 1]
        o = acc1 / l1 - lam * (acc2 / l2)
        o_ref[:, hh * w:(hh + 1) * w] = (_rms(o, subln_ref[...]) * (1.0 - LAMBDA_INIT)).astype(o_ref.dtype)


def diff_attention(y, near, far, lq1, lk1, lq2, lk2, subln, batch, seq, col0):
    t = ATTN_TILE
    nq = seq // t
    hs = DIFF_HEADS_PER_STEP
    groups = DIFF_HEADS // hs
    w = hs * 2 * HEAD_DIM
    base = col0 // w
    vec = lambda a: a.reshape(1, -1).astype(F32)
    small = lambda n: pl.BlockSpec((1, n), lambda b, g, i: (0, 0))
    return pl.pallas_call(
        _diff_body,
        grid=(batch, groups, nq),
        in_specs=[pl.BlockSpec((t, w), lambda b, g, i: (b * nq + i, base + g)),
                  pl.BlockSpec((seq, w), lambda b, g, i: (b, base + groups + g)),
                  pl.BlockSpec((seq, w), lambda b, g, i: (b, base + 2 * groups + g)),
                  pl.BlockSpec((hs, 2, t, t), lambda b, g, i: (g, 0, 0, 0)),
                  pl.BlockSpec((hs, 1, t), lambda b, g, i: (g, 0, 0)),
                  small(HEAD_DIM), small(HEAD_DIM), small(HEAD_DIM), small(HEAD_DIM), small(2 * HEAD_DIM)],
        out_specs=pl.BlockSpec((t, w), lambda b, g, i: (b * nq + i, g)),
        out_shape=jax.ShapeDtypeStruct((batch * seq, DIFF_WIDTH), BF16),
        scratch_shapes=[pltpu.VMEM((2 * hs, nq + 2, t, t), F32),
                        pltpu.VMEM((2 * hs, t, 2 * HEAD_DIM), F32),
                        pltpu.VMEM((2 * hs, t, LANES), F32),
                        pltpu.VMEM((2 * hs, t, LANES), F32)],
        compiler_params=_params(3),
        name="diff_attention",
    )(y, y, y, near, far, vec(lq1), vec(lk1), vec(lq2), vec(lk2), vec(subln))


def _mixout_body(om_ref, od_ref, ga_ref, gb_ref, wm_ref, wd_ref, wo_ref, x_ref, g_ref, x1_ref, h_ref):
    a = jnp.dot(om_ref[...], wm_ref[...], preferred_element_type=F32)
    b = jnp.dot(od_ref[...], wd_ref[...], preferred_element_type=F32)
    merged = (_sigmoid(ga_ref[...].astype(F32)) * a + _sigmoid(gb_ref[...].astype(F32)) * b).astype(BF16)
    x1 = x_ref[...] + jnp.dot(merged, wo_ref[...], preferred_element_type=F32)
    x1_ref[...] = x1
    h_ref[...] = _rms(x1, g_ref[...]).astype(h_ref.dtype)


def merge_mixout(o_moba, o_diff, y, gate_col0, w_m, w_d, w_o, x, gain, tm=256):
    m, k = o_moba.shape
    n = w_m.shape[1]
    g0 = gate_col0 // n
    rows = lambda cols, j=0: pl.BlockSpec((tm, cols), lambda i: (i, j))
    const = lambda shape: pl.BlockSpec(shape, lambda i: (0, 0))
    return pl.pallas_call(
        _mixout_body,
        grid=(m // tm,),
        in_specs=[rows(k), rows(k), rows(n, g0), rows(n, g0 + 1),
                  const((k, n)), const((k, n)), const((n, n)), rows(n), const((1, n))],
        out_specs=[rows(n), rows(n)],
        out_shape=[jax.ShapeDtypeStruct((m, n), F32), jax.ShapeDtypeStruct((m, n), BF16)],
        compiler_params=_params(1),
        name="merge_mixout",
    )(o_moba, o_diff, y, y, w_m, w_d, w_o, x, gain.reshape(1, n))


def _memkv_body(mem_ref, g_ref, wk_ref, wv_ref, k_ref, v_ref):
    mn = _rms(mem_ref[...], g_ref[...]).astype(BF16)
    k_ref[...] = jnp.dot(mn, wk_ref[...], preferred_element_type=F32).astype(k_ref.dtype)
    v_ref[...] = jnp.dot(mn, wv_ref[...], preferred_element_type=F32).astype(v_ref.dtype)


def memory_kv(mem2d, gain, w_k, w_v, rows):
    m, d = mem2d.shape
    n = w_k.shape[1]
    return pl.pallas_call(
        _memkv_body,
        grid=(m // rows,),
        in_specs=[pl.BlockSpec((rows, d), lambda i: (i, 0)),
                  pl.BlockSpec((1, d), lambda i: (0, 0)),
                  pl.BlockSpec((d, n), lambda i: (0, 0)),
                  pl.BlockSpec((d, n), lambda i: (0, 0))],
        out_specs=[pl.BlockSpec((rows, n), lambda i: (i, 0)),
                   pl.BlockSpec((rows, n), lambda i: (i, 0))],
        out_shape=[jax.ShapeDtypeStruct((m, n), BF16), jax.ShapeDtypeStruct((m, n), BF16)],
        compiler_params=_params(1),
        name="memory_kv",
    )(mem2d, gain.reshape(1, d), w_k, w_v)


def _xattn_body(h_ref, x1_ref, k_ref, v_ref, wq_ref, wo_ref, g_ref, wr_ref, br_ref,
                x2_ref, h2_ref, idx_ref, wgt_ref):
    q = jnp.dot(h_ref[...], wq_ref[...], preferred_element_type=F32).astype(BF16)
    outs = []
    for hh in range(XATTN_HEADS):
        sl = slice(hh * HEAD_DIM, (hh + 1) * HEAD_DIM)
        s = _dot_nt(q[:, sl], k_ref[:, sl]) * ATTN_SCALE
        p = jnp.exp(s - jnp.max(s, axis=1, keepdims=True))
        o = jnp.dot(p.astype(BF16), v_ref[:, sl], preferred_element_type=F32)
        outs.append((o / jnp.sum(p, axis=1, keepdims=True)).astype(BF16))
    o = jnp.concatenate(outs, axis=1)
    x2 = x1_ref[...] + jnp.dot(o, wo_ref[...], preferred_element_type=F32)
    x2_ref[...] = x2
    h2 = _rms(x2, g_ref[...])
    h2_ref[...] = h2

    h_hi = h2.astype(BF16)
    h_lo = (h2 - h_hi.astype(F32)).astype(BF16)
    w_hi = wr_ref[...].astype(BF16)
    w_lo = (wr_ref[...] - w_hi.astype(F32)).astype(BF16)
    logits = (jnp.dot(h_hi, w_hi, preferred_element_type=F32) + jnp.dot(h_lo, w_hi, preferred_element_type=F32)
              + jnp.dot(h_hi, w_lo, preferred_element_type=F32) + br_ref[...])
    lane = lax.broadcasted_iota(jnp.int32, logits.shape, 1)
    out_lane = lax.broadcasted_iota(jnp.int32, idx_ref.shape, 1)
    idx_out = jnp.zeros(idx_ref.shape, jnp.int32)
    exp_out = jnp.zeros(wgt_ref.shape, F32)
    denom = jnp.zeros((logits.shape[0], 1), F32)
    top0 = None
    for kk in range(TOP_K):
        top = jnp.max(logits, axis=1, keepdims=True)
        arg = jnp.min(jnp.where(logits == top, lane, N_EXPERTS), axis=1, keepdims=True)
        logits = jnp.where(lane == arg, NEG_INF, logits)
        top0 = top if top0 is None else top0
        e = jnp.exp(top - top0)
        denom = denom + e
        idx_out = jnp.where(out_lane == kk, arg, idx_out)
        exp_out = jnp.where(out_lane == kk, e, exp_out)
    idx_ref[...] = idx_out
    wgt_ref[...] = exp_out / denom


def cross_attention_router(hx, x1, k_mem, v_mem, w_q, w_o, gain, w_router, b_router, seq, tm=512):
    m, d = hx.shape
    mem_len = k_mem.shape[0] // (m // seq)
    n = w_q.shape[1]
    per_b = seq // tm
    const = lambda shape: pl.BlockSpec(shape, lambda i: (0,) * len(shape))
    rows = lambda cols: pl.BlockSpec((tm, cols), lambda i: (i, 0))
    return pl.pallas_call(
        _xattn_body,
        grid=(m // tm,),
        in_specs=[rows(d), rows(d),
                  pl.BlockSpec((mem_len, n), lambda i: (i // per_b, 0)),
                  pl.BlockSpec((mem_len, n), lambda i: (i // per_b, 0)),
                  const((d, n)), const((n, d)), const((1, d)), const((d, N_EXPERTS)), const((1, N_EXPERTS))],
        out_specs=[rows(d), rows(d), rows(LANES), rows(LANES)],
        out_shape=[jax.ShapeDtypeStruct((m, d), F32), jax.ShapeDtypeStruct((m, d), F32),
                   jax.ShapeDtypeStruct((m, LANES), jnp.int32), jax.ShapeDtypeStruct((m, LANES), F32)],
        compiler_params=_params(1),
        name="cross_attention_router",
    )(hx, x1, k_mem, v_mem, w_q, w_o, gain.reshape(1, d), w_router.astype(F32), b_router.reshape(1, -1).astype(F32))


def _routing_plan(top_idx, n_tokens):
    rt = EXPERT_ROW_TILE
    tiles_per_unit = EXPERT_UNIT_ROWS // rt
    slot_onehot = (top_idx[:, :, None] == jnp.arange(N_EXPERTS)[None, None, :]).astype(jnp.int32)
    onehot = slot_onehot.sum(axis=1)
    before = jnp.cumsum(onehot, axis=0) - onehot
    count = onehot.sum(axis=0)
    tiles = (count + rt - 1) // rt
    tile_start = jnp.cumsum(tiles) - tiles
    pos = (slot_onehot * (tile_start * rt + before)[:, None, :]).sum(axis=-1)

    units = (tiles + tiles_per_unit - 1) // tiles_per_unit
    unit_first = jnp.cumsum(units) - units
    n_units = units.sum()
    max_units = N_EXPERTS + (n_tokens * TOP_K) // EXPERT_UNIT_ROWS
    uid = jnp.arange(max_units)
    e_of = jnp.clip(jnp.searchsorted(jnp.cumsum(units), uid, side="right"), 0, N_EXPERTS - 1)
    k_in = uid - unit_first[e_of]
    live = uid < n_units
    last_e = e_of[jnp.maximum(n_units - 1, 0)]
    unit_expert = jnp.where(live, e_of, last_e).astype(jnp.int32)
    unit_start = jnp.where(live, (tile_start[e_of] + k_in * tiles_per_unit) * rt, 0).astype(jnp.int32)
    unit_tiles = jnp.where(live, jnp.minimum(tiles[e_of] - k_in * tiles_per_unit, tiles_per_unit), 0).astype(jnp.int32)
    totals = jnp.stack([n_units, tiles.sum()]).astype(jnp.int32)
    pad_plan = jnp.concatenate([tile_start * rt + count, (tile_start + tiles) * rt, tiles.sum()[None]]).astype(jnp.int32)
    return pos.astype(jnp.int32), unit_expert, unit_start, unit_tiles, totals, pad_plan


def _dispatch_body(pos_hbm, pad_ref, h_ref, xs_hbm, pos_smem, zeros, sem):
    i = pl.program_id(0)
    n = pl.num_programs(0)
    groups, sub, _ = h_ref.shape
    per_step = groups * sub * TOP_K
    rt = zeros.shape[0]

    def idx_copy(step):
        half = pl.ds(pl.multiple_of((step % 2) * per_step, per_step), per_step)
        return pltpu.make_async_copy(pos_hbm.at[step], pos_smem.at[half], sem.at[0])

    @pl.when(i == 0)
    def _():
        idx_copy(0).start()
        idx_copy(0).wait()

    @pl.when(i + 1 < n)
    def _():
        idx_copy(i + 1).start()

    base = (i % 2) * per_step

    def send(g, carry):
        for s in range(sub):
            for kk in range(TOP_K):
                r = pos_smem[base + g * (sub * TOP_K) + s * TOP_K + kk]
                pltpu.make_async_copy(h_ref.at[g, pl.ds(s, 1), :], xs_hbm.at[pl.ds(r, 1), :], sem.at[1]).start()
        return carry

    lax.fori_loop(0, groups, send, 0)

    def drain(g, carry):
        pltpu.make_async_copy(h_ref.at[0], xs_hbm.at[pl.ds(0, sub), :], sem.at[1]).wait()
        return carry

    lax.fori_loop(0, groups * TOP_K, drain, 0)

    @pl.when(i + 1 < n)
    def _():
        idx_copy(i + 1).wait()

    @pl.when(i == n - 1)
    def _():
        zeros[...] = jnp.zeros(zeros.shape, zeros.dtype)

        def pad_rows(r, size):
            return pltpu.make_async_copy(zeros.at[pl.ds(0, size), :], xs_hbm.at[pl.ds(r, size), :], sem.at[1])

        def pad_tile(j):
            return pltpu.make_async_copy(zeros, xs_hbm.at[pl.ds(pl.multiple_of(j * rt, rt), rt), :], sem.at[1])

        def each_pad(fn):
            def expert(e, carry):
                first = pad_ref[e]
                end = pad_ref[N_EXPERTS + e]
                aligned = jnp.minimum((first + sub - 1) // sub * sub, end)

                def row(r, carry):
                    fn(pad_rows(r, 1))
                    return carry

                lax.fori_loop(first, aligned, row, 0)
                length = end - aligned
                size = sub
                while size < rt:
                    @pl.when((length & size) != 0)
                    def _(size=size):
                        fn(pad_rows(pl.multiple_of(aligned + (length & (size - 1)), sub), size))
                    size *= 2
                return carry
            lax.fori_loop(0, N_EXPERTS, expert, 0)

            def tile(j, carry):
                fn(pad_tile(j))
                return carry
            lax.fori_loop(pad_ref[2 * N_EXPERTS], xs_hbm.shape[0] // rt, tile, 0)

        each_pad(lambda copy: copy.start())
        each_pad(lambda copy: copy.wait())


def dispatch_rows(h2, pos, pad_plan, p_rows):
    t, d = h2.shape
    tm = DISPATCH_TOKENS
    sub = SUBLANES
    return pl.pallas_call(
        _dispatch_body,
        grid=(t // tm,),
        in_specs=[pl.BlockSpec(memory_space=pl.ANY),
                  pl.BlockSpec(memory_space=pltpu.SMEM),
                  pl.BlockSpec((tm // sub, sub, d), lambda i: (i, 0, 0))],
        out_specs=pl.BlockSpec(memory_space=pl.ANY),
        out_shape=jax.ShapeDtypeStruct((p_rows, d), h2.dtype),
        scratch_shapes=[pltpu.SMEM((2 * tm * TOP_K,), jnp.int32),
                        pltpu.VMEM((EXPERT_ROW_TILE, d), h2.dtype),
                        pltpu.SemaphoreType.DMA((2,))],
        compiler_params=_params(1),
        name="dispatch_rows",
    )(pos.reshape(t // tm, tm * TOP_K), pad_plan, h2.reshape(t // sub, sub, d))


def _expert_body(ue_ref, us_ref, un_ref, nu_ref,
                 x_hbm, wgu_ref, bgu_ref, wd_ref, bd_ref, y_hbm,
                 xbuf, actbuf, gubuf, ystage, wd_f32, sem_x, sem_y, *, n_up, n_down):
    u = pl.program_id(0)
    c = pl.program_id(1)
    rt = EXPERT_ROW_TILE
    half = EXPERT_UP_CHUNK // 2
    quarter = half // 2
    chunk = EXPERT_DOWN_CHUNK
    n_live = pl.num_programs(0)
    start = us_ref[u]
    n_tiles = un_ref[u]

    n_pairs = n_tiles // 2
    odd = n_tiles % 2 == 1

    def span_rows(j, tiles):
        return pl.ds(pl.multiple_of(j * rt, rt), tiles * rt)

    def tile_rows(j):
        return span_rows(j, 1)

    def x_copy(unit, j):
        rows = pl.ds(pl.multiple_of(us_ref[unit] + j * rt, rt), rt)
        return pltpu.make_async_copy(x_hbm.at[rows, :], xbuf.at[tile_rows(j), :], sem_x.at[0])

    def fetch_rows(unit):
        def body(j, carry):
            x_copy(unit, j).start()
            return carry
        lax.fori_loop(0, un_ref[unit], body, 0)

    @pl.when((u == 0) & (c == 0))
    def _():
        fetch_rows(0)

    @pl.when(c == 0)
    def _():
        def body(j, carry):
            x_copy(u, j).wait()
            return carry
        lax.fori_loop(0, n_tiles, body, 0)

    @pl.when((c == n_up) & (u + 1 < n_live))
    def _():
        fetch_rows(u + 1)

    @pl.when(c < n_up)
    def _():
        bias = bgu_ref[ue_ref[u], pl.ds(c, 1), :]

        n_spans = n_pairs + n_tiles % 2
        first_slot = (c * n_spans) % 2

        def slot_of(i):
            return (first_slot + i) % 2

        def project(j, tiles, slot):
            x = xbuf[span_rows(j, tiles), :].astype(BF16)
            gubuf[slot, :tiles * rt, :] = jnp.dot(x, wgu_ref[0].astype(BF16), preferred_element_type=F32) + bias

        def activate(group, j, tiles, slot):
            gu = gubuf[slot, :tiles * rt, :]
            even = (lax.broadcasted_iota(jnp.int32, (tiles * rt, half), 1) % 2) == 0
            lo = gu[:, :half]
            hi = gu[:, half:]
            gate = jnp.where(even, lo, pltpu.roll(hi, 1, axis=1))
            up = jnp.where(even, pltpu.roll(lo, half - 1, axis=1), hi)
            gate = jnp.minimum(gate, SWIGLU_LIMIT)
            up = jnp.clip(up, -SWIGLU_LIMIT, SWIGLU_LIMIT)
            act = (up + 1.0) * gate * _sigmoid(SWIGLU_ALPHA * gate)
            actbuf[group, span_rows(j, tiles), :] = act.astype(BF16)

        has_pairs = n_pairs >= 1
        shapes = [
            (has_pairs & odd, 2, 2 * n_pairs, 1),
            (has_pairs & jnp.logical_not(odd), 2, 2 * (n_pairs - 1), 2),
            (jnp.logical_not(has_pairs), 1, 0, 1),
        ]
        for case, first_tiles, last_j, last_tiles in shapes:
            @pl.when(case & (c == 0))
            def _(first_tiles=first_tiles):
                project(0, first_tiles, slot_of(0))

            @pl.when(case & (c > 0))
            def _(first_tiles=first_tiles, last_j=last_j, last_tiles=last_tiles):
                activate(c - 1, last_j, last_tiles, slot_of(1))
                project(0, first_tiles, slot_of(0))

        @pl.when(has_pairs)
        def _():
            def body(i, carry):
                activate(c, 2 * (i - 1), 2, slot_of(i - 1))
                project(2 * i, 2, slot_of(i))
                return carry

            lax.fori_loop(1, n_pairs, body, 0)

        @pl.when(has_pairs & odd)
        def _():
            activate(c, 2 * (n_pairs - 1), 2, slot_of(n_pairs - 1))
            project(2 * n_pairs, 1, slot_of(n_pairs))

        for case, first_tiles, last_j, last_tiles in shapes:
            @pl.when(case & (c == n_up - 1))
            def _(last_j=last_j, last_tiles=last_tiles):
                activate(c, last_j, last_tiles, slot_of(n_spans - 1))

    @pl.when(c >= n_up)
    def _():
        cd = c - n_up
        for g in range(chunk // LANES):
            lanes = slice(g * LANES, (g + 1) * LANES)
            for f in range(actbuf.shape[0]):
                base = f * half
                wd_f32[g, pl.ds(base, quarter, stride=2), :] = wd_ref[0, base:base + quarter, lanes]
                wd_f32[g, pl.ds(base + 1, quarter, stride=2), :] = wd_ref[0, base + quarter:base + half, lanes]
        bias = bd_ref[ue_ref[u], pl.ds(cd, 1), :]

        def y_copy(j, tiles, slot):
            rows = pl.ds(pl.multiple_of(start + j * rt, rt), tiles * rt)
            cols = pl.ds(pl.multiple_of(cd * chunk, chunk), chunk)
            return pltpu.make_async_copy(ystage.at[slot, :tiles * rt, :], y_hbm.at[rows, cols], sem_y.at[slot])

        def emit(j, tiles, slot):
            act = jnp.concatenate([actbuf[f, span_rows(j, tiles), :] for f in range(actbuf.shape[0])], axis=1)
            w = jnp.concatenate([wd_f32[g].astype(BF16) for g in range(chunk // LANES)], axis=1)
            ystage[slot, :tiles * rt, :] = jnp.dot(act, w, preferred_element_type=F32) + bias
            y_copy(j, tiles, slot).start()

        def pair(i, carry):
            @pl.when(i >= 2)
            def _():
                y_copy(2 * (i - 2), 2, i % 2).wait()

            emit(2 * i, 2, i % 2)
            return carry

        lax.fori_loop(0, n_pairs, pair, 0)

        @pl.when(n_pairs >= 2)
        def _():
            y_copy(2 * (n_pairs - 2), 2, n_pairs % 2).wait()

        @pl.when(odd)
        def _():
            emit(2 * n_pairs, 1, n_pairs % 2)

        @pl.when(n_pairs >= 1)
        def _():
            y_copy(2 * (n_pairs - 1), 2, (n_pairs - 1) % 2).wait()

        @pl.when(odd)
        def _():
            y_copy(2 * n_pairs, 1, n_pairs % 2).wait()

    @pl.when((u == pl.num_programs(0) - 1) & (c == n_up + n_down - 1))
    def _():
        zero_rows = xbuf.at[:rt, :]
        zero_rows[...] = jnp.zeros(zero_rows.shape, F32)
        used_tiles = nu_ref[1]

        def pad_copy(j):
            return pltpu.make_async_copy(zero_rows, y_hbm.at[pl.ds(pl.multiple_of(j * rt, rt), rt), :], sem_x.at[0])

        def pad_start(j, carry):
            pad_copy(j).start()
            return carry

        def pad_wait(j, carry):
            pad_copy(j).wait()
            return carry

        lax.fori_loop(used_tiles, y_hbm.shape[0] // rt, pad_start, 0)
        lax.fori_loop(used_tiles, y_hbm.shape[0] // rt, pad_wait, 0)


def expert_ffn(x_sorted, w_gate_up, b_gate_up, w_down, b_down, unit_expert, unit_start, unit_tiles, totals):
    p_rows, d = x_sorted.shape
    n_exp, _, two_ff = w_gate_up.shape
    d_ff = two_ff // 2
    up_chunk = EXPERT_UP_CHUNK
    chunk = EXPERT_DOWN_CHUNK
    n_up = two_ff // up_chunk
    n_down = d // chunk
    n_steps = n_up + n_down
    rt = EXPERT_ROW_TILE

    def up_idx(u, c, nu):
        return jnp.minimum(c, n_up - 1)

    def down_idx(u, c, nu):
        return jnp.maximum(c - n_up, 0)

    grid_spec = pltpu.PrefetchScalarGridSpec(
        num_scalar_prefetch=4,
        grid=(totals[0], n_steps),
        in_specs=[pl.BlockSpec(memory_space=pl.ANY),
                  pl.BlockSpec((1, d, up_chunk), lambda u, c, ue, us, un, nu: (ue[u], 0, up_idx(u, c, nu))),
                  pl.BlockSpec((n_exp, n_up, up_chunk), lambda u, c, ue, us, un, nu: (0, 0, 0)),
                  pl.BlockSpec((1, d_ff, chunk), lambda u, c, ue, us, un, nu: (ue[u], 0, down_idx(u, c, nu))),
                  pl.BlockSpec((n_exp, n_down, chunk), lambda u, c, ue, us, un, nu: (0, 0, 0))],
        out_specs=pl.BlockSpec(memory_space=pl.ANY),
        scratch_shapes=[pltpu.VMEM((EXPERT_UNIT_ROWS, d), F32),
                        pltpu.VMEM((two_ff // EXPERT_UP_CHUNK, EXPERT_UNIT_ROWS, EXPERT_UP_CHUNK // 2), BF16),
                        pltpu.VMEM((2, 2 * rt, EXPERT_UP_CHUNK), F32),
                        pltpu.VMEM((2, 2 * rt, chunk), F32),
                        pltpu.VMEM((chunk // LANES, d_ff, LANES), F32),
                        pltpu.SemaphoreType.DMA((1,)),
                        pltpu.SemaphoreType.DMA((2,))],
    )
    return pl.pallas_call(
        functools.partial(_expert_body, n_up=n_up, n_down=n_down),
        grid_spec=grid_spec,
        out_shape=jax.ShapeDtypeStruct((p_rows, d), F32),
        compiler_params=_params(2),
        name="expert_ffn",
    )(unit_expert, unit_start, unit_tiles, totals,
      x_sorted, w_gate_up, b_gate_up.reshape(n_exp, n_up, up_chunk), w_down,
      b_down.reshape(n_exp, n_down, chunk))


def _combine_body(pos_hbm, x2_ref, w_ref, g_ref, y_hbm, o_ref, pos_smem, ybuf, sem, *, final_norm):
    i = pl.program_id(0)
    n = pl.num_programs(0)
    groups, sub, _ = x2_ref.shape
    slot = i % 2
    per_step = groups * sub * TOP_K

    def idx_copy(step):
        half = pl.ds(pl.multiple_of((step % 2) * per_step, per_step), per_step)
        return pltpu.make_async_copy(pos_hbm.at[step], pos_smem.at[half], sem.at[2])

    def fetch_rows(step):
        into = step % 2
        base = into * per_step

        def recv(g, carry):
            for s in range(sub):
                for kk in range(TOP_K):
                    r = pos_smem[base + g * (sub * TOP_K) + s * TOP_K + kk]
                    pltpu.make_async_copy(y_hbm.at[pl.ds(r, 1), :], ybuf.at[into, kk, g, pl.ds(s, 1), :],
                                          sem.at[into]).start()
            return carry

        lax.fori_loop(0, groups, recv, 0)

    @pl.when(i == 0)
    def _():
        idx_copy(0).start()
        idx_copy(0).wait()
        fetch_rows(0)

        @pl.when(n > 1)
        def _():
            idx_copy(1).start()

    @pl.when(i + 1 < n)
    def _():
        idx_copy(i + 1).wait()
        fetch_rows(i + 1)

    @pl.when(i + 2 < n)
    def _():
        idx_copy(i + 2).start()

    def drain(g, carry):
        pltpu.make_async_copy(y_hbm.at[pl.ds(0, sub), :], ybuf.at[slot, 0, 0], sem.at[slot]).wait()
        return carry

    lax.fori_loop(0, groups * TOP_K, drain, 0)

    x3 = x2_ref[...]
    for kk in range(TOP_K):
        x3 = x3 + w_ref[:, :, kk:kk + 1] * ybuf[slot, kk]
    o_ref[...] = _rms(x3, g_ref[...]) if final_norm else x3


def combine(x2, y_sorted, pos, weights, gain):
    m, d = x2.shape
    tm = GATHER_TOKENS
    sub = SUBLANES
    final_norm = gain is not None
    gain = gain if final_norm else jnp.ones((d,), F32)
    rows = lambda width: pl.BlockSpec((tm // sub, sub, width), lambda i: (i, 0, 0))
    out = pl.pallas_call(
        functools.partial(_combine_body, final_norm=final_norm),
        grid=(m // tm,),
        in_specs=[pl.BlockSpec(memory_space=pl.ANY),
                  rows(d), rows(LANES),
                  pl.BlockSpec((1, d), lambda i: (0, 0)),
                  pl.BlockSpec(memory_space=pl.ANY)],
        out_specs=rows(d),
        out_shape=jax.ShapeDtypeStruct((m // sub, sub, d), F32),
        scratch_shapes=[pltpu.SMEM((2 * tm * TOP_K,), jnp.int32),
                        pltpu.VMEM((2, TOP_K, tm // sub, sub, d), F32),
                        pltpu.SemaphoreType.DMA((3,))],
        compiler_params=_params(1),
        name="combine",
    )(pos.reshape(m // tm, tm * TOP_K), x2.reshape(m // sub, sub, d), weights.reshape(m // sub, sub, LANES),
      gain.reshape(1, d), y_sorted)
    return out.reshape(m, d)


def kernel(x, mem, rel_bias_table, mix_norm, w_in, diff_lambda_q1, diff_lambda_k1, diff_lambda_q2, diff_lambda_k2, diff_subln, w_branch_moba, w_branch_diff, w_mix_out, xattn_norm, mem_norm, w_xq, w_xk, w_xv, w_xo, ffn_norm, w_router, b_router, w_gate_up, b_gate_up, w_down, b_down, final_norm):
    batch, seq, d = x.shape
    n_tok = batch * seq
    assert ATTN_TILE == 2 * LANES and ATTN_TILE > REL_MAX_DISTANCE and seq % ATTN_TILE == 0
    assert seq // ATTN_TILE <= LANES and d == 2 * MOBA_WIDTH == 2 * DIFF_WIDTH
    assert n_tok % IN_PROJ_ROWS == 0 and IN_PROJ_ROWS % DISPATCH_TOKENS == 0 and DISPATCH_TOKENS % GATHER_TOKENS == 0
    assert d % EXPERT_DOWN_CHUNK == 0
    assert w_gate_up.shape[1:] == (N_EXPERTS, d, 2 * w_down.shape[2]) and w_gate_up.shape[3] % EXPERT_UP_CHUNK == 0
    x2d = x.reshape(n_tok, d)
    near, far = _bias_tiles(rel_bias_table)
    diff_col0 = 3 * MOBA_WIDTH
    gate_col0 = diff_col0 + 3 * DIFF_WIDTH
    cols = jnp.arange(w_in.shape[2])
    is_q = (cols < MOBA_WIDTH) | ((cols >= diff_col0) & (cols < diff_col0 + DIFF_WIDTH))
    col_scale = jnp.where(is_q, ATTN_SCALE * LOG2E, 1.0)
    p_rows = n_tok * TOP_K + N_EXPERTS * EXPERT_ROW_TILE
    for l in range(w_in.shape[0]):
        h = rmsnorm_rows(x2d, mix_norm[l])
        y = matmul_colscale(h, w_in[l], col_scale, tm=IN_PROJ_ROWS)
        o_moba = moba_attention(y, near[:MOBA_HEADS], far[:MOBA_HEADS], batch, seq)
        o_diff = diff_attention(y, near[MOBA_HEADS:], far[MOBA_HEADS:], diff_lambda_q1[l], diff_lambda_k1[l],
                                diff_lambda_q2[l], diff_lambda_k2[l], diff_subln[l], batch, seq, diff_col0)
        x1, hx = merge_mixout(o_moba, o_diff, y, gate_col0, w_branch_moba[l].astype(BF16),
                              w_branch_diff[l].astype(BF16), w_mix_out[l].astype(BF16), x2d, xattn_norm[l])
        k_mem, v_mem = memory_kv(mem.reshape(-1, d), mem_norm[l], w_xk[l].astype(BF16), w_xv[l].astype(BF16),
                                 mem.shape[1])
        x2, h2, idx_pad, wgt_pad = cross_attention_router(hx, x1, k_mem, v_mem, w_xq[l].astype(BF16),
                                                          w_xo[l].astype(BF16), ffn_norm[l], w_router[l],
                                                          b_router[l], seq)
        pos, unit_expert, unit_start, unit_tiles, totals, pad_plan = _routing_plan(idx_pad[:, :TOP_K], n_tok)
        x_sorted = dispatch_rows(h2, pos, pad_plan, p_rows)
        y_sorted = expert_ffn(x_sorted, w_gate_up[l], b_gate_up[l], w_down[l], b_down[l],
                              unit_expert, unit_start, unit_tiles, totals)
        last = l == w_in.shape[0] - 1
        x2d = combine(x2, y_sorted, pos, wgt_pad, final_norm if last else None)
    return x2d.reshape(batch, seq, d)
```

```python
import collections
import functools
import math

import jax
import jax.numpy as jnp
from jax import lax
from jax.experimental import pallas as pl
from jax.experimental.pallas import tpu as pltpu

F32 = jnp.float32
BF16 = jnp.bfloat16
NEG_INF = float("-inf")

HEAD_DIM = 128
MOBA_HEADS = 8
MOBA_WIDTH = MOBA_HEADS * HEAD_DIM
MOBA_BLOCK = 256
MOBA_TOPK = 3
DIFF_HEADS = 4
DIFF_WIDTH = DIFF_HEADS * 2 * HEAD_DIM
REL_BUCKETS = 32
REL_MAX_DISTANCE = 128
XATTN_HEADS = 4
N_EXPERTS = 32
TOP_K = 4
SWIGLU_LIMIT = 7.0
SWIGLU_ALPHA = 1.702
NORM_EPS = 1e-5
LAMBDA_INIT = 0.8 - 0.6 * math.exp(-0.3 * 0)
ATTN_SCALE = HEAD_DIM ** -0.5
LOG2E = math.log2(math.e)

ATTN_TILE = MOBA_BLOCK
MOBA_HEADS_PER_STEP = 4
DIFF_HEADS_PER_STEP = 2
LANES = 128
SUBLANES = 8
EXPERT_ROW_TILE = 256
EXPERT_UNIT_ROWS = 1536
EXPERT_UP_CHUNK = 512
EXPERT_DOWN_CHUNK = 512
GATHER_TOKENS = 256
IN_PROJ_ROWS = 2048
VMEM_LIMIT = 56 * 1024 * 1024


def _params(n_axes):
    return pltpu.CompilerParams(dimension_semantics=("arbitrary",) * n_axes,
                                vmem_limit_bytes=VMEM_LIMIT)


def _rms(x, gain):
    return x * lax.rsqrt(jnp.mean(x * x, axis=-1, keepdims=True) + NORM_EPS) * gain


def _sigmoid(x):
    return 1.0 / (1.0 + jnp.exp(-x))


def _dot_nt(a, b):
    return lax.dot_general(a, b, (((1,), (1,)), ((), ())), preferred_element_type=F32)


def _matmul_body(x_ref, g_ref, w_ref, cs_ref, o_ref, h_bf):
    @pl.when(pl.program_id(1) == 0)
    def _():
        h_bf[...] = _rms(x_ref[...], g_ref[...]).astype(BF16)

    acc = jnp.dot(h_bf[...], w_ref[...].astype(BF16), preferred_element_type=F32)
    o_ref[...] = (acc * cs_ref[...]).astype(o_ref.dtype)


def norm_matmul_colscale(x, gain, w, col_scale, tm=1024, tn=1024, out_dtype=BF16):
    m, k = x.shape
    n = w.shape[1]
    tm, tn = min(tm, m), min(tn, n)
    return pl.pallas_call(
        _matmul_body,
        grid=(m // tm, n // tn),
        in_specs=[pl.BlockSpec((tm, k), lambda i, j: (i, 0)),
                  pl.BlockSpec((1, k), lambda i, j: (0, 0)),
                  pl.BlockSpec((k, tn), lambda i, j: (0, j)),
                  pl.BlockSpec((1, tn), lambda i, j: (0, j))],
        out_specs=pl.BlockSpec((tm, tn), lambda i, j: (i, j)),
        out_shape=jax.ShapeDtypeStruct((m, n), out_dtype),
        scratch_shapes=[pltpu.VMEM((tm, k), BF16)],
        compiler_params=_params(2),
        name="matmul",
    )(x, gain.reshape(1, k), w, col_scale.reshape(1, n).astype(F32))


def _rel_bucket(dist):
    n = jnp.maximum(dist, 0)
    max_exact = REL_BUCKETS // 2
    nf = jnp.maximum(n, max_exact).astype(F32)
    large = max_exact + (jnp.log(nf / max_exact) / math.log(REL_MAX_DISTANCE / max_exact)
                         * (REL_BUCKETS - max_exact)).astype(jnp.int32)
    return jnp.where(n < max_exact, n, jnp.minimum(large, REL_BUCKETS - 1))


def _bias_tiles(table):
    t = ATTN_TILE
    r = jnp.arange(t)[:, None]
    c = jnp.arange(t)[None, :]
    dist = jnp.stack([r - c, t + r - c])
    onehot = (_rel_bucket(dist)[..., None] == jnp.arange(REL_BUCKETS)).astype(F32)
    near = jnp.einsum("irck,kh->hirc", onehot, table.astype(F32), precision=lax.Precision.HIGHEST)
    far = jnp.broadcast_to(table[REL_BUCKETS - 1][:, None, None], (table.shape[1], 1, t))
    return near * LOG2E, far.astype(F32) * LOG2E


def _lane_halves(x, op):
    return op(x[:, :LANES], x[:, LANES:])


AttnStream = collections.namedtuple("AttnStream", "q k_rows v_rows near cfar pen_at s_scr acc lsum mpast")


def _causal_attention(streams, qi):
    t = ATTN_TILE
    own_slot = streams[0].s_scr.shape[0] - 2
    prev_slot = own_slot + 1
    prev = jnp.maximum(qi - 1, 0)
    n_far = prev
    n_pairs = (n_far + 1) // 2
    row = lax.broadcasted_iota(jnp.int32, (t, t), 0)
    col = lax.broadcasted_iota(jnp.int32, (t, t), 1)
    has_prev = jnp.where(qi >= 1, 0.0, NEG_INF)

    def masked(st, s, n):
        return s if st.pen_at is None else s + st.pen_at(n)

    def tile_max(m, s):
        return jnp.maximum(m, _lane_halves(s, jnp.maximum))

    def tile_sum(l, p):
        return l + _lane_halves(p, jnp.add)

    m_near = []
    for st in streams:
        s2 = _dot_nt(st.q, jnp.concatenate([st.k_rows(qi, 1), st.k_rows(prev, 1)], axis=0))
        s_own = jnp.where(col <= row, s2[:, :t] + st.near(0), NEG_INF)
        s_prev = masked(st, s2[:, t:] + st.near(1) + has_prev, prev)
        st.s_scr[own_slot] = s_own
        st.s_scr[prev_slot] = s_prev - st.cfar
        m_near.append(tile_max(_lane_halves(s_own, jnp.maximum), s_prev))

    def pair_scores(i, m_far):
        second_is_far = jnp.where(2 * i + 1 < n_far, 0.0, NEG_INF)
        out = []
        for st, m in zip(streams, m_far):
            s = _dot_nt(st.q, st.k_rows(2 * i, 2))
            for half in range(2):
                sh = masked(st, s[:, half * t:(half + 1) * t], 2 * i + half)
                if half == 1:
                    sh = sh + second_is_far
                st.s_scr[2 * i + half] = sh
                m = tile_max(m, sh)
            out.append(m)
        return tuple(out)

    m_far = lax.fori_loop(0, n_pairs, pair_scores, tuple(jnp.full((t, LANES), NEG_INF, F32) for _ in streams))

    for st, mn, mf in zip(streams, m_near, m_far):
        m_row = jnp.max(jnp.maximum(mn, mf + st.cfar), axis=1, keepdims=True)
        mp = m_row - st.cfar
        p_own = jnp.exp2(st.s_scr[own_slot] - m_row)
        p_prev = jnp.exp2(st.s_scr[prev_slot] - mp)
        st.mpast[...] = jnp.broadcast_to(mp, st.mpast.shape)
        st.lsum[...] = tile_sum(_lane_halves(p_own, jnp.add), p_prev)
        st.acc[...] = jnp.dot(jnp.concatenate([p_own, p_prev], axis=1).astype(BF16),
                              jnp.concatenate([st.v_rows(qi, 1), st.v_rows(prev, 1)], axis=0),
                              preferred_element_type=F32)

    def pair_weights(i, carry):
        for st in streams:
            mp = jnp.concatenate([st.mpast[...]] * (2 * t // LANES), axis=1)
            p = jnp.exp2(jnp.concatenate([st.s_scr[2 * i], st.s_scr[2 * i + 1]], axis=1) - mp)
            st.lsum[...] = tile_sum(tile_sum(st.lsum[...], p[:, :t]), p[:, t:])
            st.acc[...] += jnp.dot(p.astype(BF16), st.v_rows(2 * i, 2), preferred_element_type=F32)
        return carry

    lax.fori_loop(0, n_pairs, pair_weights, 0)
    return [(st.acc[...], jnp.sum(st.lsum[...], axis=1, keepdims=True)) for st in streams]


def _block_rows(n, w=1):
    return pl.ds(pl.multiple_of(n * ATTN_TILE, ATTN_TILE), w * ATTN_TILE)


def _moba_body(q_ref, k_ref, v_ref, near_ref, far_ref, o_ref, kmean_ref, s_scr, acc_scr, lsum_scr, mpast_scr, *,
               n_blocks):
    qi = pl.program_id(2)
    t = ATTN_TILE
    dh = HEAD_DIM
    heads = MOBA_HEADS_PER_STEP

    @pl.when(qi == 0)
    def _():
        for j in range(heads):
            for n in range(n_blocks):
                kmean_ref[j, n:n + 1, :] = jnp.mean(k_ref[n * t:(n + 1) * t, j * dh:(j + 1) * dh].astype(F32),
                                                    axis=0, keepdims=True)

    def stream(j):
        cols = slice(j * dh, (j + 1) * dh)
        q = q_ref[:, cols]
        gate = lax.dot_general(kmean_ref[j], q.astype(F32), (((1,), (1,)), ((), ())),
                               precision=lax.Precision.HIGHEST, preferred_element_type=F32)
        valid = lax.broadcasted_iota(jnp.int32, gate.shape, 0) < qi
        g = jnp.where(valid, gate, NEG_INF)
        kth = g
        for _ in range(MOBA_TOPK - 1):
            top = jnp.max(kth, axis=0, keepdims=True)
            kth = jnp.where(kth == top, NEG_INF, kth)
        third = jnp.max(kth, axis=0, keepdims=True)
        pen_t = jnp.where(valid & (g >= third), 0.0, NEG_INF)
        pen_t = jnp.concatenate([pen_t, jnp.full((LANES - n_blocks, t), NEG_INF, F32)], axis=0)
        pen = pen_t.T
        blk = lax.broadcasted_iota(jnp.int32, pen.shape, 1)

        def pen_at(n):
            return jnp.max(jnp.where(blk == n, pen, NEG_INF), axis=1, keepdims=True)

        return AttnStream(q, lambda n, w: k_ref[_block_rows(n, w), cols], lambda n, w: v_ref[_block_rows(n, w), cols],
                          lambda i: near_ref[j, i], far_ref[j][:, :1], pen_at, s_scr.at[j],
                          acc_scr.at[j], lsum_scr.at[j], mpast_scr.at[j])

    results = _causal_attention([stream(j) for j in range(heads)], qi)
    for j, (acc, l) in enumerate(results):
        o_ref[:, j * dh:(j + 1) * dh] = (acc / l).astype(o_ref.dtype)


def moba_attention(y, near, far, batch, seq):
    t = ATTN_TILE
    nq = seq // t
    hs = MOBA_HEADS_PER_STEP
    groups = MOBA_HEADS // hs
    w = hs * HEAD_DIM
    return pl.pallas_call(
        functools.partial(_moba_body, n_blocks=nq),
        grid=(batch, groups, nq),
        in_specs=[pl.BlockSpec((t, w), lambda b, g, i: (b * nq + i, g)),
                  pl.BlockSpec((seq, w), lambda b, g, i: (b, groups + g)),
                  pl.BlockSpec((seq, w), lambda b, g, i: (b, 2 * groups + g)),
                  pl.BlockSpec((hs, 2, t, t), lambda b, g, i: (g, 0, 0, 0)),
                  pl.BlockSpec((hs, 1, t), lambda b, g, i: (g, 0, 0))],
        out_specs=pl.BlockSpec((t, w), lambda b, g, i: (b * nq + i, g)),
        out_shape=jax.ShapeDtypeStruct((batch * seq, MOBA_WIDTH), BF16),
        scratch_shapes=[pltpu.VMEM((hs, nq, HEAD_DIM), F32),
                        pltpu.VMEM((hs, nq + 2, t, t), F32),
                        pltpu.VMEM((hs, t, HEAD_DIM), F32),
                        pltpu.VMEM((hs, t, LANES), F32),
                        pltpu.VMEM((hs, t, LANES), F32)],
        compiler_params=_params(3),
        name="moba_attention",
    )(y, y, y, near, far)


def _diff_body(q_ref, k_ref, v_ref, near_ref, far_ref, lq1_ref, lk1_ref, lq2_ref, lk2_ref, subln_ref, o_ref,
               s_scr, acc_scr, lsum_scr, mpast_scr):
    qi = pl.program_id(2)
    dh = HEAD_DIM
    w = 2 * dh
    heads = DIFF_HEADS_PER_STEP

    def stream(hh, j):
        qk_cols = slice(hh * w + j * dh, hh * w + (j + 1) * dh)
        v_cols = slice(hh * w, (hh + 1) * w)
        return AttnStream(q_ref[:, qk_cols], lambda n, nb: k_ref[_block_rows(n, nb), qk_cols],
                          lambda n, nb: v_ref[_block_rows(n, nb), v_cols], lambda i: near_ref[hh, i],
                          far_ref[hh][:, :1], None, s_scr.at[2 * hh + j],
                          acc_scr.at[2 * hh + j], lsum_scr.at[2 * hh + j], mpast_scr.at[2 * hh + j])

    results = _causal_attention([stream(hh, j) for hh in range(heads) for j in range(2)], qi)
    lam = (jnp.exp(jnp.sum(lq1_ref[...] * lk1_ref[...], axis=1, keepdims=True))
           - jnp.exp(jnp.sum(lq2_ref[...] * lk2_ref[...], axis=1, keepdims=True)) + LAMBDA_INIT)
    for hh in range(heads):
        (acc1, l1), (acc2, l2) = results[2 * hh], results[2 * hh + 1]
        o = acc1 / l1 - lam * (acc2 / l2)
        o_ref[:, hh * w:(hh + 1) * w] = (_rms(o, subln_ref[...]) * (1.0 - LAMBDA_INIT)).astype(o_ref.dtype)


def diff_attention(y, near, far, lq1, lk1, lq2, lk2, subln, batch, seq, col0):
    t = ATTN_TILE
    nq = seq // t
    hs = DIFF_HEADS_PER_STEP
    groups = DIFF_HEADS // hs
    w = hs * 2 * HEAD_DIM
    base = col0 // w
    vec = lambda a: a.reshape(1, -1).astype(F32)
    small = lambda n: pl.BlockSpec((1, n), lambda b, g, i: (0, 0))
    return pl.pallas_call(
        _diff_body,
        grid=(batch, groups, nq),
        in_specs=[pl.BlockSpec((t, w), lambda b, g, i: (b * nq + i, base + g)),
                  pl.BlockSpec((seq, w), lambda b, g, i: (b, base + groups + g)),
                  pl.BlockSpec((seq, w), lambda b, g, i: (b, base + 2 * groups + g)),
                  pl.BlockSpec((hs, 2, t, t), lambda b, g, i: (g, 0, 0, 0)),
                  pl.BlockSpec((hs, 1, t), lambda b, g, i: (g, 0, 0)),
                  small(HEAD_DIM), small(HEAD_DIM), small(HEAD_DIM), small(HEAD_DIM), small(2 * HEAD_DIM)],
        out_specs=pl.BlockSpec((t, w), lambda b, g, i: (b * nq + i, g)),
        out_shape=jax.ShapeDtypeStruct((batch * seq, DIFF_WIDTH), BF16),
        scratch_shapes=[pltpu.VMEM((2 * hs, nq + 2, t, t), F32),
                        pltpu.VMEM((2 * hs, t, 2 * HEAD_DIM), F32),
                        pltpu.VMEM((2 * hs, t, LANES), F32),
                        pltpu.VMEM((2 * hs, t, LANES), F32)],
        compiler_params=_params(3),
        name="diff_attention",
    )(y, y, y, near, far, vec(lq1), vec(lk1), vec(lq2), vec(lk2), vec(subln))


def _mixout_body(om_ref, od_ref, ga_ref, gb_ref, wm_ref, wd_ref, wo_ref, x_ref, g_ref, x1_ref, h_ref):
    a = jnp.dot(om_ref[...], wm_ref[...], preferred_element_type=F32)
    b = jnp.dot(od_ref[...], wd_ref[...], preferred_element_type=F32)
    merged = (_sigmoid(ga_ref[...].astype(F32)) * a + _sigmoid(gb_ref[...].astype(F32)) * b).astype(BF16)
    x1 = x_ref[...] + jnp.dot(merged, wo_ref[...], preferred_element_type=F32)
    x1_ref[...] = x1
    h_ref[...] = _rms(x1, g_ref[...]).astype(h_ref.dtype)


def merge_mixout(o_moba, o_diff, y, gate_col0, w_m, w_d, w_o, x, gain, tm=256):
    m, k = o_moba.shape
    n = w_m.shape[1]
    g0 = gate_col0 // n
    rows = lambda cols, j=0: pl.BlockSpec((tm, cols), lambda i: (i, j))
    const = lambda shape: pl.BlockSpec(shape, lambda i: (0, 0))
    return pl.pallas_call(
        _mixout_body,
        grid=(m // tm,),
        in_specs=[rows(k), rows(k), rows(n, g0), rows(n, g0 + 1),
                  const((k, n)), const((k, n)), const((n, n)), rows(n), const((1, n))],
        out_specs=[rows(n), rows(n)],
        out_shape=[jax.ShapeDtypeStruct((m, n), F32), jax.ShapeDtypeStruct((m, n), BF16)],
        compiler_params=_params(1),
        name="merge_mixout",
    )(o_moba, o_diff, y, y, w_m, w_d, w_o, x, gain.reshape(1, n))


def _memkv_body(mem_ref, g_ref, wk_ref, wv_ref, k_ref, v_ref):
    mn = _rms(mem_ref[...], g_ref[...]).astype(BF16)
    k_ref[...] = jnp.dot(mn, wk_ref[...], preferred_element_type=F32).astype(k_ref.dtype)
    v_ref[...] = jnp.dot(mn, wv_ref[...], preferred_element_type=F32).astype(v_ref.dtype)


def memory_kv(mem2d, gain, w_k, w_v, rows):
    m, d = mem2d.shape
    n = w_k.shape[1]
    return pl.pallas_call(
        _memkv_body,
        grid=(m // rows,),
        in_specs=[pl.BlockSpec((rows, d), lambda i: (i, 0)),
                  pl.BlockSpec((1, d), lambda i: (0, 0)),
                  pl.BlockSpec((d, n), lambda i: (0, 0)),
                  pl.BlockSpec((d, n), lambda i: (0, 0))],
        out_specs=[pl.BlockSpec((rows, n), lambda i: (i, 0)),
                   pl.BlockSpec((rows, n), lambda i: (i, 0))],
        out_shape=[jax.ShapeDtypeStruct((m, n), BF16), jax.ShapeDtypeStruct((m, n), BF16)],
        compiler_params=_params(1),
        name="memory_kv",
    )(mem2d, gain.reshape(1, d), w_k, w_v)


def _xattn_body(h_ref, x1_ref, k_ref, v_ref, wq_ref, wo_ref, g_ref, wr_ref, br_ref,
                x2_ref, h2_ref, idx_ref, wgt_ref):
    q = jnp.dot(h_ref[...], wq_ref[...], preferred_element_type=F32).astype(BF16)
    outs = []
    for hh in range(XATTN_HEADS):
        sl = slice(hh * HEAD_DIM, (hh + 1) * HEAD_DIM)
        s = _dot_nt(q[:, sl], k_ref[:, sl]) * ATTN_SCALE
        p = jnp.exp(s - jnp.max(s, axis=1, keepdims=True))
        o = jnp.dot(p.astype(BF16), v_ref[:, sl], preferred_element_type=F32)
        outs.append((o / jnp.sum(p, axis=1, keepdims=True)).astype(BF16))
    o = jnp.concatenate(outs, axis=1)
    x2 = x1_ref[...] + jnp.dot(o, wo_ref[...], preferred_element_type=F32)
    x2_ref[...] = x2
    h2 = _rms(x2, g_ref[...])
    h2_ref[...] = h2

    h_hi = h2.astype(BF16)
    h_lo = (h2 - h_hi.astype(F32)).astype(BF16)
    w_hi = wr_ref[...].astype(BF16)
    w_lo = (wr_ref[...] - w_hi.astype(F32)).astype(BF16)
    logits = (jnp.dot(h_hi, w_hi, preferred_element_type=F32) + jnp.dot(h_lo, w_hi, preferred_element_type=F32)
              + jnp.dot(h_hi, w_lo, preferred_element_type=F32) + br_ref[...])
    lane = lax.broadcasted_iota(jnp.int32, logits.shape, 1)
    out_lane = lax.broadcasted_iota(jnp.int32, idx_ref.shape, 1)
    idx_out = jnp.zeros(idx_ref.shape, jnp.int32)
    exp_out = jnp.zeros(wgt_ref.shape, F32)
    denom = jnp.zeros((logits.shape[0], 1), F32)
    top0 = None
    for kk in range(TOP_K):
        top = jnp.max(logits, axis=1, keepdims=True)
        arg = jnp.min(jnp.where(logits == top, lane, N_EXPERTS), axis=1, keepdims=True)
        logits = jnp.where(lane == arg, NEG_INF, logits)
        top0 = top if top0 is None else top0
        e = jnp.exp(top - top0)
        denom = denom + e
        idx_out = jnp.where(out_lane == kk, arg, idx_out)
        exp_out = jnp.where(out_lane == kk, e, exp_out)
    idx_ref[...] = idx_out
    wgt_ref[...] = exp_out / denom


def cross_attention_router(hx, x1, k_mem, v_mem, w_q, w_o, gain, w_router, b_router, seq, tm=512):
    m, d = hx.shape
    mem_len = k_mem.shape[0] // (m // seq)
    n = w_q.shape[1]
    per_b = seq // tm
    const = lambda shape: pl.BlockSpec(shape, lambda i: (0,) * len(shape))
    rows = lambda cols: pl.BlockSpec((tm, cols), lambda i: (i, 0))
    return pl.pallas_call(
        _xattn_body,
        grid=(m // tm,),
        in_specs=[rows(d), rows(d),
                  pl.BlockSpec((mem_len, n), lambda i: (i // per_b, 0)),
                  pl.BlockSpec((mem_len, n), lambda i: (i // per_b, 0)),
                  const((d, n)), const((n, d)), const((1, d)), const((d, N_EXPERTS)), const((1, N_EXPERTS))],
        out_specs=[rows(d), rows(d), rows(LANES), rows(LANES)],
        out_shape=[jax.ShapeDtypeStruct((m, d), F32), jax.ShapeDtypeStruct((m, d), F32),
                   jax.ShapeDtypeStruct((m, LANES), jnp.int32), jax.ShapeDtypeStruct((m, LANES), F32)],
        compiler_params=_params(1),
        name="cross_attention_router",
    )(hx, x1, k_mem, v_mem, w_q, w_o, gain.reshape(1, d), w_router.astype(F32), b_router.reshape(1, -1).astype(F32))


def _routing_plan(top_idx, n_tokens):
    rt = EXPERT_ROW_TILE
    tiles_per_unit = EXPERT_UNIT_ROWS // rt
    slot_onehot = (top_idx[:, :, None] == jnp.arange(N_EXPERTS)[None, None, :]).astype(jnp.int32)
    onehot = slot_onehot.sum(axis=1)
    before = jnp.cumsum(onehot, axis=0) - onehot
    count = onehot.sum(axis=0)
    tiles = (count + rt - 1) // rt
    tile_start = jnp.cumsum(tiles) - tiles
    pos = (slot_onehot * (tile_start * rt + before)[:, None, :]).sum(axis=-1)

    units = (tiles + tiles_per_unit - 1) // tiles_per_unit
    unit_first = jnp.cumsum(units) - units
    n_units = units.sum()
    max_units = N_EXPERTS + (n_tokens * TOP_K) // EXPERT_UNIT_ROWS
    uid = jnp.arange(max_units)
    e_of = jnp.clip(jnp.searchsorted(jnp.cumsum(units), uid, side="right"), 0, N_EXPERTS - 1)
    k_in = uid - unit_first[e_of]
    live = uid < n_units
    last_e = e_of[jnp.maximum(n_units - 1, 0)]
    unit_expert = jnp.where(live, e_of, last_e).astype(jnp.int32)
    unit_start = jnp.where(live, (tile_start[e_of] + k_in * tiles_per_unit) * rt, 0).astype(jnp.int32)
    unit_tiles = jnp.where(live, jnp.minimum(tiles[e_of] - k_in * tiles_per_unit, tiles_per_unit), 0).astype(jnp.int32)
    totals = jnp.stack([n_units, tiles.sum()]).astype(jnp.int32)
    pad_plan = jnp.concatenate([tile_start * rt + count, (tile_start + tiles) * rt, tiles.sum()[None]]).astype(jnp.int32)
    return pos.astype(jnp.int32), unit_expert, unit_start, unit_tiles, totals, pad_plan


def _dispatch_body(pos_hbm, pad_ref, h_ref, xs_hbm, pos_smem, zeros, sem):
    i = pl.program_id(0)
    n = pl.num_programs(0)
    groups, sub, _ = h_ref.shape
    per_step = groups * sub * TOP_K
    rt = zeros.shape[0]

    def idx_copy(step):
        half = pl.ds(pl.multiple_of((step % 2) * per_step, per_step), per_step)
        return pltpu.make_async_copy(pos_hbm.at[step], pos_smem.at[half], sem.at[0])

    @pl.when(i == 0)
    def _():
        idx_copy(0).start()
        idx_copy(0).wait()

    @pl.when(i + 1 < n)
    def _():
        idx_copy(i + 1).start()

    base = (i % 2) * per_step

    def send(g, carry):
        for s in range(sub):
            for kk in range(TOP_K):
                r = pos_smem[base + g * (sub * TOP_K) + s * TOP_K + kk]
                pltpu.make_async_copy(h_ref.at[g, pl.ds(s, 1), :], xs_hbm.at[pl.ds(r, 1), :], sem.at[1]).start()
        return carry

    lax.fori_loop(0, groups, send, 0)

    def drain(g, carry):
        pltpu.make_async_copy(h_ref.at[0], xs_hbm.at[pl.ds(0, sub), :], sem.at[1]).wait()
        return carry

    lax.fori_loop(0, groups * TOP_K, drain, 0)

    @pl.when(i + 1 < n)
    def _():
        idx_copy(i + 1).wait()

    @pl.when(i == n - 1)
    def _():
        zeros[...] = jnp.zeros(zeros.shape, zeros.dtype)

        def pad_rows(r, size):
            return pltpu.make_async_copy(zeros.at[pl.ds(0, size), :], xs_hbm.at[pl.ds(r, size), :], sem.at[1])

        def pad_tile(j):
            return pltpu.make_async_copy(zeros, xs_hbm.at[pl.ds(pl.multiple_of(j * rt, rt), rt), :], sem.at[1])

        def each_pad(fn):
            def expert(e, carry):
                first = pad_ref[e]
                end = pad_ref[N_EXPERTS + e]
                aligned = jnp.minimum((first + sub - 1) // sub * sub, end)

                def row(r, carry):
                    fn(pad_rows(r, 1))
                    return carry

                lax.fori_loop(first, aligned, row, 0)
                length = end - aligned
                size = sub
                while size < rt:
                    @pl.when((length & size) != 0)
                    def _(size=size):
                        fn(pad_rows(pl.multiple_of(aligned + (length & (size - 1)), sub), size))
                    size *= 2
                return carry
            lax.fori_loop(0, N_EXPERTS, expert, 0)

            def tile(j, carry):
                fn(pad_tile(j))
                return carry
            lax.fori_loop(pad_ref[2 * N_EXPERTS], xs_hbm.shape[0] // rt, tile, 0)

        each_pad(lambda copy: copy.start())
        each_pad(lambda copy: copy.wait())


def dispatch_rows(h2, pos, pad_plan, p_rows):
    t, d = h2.shape
    tm = GATHER_TOKENS
    sub = SUBLANES
    return pl.pallas_call(
        _dispatch_body,
        grid=(t // tm,),
        in_specs=[pl.BlockSpec(memory_space=pl.ANY),
                  pl.BlockSpec(memory_space=pltpu.SMEM),
                  pl.BlockSpec((tm // sub, sub, d), lambda i: (i, 0, 0))],
        out_specs=pl.BlockSpec(memory_space=pl.ANY),
        out_shape=jax.ShapeDtypeStruct((p_rows, d), h2.dtype),
        scratch_shapes=[pltpu.SMEM((2 * tm * TOP_K,), jnp.int32),
                        pltpu.VMEM((EXPERT_ROW_TILE, d), h2.dtype),
                        pltpu.SemaphoreType.DMA((2,))],
        compiler_params=_params(1),
        name="dispatch_rows",
    )(pos.reshape(t // tm, tm * TOP_K), pad_plan, h2.reshape(t // sub, sub, d))


def _expert_body(ue_ref, us_ref, un_ref, nu_ref,
                 x_hbm, wgu_ref, bgu_ref, wd_ref, bd_ref, y_hbm,
                 xbuf, actbuf, gubuf, ystage, wd_f32, sem_x, sem_y, *, n_up, n_down):
    u = pl.program_id(0)
    c = pl.program_id(1)
    rt = EXPERT_ROW_TILE
    half = EXPERT_UP_CHUNK // 2
    quarter = half // 2
    chunk = EXPERT_DOWN_CHUNK
    n_live = pl.num_programs(0)
    start = us_ref[u]
    n_tiles = un_ref[u]

    n_pairs = n_tiles // 2
    odd = n_tiles % 2 == 1

    def span_rows(j, tiles):
        return pl.ds(pl.multiple_of(j * rt, rt), tiles * rt)

    def tile_rows(j):
        return span_rows(j, 1)

    def x_copy(unit, j):
        rows = pl.ds(pl.multiple_of(us_ref[unit] + j * rt, rt), rt)
        return pltpu.make_async_copy(x_hbm.at[rows, :], xbuf.at[tile_rows(j), :], sem_x.at[0])

    def fetch_rows(unit):
        def body(j, carry):
            x_copy(unit, j).start()
            return carry
        lax.fori_loop(0, un_ref[unit], body, 0)

    @pl.when((u == 0) & (c == 0))
    def _():
        fetch_rows(0)

    @pl.when(c == 0)
    def _():
        def body(j, carry):
            x_copy(u, j).wait()
            return carry
        lax.fori_loop(0, n_tiles, body, 0)

    @pl.when((c == n_up) & (u + 1 < n_live))
    def _():
        fetch_rows(u + 1)

    @pl.when(c < n_up)
    def _():
        bias = bgu_ref[ue_ref[u], pl.ds(c, 1), :]

        n_spans = n_pairs + n_tiles % 2
        first_slot = (c * n_spans) % 2

        def slot_of(i):
            return (first_slot + i) % 2

        def project(j, tiles, slot):
            x = xbuf[span_rows(j, tiles), :].astype(BF16)
            gubuf[slot, :tiles * rt, :] = jnp.dot(x, wgu_ref[0].astype(BF16), preferred_element_type=F32) + bias

        def activate(group, j, tiles, slot):
            gu = gubuf[slot, :tiles * rt, :]
            even = (lax.broadcasted_iota(jnp.int32, (tiles * rt, half), 1) % 2) == 0
            lo = gu[:, :half]
            hi = gu[:, half:]
            gate = jnp.where(even, lo, pltpu.roll(hi, 1, axis=1))
            up = jnp.where(even, pltpu.roll(lo, half - 1, axis=1), hi)
            gate = jnp.minimum(gate, SWIGLU_LIMIT)
            up = jnp.clip(up, -SWIGLU_LIMIT, SWIGLU_LIMIT)
            act = (up + 1.0) * gate * _sigmoid(SWIGLU_ALPHA * gate)
            actbuf[group, span_rows(j, tiles), :] = act.astype(BF16)

        has_pairs = n_pairs >= 1
        shapes = [
            (has_pairs & odd, 2, 2 * n_pairs, 1),
            (has_pairs & jnp.logical_not(odd), 2, 2 * (n_pairs - 1), 2),
            (jnp.logical_not(has_pairs), 1, 0, 1),
        ]
        for case, first_tiles, last_j, last_tiles in shapes:
            @pl.when(case & (c == 0))
            def _(first_tiles=first_tiles):
                project(0, first_tiles, slot_of(0))

            @pl.when(case & (c > 0))
            def _(first_tiles=first_tiles, last_j=last_j, last_tiles=last_tiles):
                activate(c - 1, last_j, last_tiles, slot_of(1))
                project(0, first_tiles, slot_of(0))

        @pl.when(has_pairs)
        def _():
            def body(i, carry):
                activate(c, 2 * (i - 1), 2, slot_of(i - 1))
                project(2 * i, 2, slot_of(i))
                return carry

            lax.fori_loop(1, n_pairs, body, 0)

        @pl.when(has_pairs & odd)
        def _():
            activate(c, 2 * (n_pairs - 1), 2, slot_of(n_pairs - 1))
            project(2 * n_pairs, 1, slot_of(n_pairs))

        for case, first_tiles, last_j, last_tiles in shapes:
            @pl.when(case & (c == n_up - 1))
            def _(last_j=last_j, last_tiles=last_tiles):
                activate(c, last_j, last_tiles, slot_of(n_spans - 1))

    @pl.when(c >= n_up)
    def _():
        cd = c - n_up
        for g in range(chunk // LANES):
            lanes = slice(g * LANES, (g + 1) * LANES)
            for f in range(actbuf.shape[0]):
                base = f * half
                wd_f32[g, pl.ds(base, quarter, stride=2), :] = wd_ref[0, base:base + quarter, lanes]
                wd_f32[g, pl.ds(base + 1, quarter, stride=2), :] = wd_ref[0, base + quarter:base + half, lanes]
        bias = bd_ref[ue_ref[u], pl.ds(cd, 1), :]

        def y_copy(j, tiles, slot):
            rows = pl.ds(pl.multiple_of(start + j * rt, rt), tiles * rt)
            cols = pl.ds(pl.multiple_of(cd * chunk, chunk), chunk)
            return pltpu.make_async_copy(ystage.at[slot, :tiles * rt, :], y_hbm.at[rows, cols], sem_y.at[slot])

        def emit(j, tiles, slot):
            act = jnp.concatenate([actbuf[f, span_rows(j, tiles), :] for f in range(actbuf.shape[0])], axis=1)
            w = jnp.concatenate([wd_f32[g].astype(BF16) for g in range(chunk // LANES)], axis=1)
            ystage[slot, :tiles * rt, :] = jnp.dot(act, w, preferred_element_type=F32) + bias
            y_copy(j, tiles, slot).start()

        def pair(i, carry):
            @pl.when(i >= 2)
            def _():
                y_copy(2 * (i - 2), 2, i % 2).wait()

            emit(2 * i, 2, i % 2)
            return carry

        lax.fori_loop(0, n_pairs, pair, 0)

        @pl.when(n_pairs >= 2)
        def _():
            y_copy(2 * (n_pairs - 2), 2, n_pairs % 2).wait()

        @pl.when(odd)
        def _():
            emit(2 * n_pairs, 1, n_pairs % 2)

        @pl.when(n_pairs >= 1)
        def _():
            y_copy(2 * (n_pairs - 1), 2, (n_pairs - 1) % 2).wait()

        @pl.when(odd)
        def _():
            y_copy(2 * n_pairs, 1, n_pairs % 2).wait()

    @pl.when((u == pl.num_programs(0) - 1) & (c == n_up + n_down - 1))
    def _():
        zero_rows = xbuf.at[:rt, :]
        zero_rows[...] = jnp.zeros(zero_rows.shape, F32)
        used_tiles = nu_ref[1]

        def pad_copy(j):
            return pltpu.make_async_copy(zero_rows, y_hbm.at[pl.ds(pl.multiple_of(j * rt, rt), rt), :], sem_x.at[0])

        def pad_start(j, carry):
            pad_copy(j).start()
            return carry

        def pad_wait(j, carry):
            pad_copy(j).wait()
            return carry

        lax.fori_loop(used_tiles, y_hbm.shape[0] // rt, pad_start, 0)
        lax.fori_loop(used_tiles, y_hbm.shape[0] // rt, pad_wait, 0)


def expert_ffn(x_sorted, w_gate_up, b_gate_up, w_down, b_down, unit_expert, unit_start, unit_tiles, totals):
    p_rows, d = x_sorted.shape
    n_exp, _, two_ff = w_gate_up.shape
    d_ff = two_ff // 2
    up_chunk = EXPERT_UP_CHUNK
    chunk = EXPERT_DOWN_CHUNK
    n_up = two_ff // up_chunk
    n_down = d // chunk
    n_steps = n_up + n_down
    rt = EXPERT_ROW_TILE

    def up_idx(u, c, nu):
        return jnp.minimum(c, n_up - 1)

    def down_idx(u, c, nu):
        return jnp.maximum(c - n_up, 0)

    grid_spec = pltpu.PrefetchScalarGridSpec(
        num_scalar_prefetch=4,
        grid=(totals[0], n_steps),
        in_specs=[pl.BlockSpec(memory_space=pl.ANY),
                  pl.BlockSpec((1, d, up_chunk), lambda u, c, ue, us, un, nu: (ue[u], 0, up_idx(u, c, nu))),
                  pl.BlockSpec((n_exp, n_up, up_chunk), lambda u, c, ue, us, un, nu: (0, 0, 0)),
                  pl.BlockSpec((1, d_ff, chunk), lambda u, c, ue, us, un, nu: (ue[u], 0, down_idx(u, c, nu))),
                  pl.BlockSpec((n_exp, n_down, chunk), lambda u, c, ue, us, un, nu: (0, 0, 0))],
        out_specs=pl.BlockSpec(memory_space=pl.ANY),
        scratch_shapes=[pltpu.VMEM((EXPERT_UNIT_ROWS, d), F32),
                        pltpu.VMEM((two_ff // EXPERT_UP_CHUNK, EXPERT_UNIT_ROWS, EXPERT_UP_CHUNK // 2), BF16),
                        pltpu.VMEM((2, 2 * rt, EXPERT_UP_CHUNK), F32),
                        pltpu.VMEM((2, 2 * rt, chunk), F32),
                        pltpu.VMEM((chunk // LANES, d_ff, LANES), F32),
                        pltpu.SemaphoreType.DMA((1,)),
                        pltpu.SemaphoreType.DMA((2,))],
    )
    return pl.pallas_call(
        functools.partial(_expert_body, n_up=n_up, n_down=n_down),
        grid_spec=grid_spec,
        out_shape=jax.ShapeDtypeStruct((p_rows, d), F32),
        compiler_params=_params(2),
        name="expert_ffn",
    )(unit_expert, unit_start, unit_tiles, totals,
      x_sorted, w_gate_up, b_gate_up.reshape(n_exp, n_up, up_chunk), w_down,
      b_down.reshape(n_exp, n_down, chunk))


def _combine_body(pos_hbm, x2_ref, w_ref, g_ref, y_hbm, o_ref, pos_smem, ybuf, sem, *, final_norm):
    i = pl.program_id(0)
    n = pl.num_programs(0)
    groups, sub, _ = x2_ref.shape
    slot = i % 2
    per_step = groups * sub * TOP_K

    def idx_copy(step):
        half = pl.ds(pl.multiple_of((step % 2) * per_step, per_step), per_step)
        return pltpu.make_async_copy(pos_hbm.at[step], pos_smem.at[half], sem.at[2])

    def fetch_rows(step):
        into = step % 2
        base = into * per_step

        def recv(g, carry):
            for s in range(sub):
                for kk in range(TOP_K):
                    r = pos_smem[base + g * (sub * TOP_K) + s * TOP_K + kk]
                    pltpu.make_async_copy(y_hbm.at[pl.ds(r, 1), :], ybuf.at[into, kk, g, pl.ds(s, 1), :],
                                          sem.at[into]).start()
            return carry

        lax.fori_loop(0, groups, recv, 0)

    @pl.when(i == 0)
    def _():
        idx_copy(0).start()
        idx_copy(0).wait()
        fetch_rows(0)

        @pl.when(n > 1)
        def _():
            idx_copy(1).start()

    @pl.when(i + 1 < n)
    def _():
        idx_copy(i + 1).wait()
        fetch_rows(i + 1)

    @pl.when(i + 2 < n)
    def _():
        idx_copy(i + 2).start()

    def drain(g, carry):
        pltpu.make_async_copy(y_hbm.at[pl.ds(0, sub), :], ybuf.at[slot, 0, 0], sem.at[slot]).wait()
        return carry

    lax.fori_loop(0, groups * TOP_K, drain, 0)

    x3 = x2_ref[...]
    for kk in range(TOP_K):
        x3 = x3 + w_ref[:, :, kk:kk + 1] * ybuf[slot, kk]
    o_ref[...] = _rms(x3, g_ref[...]) if final_norm else x3


def combine(x2, y_sorted, pos, weights, gain):
    m, d = x2.shape
    tm = GATHER_TOKENS
    sub = SUBLANES
    final_norm = gain is not None
    gain = gain if final_norm else jnp.ones((d,), F32)
    rows = lambda width: pl.BlockSpec((tm // sub, sub, width), lambda i: (i, 0, 0))
    out = pl.pallas_call(
        functools.partial(_combine_body, final_norm=final_norm),
        grid=(m // tm,),
        in_specs=[pl.BlockSpec(memory_space=pl.ANY),
                  rows(d), rows(LANES),
                  pl.BlockSpec((1, d), lambda i: (0, 0)),
                  pl.BlockSpec(memory_space=pl.ANY)],
        out_specs=rows(d),
        out_shape=jax.ShapeDtypeStruct((m // sub, sub, d), F32),
        scratch_shapes=[pltpu.SMEM((2 * tm * TOP_K,), jnp.int32),
                        pltpu.VMEM((2, TOP_K, tm // sub, sub, d), F32),
                        pltpu.SemaphoreType.DMA((3,))],
        compiler_params=_params(1),
        name="combine",
    )(pos.reshape(m // tm, tm * TOP_K), x2.reshape(m // sub, sub, d), weights.reshape(m // sub, sub, LANES),
      gain.reshape(1, d), y_sorted)
    return out.reshape(m, d)


def kernel(x, mem, rel_bias_table, mix_norm, w_in, diff_lambda_q1, diff_lambda_k1, diff_lambda_q2, diff_lambda_k2, diff_subln, w_branch_moba, w_branch_diff, w_mix_out, xattn_norm, mem_norm, w_xq, w_xk, w_xv, w_xo, ffn_norm, w_router, b_router, w_gate_up, b_gate_up, w_down, b_down, final_norm):
    batch, seq, d = x.shape
    n_tok = batch * seq
    assert ATTN_TILE == 2 * LANES and ATTN_TILE > REL_MAX_DISTANCE and seq % ATTN_TILE == 0
    assert seq // ATTN_TILE <= LANES and d == 2 * MOBA_WIDTH == 2 * DIFF_WIDTH
    assert n_tok % IN_PROJ_ROWS == 0 and IN_PROJ_ROWS % GATHER_TOKENS == 0 and d % EXPERT_DOWN_CHUNK == 0
    assert w_gate_up.shape[1:] == (N_EXPERTS, d, 2 * w_down.shape[2]) and w_gate_up.shape[3] % EXPERT_UP_CHUNK == 0
    x2d = x.reshape(n_tok, d)
    near, far = _bias_tiles(rel_bias_table)
    diff_col0 = 3 * MOBA_WIDTH
    gate_col0 = diff_col0 + 3 * DIFF_WIDTH
    cols = jnp.arange(w_in.shape[2])
    is_q = (cols < MOBA_WIDTH) | ((cols >= diff_col0) & (cols < diff_col0 + DIFF_WIDTH))
    col_scale = jnp.where(is_q, ATTN_SCALE * LOG2E, 1.0)
    p_rows = n_tok * TOP_K + N_EXPERTS * EXPERT_ROW_TILE
    for l in range(w_in.shape[0]):
        y = norm_matmul_colscale(x2d, mix_norm[l], w_in[l], col_scale)
        o_moba = moba_attention(y, near[:MOBA_HEADS], far[:MOBA_HEADS], batch, seq)
        o_diff = diff_attention(y, near[MOBA_HEADS:], far[MOBA_HEADS:], diff_lambda_q1[l], diff_lambda_k1[l],
                                diff_lambda_q2[l], diff_lambda_k2[l], diff_subln[l], batch, seq, diff_col0)
        x1, hx = merge_mixout(o_moba, o_diff, y, gate_col0, w_branch_moba[l].astype(BF16),
                              w_branch_diff[l].astype(BF16), w_mix_out[l].astype(BF16), x2d, xattn_norm[l])
        k_mem, v_mem = memory_kv(mem.reshape(-1, d), mem_norm[l], w_xk[l].astype(BF16), w_xv[l].astype(BF16),
                                 mem.shape[1])
        x2, h2, idx_pad, wgt_pad = cross_attention_router(hx, x1, k_mem, v_mem, w_xq[l].astype(BF16),
                                                          w_xo[l].astype(BF16), ffn_norm[l], w_router[l],
                                                          b_router[l], seq)
        pos, unit_expert, unit_start, unit_tiles, totals, pad_plan = _routing_plan(idx_pad[:, :TOP_K], n_tok)
        x_sorted = dispatch_rows(h2, pos, pad_plan, p_rows)
        y_sorted = expert_ffn(x_sorted, w_gate_up[l], b_gate_up[l], w_down[l], b_down[l],
                              unit_expert, unit_start, unit_tiles, totals)
        last = l == w_in.shape[0] - 1
        x2d = combine(x2, y_sorted, pos, wgt_pad, final_norm if last else None)
    return x2d.reshape(batch, seq, d)
```
